```python
import jax
import jax.numpy as jnp
from jax import lax
import numpy as np

D_MODEL = 1024
BATCH = 2
SEQ = 8192
DEPTH = 1

EPS = 1e-6
HG_HEADS = 4
HG_DK = 128
HG_DV = 128
HG_QF = HG_HEADS * HG_DK
HG_WIDTH = HG_HEADS * HG_DV
HG_CHUNK = 64
SSM_HEADS = 8
SSM_HEAD_DIM = 64
SSM_WIDTH = SSM_HEADS * SSM_HEAD_DIM
SSM_GROUPS = 2
SSM_STATE = 128
SSM_CONV = 4
SSM_CONV_CH = SSM_WIDTH + 2 * SSM_GROUPS * SSM_STATE
SSM_CHUNK = 128
DT_MIN = 1e-3
DT_MAX = 1e-1
D_MIX = HG_WIDTH + SSM_WIDTH
IN_SPLITS = (HG_QF, HG_QF, HG_WIDTH, HG_WIDTH, SSM_WIDTH, SSM_CONV_CH, SSM_HEADS)
IN_COLS = sum(IN_SPLITS)
N_EXPERTS = 32
TOP_K = 4
D_FF = 1024
SWIGLU_LIMIT = 7.0
SWIGLU_ALPHA = 1.702
MOE_BLOCK = 128
N_MOD = 6

kernel_name = "hgrn2_mamba2_parallel_moe_adaln"


def rmsnorm(x, w):
    xf = x.astype(jnp.float32)
    y = xf * lax.rsqrt(jnp.mean(xf * xf, axis=-1, keepdims=True) + EPS)
    return (y * w.astype(jnp.float32)).astype(x.dtype)


def modulate(h, shift, scale):
    return h * (1.0 + scale[:, None, :]) + shift[:, None, :]


def scan_chunk_states(decay, d_state):
    def step(s, inp):
        dec, ds = inp
        return dec * s + ds, s
    s0 = jnp.zeros_like(d_state[:, 0])
    _, prev = lax.scan(step, s0, (jnp.moveaxis(decay, 1, 0), jnp.moveaxis(d_state, 1, 0)))
    return jnp.moveaxis(prev, 0, 1)


def hgrn2_chunked(q, k, v, log_f):
    bsz, seqlen, nh, _ = q.shape
    c = HG_CHUNK
    n = seqlen // c
    r = lambda t: t.reshape(bsz, n, c, nh, t.shape[-1])
    q, k, v, log_f = r(q), r(k), r(v), r(log_f)
    b = jnp.cumsum(log_f, axis=2)
    b_mid = b[:, :, c // 2 - 1:c // 2]
    b_last = b[:, :, -1]
    causal = jnp.tril(jnp.ones((c, c), dtype=bool))
    att = jnp.einsum('bnthk,bnshk->bnhts', q * jnp.exp(b - b_mid), k * jnp.exp(b_mid - b))
    att = jnp.where(causal, att, 0.0)
    o_intra = jnp.einsum('bnhts,bnshv->bnthv', att, v)
    d_state = jnp.einsum('bnshk,bnshv->bnhkv', k * jnp.exp(b_last[:, :, None] - b), v)
    s_prev = scan_chunk_states(jnp.exp(b_last)[..., None], d_state)
    o_inter = jnp.einsum('bnthk,bnhkv->bnthv', q * jnp.exp(b), s_prev)
    return (o_intra + o_inter).reshape(bsz, seqlen, nh, v.shape[-1])


def ssd_chunked(x, dt, a, bm, cm):
    bsz, seqlen, nh, p = x.shape
    g = bm.shape[2]
    hg = nh // g
    c = SSM_CHUNK
    n = seqlen // c
    x = x.reshape(bsz, n, c, g, hg, p)
    dt = dt.reshape(bsz, n, c, g, hg)
    bm = bm.reshape(bsz, n, c, g, -1)
    cm = cm.reshape(bsz, n, c, g, -1)
    a_cum = jnp.cumsum(dt * a.reshape(g, hg), axis=2)
    a_last = a_cum[:, :, -1]
    xdt = x * dt[..., None]
    a_t = jnp.moveaxis(a_cum, 2, -1)
    causal = jnp.tril(jnp.ones((c, c), dtype=bool))
    seg = jnp.exp(jnp.where(causal, a_t[..., :, None] - a_t[..., None, :], -jnp.inf))
    cb = jnp.einsum('bntgd,bnsgd->bngts', cm, bm)
    scores = cb[:, :, :, None] * seg
    y_intra = jnp.einsum('bnghts,bnsghp->bntghp', scores, xdt)
    d_state = jnp.einsum('bnsgd,bnsgh,bnsghp->bnghpd', bm, jnp.exp(a_last[:, :, None] - a_cum), xdt)
    s_prev = scan_chunk_states(jnp.exp(a_last)[..., None, None], d_state)
    y_inter = jnp.einsum('bntgd,bntgh,bnghpd->bntghp', cm, jnp.exp(a_cum), s_prev)
    return (y_intra + y_inter).reshape(bsz, seqlen, nh, p)


def causal_depthwise_conv(x, w, b):
    y = lax.conv_general_dilated(x, w[:, None, :], window_strides=(1,), padding=[(SSM_CONV - 1, 0)],
                                 dimension_numbers=('NWC', 'WIO', 'NWC'), feature_group_count=x.shape[-1])
    return y + b


def hybrid_mixer(u, w_in, lb, hg_norm_w, a_log, dt_bias, d_skip, conv_w, conv_b, ssm_norm_w, w_out):
    bsz, seqlen, _ = u.shape
    f32 = jnp.float32
    proj = u @ w_in
    q, f, i, g, z, xbc, dt_raw = jnp.split(proj, np.cumsum(IN_SPLITS[:-1]).tolist(), axis=-1)
    hd = lambda t, d: t.reshape(bsz, seqlen, HG_HEADS, d)
    fg = lb + (1.0 - lb) * jax.nn.sigmoid(f.astype(f32))
    o_hg = hgrn2_chunked(hd(jax.nn.silu(q.astype(f32)), HG_DK), hd(1.0 - fg, HG_DK),
                         hd(i.astype(f32), HG_DV), hd(jnp.log(fg), HG_DK))
    o_hg = rmsnorm(o_hg, hg_norm_w) * jax.nn.silu(hd(g.astype(f32), HG_DV))
    o_hg = o_hg.reshape(bsz, seqlen, HG_WIDTH)
    xbc = jax.nn.silu(causal_depthwise_conv(xbc, conv_w, conv_b)).astype(f32)
    xs, bm, cm = jnp.split(xbc, [SSM_WIDTH, SSM_WIDTH + SSM_GROUPS * SSM_STATE], axis=-1)
    xs = xs.reshape(bsz, seqlen, SSM_HEADS, SSM_HEAD_DIM)
    dt = jax.nn.softplus(dt_raw.astype(f32) + dt_bias.astype(f32))
    a = -jnp.exp(a_log.astype(f32))
    y = ssd_chunked(xs, dt, a, bm.reshape(bsz, seqlen, SSM_GROUPS, SSM_STATE),
                    cm.reshape(bsz, seqlen, SSM_GROUPS, SSM_STATE))
    y = y + d_skip.astype(f32)[:, None] * xs
    y = y.reshape(bsz, seqlen, SSM_WIDTH) * jax.nn.silu(z.astype(f32))
    y = rmsnorm(y.reshape(bsz, seqlen, SSM_GROUPS, SSM_WIDTH // SSM_GROUPS),
                ssm_norm_w.reshape(SSM_GROUPS, SSM_WIDTH // SSM_GROUPS)).reshape(bsz, seqlen, SSM_WIDTH)
    mixed = jnp.concatenate([o_hg, y], axis=-1).astype(u.dtype)
    return mixed @ w_out


def moe_ffn(h, router_w, router_b, w1, b1, w2, b2):
    bsz, seqlen, d = h.shape
    t = bsz * seqlen
    n_assign = t * TOP_K
    hf = h.reshape(t, d)
    logits = (hf @ router_w).astype(jnp.float32) + router_b.astype(jnp.float32)
    top_logits, top_idx = lax.top_k(logits, TOP_K)
    gates = jax.nn.softmax(top_logits, axis=-1)
    flat_e = top_idx.reshape(n_assign)
    order = jnp.argsort(flat_e).astype(jnp.int32)
    sorted_e = flat_e[order]
    counts = jnp.bincount(flat_e, length=N_EXPERTS)
    padded = (counts + MOE_BLOCK - 1) // MOE_BLOCK * MOE_BLOCK
    start = jnp.cumsum(counts) - counts
    pend = jnp.cumsum(padded)
    pstart = pend - padded
    dest = pstart[sorted_e] + jnp.arange(n_assign, dtype=jnp.int32) - start[sorted_e]
    n_rows = (n_assign + MOE_BLOCK - 1) // MOE_BLOCK * MOE_BLOCK + N_EXPERTS * MOE_BLOCK
    n_blocks = n_rows // MOE_BLOCK
    row_assign = jnp.full((n_rows,), n_assign, jnp.int32).at[dest].set(order)
    row_tok = row_assign // TOP_K
    block_e = jnp.minimum(jnp.searchsorted(pend, jnp.arange(n_blocks) * MOE_BLOCK, side='right'),
                          N_EXPERTS - 1)
    h_pad = jnp.concatenate([hf, jnp.zeros((1, d), hf.dtype)], axis=0)
    xin = h_pad[row_tok].reshape(n_blocks, MOE_BLOCK, d)

    def expert_block(args):
        xb, e = args
        hb = xb @ w1[e] + b1[e]
        glu, lin = jnp.split(hb, 2, axis=-1)
        glu = jnp.minimum(glu, SWIGLU_LIMIT)
        lin = jnp.clip(lin, -SWIGLU_LIMIT, SWIGLU_LIMIT)
        act = glu * jax.nn.sigmoid(SWIGLU_ALPHA * glu) * (lin + 1.0)
        return act @ w2[e] + b2[e]

    yout = lax.map(expert_block, (xin, block_e)).reshape(n_rows, d)
    gate_rows = jnp.concatenate([gates.reshape(n_assign), jnp.zeros((1,), gates.dtype)])[row_assign]
    out = jax.ops.segment_sum(yout * gate_rows[:, None].astype(yout.dtype), row_tok, num_segments=t + 1)[:t]
    return out.reshape(bsz, seqlen, d)


def setup_inputs(seed: int = 0) -> dict:
    key = jax.random.key(seed)
    ks = jax.random.split(key, 24)
    nrm = lambda k, shape, s: jax.random.normal(k, shape, jnp.float32) * s
    dt0 = jnp.exp(jax.random.uniform(ks[9], (DEPTH, SSM_HEADS), jnp.float32,
                                     jnp.log(DT_MIN), jnp.log(DT_MAX)))
    return {
        'x': nrm(ks[0], (BATCH, SEQ, D_MODEL), 1.0),
        'c': nrm(ks[1], (BATCH, D_MODEL), 1.0),
        'ada_w': nrm(ks[2], (DEPTH, D_MODEL, N_MOD * D_MODEL), 0.5 * D_MODEL ** -0.5),
        'ada_b': nrm(ks[3], (DEPTH, N_MOD * D_MODEL), 0.02),
        'norm1_w': 1.0 + nrm(ks[4], (DEPTH, D_MODEL), 0.02),
        'w_in': nrm(ks[5], (DEPTH, D_MODEL, IN_COLS), D_MODEL ** -0.5),
        'hg_lb_logits': nrm(ks[6], (DEPTH + 1, HG_QF), 0.1),
        'hg_norm_w': 1.0 + nrm(ks[7], (DEPTH, HG_DV), 0.02),
        'ssm_a_log': jnp.log(jax.random.uniform(ks[8], (DEPTH, SSM_HEADS), jnp.float32, 1.0, 16.0)),
        'ssm_dt_bias': dt0 + jnp.log(-jnp.expm1(-dt0)),
        'ssm_d': 1.0 + nrm(ks[10], (DEPTH, SSM_HEADS), 0.02),
        'ssm_conv_w': nrm(ks[11], (DEPTH, SSM_CONV, SSM_CONV_CH), SSM_CONV ** -0.5),
        'ssm_conv_b': nrm(ks[12], (DEPTH, SSM_CONV_CH), 0.02),
        'ssm_norm_w': 1.0 + nrm(ks[13], (DEPTH, SSM_WIDTH), 0.02),
        'w_out': nrm(ks[14], (DEPTH, D_MIX, D_MODEL), D_MIX ** -0.5),
        'norm2_w': 1.0 + nrm(ks[15], (DEPTH, D_MODEL), 0.02),
        'router_w': nrm(ks[16], (DEPTH, D_MODEL, N_EXPERTS), D_MODEL ** -0.5),
        'router_b': nrm(ks[17], (DEPTH, N_EXPERTS), 0.01),
        'exp_w1': nrm(ks[18], (DEPTH, N_EXPERTS, D_MODEL, 2 * D_FF), D_MODEL ** -0.5),
        'exp_b1': nrm(ks[19], (DEPTH, N_EXPERTS, 2 * D_FF), 0.01),
        'exp_w2': nrm(ks[20], (DEPTH, N_EXPERTS, D_FF, D_MODEL), D_FF ** -0.5),
        'exp_b2': nrm(ks[21], (DEPTH, N_EXPERTS, D_MODEL), 0.01),
        'final_norm_w': 1.0 + nrm(ks[22], (D_MODEL,), 0.02),
    }


def reference(x, c, ada_w, ada_b, norm1_w, w_in, hg_lb_logits, hg_norm_w, ssm_a_log, ssm_dt_bias,
              ssm_d, ssm_conv_w, ssm_conv_b, ssm_norm_w, w_out, norm2_w, router_w, router_b,
              exp_w1, exp_b1, exp_w2, exp_b2, final_norm_w):
    lb_all = jnp.cumsum(jax.nn.softmax(hg_lb_logits.astype(jnp.float32), axis=0), axis=0)
    c_act = jax.nn.silu(c)
    for l in range(DEPTH):
        mod = c_act @ ada_w[l] + ada_b[l]
        sh1, sc1, g1, sh2, sc2, g2 = jnp.split(mod, N_MOD, axis=-1)
        h = modulate(rmsnorm(x, norm1_w[l]), sh1, sc1)
        x = x + g1[:, None, :] * hybrid_mixer(h, w_in[l], lb_all[l], hg_norm_w[l], ssm_a_log[l],
                                              ssm_dt_bias[l], ssm_d[l], ssm_conv_w[l], ssm_conv_b[l],
                                              ssm_norm_w[l], w_out[l])
        h = modulate(rmsnorm(x, norm2_w[l]), sh2, sc2)
        x = x + g2[:, None, :] * moe_ffn(h, router_w[l], router_b[l], exp_w1[l], exp_b1[l],
                                         exp_w2[l], exp_b2[l])
    return rmsnorm(x, final_norm_w)
```

```python
import functools

import jax
import jax.numpy as jnp
from jax import lax
from jax.experimental import pallas as pl
from jax.experimental.pallas import tpu as pltpu

F32 = jnp.float32
BF16 = jnp.bfloat16
HIGHEST = lax.Precision.HIGHEST

EPS = 1e-6
D_MODEL = 1024
HG_HEADS = 4
HG_DK = 128
HG_DV = 128
HG_QF = HG_HEADS * HG_DK
HG_WIDTH = HG_HEADS * HG_DV
HG_CHUNK = 64
SSM_HEADS = 8
SSM_HEAD_DIM = 64
SSM_WIDTH = SSM_HEADS * SSM_HEAD_DIM
SSM_GROUPS = 2
SSM_GROUP_HEADS = SSM_HEADS // SSM_GROUPS
SSM_GROUP_WIDTH = SSM_WIDTH // SSM_GROUPS
SSM_STATE = 128
SSM_CONV = 4
SSM_CONV_CH = SSM_WIDTH + 2 * SSM_GROUPS * SSM_STATE
SSM_CHUNK = 128
IN_SPLITS = (HG_QF, HG_QF, HG_WIDTH, HG_WIDTH, SSM_WIDTH, SSM_CONV_CH, SSM_HEADS)
IN_COLS = sum(IN_SPLITS)
N_EXPERTS = 32
TOP_K = 4
D_FF = 1024
SWIGLU_LIMIT = 7.0
SWIGLU_ALPHA = 1.702
N_MOD = 6

LANES = 128
SUBLANES = 8
VMEM_LIMIT = 56 * 1024 * 1024

ROW_TILE = 512
SEQ_TILE = 512
MOE_ROWS = 256
GATHER_TILE = 128
NEG_BIG = -1e30


def _silu(v):
    return v * jax.nn.sigmoid(v)


def _softplus(v):
    return jnp.maximum(v, 0.0) + jnp.log1p(jnp.exp(-jnp.abs(v)))


def _params(sem):
    return pltpu.CompilerParams(dimension_semantics=sem, vmem_limit_bytes=VMEM_LIMIT)


def _ada_kernel(c_ref, w_ref, b_ref, o_ref):
    o_ref[...] = jnp.dot(_silu(c_ref[...]), w_ref[...], precision=HIGHEST,
                         preferred_element_type=F32) + b_ref[...]


def _ada_mod(c_pad, ada_w, ada_b):
    n = ada_w.shape[1]
    tn = D_MODEL
    return pl.pallas_call(
        _ada_kernel,
        grid=(n // tn,),
        in_specs=[pl.BlockSpec((SUBLANES, D_MODEL), lambda j: (0, 0)),
                  pl.BlockSpec((D_MODEL, tn), lambda j: (0, j)),
                  pl.BlockSpec((1, tn), lambda j: (0, j))],
        out_specs=pl.BlockSpec((SUBLANES, tn), lambda j: (0, j)),
        out_shape=jax.ShapeDtypeStruct((SUBLANES, n), F32),
        compiler_params=_params(("arbitrary",)),
        name="ada_mod",
    )(c_pad, ada_w, ada_b.reshape(1, n))


_OFF = [0]
for _w in IN_SPLITS:
    _OFF.append(_OFF[-1] + _w)
IN_COLS_PAD = _OFF[6] + LANES


def _inproj_kernel(x_ref, nw_ref, sh_ref, sc_ref, w_ref, lbl_ref, dtb_ref,
                   qs_ref, kk_ref, lf_ref, vi_ref, gg_ref, zz_ref, xbc_ref, dt_ref):
    x = x_ref[0]
    ms = jnp.mean(x * x, axis=-1, keepdims=True)
    h = x * lax.rsqrt(ms + EPS) * nw_ref[...]
    h = h * (1.0 + sc_ref[0]) + sh_ref[0]
    hb = h.astype(BF16)

    def seg(k):
        return jnp.dot(hb, w_ref[:, _OFF[k]:_OFF[k] + (IN_SPLITS[k] if k < 6 else LANES)],
                       preferred_element_type=F32)

    lbl = lbl_ref[...]
    le = jnp.exp(lbl - jnp.max(lbl, axis=0, keepdims=True))
    lb = le[0:1, :] / jnp.sum(le, axis=0, keepdims=True)

    qs_ref[0] = _silu(seg(0)).astype(BF16)
    fg = lb + (1.0 - lb) * jax.nn.sigmoid(seg(1))
    kk_ref[0] = (1.0 - fg).astype(BF16)
    lf_ref[0] = jnp.log(fg)
    vi_ref[0] = seg(2).astype(BF16)
    gg_ref[0] = _silu(seg(3)).astype(BF16)
    zz_ref[0] = _silu(seg(4)).astype(BF16)
    xbc_ref[0] = seg(5).astype(BF16)
    dt_ref[0] = _softplus(seg(6) + dtb_ref[...])


def _inproj(x, nw, sh, sc, w_in_pad, lb_logits, dtb_pad):
    bsz, seqlen, d = x.shape
    tm = min(ROW_TILE, seqlen)
    nt = seqlen // tm
    tok = lambda w: pl.BlockSpec((1, tm, w), lambda b, i: (b, i, 0))
    full = lambda a: pl.BlockSpec(a.shape, lambda b, i: (0,) * a.ndim)
    mod = pl.BlockSpec((1, 1, d), lambda b, i: (b, 0, 0))
    shp = lambda w, dt: jax.ShapeDtypeStruct((bsz, seqlen, w), dt)
    return pl.pallas_call(
        _inproj_kernel,
        grid=(bsz, nt),
        in_specs=[tok(d), full(nw), mod, mod, full(w_in_pad), full(lb_logits), full(dtb_pad)],
        out_specs=[tok(HG_QF), tok(HG_QF), tok(HG_QF), tok(HG_WIDTH), tok(HG_WIDTH),
                   tok(SSM_WIDTH), tok(SSM_CONV_CH), tok(LANES)],
        out_shape=[shp(HG_QF, BF16), shp(HG_QF, BF16), shp(HG_QF, F32), shp(HG_WIDTH, BF16),
                   shp(HG_WIDTH, BF16), shp(SSM_WIDTH, BF16), shp(SSM_CONV_CH, BF16),
                   shp(LANES, F32)],
        compiler_params=_params(("arbitrary", "arbitrary")),
        name="inproj",
    )(x, nw, sh, sc, w_in_pad, lb_logits, dtb_pad)


def _hgrn_kernel(qs_ref, kk_ref, lf_ref, vi_ref, gg_ref, nw_ref, o_ref, st_ref):
    @pl.when(pl.program_id(2) == 0)
    def _():
        st_ref[...] = jnp.zeros_like(st_ref)

    c = HG_CHUNK
    rows = lax.broadcasted_iota(jnp.int32, (c, c), 0)
    cols = lax.broadcasted_iota(jnp.int32, (c, c), 1)
    causal = rows >= cols
    tril = causal.astype(F32)
    nw = nw_ref[...]
    tq = qs_ref.shape[1]
    nn = (((1,), (1,)), ((), ()))
    tn = (((0,), (0,)), ((), ()))
    for ci in range(tq // c):
        sl = pl.ds(ci * c, c)
        q = qs_ref[0, sl, :].astype(F32)
        k = kk_ref[0, sl, :].astype(F32)
        v = vi_ref[0, sl, :]
        b = jnp.dot(tril, lf_ref[0, sl, :], precision=HIGHEST, preferred_element_type=F32)
        b_mid = b[c // 2 - 1:c // 2, :]
        b_last = b[c - 1:c, :]
        qa = (q * jnp.exp(b - b_mid)).astype(BF16)
        ka = (k * jnp.exp(b_mid - b)).astype(BF16)
        att = lax.dot_general(qa, ka, nn, preferred_element_type=F32)
        att = jnp.where(causal, att, 0.0).astype(BF16)
        o = jnp.dot(att, v, preferred_element_type=F32)
        st = st_ref[...]
        qb = (q * jnp.exp(b)).astype(BF16)
        o = o + lax.dot_general(qb, st.astype(BF16), nn, preferred_element_type=F32)
        kd = (k * jnp.exp(b_last - b)).astype(BF16)
        st_ref[...] = st * jnp.exp(b_last) + lax.dot_general(v, kd, tn, preferred_element_type=F32)
        ms = jnp.mean(o * o, axis=-1, keepdims=True)
        y = o * lax.rsqrt(ms + EPS) * nw * gg_ref[0, sl, :].astype(F32)
        o_ref[0, sl, :] = y.astype(BF16)


def _hgrn(qs, kk, lf, vi, gg, nw):
    bsz, seqlen, _ = qs.shape
    tq = min(SEQ_TILE, seqlen)
    blk = pl.BlockSpec((1, tq, HG_DK), lambda b, h, i: (b, i, h))
    return pl.pallas_call(
        _hgrn_kernel,
        grid=(bsz, HG_HEADS, seqlen // tq),
        in_specs=[blk, blk, blk, blk, blk, pl.BlockSpec((1, HG_DV), lambda b, h, i: (0, 0))],
        out_specs=blk,
        out_shape=jax.ShapeDtypeStruct((bsz, seqlen, HG_WIDTH), BF16),
        scratch_shapes=[pltpu.VMEM((HG_DV, HG_DK), F32)],
        compiler_params=_params(("arbitrary", "arbitrary", "arbitrary")),
        name="hgrn2",
    )(qs, kk, lf, vi, gg, nw)


def _ssd_kernel(xbc_ref, zz_ref, dt_ref, cw_ref, cb_ref, alog_ref, dsk_ref, nw_ref,
                y_ref, buf_ref, xc_ref, yc_ref, st_ref):
    tq = xbc_ref.shape[1]
    halo = SUBLANES

    @pl.when(pl.program_id(1) == 0)
    def _():
        buf_ref[0:halo, :] = jnp.zeros((halo, SSM_CONV_CH), F32)
        st_ref[...] = jnp.zeros_like(st_ref)

    buf_ref[halo:halo + tq, :] = xbc_ref[0].astype(F32)
    conv = cb_ref[...] + buf_ref[halo - 3:halo - 3 + tq, :] * cw_ref[0:1, :]
    for w in range(1, SSM_CONV):
        conv = conv + buf_ref[halo - 3 + w:halo - 3 + w + tq, :] * cw_ref[w:w + 1, :]
    buf_ref[0:halo, :] = buf_ref[tq:tq + halo, :]
    xc_ref[...] = _silu(conv)

    c = SSM_CHUNK
    p = SSM_HEAD_DIM
    rows = lax.broadcasted_iota(jnp.int32, (c, c), 0)
    cols = lax.broadcasted_iota(jnp.int32, (c, c), 1)
    causal = rows >= cols
    tril = causal.astype(F32)
    a_neg = -jnp.exp(alog_ref[...])
    nn = (((1,), (1,)), ((), ()))
    tn = (((0,), (0,)), ((), ()))
    for ci in range(tq // c):
        sl = pl.ds(ci * c, c)
        dt = dt_ref[0, sl, :]
        acum = jnp.dot(tril, dt * a_neg, precision=HIGHEST, preferred_element_type=F32)
        acum_t = acum.T
        decay_out = jnp.exp(acum)
        decay_in = jnp.exp(acum[c - 1:c, :] - acum)
        for g in range(SSM_GROUPS):
            bm = xc_ref[sl, SSM_WIDTH + g * SSM_STATE:SSM_WIDTH + (g + 1) * SSM_STATE].astype(BF16)
            cm = xc_ref[sl, SSM_WIDTH + (SSM_GROUPS + g) * SSM_STATE:
                        SSM_WIDTH + (SSM_GROUPS + g + 1) * SSM_STATE].astype(BF16)
            cb = lax.dot_general(cm, bm, nn, preferred_element_type=F32)
            st = st_ref[g]
            y_inter = lax.dot_general(cm, st.astype(BF16), nn, preferred_element_type=F32)
            for hh in range(SSM_GROUP_HEADS):
                h = g * SSM_GROUP_HEADS + hh
                lo = h * p
                xdt = xc_ref[sl, lo:lo + p] * dt[:, h:h + 1]
                seg = jnp.exp(jnp.where(causal, acum[:, h:h + 1] - acum_t[h:h + 1, :], -jnp.inf))
                y = jnp.dot((cb * seg).astype(BF16), xdt.astype(BF16), preferred_element_type=F32)
                y = y + decay_out[:, h:h + 1] * y_inter[:, hh * p:(hh + 1) * p]
                yc_ref[:, lo:lo + p] = y
                d_st = lax.dot_general((xdt * decay_in[:, h:h + 1]).astype(BF16), bm, tn,
                                       preferred_element_type=F32)
                a_last = jnp.exp(acum_t[h:h + 1, c - 1:c])
                st_ref[g, hh * p:(hh + 1) * p, :] = st[hh * p:(hh + 1) * p, :] * a_last + d_st
        y = (yc_ref[...] + dsk_ref[...] * xc_ref[sl, 0:SSM_WIDTH]) * zz_ref[0, sl, :].astype(F32)
        for g in range(SSM_GROUPS):
            gs = slice(g * SSM_GROUP_WIDTH, (g + 1) * SSM_GROUP_WIDTH)
            yg = y[:, gs]
            ms = jnp.mean(yg * yg, axis=-1, keepdims=True)
            y_ref[0, sl, gs] = (yg * lax.rsqrt(ms + EPS) * nw_ref[:, gs]).astype(BF16)


def _ssd(xbc, zz, dt, conv_w, conv_b, alog_pad, dsk_wide, nw):
    bsz, seqlen, _ = xbc.shape
    tq = min(SEQ_TILE, seqlen)
    tok = lambda w: pl.BlockSpec((1, tq, w), lambda b, i: (b, i, 0))
    full = lambda a: pl.BlockSpec(a.shape, lambda b, i: (0,) * a.ndim)
    return pl.pallas_call(
        _ssd_kernel,
        grid=(bsz, seqlen // tq),
        in_specs=[tok(SSM_CONV_CH), tok(SSM_WIDTH), tok(LANES), full(conv_w), full(conv_b),
                  full(alog_pad), full(dsk_wide), full(nw)],
        out_specs=tok(SSM_WIDTH),
        out_shape=jax.ShapeDtypeStruct((bsz, seqlen, SSM_WIDTH), BF16),
        scratch_shapes=[pltpu.VMEM((tq + 2 * SUBLANES, SSM_CONV_CH), F32),
                        pltpu.VMEM((tq, SSM_CONV_CH), F32),
                        pltpu.VMEM((SSM_CHUNK, SSM_WIDTH), F32),
                        pltpu.VMEM((SSM_GROUPS, SSM_GROUP_WIDTH, SSM_STATE), F32)],
        compiler_params=_params(("arbitrary", "arbitrary")),
        name="ssd",
    )(xbc, zz, dt, conv_w, conv_b, alog_pad, dsk_wide, nw)


def _outproj_kernel(x_ref, oh_ref, ys_ref, wo_ref, g1_ref, nw_ref, sh_ref, sc_ref, rw_ref, rb_ref,
                    x1_ref, h2_ref, idx_ref, gate_ref, cnt_ref):
    first = jnp.logical_and(pl.program_id(0) == 0, pl.program_id(1) == 0)

    @pl.when(first)
    def _():
        cnt_ref[...] = jnp.zeros_like(cnt_ref)

    mix = jnp.dot(oh_ref[0], wo_ref[0:HG_WIDTH, :], preferred_element_type=F32)
    mix = mix + jnp.dot(ys_ref[0], wo_ref[HG_WIDTH:, :], preferred_element_type=F32)
    x1 = x_ref[0] + g1_ref[0] * mix
    x1_ref[0] = x1
    ms = jnp.mean(x1 * x1, axis=-1, keepdims=True)
    h2 = x1 * lax.rsqrt(ms + EPS) * nw_ref[...]
    h2 = h2 * (1.0 + sc_ref[0]) + sh_ref[0]
    h2_ref[0] = h2
    logits = jnp.dot(h2, rw_ref[...], precision=HIGHEST, preferred_element_type=F32) + rb_ref[...]
    lane = lax.broadcasted_iota(jnp.int32, logits.shape, 1).astype(F32)
    idx_out = jnp.zeros(logits.shape, F32)
    val_out = jnp.zeros(logits.shape, F32)
    sel = jnp.zeros(logits.shape, F32)
    work = logits
    top = None
    denom = None
    for k in range(TOP_K):
        m = jnp.max(work, axis=-1, keepdims=True)
        am = jnp.min(jnp.where(work == m, lane, float(LANES)), axis=-1, keepdims=True)
        hit = lane == am
        work = jnp.where(hit, -jnp.inf, work)
        sel = jnp.where(hit, 1.0, sel)
        if k == 0:
            top = m
        e = jnp.exp(m - top)
        denom = e if k == 0 else denom + e
        idx_out = jnp.where(lane == float(k), am, idx_out)
        val_out = jnp.where(lane == float(k), e, val_out)
    idx_ref[0] = idx_out.astype(jnp.int32)
    gate_ref[0] = val_out / denom
    cnt_ref[0:1, :] += jnp.sum(sel, axis=0, keepdims=True)


def _outproj(x, o_hg, y_ssd, w_out_b, g1, nw, sh, sc, rw_pad, rb_pad):
    bsz, seqlen, d = x.shape
    tm = min(ROW_TILE, seqlen)
    tok = lambda w: pl.BlockSpec((1, tm, w), lambda b, i: (b, i, 0))
    full = lambda a: pl.BlockSpec(a.shape, lambda b, i: (0,) * a.ndim)
    mod = pl.BlockSpec((1, 1, d), lambda b, i: (b, 0, 0))
    shp = lambda w, dt: jax.ShapeDtypeStruct((bsz, seqlen, w), dt)
    return pl.pallas_call(
        _outproj_kernel,
        grid=(bsz, seqlen // tm),
        in_specs=[tok(d), tok(HG_WIDTH), tok(SSM_WIDTH), full(w_out_b), mod, full(nw), mod, mod,
                  full(rw_pad), full(rb_pad)],
        out_specs=[tok(d), tok(d), tok(LANES), tok(LANES),
                   pl.BlockSpec((SUBLANES, LANES), lambda b, i: (0, 0))],
        out_shape=[shp(d, F32), shp(d, F32), shp(LANES, jnp.int32), shp(LANES, F32),
                   jax.ShapeDtypeStruct((SUBLANES, LANES), F32)],
        compiler_params=_params(("arbitrary", "arbitrary")),
        name="outproj_router",
    )(x, o_hg, y_ssd, w_out_b, g1, nw, sh, sc, rw_pad, rb_pad)


def _route_kernel(idx_ref, pstart_ref, dest_ref, carry_ref):
    @pl.when(pl.program_id(0) == 0)
    def _():
        carry_ref[...] = jnp.zeros_like(carry_ref)

    idx = idx_ref[...]
    tt = idx.shape[0]
    lane = lax.broadcasted_iota(jnp.int32, idx.shape, 1)
    hits = [lane == idx[:, k:k + 1] for k in range(TOP_K)]
    sel = jnp.zeros(idx.shape, F32)
    for hit in hits:
        sel = jnp.where(hit, 1.0, sel)
    rows = lax.broadcasted_iota(jnp.int32, (tt, tt), 0)
    cols = lax.broadcasted_iota(jnp.int32, (tt, tt), 1)
    before = (rows > cols).astype(BF16)
    rank = jnp.dot(before, sel.astype(BF16), preferred_element_type=F32) + carry_ref[0:1, :]
    carry_ref[0:1, :] += jnp.sum(sel, axis=0, keepdims=True)
    dense = pstart_ref[...] + rank
    out = jnp.zeros(idx.shape, F32)
    for k, hit in enumerate(hits):
        dk = jnp.sum(jnp.where(hit, dense, 0.0), axis=-1, keepdims=True)
        out = jnp.where(lane == k, dk, out)
    dest_ref[...] = out.astype(jnp.int32)


def _route(idx_pad, pstart_row):
    t = idx_pad.shape[0]
    tt = min(ROW_TILE, t)
    return pl.pallas_call(
        _route_kernel,
        grid=(t // tt,),
        in_specs=[pl.BlockSpec((tt, LANES), lambda i: (i, 0)),
                  pl.BlockSpec((1, LANES), lambda i: (0, 0))],
        out_specs=pl.BlockSpec((tt, LANES), lambda i: (i, 0)),
        out_shape=jax.ShapeDtypeStruct((t, LANES), jnp.int32),
        scratch_shapes=[pltpu.VMEM((SUBLANES, LANES), F32)],
        compiler_params=_params(("arbitrary",)),
        name="route_rank",
    )(idx_pad, pstart_row)


def _scatter_kernel(dest_ref, cnt_ref, pstart_ref, h_ref, zero_ref, xin_ref, sem, zsem):
    i = pl.program_id(0)
    n = pl.num_programs(0)
    gt = h_ref.shape[0] // n

    def row_copy(src_row, dst_row):
        return pltpu.make_async_copy(h_ref.at[pl.ds(src_row, 1), :], xin_ref.at[pl.ds(dst_row, 1), :], sem)

    def wait_tile():
        def body(_, carry):
            row_copy(0, 0).wait()
            return carry
        lax.fori_loop(0, gt * TOP_K, body, 0)

    def tok_body(j, carry):
        tok = i * gt + j
        for k in range(TOP_K):
            row_copy(tok, dest_ref[tok * TOP_K + k]).start()
        return carry
    lax.fori_loop(0, gt, tok_body, 0)

    @pl.when(i > 0)
    def _():
        wait_tile()

    def zero_copy(dst_row):
        return pltpu.make_async_copy(zero_ref.at[pl.ds(0, 1), :], xin_ref.at[pl.ds(dst_row, 1), :], zsem)

    @pl.when(i == n - 1)
    def _():
        wait_tile()

        def exp_body(e, carry):
            cnt = cnt_ref[e]
            n_pad = (MOE_ROWS - cnt % MOE_ROWS) % MOE_ROWS
            base = pstart_ref[e] + cnt

            def zb(r, c2):
                zero_copy(base + r).start()
                return c2
            lax.fori_loop(0, n_pad, zb, 0)

            def zw(r, c2):
                zero_copy(0).wait()
                return c2
            lax.fori_loop(0, n_pad, zw, 0)
            return carry
        lax.fori_loop(0, N_EXPERTS, exp_body, 0)


def _scatter_rows(dest_flat, cnt, pstart, h2, n_rows, seqlen):
    t, d = h2.shape
    gt = min(GATHER_TILE, seqlen)
    zero_rows = jnp.zeros((SUBLANES, d), F32)
    return pl.pallas_call(
        _scatter_kernel,
        grid_spec=pltpu.PrefetchScalarGridSpec(
            num_scalar_prefetch=3,
            grid=(t // gt,),
            in_specs=[pl.BlockSpec(memory_space=pl.ANY), pl.BlockSpec(memory_space=pl.ANY)],
            out_specs=pl.BlockSpec(memory_space=pl.ANY),
            scratch_shapes=[pltpu.SemaphoreType.DMA(()), pltpu.SemaphoreType.DMA(())],
        ),
        out_shape=jax.ShapeDtypeStruct((n_rows, d), F32),
        compiler_params=_params(("arbitrary",)),
        name="scatter_rows",
    )(dest_flat, cnt, pstart, h2, zero_rows)


def _expert_kernel(be_ref, nact_ref, x_ref, w1_ref, b1_ref, w2_ref, b2_ref, y_ref, w1b_ref, w2b_ref):
    j = pl.program_id(0)
    active = j < nact_ref[0]
    fresh = jnp.logical_or(j == 0, be_ref[j] != be_ref[jnp.maximum(j - 1, 0)])

    @pl.when(jnp.logical_and(active, fresh))
    def _():
        w1b_ref[...] = w1_ref[0].astype(BF16)
        w2b_ref[...] = w2_ref[0].astype(BF16)

    @pl.when(active)
    def _():
        hb = jnp.dot(x_ref[...].astype(BF16), w1b_ref[...], preferred_element_type=F32) + b1_ref[0]
        glu = jnp.minimum(hb[:, :D_FF], SWIGLU_LIMIT)
        lin = jnp.clip(hb[:, D_FF:], -SWIGLU_LIMIT, SWIGLU_LIMIT)
        act = glu * jax.nn.sigmoid(SWIGLU_ALPHA * glu) * (lin + 1.0)
        y_ref[...] = jnp.dot(act.astype(BF16), w2b_ref[...], preferred_element_type=F32) + b2_ref[0]


def _experts(block_e, n_act, xin, w1, b1, w2, b2):
    n_rows, d = xin.shape
    nb = n_rows // MOE_ROWS
    row_map = lambda j, be, na: (jnp.minimum(j, na[0] - 1), 0)
    exp_map = lambda j, be, na: (be[j], 0, 0)
    return pl.pallas_call(
        _expert_kernel,
        grid_spec=pltpu.PrefetchScalarGridSpec(
            num_scalar_prefetch=2,
            grid=(nb,),
            in_specs=[pl.BlockSpec((MOE_ROWS, d), row_map),
                      pl.BlockSpec((1, d, 2 * D_FF), exp_map),
                      pl.BlockSpec((1, 1, 2 * D_FF), exp_map),
                      pl.BlockSpec((1, D_FF, d), exp_map),
                      pl.BlockSpec((1, 1, d), exp_map)],
            out_specs=pl.BlockSpec((MOE_ROWS, d), row_map),
            scratch_shapes=[pltpu.VMEM((d, 2 * D_FF), BF16), pltpu.VMEM((D_FF, d), BF16)],
        ),
        out_shape=jax.ShapeDtypeStruct((n_rows, d), F32),
        compiler_params=_params(("arbitrary",)),
        name="expert_ffn",
    )(block_e, n_act, xin, w1, b1.reshape(N_EXPERTS, 1, 2 * D_FF), w2, b2.reshape(N_EXPERTS, 1, d))


def _combine_kernel(dest_ref, x1_ref, gate_ref, g2_ref, fw_ref, y_ref, o_ref, buf_ref, sem):
    i = pl.program_id(0)
    n = pl.num_programs(0)
    gt = x1_ref.shape[0]

    def row_copy(slot, k, j, src_row):
        return pltpu.make_async_copy(y_ref.at[pl.ds(src_row, 1), :],
                                     buf_ref.at[slot, k, pl.ds(j, 1), :], sem.at[slot])

    def issue(tile, slot):
        def body(j, carry):
            tok = tile * gt + j
            for k in range(TOP_K):
                row_copy(slot, k, j, dest_ref[tok * TOP_K + k]).start()
            return carry
        lax.fori_loop(0, gt, body, 0)

    @pl.when(i == 0)
    def _():
        issue(0, 0)

    @pl.when(i + 1 < n)
    def _():
        issue(i + 1, (i + 1) % 2)

    slot = i % 2

    def wait_body(_, carry):
        row_copy(slot, 0, 0, 0).wait()
        return carry
    lax.fori_loop(0, gt * TOP_K, wait_body, 0)

    gates = gate_ref[...]
    moe = gates[:, 0:1] * buf_ref[slot, 0]
    for k in range(1, TOP_K):
        moe = moe + gates[:, k:k + 1] * buf_ref[slot, k]
    x2 = x1_ref[...] + g2_ref[0] * moe
    ms = jnp.mean(x2 * x2, axis=-1, keepdims=True)
    o_ref[...] = x2 * lax.rsqrt(ms + EPS) * fw_ref[...]


def _combine(dest_flat, x1, gate_pad, g2, fw, yout, seqlen):
    t, d = x1.shape
    gt = min(GATHER_TILE, seqlen)
    per_seq = seqlen // gt
    return pl.pallas_call(
        _combine_kernel,
        grid_spec=pltpu.PrefetchScalarGridSpec(
            num_scalar_prefetch=1,
            grid=(t // gt,),
            in_specs=[pl.BlockSpec((gt, d), lambda i, dst: (i, 0)),
                      pl.BlockSpec((gt, LANES), lambda i, dst: (i, 0)),
                      pl.BlockSpec((1, 1, d), lambda i, dst: (i // per_seq, 0, 0)),
                      pl.BlockSpec((1, d), lambda i, dst: (0, 0)),
                      pl.BlockSpec(memory_space=pl.ANY)],
            out_specs=pl.BlockSpec((gt, d), lambda i, dst: (i, 0)),
            scratch_shapes=[pltpu.VMEM((2, TOP_K, gt, d), F32), pltpu.SemaphoreType.DMA((2,))],
        ),
        out_shape=jax.ShapeDtypeStruct((t, d), F32),
        compiler_params=_params(("arbitrary",)),
        name="combine_norm",
    )(dest_flat, x1, gate_pad, g2, fw, yout)


def _pad_lanes(v, fill=0.0):
    v = v.reshape(1, -1).astype(F32)
    return jnp.pad(v, ((0, 0), (0, LANES - v.shape[1])), constant_values=fill)


def kernel(x, c, ada_w, ada_b, norm1_w, w_in, hg_lb_logits, hg_norm_w, ssm_a_log, ssm_dt_bias, ssm_d,
           ssm_conv_w, ssm_conv_b, ssm_norm_w, w_out, norm2_w, router_w, router_b, exp_w1, exp_b1,
           exp_w2, exp_b2, final_norm_w):
    bsz, seqlen, d = x.shape
    t = bsz * seqlen
    l = 0

    c_pad = jnp.pad(c, ((0, SUBLANES - bsz), (0, 0)))
    mod = _ada_mod(c_pad, ada_w[l], ada_b[l])[:bsz]
    sh1, sc1, g1, sh2, sc2, g2 = [m.reshape(bsz, 1, d) for m in jnp.split(mod, N_MOD, axis=-1)]

    w_in_pad = jnp.pad(w_in[l], ((0, 0), (0, IN_COLS_PAD - IN_COLS))).astype(BF16)
    qs, kk, lf, vi, gg, zz, xbc, dt = _inproj(
        x, norm1_w[l].reshape(1, d), sh1, sc1, w_in_pad, hg_lb_logits, _pad_lanes(ssm_dt_bias[l]))

    o_hg = _hgrn(qs, kk, lf, vi, gg, hg_norm_w[l].reshape(1, HG_DV))
    y_ssd = _ssd(xbc, zz, dt, ssm_conv_w[l], ssm_conv_b[l].reshape(1, SSM_CONV_CH),
                 _pad_lanes(ssm_a_log[l]), jnp.repeat(ssm_d[l], SSM_HEAD_DIM).reshape(1, SSM_WIDTH),
                 ssm_norm_w[l].reshape(1, SSM_WIDTH))

    rw_pad = jnp.pad(router_w[l], ((0, 0), (0, LANES - N_EXPERTS)))
    rb_pad = _pad_lanes(router_b[l], NEG_BIG)
    x1, h2, idx_pad, gate_pad, counts = _outproj(
        x, o_hg, y_ssd, w_out[l].astype(BF16), g1, norm2_w[l].reshape(1, d), sh2, sc2, rw_pad, rb_pad)

    cnt = counts[0, :N_EXPERTS].astype(jnp.int32)
    blocks_e = (cnt + MOE_ROWS - 1) // MOE_ROWS
    blk_end = jnp.cumsum(blocks_e)
    pstart = (blk_end - blocks_e) * MOE_ROWS
    n_blocks = (t * TOP_K) // MOE_ROWS + N_EXPERTS
    n_rows = n_blocks * MOE_ROWS
    block_e = jnp.minimum(jnp.searchsorted(blk_end, jnp.arange(n_blocks), side='right'),
                          N_EXPERTS - 1).astype(jnp.int32)
    n_act = blk_end[-1:].astype(jnp.int32)

    dest_pad = _route(idx_pad.reshape(t, LANES), _pad_lanes(pstart))
    dest_flat = dest_pad[:, :TOP_K].reshape(t * TOP_K)

    xin = _scatter_rows(dest_flat, cnt, pstart.astype(jnp.int32), h2.reshape(t, d), n_rows, seqlen)
    yout = _experts(block_e, n_act, xin, exp_w1[l], exp_b1[l], exp_w2[l], exp_b2[l])
    out = _combine(dest_flat, x1.reshape(t, d), gate_pad.reshape(t, LANES), g2,
                   final_norm_w.reshape(1, d), yout, seqlen)
    return out.reshape(bsz, seqlen, d)
```

```python
import functools

import jax
import jax.numpy as jnp
from jax import lax
from jax.experimental import pallas as pl
from jax.experimental.pallas import tpu as pltpu

F32 = jnp.float32
BF16 = jnp.bfloat16
HIGHEST = lax.Precision.HIGHEST

EPS = 1e-6
D_MODEL = 1024
HG_HEADS = 4
HG_DK = 128
HG_DV = 128
HG_QF = HG_HEADS * HG_DK
HG_WIDTH = HG_HEADS * HG_DV
HG_CHUNK = 64
SSM_HEADS = 8
SSM_HEAD_DIM = 64
SSM_WIDTH = SSM_HEADS * SSM_HEAD_DIM
SSM_GROUPS = 2
SSM_GROUP_HEADS = SSM_HEADS // SSM_GROUPS
SSM_GROUP_WIDTH = SSM_WIDTH // SSM_GROUPS
SSM_STATE = 128
SSM_CONV = 4
SSM_CONV_CH = SSM_WIDTH + 2 * SSM_GROUPS * SSM_STATE
SSM_CHUNK = 128
IN_SPLITS = (HG_QF, HG_QF, HG_WIDTH, HG_WIDTH, SSM_WIDTH, SSM_CONV_CH, SSM_HEADS)
IN_COLS = sum(IN_SPLITS)
N_EXPERTS = 32
TOP_K = 4
D_FF = 1024
SWIGLU_LIMIT = 7.0
SWIGLU_ALPHA = 1.702
N_MOD = 6

LANES = 128
SUBLANES = 8
VMEM_LIMIT = 56 * 1024 * 1024

ROW_TILE = 512
SEQ_TILE = 512
MOE_ROWS = 256
GATHER_TILE = 128
SCATTER_TILE = 256
NEG_BIG = -1e30


def _silu(v):
    return v * jax.nn.sigmoid(v)


def _softplus(v):
    return jnp.maximum(v, 0.0) + jnp.log1p(jnp.exp(-jnp.abs(v)))


def _params(sem):
    return pltpu.CompilerParams(dimension_semantics=sem, vmem_limit_bytes=VMEM_LIMIT)


def _ada_kernel(c_ref, w_ref, b_ref, o_ref):
    o_ref[...] = jnp.dot(_silu(c_ref[...]), w_ref[...], precision=HIGHEST,
                         preferred_element_type=F32) + b_ref[...]


def _ada_mod(c_pad, ada_w, ada_b):
    n = ada_w.shape[1]
    tn = D_MODEL
    return pl.pallas_call(
        _ada_kernel,
        grid=(n // tn,),
        in_specs=[pl.BlockSpec((SUBLANES, D_MODEL), lambda j: (0, 0)),
                  pl.BlockSpec((D_MODEL, tn), lambda j: (0, j)),
                  pl.BlockSpec((1, tn), lambda j: (0, j))],
        out_specs=pl.BlockSpec((SUBLANES, tn), lambda j: (0, j)),
        out_shape=jax.ShapeDtypeStruct((SUBLANES, n), F32),
        compiler_params=_params(("arbitrary",)),
        name="ada_mod",
    )(c_pad, ada_w, ada_b.reshape(1, n))


_OFF = [0]
for _w in IN_SPLITS:
    _OFF.append(_OFF[-1] + _w)
IN_COLS_PAD = _OFF[6] + LANES


def _inproj_kernel(x_ref, nw_ref, sh_ref, sc_ref, w_ref, lbl_ref, dtb_ref,
                   qs_ref, kk_ref, lf_ref, vi_ref, gg_ref, zz_ref, xbc_ref, dt_ref):
    x = x_ref[0]
    ms = jnp.mean(x * x, axis=-1, keepdims=True)
    h = x * lax.rsqrt(ms + EPS) * nw_ref[...]
    h = h * (1.0 + sc_ref[0]) + sh_ref[0]
    hb = h.astype(BF16)

    def seg(k):
        return jnp.dot(hb, w_ref[:, _OFF[k]:_OFF[k] + (IN_SPLITS[k] if k < 6 else LANES)],
                       preferred_element_type=F32)

    lbl = lbl_ref[...]
    le = jnp.exp(lbl - jnp.max(lbl, axis=0, keepdims=True))
    lb = le[0:1, :] / jnp.sum(le, axis=0, keepdims=True)

    qs_ref[0] = _silu(seg(0)).astype(BF16)
    fg = lb + (1.0 - lb) * jax.nn.sigmoid(seg(1))
    kk_ref[0] = (1.0 - fg).astype(BF16)
    lf_ref[0] = jnp.log(fg)
    vi_ref[0] = seg(2).astype(BF16)
    gg_ref[0] = _silu(seg(3)).astype(BF16)
    zz_ref[0] = _silu(seg(4)).astype(BF16)
    xbc_ref[0] = seg(5).astype(BF16)
    dt_ref[0] = _softplus(seg(6) + dtb_ref[...])


def _inproj(x, nw, sh, sc, w_in_pad, lb_logits, dtb_pad):
    bsz, seqlen, d = x.shape
    tm = min(ROW_TILE, seqlen)
    nt = seqlen // tm
    tok = lambda w: pl.BlockSpec((1, tm, w), lambda b, i: (b, i, 0))
    full = lambda a: pl.BlockSpec(a.shape, lambda b, i: (0,) * a.ndim)
    mod = pl.BlockSpec((1, 1, d), lambda b, i: (b, 0, 0))
    shp = lambda w, dt: jax.ShapeDtypeStruct((bsz, seqlen, w), dt)
    return pl.pallas_call(
        _inproj_kernel,
        grid=(bsz, nt),
        in_specs=[tok(d), full(nw), mod, mod, full(w_in_pad), full(lb_logits), full(dtb_pad)],
        out_specs=[tok(HG_QF), tok(HG_QF), tok(HG_QF), tok(HG_WIDTH), tok(HG_WIDTH),
                   tok(SSM_WIDTH), tok(SSM_CONV_CH), tok(LANES)],
        out_shape=[shp(HG_QF, BF16), shp(HG_QF, BF16), shp(HG_QF, F32), shp(HG_WIDTH, BF16),
                   shp(HG_WIDTH, BF16), shp(SSM_WIDTH, BF16), shp(SSM_CONV_CH, BF16),
                   shp(LANES, F32)],
        compiler_params=_params(("arbitrary", "arbitrary")),
        name="inproj",
    )(x, nw, sh, sc, w_in_pad, lb_logits, dtb_pad)


def _hgrn_kernel(qs_ref, kk_ref, lf_ref, vi_ref, gg_ref, nw_ref, o_ref, st_ref):
    @pl.when(pl.program_id(2) == 0)
    def _():
        st_ref[...] = jnp.zeros_like(st_ref)

    c = HG_CHUNK
    rows = lax.broadcasted_iota(jnp.int32, (c, c), 0)
    cols = lax.broadcasted_iota(jnp.int32, (c, c), 1)
    causal = rows >= cols
    tril = causal.astype(F32)
    nw = nw_ref[...]
    tq = qs_ref.shape[1]
    nn = (((1,), (1,)), ((), ()))
    tn = (((0,), (0,)), ((), ()))
    for ci in range(tq // c):
        sl = pl.ds(ci * c, c)
        q = qs_ref[0, sl, :].astype(F32)
        k = kk_ref[0, sl, :].astype(F32)
        v = vi_ref[0, sl, :]
        b = jnp.dot(tril, lf_ref[0, sl, :], precision=HIGHEST, preferred_element_type=F32)
        b_mid = b[c // 2 - 1:c // 2, :]
        b_last = b[c - 1:c, :]
        qa = (q * jnp.exp(b - b_mid)).astype(BF16)
        ka = (k * jnp.exp(b_mid - b)).astype(BF16)
        att = lax.dot_general(qa, ka, nn, preferred_element_type=F32)
        att = jnp.where(causal, att, 0.0).astype(BF16)
        o = jnp.dot(att, v, preferred_element_type=F32)
        st = st_ref[...]
        qb = (q * jnp.exp(b)).astype(BF16)
        o = o + lax.dot_general(qb, st.astype(BF16), nn, preferred_element_type=F32)
        kd = (k * jnp.exp(b_last - b)).astype(BF16)
        st_ref[...] = st * jnp.exp(b_last) + lax.dot_general(v, kd, tn, preferred_element_type=F32)
        ms = jnp.mean(o * o, axis=-1, keepdims=True)
        y = o * lax.rsqrt(ms + EPS) * nw * gg_ref[0, sl, :].astype(F32)
        o_ref[0, sl, :] = y.astype(BF16)


def _hgrn(qs, kk, lf, vi, gg, nw):
    bsz, seqlen, _ = qs.shape
    tq = min(SEQ_TILE, seqlen)
    blk = pl.BlockSpec((1, tq, HG_DK), lambda b, h, i: (b, i, h))
    return pl.pallas_call(
        _hgrn_kernel,
        grid=(bsz, HG_HEADS, seqlen // tq),
        in_specs=[blk, blk, blk, blk, blk, pl.BlockSpec((1, HG_DV), lambda b, h, i: (0, 0))],
        out_specs=blk,
        out_shape=jax.ShapeDtypeStruct((bsz, seqlen, HG_WIDTH), BF16),
        scratch_shapes=[pltpu.VMEM((HG_DV, HG_DK), F32)],
        compiler_params=_params(("arbitrary", "arbitrary", "arbitrary")),
        name="hgrn2",
    )(qs, kk, lf, vi, gg, nw)


def _ssd_kernel(xbc_ref, zz_ref, dt_ref, cw_ref, cb_ref, alog_ref, dsk_ref, nw_ref,
                y_ref, buf_ref, xc_ref, yc_ref, st_ref):
    tq = xbc_ref.shape[1]
    halo = SUBLANES

    @pl.when(pl.program_id(1) == 0)
    def _():
        buf_ref[0:halo, :] = jnp.zeros((halo, SSM_CONV_CH), F32)
        st_ref[...] = jnp.zeros_like(st_ref)

    buf_ref[halo:halo + tq, :] = xbc_ref[0].astype(F32)
    conv = cb_ref[...] + buf_ref[halo - 3:halo - 3 + tq, :] * cw_ref[0:1, :]
    for w in range(1, SSM_CONV):
        conv = conv + buf_ref[halo - 3 + w:halo - 3 + w + tq, :] * cw_ref[w:w + 1, :]
    buf_ref[0:halo, :] = buf_ref[tq:tq + halo, :]
    xc_ref[...] = _silu(conv)

    c = SSM_CHUNK
    p = SSM_HEAD_DIM
    rows = lax.broadcasted_iota(jnp.int32, (c, c), 0)
    cols = lax.broadcasted_iota(jnp.int32, (c, c), 1)
    causal = rows >= cols
    tril = causal.astype(F32)
    a_neg = -jnp.exp(alog_ref[...])
    nn = (((1,), (1,)), ((), ()))
    tn = (((0,), (0,)), ((), ()))
    for ci in range(tq // c):
        sl = pl.ds(ci * c, c)
        dt = dt_ref[0, sl, :]
        acum = jnp.dot(tril, dt * a_neg, precision=HIGHEST, preferred_element_type=F32)
        acum_t = acum.T
        decay_out = jnp.exp(acum)
        decay_in = jnp.exp(acum[c - 1:c, :] - acum)
        for g in range(SSM_GROUPS):
            bm = xc_ref[sl, SSM_WIDTH + g * SSM_STATE:SSM_WIDTH + (g + 1) * SSM_STATE].astype(BF16)
            cm = xc_ref[sl, SSM_WIDTH + (SSM_GROUPS + g) * SSM_STATE:
                        SSM_WIDTH + (SSM_GROUPS + g + 1) * SSM_STATE].astype(BF16)
            cb = lax.dot_general(cm, bm, nn, preferred_element_type=F32)
            st = st_ref[g]
            y_inter = lax.dot_general(cm, st.astype(BF16), nn, preferred_element_type=F32)
            for hh in range(SSM_GROUP_HEADS):
                h = g * SSM_GROUP_HEADS + hh
                lo = h * p
                xdt = xc_ref[sl, lo:lo + p] * dt[:, h:h + 1]
                seg = jnp.exp(jnp.where(causal, acum[:, h:h + 1] - acum_t[h:h + 1, :], -jnp.inf))
                y = jnp.dot((cb * seg).astype(BF16), xdt.astype(BF16), preferred_element_type=F32)
                y = y + decay_out[:, h:h + 1] * y_inter[:, hh * p:(hh + 1) * p]
                yc_ref[:, lo:lo + p] = y
                d_st = lax.dot_general((xdt * decay_in[:, h:h + 1]).astype(BF16), bm, tn,
                                       preferred_element_type=F32)
                a_last = jnp.exp(acum_t[h:h + 1, c - 1:c])
                st_ref[g, hh * p:(hh + 1) * p, :] = st[hh * p:(hh + 1) * p, :] * a_last + d_st
        y = (yc_ref[...] + dsk_ref[...] * xc_ref[sl, 0:SSM_WIDTH]) * zz_ref[0, sl, :].astype(F32)
        for g in range(SSM_GROUPS):
            gs = slice(g * SSM_GROUP_WIDTH, (g + 1) * SSM_GROUP_WIDTH)
            yg = y[:, gs]
            ms = jnp.mean(yg * yg, axis=-1, keepdims=True)
            y_ref[0, sl, gs] = (yg * lax.rsqrt(ms + EPS) * nw_ref[:, gs]).astype(BF16)


def _ssd(xbc, zz, dt, conv_w, conv_b, alog_pad, dsk_wide, nw):
    bsz, seqlen, _ = xbc.shape
    tq = min(SEQ_TILE, seqlen)
    tok = lambda w: pl.BlockSpec((1, tq, w), lambda b, i: (b, i, 0))
    full = lambda a: pl.BlockSpec(a.shape, lambda b, i: (0,) * a.ndim)
    return pl.pallas_call(
        _ssd_kernel,
        grid=(bsz, seqlen // tq),
        in_specs=[tok(SSM_CONV_CH), tok(SSM_WIDTH), tok(LANES), full(conv_w), full(conv_b),
                  full(alog_pad), full(dsk_wide), full(nw)],
        out_specs=tok(SSM_WIDTH),
        out_shape=jax.ShapeDtypeStruct((bsz, seqlen, SSM_WIDTH), BF16),
        scratch_shapes=[pltpu.VMEM((tq + 2 * SUBLANES, SSM_CONV_CH), F32),
                        pltpu.VMEM((tq, SSM_CONV_CH), F32),
                        pltpu.VMEM((SSM_CHUNK, SSM_WIDTH), F32),
                        pltpu.VMEM((SSM_GROUPS, SSM_GROUP_WIDTH, SSM_STATE), F32)],
        compiler_params=_params(("arbitrary", "arbitrary")),
        name="ssd",
    )(xbc, zz, dt, conv_w, conv_b, alog_pad, dsk_wide, nw)


def _outproj_kernel(x_ref, oh_ref, ys_ref, wo_ref, g1_ref, nw_ref, sh_ref, sc_ref, rw_ref, rb_ref,
                    x1_ref, h2_ref, idx_ref, gate_ref, cnt_ref):
    first = jnp.logical_and(pl.program_id(0) == 0, pl.program_id(1) == 0)

    @pl.when(first)
    def _():
        cnt_ref[...] = jnp.zeros_like(cnt_ref)

    mix = jnp.dot(oh_ref[0], wo_ref[0:HG_WIDTH, :], preferred_element_type=F32)
    mix = mix + jnp.dot(ys_ref[0], wo_ref[HG_WIDTH:, :], preferred_element_type=F32)
    x1 = x_ref[0] + g1_ref[0] * mix
    x1_ref[0] = x1
    ms = jnp.mean(x1 * x1, axis=-1, keepdims=True)
    h2 = x1 * lax.rsqrt(ms + EPS) * nw_ref[...]
    h2 = h2 * (1.0 + sc_ref[0]) + sh_ref[0]
    h2_ref[0] = h2
    logits = jnp.dot(h2, rw_ref[...], precision=HIGHEST, preferred_element_type=F32) + rb_ref[...]
    lane = lax.broadcasted_iota(jnp.int32, logits.shape, 1).astype(F32)
    idx_out = jnp.zeros(logits.shape, F32)
    val_out = jnp.zeros(logits.shape, F32)
    sel = jnp.zeros(logits.shape, F32)
    work = logits
    top = None
    denom = None
    for k in range(TOP_K):
        m = jnp.max(work, axis=-1, keepdims=True)
        am = jnp.min(jnp.where(work == m, lane, float(LANES)), axis=-1, keepdims=True)
        hit = lane == am
        work = jnp.where(hit, -jnp.inf, work)
        sel = jnp.where(hit, 1.0, sel)
        if k == 0:
            top = m
        e = jnp.exp(m - top)
        denom = e if k == 0 else denom + e
        idx_out = jnp.where(lane == float(k), am, idx_out)
        val_out = jnp.where(lane == float(k), e, val_out)
    idx_ref[0] = idx_out.astype(jnp.int32)
    gate_ref[0] = val_out / denom
    cnt_ref[0:1, :] += jnp.sum(sel, axis=0, keepdims=True)


def _outproj(x, o_hg, y_ssd, w_out_b, g1, nw, sh, sc, rw_pad, rb_pad):
    bsz, seqlen, d = x.shape
    tm = min(ROW_TILE, seqlen)
    tok = lambda w: pl.BlockSpec((1, tm, w), lambda b, i: (b, i, 0))
    full = lambda a: pl.BlockSpec(a.shape, lambda b, i: (0,) * a.ndim)
    mod = pl.BlockSpec((1, 1, d), lambda b, i: (b, 0, 0))
    shp = lambda w, dt: jax.ShapeDtypeStruct((bsz, seqlen, w), dt)
    return pl.pallas_call(
        _outproj_kernel,
        grid=(bsz, seqlen // tm),
        in_specs=[tok(d), tok(HG_WIDTH), tok(SSM_WIDTH), full(w_out_b), mod, full(nw), mod, mod,
                  full(rw_pad), full(rb_pad)],
        out_specs=[tok(d), tok(d), tok(LANES), tok(LANES),
                   pl.BlockSpec((SUBLANES, LANES), lambda b, i: (0, 0))],
        out_shape=[shp(d, F32), shp(d, F32), shp(LANES, jnp.int32), shp(LANES, F32),
                   jax.ShapeDtypeStruct((SUBLANES, LANES), F32)],
        compiler_params=_params(("arbitrary", "arbitrary")),
        name="outproj_router",
    )(x, o_hg, y_ssd, w_out_b, g1, nw, sh, sc, rw_pad, rb_pad)


def _route_kernel(idx_ref, pstart_ref, dest_ref, carry_ref):
    @pl.when(pl.program_id(0) == 0)
    def _():
        carry_ref[...] = jnp.zeros_like(carry_ref)

    idx = idx_ref[...]
    tt = idx.shape[0]
    lane = lax.broadcasted_iota(jnp.int32, idx.shape, 1)
    hits = [lane == idx[:, k:k + 1] for k in range(TOP_K)]
    sel = jnp.zeros(idx.shape, F32)
    for hit in hits:
        sel = jnp.where(hit, 1.0, sel)
    rows = lax.broadcasted_iota(jnp.int32, (tt, tt), 0)
    cols = lax.broadcasted_iota(jnp.int32, (tt, tt), 1)
    before = (rows > cols).astype(BF16)
    rank = jnp.dot(before, sel.astype(BF16), preferred_element_type=F32) + carry_ref[0:1, :]
    carry_ref[0:1, :] += jnp.sum(sel, axis=0, keepdims=True)
    dense = pstart_ref[...] + rank
    out = jnp.zeros(idx.shape, F32)
    for k, hit in enumerate(hits):
        dk = jnp.sum(jnp.where(hit, dense, 0.0), axis=-1, keepdims=True)
        out = jnp.where(lane == k, dk, out)
    dest_ref[...] = out.astype(jnp.int32)


def _route(idx_pad, pstart_row):
    t = idx_pad.shape[0]
    tt = min(ROW_TILE, t)
    return pl.pallas_call(
        _route_kernel,
        grid=(t // tt,),
        in_specs=[pl.BlockSpec((tt, LANES), lambda i: (i, 0)),
                  pl.BlockSpec((1, LANES), lambda i: (0, 0))],
        out_specs=pl.BlockSpec((tt, LANES), lambda i: (i, 0)),
        out_shape=jax.ShapeDtypeStruct((t, LANES), jnp.int32),
        scratch_shapes=[pltpu.VMEM((SUBLANES, LANES), F32)],
        compiler_params=_params(("arbitrary",)),
        name="route_rank",
    )(idx_pad, pstart_row)


def _scatter_kernel(dest_ref, cnt_ref, pstart_ref, h_ref, zero_ref, xin_ref, src_ref, sem, zsem):
    i = pl.program_id(0)
    n = pl.num_programs(0)
    gt = h_ref.shape[0]
    slot = i % 2

    def wait_tile(s):
        for _ in range(TOP_K):
            pltpu.make_async_copy(src_ref.at[s], xin_ref.at[pl.ds(0, gt), :], sem.at[s]).wait()

    src_ref[slot] = h_ref[...]

    def tok_body(j, carry):
        tok = i * gt + j
        for k in range(TOP_K):
            pltpu.make_async_copy(src_ref.at[slot, pl.ds(j, 1), :],
                                  xin_ref.at[pl.ds(dest_ref[tok * TOP_K + k], 1), :], sem.at[slot]).start()
        return carry
    lax.fori_loop(0, gt, tok_body, 0, unroll=4)

    @pl.when(i > 0)
    def _():
        wait_tile(1 - slot)

    def zero_copy(dst_row):
        return pltpu.make_async_copy(zero_ref.at[pl.ds(0, 1), :], xin_ref.at[pl.ds(dst_row, 1), :], zsem)

    @pl.when(i == n - 1)
    def _():
        wait_tile(slot)

        def exp_body(e, carry):
            cnt = cnt_ref[e]
            n_pad = (MOE_ROWS - cnt % MOE_ROWS) % MOE_ROWS
            base = pstart_ref[e] + cnt

            def zb(r, c2):
                zero_copy(base + r).start()
                return c2
            lax.fori_loop(0, n_pad, zb, 0)

            def zw(r, c2):
                zero_copy(0).wait()
                return c2
            lax.fori_loop(0, n_pad, zw, 0)
            return carry
        lax.fori_loop(0, N_EXPERTS, exp_body, 0)


def _scatter_rows(dest_flat, cnt, pstart, h2, n_rows, seqlen):
    t, d = h2.shape
    gt = min(SCATTER_TILE, seqlen)
    zero_rows = jnp.zeros((SUBLANES, d), F32)
    return pl.pallas_call(
        _scatter_kernel,
        grid_spec=pltpu.PrefetchScalarGridSpec(
            num_scalar_prefetch=3,
            grid=(t // gt,),
            in_specs=[pl.BlockSpec((gt, d), lambda i, *_: (i, 0)), pl.BlockSpec(memory_space=pl.ANY)],
            out_specs=pl.BlockSpec(memory_space=pl.ANY),
            scratch_shapes=[pltpu.VMEM((2, gt, d), F32), pltpu.SemaphoreType.DMA((2,)),
                            pltpu.SemaphoreType.DMA(())],
        ),
        out_shape=jax.ShapeDtypeStruct((n_rows, d), F32),
        compiler_params=_params(("arbitrary",)),
        name="scatter_rows",
    )(dest_flat, cnt, pstart, h2, zero_rows)


def _expert_kernel(be_ref, nact_ref, x_ref, w1_ref, b1_ref, w2_ref, b2_ref, y_ref, w1b_ref, w2b_ref):
    j = pl.program_id(0)
    active = j < nact_ref[0]
    fresh = jnp.logical_or(j == 0, be_ref[j] != be_ref[jnp.maximum(j - 1, 0)])

    @pl.when(jnp.logical_and(active, fresh))
    def _():
        w1b_ref[...] = w1_ref[0].astype(BF16)
        w2b_ref[...] = w2_ref[0].astype(BF16)

    @pl.when(active)
    def _():
        hb = jnp.dot(x_ref[...].astype(BF16), w1b_ref[...], preferred_element_type=F32) + b1_ref[0]
        glu = jnp.minimum(hb[:, :D_FF], SWIGLU_LIMIT)
        lin = jnp.clip(hb[:, D_FF:], -SWIGLU_LIMIT, SWIGLU_LIMIT)
        act = glu * jax.nn.sigmoid(SWIGLU_ALPHA * glu) * (lin + 1.0)
        y_ref[...] = jnp.dot(act.astype(BF16), w2b_ref[...], preferred_element_type=F32) + b2_ref[0]


def _experts(block_e, n_act, xin, w1, b1, w2, b2):
    n_rows, d = xin.shape
    nb = n_rows // MOE_ROWS
    row_map = lambda j, be, na: (jnp.minimum(j, na[0] - 1), 0)
    exp_map = lambda j, be, na: (be[j], 0, 0)
    return pl.pallas_call(
        _expert_kernel,
        grid_spec=pltpu.PrefetchScalarGridSpec(
            num_scalar_prefetch=2,
            grid=(nb,),
            in_specs=[pl.BlockSpec((MOE_ROWS, d), row_map),
                      pl.BlockSpec((1, d, 2 * D_FF), exp_map),
                      pl.BlockSpec((1, 1, 2 * D_FF), exp_map),
                      pl.BlockSpec((1, D_FF, d), exp_map),
                      pl.BlockSpec((1, 1, d), exp_map)],
            out_specs=pl.BlockSpec((MOE_ROWS, d), row_map),
            scratch_shapes=[pltpu.VMEM((d, 2 * D_FF), BF16), pltpu.VMEM((D_FF, d), BF16)],
        ),
        out_shape=jax.ShapeDtypeStruct((n_rows, d), F32),
        compiler_params=_params(("arbitrary",)),
        name="expert_ffn",
    )(block_e, n_act, xin, w1, b1.reshape(N_EXPERTS, 1, 2 * D_FF), w2, b2.reshape(N_EXPERTS, 1, d))


def _combine_kernel(dest_ref, x1_ref, gate_ref, g2_ref, fw_ref, y_ref, o_ref, buf_ref, sem):
    i = pl.program_id(0)
    n = pl.num_programs(0)
    gt = x1_ref.shape[0]

    def row_copy(slot, k, j, src_row):
        return pltpu.make_async_copy(y_ref.at[pl.ds(src_row, 1), :],
                                     buf_ref.at[slot, k, pl.ds(j, 1), :], sem.at[slot])

    def issue(tile, slot):
        def body(j, carry):
            tok = tile * gt + j
            for k in range(TOP_K):
                row_copy(slot, k, j, dest_ref[tok * TOP_K + k]).start()
            return carry
        lax.fori_loop(0, gt, body, 0, unroll=4)

    @pl.when(i == 0)
    def _():
        issue(0, 0)

    @pl.when(i + 1 < n)
    def _():
        issue(i + 1, (i + 1) % 2)

    slot = i % 2

    for k in range(TOP_K):
        pltpu.make_async_copy(y_ref.at[pl.ds(0, gt), :], buf_ref.at[slot, k], sem.at[slot]).wait()

    gates = gate_ref[...]
    moe = gates[:, 0:1] * buf_ref[slot, 0]
    for k in range(1, TOP_K):
        moe = moe + gates[:, k:k + 1] * buf_ref[slot, k]
    x2 = x1_ref[...] + g2_ref[0] * moe
    ms = jnp.mean(x2 * x2, axis=-1, keepdims=True)
    o_ref[...] = x2 * lax.rsqrt(ms + EPS) * fw_ref[...]


def _combine(dest_flat, x1, gate_pad, g2, fw, yout, seqlen):
    t, d = x1.shape
    gt = min(GATHER_TILE, seqlen)
    per_seq = seqlen // gt
    return pl.pallas_call(
        _combine_kernel,
        grid_spec=pltpu.PrefetchScalarGridSpec(
            num_scalar_prefetch=1,
            grid=(t // gt,),
            in_specs=[pl.BlockSpec((gt, d), lambda i, dst: (i, 0)),
                      pl.BlockSpec((gt, LANES), lambda i, dst: (i, 0)),
                      pl.BlockSpec((1, 1, d), lambda i, dst: (i // per_seq, 0, 0)),
                      pl.BlockSpec((1, d), lambda i, dst: (0, 0)),
                      pl.BlockSpec(memory_space=pl.ANY)],
            out_specs=pl.BlockSpec((gt, d), lambda i, dst: (i, 0)),
            scratch_shapes=[pltpu.VMEM((2, TOP_K, gt, d), F32), pltpu.SemaphoreType.DMA((2,))],
        ),
        out_shape=jax.ShapeDtypeStruct((t, d), F32),
        compiler_params=_params(("arbitrary",)),
        name="combine_norm",
    )(dest_flat, x1, gate_pad, g2, fw, yout)


def _pad_lanes(v, fill=0.0):
    v = v.reshape(1, -1).astype(F32)
    return jnp.pad(v, ((0, 0), (0, LANES - v.shape[1])), constant_values=fill)


def kernel(x, c, ada_w, ada_b, norm1_w, w_in, hg_lb_logits, hg_norm_w, ssm_a_log, ssm_dt_bias, ssm_d,
           ssm_conv_w, ssm_conv_b, ssm_norm_w, w_out, norm2_w, router_w, router_b, exp_w1, exp_b1,
           exp_w2, exp_b2, final_norm_w):
    bsz, seqlen, d = x.shape
    t = bsz * seqlen
    l = 0

    c_pad = jnp.pad(c, ((0, SUBLANES - bsz), (0, 0)))
    mod = _ada_mod(c_pad, ada_w[l], ada_b[l])[:bsz]
    sh1, sc1, g1, sh2, sc2, g2 = [m.reshape(bsz, 1, d) for m in jnp.split(mod, N_MOD, axis=-1)]

    w_in_pad = jnp.pad(w_in[l], ((0, 0), (0, IN_COLS_PAD - IN_COLS))).astype(BF16)
    qs, kk, lf, vi, gg, zz, xbc, dt = _inproj(
        x, norm1_w[l].reshape(1, d), sh1, sc1, w_in_pad, hg_lb_logits, _pad_lanes(ssm_dt_bias[l]))

    o_hg = _hgrn(qs, kk, lf, vi, gg, hg_norm_w[l].reshape(1, HG_DV))
    y_ssd = _ssd(xbc, zz, dt, ssm_conv_w[l], ssm_conv_b[l].reshape(1, SSM_CONV_CH),
                 _pad_lanes(ssm_a_log[l]), jnp.repeat(ssm_d[l], SSM_HEAD_DIM).reshape(1, SSM_WIDTH),
                 ssm_norm_w[l].reshape(1, SSM_WIDTH))

    rw_pad = jnp.pad(router_w[l], ((0, 0), (0, LANES - N_EXPERTS)))
    rb_pad = _pad_lanes(router_b[l], NEG_BIG)
    x1, h2, idx_pad, gate_pad, counts = _outproj(
        x, o_hg, y_ssd, w_out[l].astype(BF16), g1, norm2_w[l].reshape(1, d), sh2, sc2, rw_pad, rb_pad)

    cnt = counts[0, :N_EXPERTS].astype(jnp.int32)
    blocks_e = (cnt + MOE_ROWS - 1) // MOE_ROWS
    blk_end = jnp.cumsum(blocks_e)
    pstart = (blk_end - blocks_e) * MOE_ROWS
    n_blocks = (t * TOP_K) // MOE_ROWS + N_EXPERTS
    n_rows = n_blocks * MOE_ROWS
    block_e = jnp.minimum(jnp.sum(blk_end[None, :] <= jnp.arange(n_blocks)[:, None], axis=1),
                          N_EXPERTS - 1).astype(jnp.int32)
    n_act = blk_end[-1:].astype(jnp.int32)

    dest_pad = _route(idx_pad.reshape(t, LANES), _pad_lanes(pstart))
    dest_flat = dest_pad[:, :TOP_K].reshape(t * TOP_K)

    xin = _scatter_rows(dest_flat, cnt, pstart.astype(jnp.int32), h2.reshape(t, d), n_rows, seqlen)
    yout = _experts(block_e, n_act, xin, exp_w1[l], exp_b1[l], exp_w2[l], exp_b2[l])
    out = _combine(dest_flat, x1.reshape(t, d), gate_pad.reshape(t, LANES), g2,
                   final_norm_w.reshape(1, d), yout, seqlen)
    return out.reshape(bsz, seqlen, d)
```

```python
import functools

import jax
import jax.numpy as jnp
from jax import lax
from jax.experimental import pallas as pl
from jax.experimental.pallas import tpu as pltpu

F32 = jnp.float32
BF16 = jnp.bfloat16
HIGHEST = lax.Precision.HIGHEST

EPS = 1e-6
D_MODEL = 1024
HG_HEADS = 4
HG_DK = 128
HG_DV = 128
HG_QF = HG_HEADS * HG_DK
HG_WIDTH = HG_HEADS * HG_DV
HG_CHUNK = 64
SSM_HEADS = 8
SSM_HEAD_DIM = 64
SSM_WIDTH = SSM_HEADS * SSM_HEAD_DIM
SSM_GROUPS = 2
SSM_GROUP_HEADS = SSM_HEADS // SSM_GROUPS
SSM_GROUP_WIDTH = SSM_WIDTH // SSM_GROUPS
SSM_STATE = 128
SSM_CONV = 4
SSM_CONV_CH = SSM_WIDTH + 2 * SSM_GROUPS * SSM_STATE
SSM_CHUNK = 128
IN_SPLITS = (HG_QF, HG_QF, HG_WIDTH, HG_WIDTH, SSM_WIDTH, SSM_CONV_CH, SSM_HEADS)
IN_COLS = sum(IN_SPLITS)
N_EXPERTS = 32
TOP_K = 4
D_FF = 1024
SWIGLU_LIMIT = 7.0
SWIGLU_ALPHA = 1.702
N_MOD = 6

LANES = 128
SUBLANES = 8
ROW_SLABS = D_MODEL // LANES
VMEM_LIMIT = 56 * 1024 * 1024

ROW_TILE = 512
SEQ_TILE = 512
MOE_ROWS = 256
GATHER_TILE = 128
SCATTER_TILE = 256
NEG_BIG = -1e30


def _silu(v):
    return v * jax.nn.sigmoid(v)


def _softplus(v):
    return jnp.maximum(v, 0.0) + jnp.log1p(jnp.exp(-jnp.abs(v)))


def _params(sem):
    return pltpu.CompilerParams(dimension_semantics=sem, vmem_limit_bytes=VMEM_LIMIT)


def _ada_kernel(c_ref, w_ref, b_ref, o_ref):
    o_ref[...] = jnp.dot(_silu(c_ref[...]), w_ref[...], precision=HIGHEST,
                         preferred_element_type=F32) + b_ref[...]


def _ada_mod(c_pad, ada_w, ada_b):
    n = ada_w.shape[1]
    tn = D_MODEL
    return pl.pallas_call(
        _ada_kernel,
        grid=(n // tn,),
        in_specs=[pl.BlockSpec((SUBLANES, D_MODEL), lambda j: (0, 0)),
                  pl.BlockSpec((D_MODEL, tn), lambda j: (0, j)),
                  pl.BlockSpec((1, tn), lambda j: (0, j))],
        out_specs=pl.BlockSpec((SUBLANES, tn), lambda j: (0, j)),
        out_shape=jax.ShapeDtypeStruct((SUBLANES, n), F32),
        compiler_params=_params(("arbitrary",)),
        name="ada_mod",
    )(c_pad, ada_w, ada_b.reshape(1, n))


_OFF = [0]
for _w in IN_SPLITS:
    _OFF.append(_OFF[-1] + _w)
IN_COLS_PAD = _OFF[6] + LANES


def _inproj_kernel(x_ref, nw_ref, sh_ref, sc_ref, w_ref, lbl_ref, dtb_ref,
                   qs_ref, kk_ref, lf_ref, vi_ref, gg_ref, zz_ref, xbc_ref, dt_ref):
    x = x_ref[0]
    ms = jnp.mean(x * x, axis=-1, keepdims=True)
    h = x * lax.rsqrt(ms + EPS) * nw_ref[...]
    h = h * (1.0 + sc_ref[0]) + sh_ref[0]
    hb = h.astype(BF16)

    def seg(k):
        return jnp.dot(hb, w_ref[:, _OFF[k]:_OFF[k] + (IN_SPLITS[k] if k < 6 else LANES)],
                       preferred_element_type=F32)

    lbl = lbl_ref[...]
    le = jnp.exp(lbl - jnp.max(lbl, axis=0, keepdims=True))
    lb = le[0:1, :] / jnp.sum(le, axis=0, keepdims=True)

    qs_ref[0] = _silu(seg(0)).astype(BF16)
    fg = lb + (1.0 - lb) * jax.nn.sigmoid(seg(1))
    kk_ref[0] = (1.0 - fg).astype(BF16)
    lf_ref[0] = jnp.log(fg)
    vi_ref[0] = seg(2).astype(BF16)
    gg_ref[0] = _silu(seg(3)).astype(BF16)
    zz_ref[0] = _silu(seg(4)).astype(BF16)
    xbc_ref[0] = seg(5).astype(BF16)
    dt_ref[0] = _softplus(seg(6) + dtb_ref[...])


def _inproj(x, nw, sh, sc, w_in_pad, lb_logits, dtb_pad):
    bsz, seqlen, d = x.shape
    tm = min(ROW_TILE, seqlen)
    nt = seqlen // tm
    tok = lambda w: pl.BlockSpec((1, tm, w), lambda b, i: (b, i, 0))
    full = lambda a: pl.BlockSpec(a.shape, lambda b, i: (0,) * a.ndim)
    mod = pl.BlockSpec((1, 1, d), lambda b, i: (b, 0, 0))
    shp = lambda w, dt: jax.ShapeDtypeStruct((bsz, seqlen, w), dt)
    return pl.pallas_call(
        _inproj_kernel,
        grid=(bsz, nt),
        in_specs=[tok(d), full(nw), mod, mod, full(w_in_pad), full(lb_logits), full(dtb_pad)],
        out_specs=[tok(HG_QF), tok(HG_QF), tok(HG_QF), tok(HG_WIDTH), tok(HG_WIDTH),
                   tok(SSM_WIDTH), tok(SSM_CONV_CH), tok(LANES)],
        out_shape=[shp(HG_QF, BF16), shp(HG_QF, BF16), shp(HG_QF, F32), shp(HG_WIDTH, BF16),
                   shp(HG_WIDTH, BF16), shp(SSM_WIDTH, BF16), shp(SSM_CONV_CH, BF16),
                   shp(LANES, F32)],
        compiler_params=_params(("arbitrary", "arbitrary")),
        name="inproj",
    )(x, nw, sh, sc, w_in_pad, lb_logits, dtb_pad)


def _hgrn_kernel(qs_ref, kk_ref, lf_ref, vi_ref, gg_ref, nw_ref, o_ref, st_ref):
    @pl.when(pl.program_id(2) == 0)
    def _():
        st_ref[...] = jnp.zeros_like(st_ref)

    c = HG_CHUNK
    rows = lax.broadcasted_iota(jnp.int32, (c, c), 0)
    cols = lax.broadcasted_iota(jnp.int32, (c, c), 1)
    causal = rows >= cols
    tril = causal.astype(F32)
    nw = nw_ref[...]
    tq = qs_ref.shape[1]
    nn = (((1,), (1,)), ((), ()))
    tn = (((0,), (0,)), ((), ()))
    for ci in range(tq // c):
        sl = pl.ds(ci * c, c)
        q = qs_ref[0, sl, :].astype(F32)
        k = kk_ref[0, sl, :].astype(F32)
        v = vi_ref[0, sl, :]
        b = jnp.dot(tril, lf_ref[0, sl, :], precision=HIGHEST, preferred_element_type=F32)
        b_mid = b[c // 2 - 1:c // 2, :]
        b_last = b[c - 1:c, :]
        qa = (q * jnp.exp(b - b_mid)).astype(BF16)
        ka = (k * jnp.exp(b_mid - b)).astype(BF16)
        att = lax.dot_general(qa, ka, nn, preferred_element_type=F32)
        att = jnp.where(causal, att, 0.0).astype(BF16)
        o = jnp.dot(att, v, preferred_element_type=F32)
        st = st_ref[...]
        qb = (q * jnp.exp(b)).astype(BF16)
        o = o + lax.dot_general(qb, st.astype(BF16), nn, preferred_element_type=F32)
        kd = (k * jnp.exp(b_last - b)).astype(BF16)
        st_ref[...] = st * jnp.exp(b_last) + lax.dot_general(v, kd, tn, preferred_element_type=F32)
        ms = jnp.mean(o * o, axis=-1, keepdims=True)
        y = o * lax.rsqrt(ms + EPS) * nw * gg_ref[0, sl, :].astype(F32)
        o_ref[0, sl, :] = y.astype(BF16)


def _hgrn(qs, kk, lf, vi, gg, nw):
    bsz, seqlen, _ = qs.shape
    tq = min(SEQ_TILE, seqlen)
    blk = pl.BlockSpec((1, tq, HG_DK), lambda b, h, i: (b, i, h))
    return pl.pallas_call(
        _hgrn_kernel,
        grid=(bsz, HG_HEADS, seqlen // tq),
        in_specs=[blk, blk, blk, blk, blk, pl.BlockSpec((1, HG_DV), lambda b, h, i: (0, 0))],
        out_specs=blk,
        out_shape=jax.ShapeDtypeStruct((bsz, seqlen, HG_WIDTH), BF16),
        scratch_shapes=[pltpu.VMEM((HG_DV, HG_DK), F32)],
        compiler_params=_params(("arbitrary", "arbitrary", "arbitrary")),
        name="hgrn2",
    )(qs, kk, lf, vi, gg, nw)


def _ssd_kernel(xbc_ref, zz_ref, dt_ref, cw_ref, cb_ref, alog_ref, dsk_ref, nw_ref,
                y_ref, buf_ref, xc_ref, yc_ref, st_ref):
    tq = xbc_ref.shape[1]
    halo = SUBLANES

    @pl.when(pl.program_id(1) == 0)
    def _():
        buf_ref[0:halo, :] = jnp.zeros((halo, SSM_CONV_CH), F32)
        st_ref[...] = jnp.zeros_like(st_ref)

    buf_ref[halo:halo + tq, :] = xbc_ref[0].astype(F32)
    conv = cb_ref[...] + buf_ref[halo - 3:halo - 3 + tq, :] * cw_ref[0:1, :]
    for w in range(1, SSM_CONV):
        conv = conv + buf_ref[halo - 3 + w:halo - 3 + w + tq, :] * cw_ref[w:w + 1, :]
    buf_ref[0:halo, :] = buf_ref[tq:tq + halo, :]
    xc_ref[...] = _silu(conv)

    c = SSM_CHUNK
    p = SSM_HEAD_DIM
    rows = lax.broadcasted_iota(jnp.int32, (c, c), 0)
    cols = lax.broadcasted_iota(jnp.int32, (c, c), 1)
    causal = rows >= cols
    tril = causal.astype(F32)
    a_neg = -jnp.exp(alog_ref[...])
    nn = (((1,), (1,)), ((), ()))
    tn = (((0,), (0,)), ((), ()))
    for ci in range(tq // c):
        sl = pl.ds(ci * c, c)
        dt = dt_ref[0, sl, :]
        acum = jnp.dot(tril, dt * a_neg, precision=HIGHEST, preferred_element_type=F32)
        acum_t = acum.T
        decay_out = jnp.exp(acum)
        decay_in = jnp.exp(acum[c - 1:c, :] - acum)
        for g in range(SSM_GROUPS):
            bm = xc_ref[sl, SSM_WIDTH + g * SSM_STATE:SSM_WIDTH + (g + 1) * SSM_STATE].astype(BF16)
            cm = xc_ref[sl, SSM_WIDTH + (SSM_GROUPS + g) * SSM_STATE:
                        SSM_WIDTH + (SSM_GROUPS + g + 1) * SSM_STATE].astype(BF16)
            cb = lax.dot_general(cm, bm, nn, preferred_element_type=F32)
            st = st_ref[g]
            y_inter = lax.dot_general(cm, st.astype(BF16), nn, preferred_element_type=F32)
            for hh in range(SSM_GROUP_HEADS):
                h = g * SSM_GROUP_HEADS + hh
                lo = h * p
                xdt = xc_ref[sl, lo:lo + p] * dt[:, h:h + 1]
                seg = jnp.exp(jnp.where(causal, acum[:, h:h + 1] - acum_t[h:h + 1, :], -jnp.inf))
                y = jnp.dot((cb * seg).astype(BF16), xdt.astype(BF16), preferred_element_type=F32)
                y = y + decay_out[:, h:h + 1] * y_inter[:, hh * p:(hh + 1) * p]
                yc_ref[:, lo:lo + p] = y
                d_st = lax.dot_general((xdt * decay_in[:, h:h + 1]).astype(BF16), bm, tn,
                                       preferred_element_type=F32)
                a_last = jnp.exp(acum_t[h:h + 1, c - 1:c])
                st_ref[g, hh * p:(hh + 1) * p, :] = st[hh * p:(hh + 1) * p, :] * a_last + d_st
        y = (yc_ref[...] + dsk_ref[...] * xc_ref[sl, 0:SSM_WIDTH]) * zz_ref[0, sl, :].astype(F32)
        for g in range(SSM_GROUPS):
            gs = slice(g * SSM_GROUP_WIDTH, (g + 1) * SSM_GROUP_WIDTH)
            yg = y[:, gs]
            ms = jnp.mean(yg * yg, axis=-1, keepdims=True)
            y_ref[0, sl, gs] = (yg * lax.rsqrt(ms + EPS) * nw_ref[:, gs]).astype(BF16)


def _ssd(xbc, zz, dt, conv_w, conv_b, alog_pad, dsk_wide, nw):
    bsz, seqlen, _ = xbc.shape
    tq = min(SEQ_TILE, seqlen)
    tok = lambda w: pl.BlockSpec((1, tq, w), lambda b, i: (b, i, 0))
    full = lambda a: pl.BlockSpec(a.shape, lambda b, i: (0,) * a.ndim)
    return pl.pallas_call(
        _ssd_kernel,
        grid=(bsz, seqlen // tq),
        in_specs=[tok(SSM_CONV_CH), tok(SSM_WIDTH), tok(LANES), full(conv_w), full(conv_b),
                  full(alog_pad), full(dsk_wide), full(nw)],
        out_specs=tok(SSM_WIDTH),
        out_shape=jax.ShapeDtypeStruct((bsz, seqlen, SSM_WIDTH), BF16),
        scratch_shapes=[pltpu.VMEM((tq + 2 * SUBLANES, SSM_CONV_CH), F32),
                        pltpu.VMEM((tq, SSM_CONV_CH), F32),
                        pltpu.VMEM((SSM_CHUNK, SSM_WIDTH), F32),
                        pltpu.VMEM((SSM_GROUPS, SSM_GROUP_WIDTH, SSM_STATE), F32)],
        compiler_params=_params(("arbitrary", "arbitrary")),
        name="ssd",
    )(xbc, zz, dt, conv_w, conv_b, alog_pad, dsk_wide, nw)


def _outproj_kernel(x_ref, oh_ref, ys_ref, wo_ref, g1_ref, nw_ref, sh_ref, sc_ref, rw_ref, rb_ref,
                    x1_ref, h2_ref, idx_ref, gate_ref, cnt_ref):
    first = jnp.logical_and(pl.program_id(0) == 0, pl.program_id(1) == 0)

    @pl.when(first)
    def _():
        cnt_ref[...] = jnp.zeros_like(cnt_ref)

    mix = jnp.dot(oh_ref[0], wo_ref[0:HG_WIDTH, :], preferred_element_type=F32)
    mix = mix + jnp.dot(ys_ref[0], wo_ref[HG_WIDTH:, :], preferred_element_type=F32)
    x1 = x_ref[0] + g1_ref[0] * mix
    x1_ref[0] = x1
    ms = jnp.mean(x1 * x1, axis=-1, keepdims=True)
    h2 = x1 * lax.rsqrt(ms + EPS) * nw_ref[...]
    h2 = h2 * (1.0 + sc_ref[0]) + sh_ref[0]
    h2_ref[0] = h2
    logits = jnp.dot(h2, rw_ref[...], precision=HIGHEST, preferred_element_type=F32) + rb_ref[...]
    lane = lax.broadcasted_iota(jnp.int32, logits.shape, 1).astype(F32)
    idx_out = jnp.zeros(logits.shape, F32)
    val_out = jnp.zeros(logits.shape, F32)
    sel = jnp.zeros(logits.shape, F32)
    work = logits
    top = None
    denom = None
    for k in range(TOP_K):
        m = jnp.max(work, axis=-1, keepdims=True)
        am = jnp.min(jnp.where(work == m, lane, float(LANES)), axis=-1, keepdims=True)
        hit = lane == am
        work = jnp.where(hit, -jnp.inf, work)
        sel = jnp.where(hit, 1.0, sel)
        if k == 0:
            top = m
        e = jnp.exp(m - top)
        denom = e if k == 0 else denom + e
        idx_out = jnp.where(lane == float(k), am, idx_out)
        val_out = jnp.where(lane == float(k), e, val_out)
    idx_ref[0] = idx_out.astype(jnp.int32)
    gate_ref[0] = val_out / denom
    cnt_ref[0:1, :] += jnp.sum(sel, axis=0, keepdims=True)


def _outproj(x, o_hg, y_ssd, w_out_b, g1, nw, sh, sc, rw_pad, rb_pad):
    bsz, seqlen, d = x.shape
    tm = min(ROW_TILE, seqlen)
    tok = lambda w: pl.BlockSpec((1, tm, w), lambda b, i: (b, i, 0))
    full = lambda a: pl.BlockSpec(a.shape, lambda b, i: (0,) * a.ndim)
    mod = pl.BlockSpec((1, 1, d), lambda b, i: (b, 0, 0))
    shp = lambda w, dt: jax.ShapeDtypeStruct((bsz, seqlen, w), dt)
    return pl.pallas_call(
        _outproj_kernel,
        grid=(bsz, seqlen // tm),
        in_specs=[tok(d), tok(HG_WIDTH), tok(SSM_WIDTH), full(w_out_b), mod, full(nw), mod, mod,
                  full(rw_pad), full(rb_pad)],
        out_specs=[tok(d), tok(d), tok(LANES), tok(LANES),
                   pl.BlockSpec((SUBLANES, LANES), lambda b, i: (0, 0))],
        out_shape=[shp(d, F32), shp(d, F32), shp(LANES, jnp.int32), shp(LANES, F32),
                   jax.ShapeDtypeStruct((SUBLANES, LANES), F32)],
        compiler_params=_params(("arbitrary", "arbitrary")),
        name="outproj_router",
    )(x, o_hg, y_ssd, w_out_b, g1, nw, sh, sc, rw_pad, rb_pad)


def _route_kernel(idx_ref, pstart_ref, dest_ref, carry_ref):
    @pl.when(pl.program_id(0) == 0)
    def _():
        carry_ref[...] = jnp.zeros_like(carry_ref)

    idx = idx_ref[...]
    tt = idx.shape[0]
    lane = lax.broadcasted_iota(jnp.int32, idx.shape, 1)
    hits = [lane == idx[:, k:k + 1] for k in range(TOP_K)]
    sel = jnp.zeros(idx.shape, F32)
    for hit in hits:
        sel = jnp.where(hit, 1.0, sel)
    rows = lax.broadcasted_iota(jnp.int32, (tt, tt), 0)
    cols = lax.broadcasted_iota(jnp.int32, (tt, tt), 1)
    before = (rows > cols).astype(BF16)
    rank = jnp.dot(before, sel.astype(BF16), preferred_element_type=F32) + carry_ref[0:1, :]
    carry_ref[0:1, :] += jnp.sum(sel, axis=0, keepdims=True)
    dense = pstart_ref[...] + rank
    out = jnp.zeros(idx.shape, F32)
    for k, hit in enumerate(hits):
        dk = jnp.sum(jnp.where(hit, dense, 0.0), axis=-1, keepdims=True)
        out = jnp.where(lane == k, dk, out)
    dest_ref[...] = out.astype(jnp.int32)


def _route(idx_pad, pstart_row):
    t = idx_pad.shape[0]
    tt = min(ROW_TILE, t)
    return pl.pallas_call(
        _route_kernel,
        grid=(t // tt,),
        in_specs=[pl.BlockSpec((tt, LANES), lambda i: (i, 0)),
                  pl.BlockSpec((1, LANES), lambda i: (0, 0))],
        out_specs=pl.BlockSpec((tt, LANES), lambda i: (i, 0)),
        out_shape=jax.ShapeDtypeStruct((t, LANES), jnp.int32),
        scratch_shapes=[pltpu.VMEM((SUBLANES, LANES), F32)],
        compiler_params=_params(("arbitrary",)),
        name="route_rank",
    )(idx_pad, pstart_row)


def _scatter_kernel(dest_ref, cnt_ref, pstart_ref, h_ref, zero_ref, xin_ref, src_ref, sem, zsem):
    i = pl.program_id(0)
    n = pl.num_programs(0)
    gt = h_ref.shape[0]
    slot = i % 2

    def wait_tile(s):
        for _ in range(TOP_K):
            pltpu.make_async_copy(src_ref.at[s], xin_ref.at[pl.ds(0, gt)], sem.at[s]).wait()

    h = h_ref[...]
    for s in range(ROW_SLABS):
        src_ref[slot, :, s, :] = h[:, s * LANES:(s + 1) * LANES]

    def tok_body(j, carry):
        tok = i * gt + j
        for k in range(TOP_K):
            pltpu.make_async_copy(src_ref.at[slot, j], xin_ref.at[dest_ref[tok * TOP_K + k]],
                                  sem.at[slot]).start()
        return carry
    lax.fori_loop(0, gt, tok_body, 0, unroll=4)

    @pl.when(i > 0)
    def _():
        wait_tile(1 - slot)

    def zero_copy(dst_row):
        return pltpu.make_async_copy(zero_ref.at[0], xin_ref.at[dst_row], zsem)

    @pl.when(i == n - 1)
    def _():
        wait_tile(slot)

        def exp_body(e, carry):
            cnt = cnt_ref[e]
            n_pad = (MOE_ROWS - cnt % MOE_ROWS) % MOE_ROWS
            base = pstart_ref[e] + cnt

            def zb(r, c2):
                zero_copy(base + r).start()
                return c2
            lax.fori_loop(0, n_pad, zb, 0)

            def zw(r, c2):
                zero_copy(0).wait()
                return c2
            lax.fori_loop(0, n_pad, zw, 0)
            return carry
        lax.fori_loop(0, N_EXPERTS, exp_body, 0)


def _scatter_rows(dest_flat, cnt, pstart, h2, n_rows, seqlen):
    t, d = h2.shape
    gt = min(SCATTER_TILE, seqlen)
    zero_rows = jnp.zeros((1, ROW_SLABS, LANES), F32)
    return pl.pallas_call(
        _scatter_kernel,
        grid_spec=pltpu.PrefetchScalarGridSpec(
            num_scalar_prefetch=3,
            grid=(t // gt,),
            in_specs=[pl.BlockSpec((gt, d), lambda i, *_: (i, 0)), pl.BlockSpec(memory_space=pl.ANY)],
            out_specs=pl.BlockSpec(memory_space=pl.ANY),
            scratch_shapes=[pltpu.VMEM((2, gt, ROW_SLABS, LANES), F32), pltpu.SemaphoreType.DMA((2,)),
                            pltpu.SemaphoreType.DMA(())],
        ),
        out_shape=jax.ShapeDtypeStruct((n_rows, ROW_SLABS, LANES), F32),
        compiler_params=_params(("arbitrary",)),
        name="scatter_rows",
    )(dest_flat, cnt, pstart, h2, zero_rows)


def _expert_kernel(be_ref, nact_ref, x_ref, w1_ref, b1_ref, w2_ref, b2_ref, y_ref, w1b_ref, w2b_ref):
    j = pl.program_id(0)
    active = j < nact_ref[0]
    fresh = jnp.logical_or(j == 0, be_ref[j] != be_ref[jnp.maximum(j - 1, 0)])

    @pl.when(jnp.logical_and(active, fresh))
    def _():
        w1b_ref[...] = w1_ref[0].astype(BF16)
        w2b_ref[...] = w2_ref[0].astype(BF16)

    @pl.when(active)
    def _():
        x = jnp.concatenate([x_ref[:, s, :] for s in range(ROW_SLABS)], axis=-1).astype(BF16)
        hb = jnp.dot(x, w1b_ref[...], preferred_element_type=F32) + b1_ref[0]
        glu = jnp.minimum(hb[:, :D_FF], SWIGLU_LIMIT)
        lin = jnp.clip(hb[:, D_FF:], -SWIGLU_LIMIT, SWIGLU_LIMIT)
        act = glu * jax.nn.sigmoid(SWIGLU_ALPHA * glu) * (lin + 1.0)
        y = jnp.dot(act.astype(BF16), w2b_ref[...], preferred_element_type=F32) + b2_ref[0]
        for s in range(ROW_SLABS):
            y_ref[:, s, :] = y[:, s * LANES:(s + 1) * LANES]


def _experts(block_e, n_act, xin, w1, b1, w2, b2):
    n_rows = xin.shape[0]
    d = D_MODEL
    nb = n_rows // MOE_ROWS
    row_map = lambda j, be, na: (jnp.minimum(j, na[0] - 1), 0, 0)
    exp_map = lambda j, be, na: (be[j], 0, 0)
    return pl.pallas_call(
        _expert_kernel,
        grid_spec=pltpu.PrefetchScalarGridSpec(
            num_scalar_prefetch=2,
            grid=(nb,),
            in_specs=[pl.BlockSpec((MOE_ROWS, ROW_SLABS, LANES), row_map),
                      pl.BlockSpec((1, d, 2 * D_FF), exp_map),
                      pl.BlockSpec((1, 1, 2 * D_FF), exp_map),
                      pl.BlockSpec((1, D_FF, d), exp_map),
                      pl.BlockSpec((1, 1, d), exp_map)],
            out_specs=pl.BlockSpec((MOE_ROWS, ROW_SLABS, LANES), row_map),
            scratch_shapes=[pltpu.VMEM((d, 2 * D_FF), BF16), pltpu.VMEM((D_FF, d), BF16)],
        ),
        out_shape=jax.ShapeDtypeStruct((n_rows, ROW_SLABS, LANES), F32),
        compiler_params=_params(("arbitrary",)),
        name="expert_ffn",
    )(block_e, n_act, xin, w1, b1.reshape(N_EXPERTS, 1, 2 * D_FF), w2, b2.reshape(N_EXPERTS, 1, d))


def _combine_kernel(dest_ref, x1_ref, gate_ref, g2_ref, fw_ref, y_ref, o_ref, buf_ref, sem):
    i = pl.program_id(0)
    n = pl.num_programs(0)
    gt = x1_ref.shape[0]

    def row_copy(slot, k, j, src_row):
        return pltpu.make_async_copy(y_ref.at[src_row], buf_ref.at[slot, k, j], sem.at[slot])

    def issue(tile, slot):
        def body(j, carry):
            tok = tile * gt + j
            for k in range(TOP_K):
                row_copy(slot, k, j, dest_ref[tok * TOP_K + k]).start()
            return carry
        lax.fori_loop(0, gt, body, 0, unroll=4)

    @pl.when(i == 0)
    def _():
        issue(0, 0)

    @pl.when(i + 1 < n)
    def _():
        issue(i + 1, (i + 1) % 2)

    slot = i % 2

    for k in range(TOP_K):
        pltpu.make_async_copy(y_ref.at[pl.ds(0, gt)], buf_ref.at[slot, k], sem.at[slot]).wait()

    gates = gate_ref[...]
    slabs = []
    for s in range(ROW_SLABS):
        acc = gates[:, 0:1] * buf_ref[slot, 0, :, s, :]
        for k in range(1, TOP_K):
            acc = acc + gates[:, k:k + 1] * buf_ref[slot, k, :, s, :]
        slabs.append(acc)
    moe = jnp.concatenate(slabs, axis=-1)
    x2 = x1_ref[...] + g2_ref[0] * moe
    ms = jnp.mean(x2 * x2, axis=-1, keepdims=True)
    o_ref[...] = x2 * lax.rsqrt(ms + EPS) * fw_ref[...]


def _combine(dest_flat, x1, gate_pad, g2, fw, yout, seqlen):
    t, d = x1.shape
    gt = min(GATHER_TILE, seqlen)
    per_seq = seqlen // gt
    return pl.pallas_call(
        _combine_kernel,
        grid_spec=pltpu.PrefetchScalarGridSpec(
            num_scalar_prefetch=1,
            grid=(t // gt,),
            in_specs=[pl.BlockSpec((gt, d), lambda i, dst: (i, 0)),
                      pl.BlockSpec((gt, LANES), lambda i, dst: (i, 0)),
                      pl.BlockSpec((1, 1, d), lambda i, dst: (i // per_seq, 0, 0)),
                      pl.BlockSpec((1, d), lambda i, dst: (0, 0)),
                      pl.BlockSpec(memory_space=pl.ANY)],
            out_specs=pl.BlockSpec((gt, d), lambda i, dst: (i, 0)),
            scratch_shapes=[pltpu.VMEM((2, TOP_K, gt, ROW_SLABS, LANES), F32),
                            pltpu.SemaphoreType.DMA((2,))],
        ),
        out_shape=jax.ShapeDtypeStruct((t, d), F32),
        compiler_params=_params(("arbitrary",)),
        name="combine_norm",
    )(dest_flat, x1, gate_pad, g2, fw, yout)


def _pad_lanes(v, fill=0.0):
    v = v.reshape(1, -1).astype(F32)
    return jnp.pad(v, ((0, 0), (0, LANES - v.shape[1])), constant_values=fill)


def kernel(x, c, ada_w, ada_b, norm1_w, w_in, hg_lb_logits, hg_norm_w, ssm_a_log, ssm_dt_bias, ssm_d,
           ssm_conv_w, ssm_conv_b, ssm_norm_w, w_out, norm2_w, router_w, router_b, exp_w1, exp_b1,
           exp_w2, exp_b2, final_norm_w):
    bsz, seqlen, d = x.shape
    t = bsz * seqlen
    l = 0

    c_pad = jnp.pad(c, ((0, SUBLANES - bsz), (0, 0)))
    mod = _ada_mod(c_pad, ada_w[l], ada_b[l])[:bsz]
    sh1, sc1, g1, sh2, sc2, g2 = [m.reshape(bsz, 1, d) for m in jnp.split(mod, N_MOD, axis=-1)]

    w_in_pad = jnp.pad(w_in[l], ((0, 0), (0, IN_COLS_PAD - IN_COLS))).astype(BF16)
    qs, kk, lf, vi, gg, zz, xbc, dt = _inproj(
        x, norm1_w[l].reshape(1, d), sh1, sc1, w_in_pad, hg_lb_logits, _pad_lanes(ssm_dt_bias[l]))

    o_hg = _hgrn(qs, kk, lf, vi, gg, hg_norm_w[l].reshape(1, HG_DV))
    y_ssd = _ssd(xbc, zz, dt, ssm_conv_w[l], ssm_conv_b[l].reshape(1, SSM_CONV_CH),
                 _pad_lanes(ssm_a_log[l]), jnp.repeat(ssm_d[l], SSM_HEAD_DIM).reshape(1, SSM_WIDTH),
                 ssm_norm_w[l].reshape(1, SSM_WIDTH))

    rw_pad = jnp.pad(router_w[l], ((0, 0), (0, LANES - N_EXPERTS)))
    rb_pad = _pad_lanes(router_b[l], NEG_BIG)
    x1, h2, idx_pad, gate_pad, counts = _outproj(
        x, o_hg, y_ssd, w_out[l].astype(BF16), g1, norm2_w[l].reshape(1, d), sh2, sc2, rw_pad, rb_pad)

    cnt = counts[0, :N_EXPERTS].astype(jnp.int32)
    blocks_e = (cnt + MOE_ROWS - 1) // MOE_ROWS
    blk_end = jnp.cumsum(blocks_e)
    pstart = (blk_end - blocks_e) * MOE_ROWS
    n_blocks = (t * TOP_K) // MOE_ROWS + N_EXPERTS
    n_rows = n_blocks * MOE_ROWS
    block_e = jnp.minimum(jnp.sum(blk_end[None, :] <= jnp.arange(n_blocks)[:, None], axis=1),
                          N_EXPERTS - 1).astype(jnp.int32)
    n_act = blk_end[-1:].astype(jnp.int32)

    dest_pad = _route(idx_pad.reshape(t, LANES), _pad_lanes(pstart))
    dest_flat = dest_pad[:, :TOP_K].reshape(t * TOP_K)

    xin = _scatter_rows(dest_flat, cnt, pstart.astype(jnp.int32), h2.reshape(t, d), n_rows, seqlen)
    yout = _experts(block_e, n_act, xin, exp_w1[l], exp_b1[l], exp_w2[l], exp_b2[l])
    out = _combine(dest_flat, x1.reshape(t, d), gate_pad.reshape(t, LANES), g2,
                   final_norm_w.reshape(1, d), yout, seqlen)
    return out.reshape(bsz, seqlen, d)
```

```python
import functools

import jax
import jax.numpy as jnp
from jax import lax
from jax.experimental import pallas as pl
from jax.experimental.pallas import tpu as pltpu

F32 = jnp.float32
BF16 = jnp.bfloat16
HIGHEST = lax.Precision.HIGHEST

EPS = 1e-6
D_MODEL = 1024
HG_HEADS = 4
HG_DK = 128
HG_DV = 128
HG_QF = HG_HEADS * HG_DK
HG_WIDTH = HG_HEADS * HG_DV
HG_CHUNK = 64
SSM_HEADS = 8
SSM_HEAD_DIM = 64
SSM_WIDTH = SSM_HEADS * SSM_HEAD_DIM
SSM_GROUPS = 2
SSM_GROUP_HEADS = SSM_HEADS // SSM_GROUPS
SSM_GROUP_WIDTH = SSM_WIDTH // SSM_GROUPS
SSM_STATE = 128
SSM_CONV = 4
SSM_CONV_CH = SSM_WIDTH + 2 * SSM_GROUPS * SSM_STATE
SSM_CHUNK = 128
IN_SPLITS = (HG_QF, HG_QF, HG_WIDTH, HG_WIDTH, SSM_WIDTH, SSM_CONV_CH, SSM_HEADS)
IN_COLS = sum(IN_SPLITS)
N_EXPERTS = 32
TOP_K = 4
D_FF = 1024
SWIGLU_LIMIT = 7.0
SWIGLU_ALPHA = 1.702
N_MOD = 6

LANES = 128
SUBLANES = 8
VMEM_LIMIT = 56 * 1024 * 1024

ROW_TILE = 512
SEQ_TILE = 512
MOE_ROWS = 256
GATHER_TILE = 128
NEG_BIG = -1e30


def _silu(v):
    return v * jax.nn.sigmoid(v)


def _softplus(v):
    return jnp.maximum(v, 0.0) + jnp.log1p(jnp.exp(-jnp.abs(v)))


def _params(sem):
    return pltpu.CompilerParams(dimension_semantics=sem, vmem_limit_bytes=VMEM_LIMIT)


def _ada_kernel(c_ref, w_ref, b_ref, o_ref):
    o_ref[...] = jnp.dot(_silu(c_ref[...]), w_ref[...], precision=HIGHEST,
                         preferred_element_type=F32) + b_ref[...]


def _ada_mod(c_pad, ada_w, ada_b):
    n = ada_w.shape[1]
    tn = D_MODEL
    return pl.pallas_call(
        _ada_kernel,
        grid=(n // tn,),
        in_specs=[pl.BlockSpec((SUBLANES, D_MODEL), lambda j: (0, 0)),
                  pl.BlockSpec((D_MODEL, tn), lambda j: (0, j)),
                  pl.BlockSpec((1, tn), lambda j: (0, j))],
        out_specs=pl.BlockSpec((SUBLANES, tn), lambda j: (0, j)),
        out_shape=jax.ShapeDtypeStruct((SUBLANES, n), F32),
        compiler_params=_params(("arbitrary",)),
        name="ada_mod",
    )(c_pad, ada_w, ada_b.reshape(1, n))


_OFF = [0]
for _w in IN_SPLITS:
    _OFF.append(_OFF[-1] + _w)
IN_COLS_PAD = _OFF[6] + LANES


def _inproj_kernel(x_ref, nw_ref, sh_ref, sc_ref, w_ref, lbl_ref, dtb_ref,
                   qs_ref, kk_ref, lf_ref, vi_ref, gg_ref, zz_ref, xbc_ref, dt_ref):
    x = x_ref[0]
    ms = jnp.mean(x * x, axis=-1, keepdims=True)
    h = x * lax.rsqrt(ms + EPS) * nw_ref[...]
    h = h * (1.0 + sc_ref[0]) + sh_ref[0]
    hb = h.astype(BF16)

    def seg(k):
        return jnp.dot(hb, w_ref[:, _OFF[k]:_OFF[k] + (IN_SPLITS[k] if k < 6 else LANES)],
                       preferred_element_type=F32)

    lbl = lbl_ref[...]
    le = jnp.exp(lbl - jnp.max(lbl, axis=0, keepdims=True))
    lb = le[0:1, :] / jnp.sum(le, axis=0, keepdims=True)

    qs_ref[0] = _silu(seg(0)).astype(BF16)
    fg = lb + (1.0 - lb) * jax.nn.sigmoid(seg(1))
    kk_ref[0] = (1.0 - fg).astype(BF16)
    lf_ref[0] = jnp.log(fg)
    vi_ref[0] = seg(2).astype(BF16)
    gg_ref[0] = _silu(seg(3)).astype(BF16)
    zz_ref[0] = _silu(seg(4)).astype(BF16)
    xbc_ref[0] = seg(5).astype(BF16)
    dt_ref[0] = _softplus(seg(6) + dtb_ref[...])


def _inproj(x, nw, sh, sc, w_in_pad, lb_logits, dtb_pad):
    bsz, seqlen, d = x.shape
    tm = min(ROW_TILE, seqlen)
    nt = seqlen // tm
    tok = lambda w: pl.BlockSpec((1, tm, w), lambda b, i: (b, i, 0))
    full = lambda a: pl.BlockSpec(a.shape, lambda b, i: (0,) * a.ndim)
    mod = pl.BlockSpec((1, 1, d), lambda b, i: (b, 0, 0))
    shp = lambda w, dt: jax.ShapeDtypeStruct((bsz, seqlen, w), dt)
    return pl.pallas_call(
        _inproj_kernel,
        grid=(bsz, nt),
        in_specs=[tok(d), full(nw), mod, mod, full(w_in_pad), full(lb_logits), full(dtb_pad)],
        out_specs=[tok(HG_QF), tok(HG_QF), tok(HG_QF), tok(HG_WIDTH), tok(HG_WIDTH),
                   tok(SSM_WIDTH), tok(SSM_CONV_CH), tok(LANES)],
        out_shape=[shp(HG_QF, BF16), shp(HG_QF, BF16), shp(HG_QF, F32), shp(HG_WIDTH, BF16),
                   shp(HG_WIDTH, BF16), shp(SSM_WIDTH, BF16), shp(SSM_CONV_CH, BF16),
                   shp(LANES, F32)],
        compiler_params=_params(("arbitrary", "arbitrary")),
        name="inproj",
    )(x, nw, sh, sc, w_in_pad, lb_logits, dtb_pad)


def _hgrn_kernel(qs_ref, kk_ref, lf_ref, vi_ref, gg_ref, nw_ref, o_ref, st_ref):
    @pl.when(pl.program_id(2) == 0)
    def _():
        st_ref[...] = jnp.zeros_like(st_ref)

    c = HG_CHUNK
    rows = lax.broadcasted_iota(jnp.int32, (c, c), 0)
    cols = lax.broadcasted_iota(jnp.int32, (c, c), 1)
    causal = rows >= cols
    tril = causal.astype(F32)
    nw = nw_ref[...]
    tq = qs_ref.shape[1]
    nn = (((1,), (1,)), ((), ()))
    tn = (((0,), (0,)), ((), ()))
    for ci in range(tq // c):
        sl = pl.ds(ci * c, c)
        q = qs_ref[0, sl, :].astype(F32)
        k = kk_ref[0, sl, :].astype(F32)
        v = vi_ref[0, sl, :]
        b = jnp.dot(tril, lf_ref[0, sl, :], precision=HIGHEST, preferred_element_type=F32)
        b_mid = b[c // 2 - 1:c // 2, :]
        b_last = b[c - 1:c, :]
        qa = (q * jnp.exp(b - b_mid)).astype(BF16)
        ka = (k * jnp.exp(b_mid - b)).astype(BF16)
        att = lax.dot_general(qa, ka, nn, preferred_element_type=F32)
        att = jnp.where(causal, att, 0.0).astype(BF16)
        o = jnp.dot(att, v, preferred_element_type=F32)
        st = st_ref[...]
        qb = (q * jnp.exp(b)).astype(BF16)
        o = o + lax.dot_general(qb, st.astype(BF16), nn, preferred_element_type=F32)
        kd = (k * jnp.exp(b_last - b)).astype(BF16)
        st_ref[...] = st * jnp.exp(b_last) + lax.dot_general(v, kd, tn, preferred_element_type=F32)
        ms = jnp.mean(o * o, axis=-1, keepdims=True)
        y = o * lax.rsqrt(ms + EPS) * nw * gg_ref[0, sl, :].astype(F32)
        o_ref[0, sl, :] = y.astype(BF16)


def _hgrn(qs, kk, lf, vi, gg, nw):
    bsz, seqlen, _ = qs.shape
    tq = min(SEQ_TILE, seqlen)
    blk = pl.BlockSpec((1, tq, HG_DK), lambda b, h, i: (b, i, h))
    return pl.pallas_call(
        _hgrn_kernel,
        grid=(bsz, HG_HEADS, seqlen // tq),
        in_specs=[blk, blk, blk, blk, blk, pl.BlockSpec((1, HG_DV), lambda b, h, i: (0, 0))],
        out_specs=blk,
        out_shape=jax.ShapeDtypeStruct((bsz, seqlen, HG_WIDTH), BF16),
        scratch_shapes=[pltpu.VMEM((HG_DV, HG_DK), F32)],
        compiler_params=_params(("arbitrary", "arbitrary", "arbitrary")),
        name="hgrn2",
    )(qs, kk, lf, vi, gg, nw)


def _ssd_kernel(xbc_ref, zz_ref, dt_ref, cw_ref, cb_ref, alog_ref, dsk_ref, nw_ref,
                y_ref, buf_ref, xc_ref, yc_ref, st_ref):
    tq = xbc_ref.shape[1]
    halo = SUBLANES

    @pl.when(pl.program_id(1) == 0)
    def _():
        buf_ref[0:halo, :] = jnp.zeros((halo, SSM_CONV_CH), F32)
        st_ref[...] = jnp.zeros_like(st_ref)

    buf_ref[halo:halo + tq, :] = xbc_ref[0].astype(F32)
    conv = cb_ref[...] + buf_ref[halo - 3:halo - 3 + tq, :] * cw_ref[0:1, :]
    for w in range(1, SSM_CONV):
        conv = conv + buf_ref[halo - 3 + w:halo - 3 + w + tq, :] * cw_ref[w:w + 1, :]
    buf_ref[0:halo, :] = buf_ref[tq:tq + halo, :]
    xc_ref[...] = _silu(conv)

    c = SSM_CHUNK
    p = SSM_HEAD_DIM
    rows = lax.broadcasted_iota(jnp.int32, (c, c), 0)
    cols = lax.broadcasted_iota(jnp.int32, (c, c), 1)
    causal = rows >= cols
    tril = causal.astype(F32)
    a_neg = -jnp.exp(alog_ref[...])
    nn = (((1,), (1,)), ((), ()))
    tn = (((0,), (0,)), ((), ()))
    for ci in range(tq // c):
        sl = pl.ds(ci * c, c)
        dt = dt_ref[0, sl, :]
        acum = jnp.dot(tril, dt * a_neg, precision=HIGHEST, preferred_element_type=F32)
        acum_t = acum.T
        decay_out = jnp.exp(acum)
        decay_in = jnp.exp(acum[c - 1:c, :] - acum)
        for g in range(SSM_GROUPS):
            bm = xc_ref[sl, SSM_WIDTH + g * SSM_STATE:SSM_WIDTH + (g + 1) * SSM_STATE].astype(BF16)
            cm = xc_ref[sl, SSM_WIDTH + (SSM_GROUPS + g) * SSM_STATE:
                        SSM_WIDTH + (SSM_GROUPS + g + 1) * SSM_STATE].astype(BF16)
            cb = lax.dot_general(cm, bm, nn, preferred_element_type=F32)
            st = st_ref[g]
            y_inter = lax.dot_general(cm, st.astype(BF16), nn, preferred_element_type=F32)
            for hh in range(SSM_GROUP_HEADS):
                h = g * SSM_GROUP_HEADS + hh
                lo = h * p
                xdt = xc_ref[sl, lo:lo + p] * dt[:, h:h + 1]
                seg = jnp.exp(jnp.where(causal, acum[:, h:h + 1] - acum_t[h:h + 1, :], -jnp.inf))
                y = jnp.dot((cb * seg).astype(BF16), xdt.astype(BF16), preferred_element_type=F32)
                y = y + decay_out[:, h:h + 1] * y_inter[:, hh * p:(hh + 1) * p]
                yc_ref[:, lo:lo + p] = y
                d_st = lax.dot_general((xdt * decay_in[:, h:h + 1]).astype(BF16), bm, tn,
                                       preferred_element_type=F32)
                a_last = jnp.exp(acum_t[h:h + 1, c - 1:c])
                st_ref[g, hh * p:(hh + 1) * p, :] = st[hh * p:(hh + 1) * p, :] * a_last + d_st
        y = (yc_ref[...] + dsk_ref[...] * xc_ref[sl, 0:SSM_WIDTH]) * zz_ref[0, sl, :].astype(F32)
        for g in range(SSM_GROUPS):
            gs = slice(g * SSM_GROUP_WIDTH, (g + 1) * SSM_GROUP_WIDTH)
            yg = y[:, gs]
            ms = jnp.mean(yg * yg, axis=-1, keepdims=True)
            y_ref[0, sl, gs] = (yg * lax.rsqrt(ms + EPS) * nw_ref[:, gs]).astype(BF16)


def _ssd(xbc, zz, dt, conv_w, conv_b, alog_pad, dsk_wide, nw):
    bsz, seqlen, _ = xbc.shape
    tq = min(SEQ_TILE, seqlen)
    tok = lambda w: pl.BlockSpec((1, tq, w), lambda b, i: (b, i, 0))
    full = lambda a: pl.BlockSpec(a.shape, lambda b, i: (0,) * a.ndim)
    return pl.pallas_call(
        _ssd_kernel,
        grid=(bsz, seqlen // tq),
        in_specs=[tok(SSM_CONV_CH), tok(SSM_WIDTH), tok(LANES), full(conv_w), full(conv_b),
                  full(alog_pad), full(dsk_wide), full(nw)],
        out_specs=tok(SSM_WIDTH),
        out_shape=jax.ShapeDtypeStruct((bsz, seqlen, SSM_WIDTH), BF16),
        scratch_shapes=[pltpu.VMEM((tq + 2 * SUBLANES, SSM_CONV_CH), F32),
                        pltpu.VMEM((tq, SSM_CONV_CH), F32),
                        pltpu.VMEM((SSM_CHUNK, SSM_WIDTH), F32),
                        pltpu.VMEM((SSM_GROUPS, SSM_GROUP_WIDTH, SSM_STATE), F32)],
        compiler_params=_params(("arbitrary", "arbitrary")),
        name="ssd",
    )(xbc, zz, dt, conv_w, conv_b, alog_pad, dsk_wide, nw)


def _outproj_kernel(x_ref, oh_ref, ys_ref, wo_ref, g1_ref, nw_ref, sh_ref, sc_ref, rw_ref, rb_ref,
                    x1_ref, h2_ref, idx_ref, gate_ref, cnt_ref):
    first = jnp.logical_and(pl.program_id(0) == 0, pl.program_id(1) == 0)

    @pl.when(first)
    def _():
        cnt_ref[...] = jnp.zeros_like(cnt_ref)

    mix = jnp.dot(oh_ref[0], wo_ref[0:HG_WIDTH, :], preferred_element_type=F32)
    mix = mix + jnp.dot(ys_ref[0], wo_ref[HG_WIDTH:, :], preferred_element_type=F32)
    x1 = x_ref[0] + g1_ref[0] * mix
    x1_ref[0] = x1
    ms = jnp.mean(x1 * x1, axis=-1, keepdims=True)
    h2 = x1 * lax.rsqrt(ms + EPS) * nw_ref[...]
    h2 = h2 * (1.0 + sc_ref[0]) + sh_ref[0]
    h2_ref[0] = h2
    logits = jnp.dot(h2, rw_ref[...], precision=HIGHEST, preferred_element_type=F32) + rb_ref[...]
    lane = lax.broadcasted_iota(jnp.int32, logits.shape, 1).astype(F32)
    idx_out = jnp.zeros(logits.shape, F32)
    val_out = jnp.zeros(logits.shape, F32)
    sel = jnp.zeros(logits.shape, F32)
    work = logits
    top = None
    denom = None
    for k in range(TOP_K):
        m = jnp.max(work, axis=-1, keepdims=True)
        am = jnp.min(jnp.where(work == m, lane, float(LANES)), axis=-1, keepdims=True)
        hit = lane == am
        work = jnp.where(hit, -jnp.inf, work)
        sel = jnp.where(hit, 1.0, sel)
        if k == 0:
            top = m
        e = jnp.exp(m - top)
        denom = e if k == 0 else denom + e
        idx_out = jnp.where(lane == float(k), am, idx_out)
        val_out = jnp.where(lane == float(k), e, val_out)
    idx_ref[0] = idx_out.astype(jnp.int32)
    gate_ref[0] = val_out / denom
    cnt_ref[0:1, :] += jnp.sum(sel, axis=0, keepdims=True)


def _outproj(x, o_hg, y_ssd, w_out_b, g1, nw, sh, sc, rw_pad, rb_pad):
    bsz, seqlen, d = x.shape
    tm = min(ROW_TILE, seqlen)
    tok = lambda w: pl.BlockSpec((1, tm, w), lambda b, i: (b, i, 0))
    full = lambda a: pl.BlockSpec(a.shape, lambda b, i: (0,) * a.ndim)
    mod = pl.BlockSpec((1, 1, d), lambda b, i: (b, 0, 0))
    shp = lambda w, dt: jax.ShapeDtypeStruct((bsz, seqlen, w), dt)
    return pl.pallas_call(
        _outproj_kernel,
        grid=(bsz, seqlen // tm),
        in_specs=[tok(d), tok(HG_WIDTH), tok(SSM_WIDTH), full(w_out_b), mod, full(nw), mod, mod,
                  full(rw_pad), full(rb_pad)],
        out_specs=[tok(d), tok(d), tok(LANES), tok(LANES),
                   pl.BlockSpec((SUBLANES, LANES), lambda b, i: (0, 0))],
        out_shape=[shp(d, F32), shp(d, F32), shp(LANES, jnp.int32), shp(LANES, F32),
                   jax.ShapeDtypeStruct((SUBLANES, LANES), F32)],
        compiler_params=_params(("arbitrary", "arbitrary")),
        name="outproj_router",
    )(x, o_hg, y_ssd, w_out_b, g1, nw, sh, sc, rw_pad, rb_pad)


def _route_kernel(idx_ref, pstart_ref, dest_ref, carry_ref):
    @pl.when(pl.program_id(0) == 0)
    def _():
        carry_ref[...] = jnp.zeros_like(carry_ref)

    idx = idx_ref[...]
    tt = idx.shape[0]
    lane = lax.broadcasted_iota(jnp.int32, idx.shape, 1)
    hits = [lane == idx[:, k:k + 1] for k in range(TOP_K)]
    sel = jnp.zeros(idx.shape, F32)
    for hit in hits:
        sel = jnp.where(hit, 1.0, sel)
    rows = lax.broadcasted_iota(jnp.int32, (tt, tt), 0)
    cols = lax.broadcasted_iota(jnp.int32, (tt, tt), 1)
    before = (rows > cols).astype(BF16)
    rank = jnp.dot(before, sel.astype(BF16), preferred_element_type=F32) + carry_ref[0:1, :]
    carry_ref[0:1, :] += jnp.sum(sel, axis=0, keepdims=True)
    dense = pstart_ref[...] + rank
    out = jnp.zeros(idx.shape, F32)
    for k, hit in enumerate(hits):
        dk = jnp.sum(jnp.where(hit, dense, 0.0), axis=-1, keepdims=True)
        out = jnp.where(lane == k, dk, out)
    dest_ref[...] = out.astype(jnp.int32)


def _route(idx_pad, pstart_row):
    t = idx_pad.shape[0]
    tt = min(ROW_TILE, t)
    return pl.pallas_call(
        _route_kernel,
        grid=(t // tt,),
        in_specs=[pl.BlockSpec((tt, LANES), lambda i: (i, 0)),
                  pl.BlockSpec((1, LANES), lambda i: (0, 0))],
        out_specs=pl.BlockSpec((tt, LANES), lambda i: (i, 0)),
        out_shape=jax.ShapeDtypeStruct((t, LANES), jnp.int32),
        scratch_shapes=[pltpu.VMEM((SUBLANES, LANES), F32)],
        compiler_params=_params(("arbitrary",)),
        name="route_rank",
    )(idx_pad, pstart_row)


def _invert_kernel(dest_ref, rowtok_ref):
    n_rows = rowtok_ref.shape[0]
    n_assign = dest_ref.shape[0]

    def zero_body(r, carry):
        rowtok_ref[r] = 0
        return carry
    lax.fori_loop(0, n_rows, zero_body, 0, unroll=8)

    def tok_body(tok, carry):
        for k in range(TOP_K):
            rowtok_ref[dest_ref[tok * TOP_K + k]] = tok
        return carry
    lax.fori_loop(0, n_assign // TOP_K, tok_body, 0, unroll=4)


def _invert(dest_flat, n_rows):
    return pl.pallas_call(
        _invert_kernel,
        in_specs=[pl.BlockSpec(memory_space=pltpu.SMEM)],
        out_specs=pl.BlockSpec(memory_space=pltpu.SMEM),
        out_shape=jax.ShapeDtypeStruct((n_rows,), jnp.int32),
        name="invert_route",
    )(dest_flat)


def _expert_kernel(be_ref, nact_ref, rowtok_ref, h_ref, w1_ref, b1_ref, w2_ref, b2_ref, y_ref,
                   xbuf0_ref, xbuf1_ref, w1b_ref, w2b_ref, sem):
    j = pl.program_id(0)
    nb = pl.num_programs(0)
    n_act = nact_ref[0]
    active = j < n_act
    fresh = jnp.logical_or(j == 0, be_ref[j] != be_ref[jnp.maximum(j - 1, 0)])
    odd = j % 2
    bufs = (xbuf0_ref, xbuf1_ref)

    def issue(block, s):
        base = block * MOE_ROWS
        for r in range(MOE_ROWS):
            pltpu.make_async_copy(h_ref.at[pl.ds(rowtok_ref[base + r], 1), :],
                                  bufs[s].at[pl.ds(r, 1), :], sem.at[s]).start()

    def wait(s):
        pltpu.make_async_copy(h_ref.at[pl.ds(0, MOE_ROWS), :], bufs[s], sem.at[s]).wait()

    @pl.when(j == 0)
    def _():
        issue(0, 0)

    @pl.when(jnp.logical_and(active, fresh))
    def _():
        w1b_ref[...] = w1_ref[0].astype(BF16)
        w2b_ref[...] = w2_ref[0].astype(BF16)

    for s in range(2):
        @pl.when(jnp.logical_and(j <= n_act, odd == s))
        def _(s=s):
            wait(s)

        @pl.when(jnp.logical_and(active, odd == s))
        def _(s=s):
            issue(jnp.minimum(j + 1, n_act - 1), 1 - s)
            hb = jnp.dot(bufs[s][...].astype(BF16), w1b_ref[...], preferred_element_type=F32) + b1_ref[0]
            glu = jnp.minimum(hb[:, :D_FF], SWIGLU_LIMIT)
            lin = jnp.clip(hb[:, D_FF:], -SWIGLU_LIMIT, SWIGLU_LIMIT)
            act = glu * jax.nn.sigmoid(SWIGLU_ALPHA * glu) * (lin + 1.0)
            y_ref[...] = jnp.dot(act.astype(BF16), w2b_ref[...], preferred_element_type=F32) + b2_ref[0]

        @pl.when(jnp.logical_and(jnp.logical_and(active, j == nb - 1), odd == s))
        def _(s=s):
            wait(1 - s)


def _experts(block_e, n_act, row_tok, h2, w1, b1, w2, b2):
    n_rows = row_tok.shape[0]
    d = h2.shape[1]
    nb = n_rows // MOE_ROWS
    row_map = lambda j, be, na, rt: (jnp.maximum(jnp.minimum(j, na[0] - 1), 0), 0)
    exp_map = lambda j, be, na, rt: (be[j], 0, 0)
    return pl.pallas_call(
        _expert_kernel,
        grid_spec=pltpu.PrefetchScalarGridSpec(
            num_scalar_prefetch=3,
            grid=(nb,),
            in_specs=[pl.BlockSpec(memory_space=pl.ANY),
                      pl.BlockSpec((1, d, 2 * D_FF), exp_map),
                      pl.BlockSpec((1, 1, 2 * D_FF), exp_map),
                      pl.BlockSpec((1, D_FF, d), exp_map),
                      pl.BlockSpec((1, 1, d), exp_map)],
            out_specs=pl.BlockSpec((MOE_ROWS, d), row_map),
            scratch_shapes=[pltpu.VMEM((MOE_ROWS, d), F32), pltpu.VMEM((MOE_ROWS, d), F32),
                            pltpu.VMEM((d, 2 * D_FF), BF16),
                            pltpu.VMEM((D_FF, d), BF16), pltpu.SemaphoreType.DMA((2,))],
        ),
        out_shape=jax.ShapeDtypeStruct((n_rows, d), F32),
        compiler_params=_params(("arbitrary",)),
        name="expert_ffn",
    )(block_e, n_act, row_tok, h2, w1, b1.reshape(N_EXPERTS, 1, 2 * D_FF), w2, b2.reshape(N_EXPERTS, 1, d))


def _combine_kernel(dest_ref, x1_ref, gate_ref, g2_ref, fw_ref, y_ref, o_ref, buf_ref, sem):
    i = pl.program_id(0)
    n = pl.num_programs(0)
    gt = x1_ref.shape[0]

    def row_copy(slot, k, j, src_row):
        return pltpu.make_async_copy(y_ref.at[pl.ds(src_row, 1), :],
                                     buf_ref.at[slot, k, pl.ds(j, 1), :], sem.at[slot])

    def issue(tile, slot):
        def body(j, carry):
            tok = tile * gt + j
            for k in range(TOP_K):
                row_copy(slot, k, j, dest_ref[tok * TOP_K + k]).start()
            return carry
        lax.fori_loop(0, gt, body, 0, unroll=4)

    @pl.when(i == 0)
    def _():
        issue(0, 0)

    @pl.when(i + 1 < n)
    def _():
        issue(i + 1, (i + 1) % 2)

    slot = i % 2

    for k in range(TOP_K):
        pltpu.make_async_copy(y_ref.at[pl.ds(0, gt), :], buf_ref.at[slot, k], sem.at[slot]).wait()

    gates = gate_ref[...]
    moe = gates[:, 0:1] * buf_ref[slot, 0]
    for k in range(1, TOP_K):
        moe = moe + gates[:, k:k + 1] * buf_ref[slot, k]
    x2 = x1_ref[...] + g2_ref[0] * moe
    ms = jnp.mean(x2 * x2, axis=-1, keepdims=True)
    o_ref[...] = x2 * lax.rsqrt(ms + EPS) * fw_ref[...]


def _combine(dest_flat, x1, gate_pad, g2, fw, yout, seqlen):
    t, d = x1.shape
    gt = min(GATHER_TILE, seqlen)
    per_seq = seqlen // gt
    return pl.pallas_call(
        _combine_kernel,
        grid_spec=pltpu.PrefetchScalarGridSpec(
            num_scalar_prefetch=1,
            grid=(t // gt,),
            in_specs=[pl.BlockSpec((gt, d), lambda i, dst: (i, 0)),
                      pl.BlockSpec((gt, LANES), lambda i, dst: (i, 0)),
                      pl.BlockSpec((1, 1, d), lambda i, dst: (i // per_seq, 0, 0)),
                      pl.BlockSpec((1, d), lambda i, dst: (0, 0)),
                      pl.BlockSpec(memory_space=pl.ANY)],
            out_specs=pl.BlockSpec((gt, d), lambda i, dst: (i, 0)),
            scratch_shapes=[pltpu.VMEM((2, TOP_K, gt, d), F32), pltpu.SemaphoreType.DMA((2,))],
        ),
        out_shape=jax.ShapeDtypeStruct((t, d), F32),
        compiler_params=_params(("arbitrary",)),
        name="combine_norm",
    )(dest_flat, x1, gate_pad, g2, fw, yout)


def _pad_lanes(v, fill=0.0):
    v = v.reshape(1, -1).astype(F32)
    return jnp.pad(v, ((0, 0), (0, LANES - v.shape[1])), constant_values=fill)


def kernel(x, c, ada_w, ada_b, norm1_w, w_in, hg_lb_logits, hg_norm_w, ssm_a_log, ssm_dt_bias, ssm_d,
           ssm_conv_w, ssm_conv_b, ssm_norm_w, w_out, norm2_w, router_w, router_b, exp_w1, exp_b1,
           exp_w2, exp_b2, final_norm_w):
    bsz, seqlen, d = x.shape
    t = bsz * seqlen
    l = 0

    c_pad = jnp.pad(c, ((0, SUBLANES - bsz), (0, 0)))
    mod = _ada_mod(c_pad, ada_w[l], ada_b[l])[:bsz]
    sh1, sc1, g1, sh2, sc2, g2 = [m.reshape(bsz, 1, d) for m in jnp.split(mod, N_MOD, axis=-1)]

    w_in_pad = jnp.pad(w_in[l], ((0, 0), (0, IN_COLS_PAD - IN_COLS))).astype(BF16)
    qs, kk, lf, vi, gg, zz, xbc, dt = _inproj(
        x, norm1_w[l].reshape(1, d), sh1, sc1, w_in_pad, hg_lb_logits, _pad_lanes(ssm_dt_bias[l]))

    o_hg = _hgrn(qs, kk, lf, vi, gg, hg_norm_w[l].reshape(1, HG_DV))
    y_ssd = _ssd(xbc, zz, dt, ssm_conv_w[l], ssm_conv_b[l].reshape(1, SSM_CONV_CH),
                 _pad_lanes(ssm_a_log[l]), jnp.repeat(ssm_d[l], SSM_HEAD_DIM).reshape(1, SSM_WIDTH),
                 ssm_norm_w[l].reshape(1, SSM_WIDTH))

    rw_pad = jnp.pad(router_w[l], ((0, 0), (0, LANES - N_EXPERTS)))
    rb_pad = _pad_lanes(router_b[l], NEG_BIG)
    x1, h2, idx_pad, gate_pad, counts = _outproj(
        x, o_hg, y_ssd, w_out[l].astype(BF16), g1, norm2_w[l].reshape(1, d), sh2, sc2, rw_pad, rb_pad)

    cnt = counts[0, :N_EXPERTS].astype(jnp.int32)
    blocks_e = (cnt + MOE_ROWS - 1) // MOE_ROWS
    blk_end = jnp.cumsum(blocks_e)
    pstart = (blk_end - blocks_e) * MOE_ROWS
    n_blocks = (t * TOP_K) // MOE_ROWS + N_EXPERTS
    n_rows = n_blocks * MOE_ROWS
    block_e = jnp.minimum(jnp.sum(blk_end[None, :] <= jnp.arange(n_blocks)[:, None], axis=1),
                          N_EXPERTS - 1).astype(jnp.int32)
    n_act = blk_end[-1:].astype(jnp.int32)

    dest_pad = _route(idx_pad.reshape(t, LANES), _pad_lanes(pstart))
    dest_flat = dest_pad[:, :TOP_K].reshape(t * TOP_K)

    row_tok = _invert(dest_flat, n_rows)
    yout = _experts(block_e, n_act, row_tok, h2.reshape(t, d), exp_w1[l], exp_b1[l], exp_w2[l], exp_b2[l])
    out = _combine(dest_flat, x1.reshape(t, d), gate_pad.reshape(t, LANES), g2,
                   final_norm_w.reshape(1, d), yout, seqlen)
    return out.reshape(bsz, seqlen, d)
```

```python
import functools

import jax
import jax.numpy as jnp
from jax import lax
from jax.experimental import pallas as pl
from jax.experimental.pallas import tpu as pltpu

F32 = jnp.float32
BF16 = jnp.bfloat16
HIGHEST = lax.Precision.HIGHEST

EPS = 1e-6
D_MODEL = 1024
HG_HEADS = 4
HG_DK = 128
HG_DV = 128
HG_QF = HG_HEADS * HG_DK
HG_WIDTH = HG_HEADS * HG_DV
HG_CHUNK = 64
SSM_HEADS = 8
SSM_HEAD_DIM = 64
SSM_WIDTH = SSM_HEADS * SSM_HEAD_DIM
SSM_GROUPS = 2
SSM_GROUP_HEADS = SSM_HEADS // SSM_GROUPS
SSM_GROUP_WIDTH = SSM_WIDTH // SSM_GROUPS
SSM_STATE = 128
SSM_CONV = 4
SSM_CONV_CH = SSM_WIDTH + 2 * SSM_GROUPS * SSM_STATE
SSM_CHUNK = 128
IN_SPLITS = (HG_QF, HG_QF, HG_WIDTH, HG_WIDTH, SSM_WIDTH, SSM_CONV_CH, SSM_HEADS)
IN_COLS = sum(IN_SPLITS)
N_EXPERTS = 32
TOP_K = 4
D_FF = 1024
SWIGLU_LIMIT = 7.0
SWIGLU_ALPHA = 1.702
N_MOD = 6

LANES = 128
SUBLANES = 8
VMEM_LIMIT = 56 * 1024 * 1024

ROW_TILE = 512
SEQ_TILE = 512
MOE_ROWS = 256
GATHER_TILE = 128
INVERT_TILE = 1024
GATHER_DEPTH = 3
NEG_BIG = -1e30


def _silu(v):
    return v * jax.nn.sigmoid(v)


def _softplus(v):
    return jnp.maximum(v, 0.0) + jnp.log1p(jnp.exp(-jnp.abs(v)))


def _params(sem):
    return pltpu.CompilerParams(dimension_semantics=sem, vmem_limit_bytes=VMEM_LIMIT)


def _ada_kernel(c_ref, w_ref, b_ref, o_ref):
    o_ref[...] = jnp.dot(_silu(c_ref[...]), w_ref[...], precision=HIGHEST,
                         preferred_element_type=F32) + b_ref[...]


def _ada_mod(c_pad, ada_w, ada_b):
    n = ada_w.shape[1]
    tn = D_MODEL
    return pl.pallas_call(
        _ada_kernel,
        grid=(n // tn,),
        in_specs=[pl.BlockSpec((SUBLANES, D_MODEL), lambda j: (0, 0)),
                  pl.BlockSpec((D_MODEL, tn), lambda j: (0, j)),
                  pl.BlockSpec((1, tn), lambda j: (0, j))],
        out_specs=pl.BlockSpec((SUBLANES, tn), lambda j: (0, j)),
        out_shape=jax.ShapeDtypeStruct((SUBLANES, n), F32),
        compiler_params=_params(("arbitrary",)),
        name="ada_mod",
    )(c_pad, ada_w, ada_b.reshape(1, n))


_OFF = [0]
for _w in IN_SPLITS:
    _OFF.append(_OFF[-1] + _w)
IN_COLS_PAD = _OFF[6] + LANES


def _inproj_kernel(x_ref, nw_ref, sh_ref, sc_ref, w_ref, lbl_ref, dtb_ref,
                   qs_ref, kk_ref, lf_ref, vi_ref, gg_ref, zz_ref, xbc_ref, dt_ref):
    x = x_ref[0]
    ms = jnp.mean(x * x, axis=-1, keepdims=True)
    h = x * lax.rsqrt(ms + EPS) * nw_ref[...]
    h = h * (1.0 + sc_ref[0]) + sh_ref[0]
    hb = h.astype(BF16)

    def seg(k):
        return jnp.dot(hb, w_ref[:, _OFF[k]:_OFF[k] + (IN_SPLITS[k] if k < 6 else LANES)],
                       preferred_element_type=F32)

    lbl = lbl_ref[...]
    le = jnp.exp(lbl - jnp.max(lbl, axis=0, keepdims=True))
    lb = le[0:1, :] / jnp.sum(le, axis=0, keepdims=True)

    qs_ref[0] = _silu(seg(0)).astype(BF16)
    fg = lb + (1.0 - lb) * jax.nn.sigmoid(seg(1))
    kk_ref[0] = (1.0 - fg).astype(BF16)
    lf_ref[0] = jnp.log(fg)
    vi_ref[0] = seg(2).astype(BF16)
    gg_ref[0] = _silu(seg(3)).astype(BF16)
    zz_ref[0] = _silu(seg(4)).astype(BF16)
    xbc_ref[0] = seg(5).astype(BF16)
    dt_ref[0] = _softplus(seg(6) + dtb_ref[...])


def _inproj(x, nw, sh, sc, w_in_pad, lb_logits, dtb_pad):
    bsz, seqlen, d = x.shape
    tm = min(ROW_TILE, seqlen)
    nt = seqlen // tm
    tok = lambda w: pl.BlockSpec((1, tm, w), lambda b, i: (b, i, 0))
    full = lambda a: pl.BlockSpec(a.shape, lambda b, i: (0,) * a.ndim)
    mod = pl.BlockSpec((1, 1, d), lambda b, i: (b, 0, 0))
    shp = lambda w, dt: jax.ShapeDtypeStruct((bsz, seqlen, w), dt)
    return pl.pallas_call(
        _inproj_kernel,
        grid=(bsz, nt),
        in_specs=[tok(d), full(nw), mod, mod, full(w_in_pad), full(lb_logits), full(dtb_pad)],
        out_specs=[tok(HG_QF), tok(HG_QF), tok(HG_QF), tok(HG_WIDTH), tok(HG_WIDTH),
                   tok(SSM_WIDTH), tok(SSM_CONV_CH), tok(LANES)],
        out_shape=[shp(HG_QF, BF16), shp(HG_QF, BF16), shp(HG_QF, F32), shp(HG_WIDTH, BF16),
                   shp(HG_WIDTH, BF16), shp(SSM_WIDTH, BF16), shp(SSM_CONV_CH, BF16),
                   shp(LANES, F32)],
        compiler_params=_params(("arbitrary", "arbitrary")),
        name="inproj",
    )(x, nw, sh, sc, w_in_pad, lb_logits, dtb_pad)


def _hgrn_kernel(qs_ref, kk_ref, lf_ref, vi_ref, gg_ref, nw_ref, o_ref, st_ref):
    @pl.when(pl.program_id(2) == 0)
    def _():
        st_ref[...] = jnp.zeros_like(st_ref)

    c = HG_CHUNK
    rows = lax.broadcasted_iota(jnp.int32, (c, c), 0)
    cols = lax.broadcasted_iota(jnp.int32, (c, c), 1)
    causal = rows >= cols
    tril = causal.astype(F32)
    nw = nw_ref[...]
    tq = qs_ref.shape[1]
    nn = (((1,), (1,)), ((), ()))
    tn = (((0,), (0,)), ((), ()))
    for ci in range(tq // c):
        sl = pl.ds(ci * c, c)
        q = qs_ref[0, sl, :].astype(F32)
        k = kk_ref[0, sl, :].astype(F32)
        v = vi_ref[0, sl, :]
        b = jnp.dot(tril, lf_ref[0, sl, :], precision=HIGHEST, preferred_element_type=F32)
        b_mid = b[c // 2 - 1:c // 2, :]
        b_last = b[c - 1:c, :]
        qa = (q * jnp.exp(b - b_mid)).astype(BF16)
        ka = (k * jnp.exp(b_mid - b)).astype(BF16)
        att = lax.dot_general(qa, ka, nn, preferred_element_type=F32)
        att = jnp.where(causal, att, 0.0).astype(BF16)
        o = jnp.dot(att, v, preferred_element_type=F32)
        st = st_ref[...]
        qb = (q * jnp.exp(b)).astype(BF16)
        o = o + lax.dot_general(qb, st.astype(BF16), nn, preferred_element_type=F32)
        kd = (k * jnp.exp(b_last - b)).astype(BF16)
        st_ref[...] = st * jnp.exp(b_last) + lax.dot_general(v, kd, tn, preferred_element_type=F32)
        ms = jnp.mean(o * o, axis=-1, keepdims=True)
        y = o * lax.rsqrt(ms + EPS) * nw * gg_ref[0, sl, :].astype(F32)
        o_ref[0, sl, :] = y.astype(BF16)


def _hgrn(qs, kk, lf, vi, gg, nw):
    bsz, seqlen, _ = qs.shape
    tq = min(SEQ_TILE, seqlen)
    blk = pl.BlockSpec((1, tq, HG_DK), lambda b, h, i: (b, i, h))
    return pl.pallas_call(
        _hgrn_kernel,
        grid=(bsz, HG_HEADS, seqlen // tq),
        in_specs=[blk, blk, blk, blk, blk, pl.BlockSpec((1, HG_DV), lambda b, h, i: (0, 0))],
        out_specs=blk,
        out_shape=jax.ShapeDtypeStruct((bsz, seqlen, HG_WIDTH), BF16),
        scratch_shapes=[pltpu.VMEM((HG_DV, HG_DK), F32)],
        compiler_params=_params(("arbitrary", "arbitrary", "arbitrary")),
        name="hgrn2",
    )(qs, kk, lf, vi, gg, nw)


def _ssd_kernel(xbc_ref, zz_ref, dt_ref, cw_ref, cb_ref, alog_ref, dsk_ref, nw_ref,
                y_ref, buf_ref, xc_ref, yc_ref, st_ref):
    tq = xbc_ref.shape[1]
    halo = SUBLANES

    @pl.when(pl.program_id(1) == 0)
    def _():
        buf_ref[0:halo, :] = jnp.zeros((halo, SSM_CONV_CH), F32)
        st_ref[...] = jnp.zeros_like(st_ref)

    buf_ref[halo:halo + tq, :] = xbc_ref[0].astype(F32)
    conv = cb_ref[...] + buf_ref[halo - 3:halo - 3 + tq, :] * cw_ref[0:1, :]
    for w in range(1, SSM_CONV):
        conv = conv + buf_ref[halo - 3 + w:halo - 3 + w + tq, :] * cw_ref[w:w + 1, :]
    buf_ref[0:halo, :] = buf_ref[tq:tq + halo, :]
    xc_ref[...] = _silu(conv)

    c = SSM_CHUNK
    p = SSM_HEAD_DIM
    rows = lax.broadcasted_iota(jnp.int32, (c, c), 0)
    cols = lax.broadcasted_iota(jnp.int32, (c, c), 1)
    causal = rows >= cols
    tril = causal.astype(F32)
    a_neg = -jnp.exp(alog_ref[...])
    nn = (((1,), (1,)), ((), ()))
    tn = (((0,), (0,)), ((), ()))
    for ci in range(tq // c):
        sl = pl.ds(ci * c, c)
        dt = dt_ref[0, sl, :]
        acum = jnp.dot(tril, dt * a_neg, precision=HIGHEST, preferred_element_type=F32)
        acum_t = acum.T
        decay_out = jnp.exp(acum)
        decay_in = jnp.exp(acum[c - 1:c, :] - acum)
        for g in range(SSM_GROUPS):
            bm = xc_ref[sl, SSM_WIDTH + g * SSM_STATE:SSM_WIDTH + (g + 1) * SSM_STATE].astype(BF16)
            cm = xc_ref[sl, SSM_WIDTH + (SSM_GROUPS + g) * SSM_STATE:
                        SSM_WIDTH + (SSM_GROUPS + g + 1) * SSM_STATE].astype(BF16)
            cb = lax.dot_general(cm, bm, nn, preferred_element_type=F32)
            st = st_ref[g]
            y_inter = lax.dot_general(cm, st.astype(BF16), nn, preferred_element_type=F32)
            for hh in range(SSM_GROUP_HEADS):
                h = g * SSM_GROUP_HEADS + hh
                lo = h * p
                xdt = xc_ref[sl, lo:lo + p] * dt[:, h:h + 1]
                seg = jnp.exp(jnp.where(causal, acum[:, h:h + 1] - acum_t[h:h + 1, :], -jnp.inf))
                y = jnp.dot((cb * seg).astype(BF16), xdt.astype(BF16), preferred_element_type=F32)
                y = y + decay_out[:, h:h + 1] * y_inter[:, hh * p:(hh + 1) * p]
                yc_ref[:, lo:lo + p] = y
                d_st = lax.dot_general((xdt * decay_in[:, h:h + 1]).astype(BF16), bm, tn,
                                       preferred_element_type=F32)
                a_last = jnp.exp(acum_t[h:h + 1, c - 1:c])
                st_ref[g, hh * p:(hh + 1) * p, :] = st[hh * p:(hh + 1) * p, :] * a_last + d_st
        y = (yc_ref[...] + dsk_ref[...] * xc_ref[sl, 0:SSM_WIDTH]) * zz_ref[0, sl, :].astype(F32)
        for g in range(SSM_GROUPS):
            gs = slice(g * SSM_GROUP_WIDTH, (g + 1) * SSM_GROUP_WIDTH)
            yg = y[:, gs]
            ms = jnp.mean(yg * yg, axis=-1, keepdims=True)
            y_ref[0, sl, gs] = (yg * lax.rsqrt(ms + EPS) * nw_ref[:, gs]).astype(BF16)


def _ssd(xbc, zz, dt, conv_w, conv_b, alog_pad, dsk_wide, nw):
    bsz, seqlen, _ = xbc.shape
    tq = min(SEQ_TILE, seqlen)
    tok = lambda w: pl.BlockSpec((1, tq, w), lambda b, i: (b, i, 0))
    full = lambda a: pl.BlockSpec(a.shape, lambda b, i: (0,) * a.ndim)
    return pl.pallas_call(
        _ssd_kernel,
        grid=(bsz, seqlen // tq),
        in_specs=[tok(SSM_CONV_CH), tok(SSM_WIDTH), tok(LANES), full(conv_w), full(conv_b),
                  full(alog_pad), full(dsk_wide), full(nw)],
        out_specs=tok(SSM_WIDTH),
        out_shape=jax.ShapeDtypeStruct((bsz, seqlen, SSM_WIDTH), BF16),
        scratch_shapes=[pltpu.VMEM((tq + 2 * SUBLANES, SSM_CONV_CH), F32),
                        pltpu.VMEM((tq, SSM_CONV_CH), F32),
                        pltpu.VMEM((SSM_CHUNK, SSM_WIDTH), F32),
                        pltpu.VMEM((SSM_GROUPS, SSM_GROUP_WIDTH, SSM_STATE), F32)],
        compiler_params=_params(("arbitrary", "arbitrary")),
        name="ssd",
    )(xbc, zz, dt, conv_w, conv_b, alog_pad, dsk_wide, nw)


def _outproj_kernel(x_ref, oh_ref, ys_ref, wo_ref, g1_ref, nw_ref, sh_ref, sc_ref, rw_ref, rb_ref,
                    x1_ref, h2_ref, idx_ref, gate_ref, cnt_ref):
    first = jnp.logical_and(pl.program_id(0) == 0, pl.program_id(1) == 0)

    @pl.when(first)
    def _():
        cnt_ref[...] = jnp.zeros_like(cnt_ref)

    mix = jnp.dot(oh_ref[0], wo_ref[0:HG_WIDTH, :], preferred_element_type=F32)
    mix = mix + jnp.dot(ys_ref[0], wo_ref[HG_WIDTH:, :], preferred_element_type=F32)
    x1 = x_ref[0] + g1_ref[0] * mix
    x1_ref[0] = x1
    ms = jnp.mean(x1 * x1, axis=-1, keepdims=True)
    h2 = x1 * lax.rsqrt(ms + EPS) * nw_ref[...]
    h2 = h2 * (1.0 + sc_ref[0]) + sh_ref[0]
    h2_ref[0] = h2
    logits = jnp.dot(h2, rw_ref[...], precision=HIGHEST, preferred_element_type=F32) + rb_ref[...]
    lane = lax.broadcasted_iota(jnp.int32, logits.shape, 1).astype(F32)
    idx_out = jnp.zeros(logits.shape, F32)
    val_out = jnp.zeros(logits.shape, F32)
    sel = jnp.zeros(logits.shape, F32)
    work = logits
    top = None
    denom = None
    for k in range(TOP_K):
        m = jnp.max(work, axis=-1, keepdims=True)
        am = jnp.min(jnp.where(work == m, lane, float(LANES)), axis=-1, keepdims=True)
        hit = lane == am
        work = jnp.where(hit, -jnp.inf, work)
        sel = jnp.where(hit, 1.0, sel)
        if k == 0:
            top = m
        e = jnp.exp(m - top)
        denom = e if k == 0 else denom + e
        idx_out = jnp.where(lane == float(k), am, idx_out)
        val_out = jnp.where(lane == float(k), e, val_out)
    idx_ref[0] = idx_out.astype(jnp.int32)
    gate_ref[0] = val_out / denom
    cnt_ref[0:1, :] += jnp.sum(sel, axis=0, keepdims=True)


def _outproj(x, o_hg, y_ssd, w_out_b, g1, nw, sh, sc, rw_pad, rb_pad):
    bsz, seqlen, d = x.shape
    tm = min(ROW_TILE, seqlen)
    tok = lambda w: pl.BlockSpec((1, tm, w), lambda b, i: (b, i, 0))
    full = lambda a: pl.BlockSpec(a.shape, lambda b, i: (0,) * a.ndim)
    mod = pl.BlockSpec((1, 1, d), lambda b, i: (b, 0, 0))
    shp = lambda w, dt: jax.ShapeDtypeStruct((bsz, seqlen, w), dt)
    return pl.pallas_call(
        _outproj_kernel,
        grid=(bsz, seqlen // tm),
        in_specs=[tok(d), tok(HG_WIDTH), tok(SSM_WIDTH), full(w_out_b), mod, full(nw), mod, mod,
                  full(rw_pad), full(rb_pad)],
        out_specs=[tok(d), tok(d), tok(LANES), tok(LANES),
                   pl.BlockSpec((SUBLANES, LANES), lambda b, i: (0, 0))],
        out_shape=[shp(d, F32), shp(d, F32), shp(LANES, jnp.int32), shp(LANES, F32),
                   jax.ShapeDtypeStruct((SUBLANES, LANES), F32)],
        compiler_params=_params(("arbitrary", "arbitrary")),
        name="outproj_router",
    )(x, o_hg, y_ssd, w_out_b, g1, nw, sh, sc, rw_pad, rb_pad)


def _route_kernel(idx_ref, pstart_ref, dest_ref, carry_ref):
    @pl.when(pl.program_id(0) == 0)
    def _():
        carry_ref[...] = jnp.zeros_like(carry_ref)

    idx = idx_ref[...]
    tt = idx.shape[0]
    lane = lax.broadcasted_iota(jnp.int32, idx.shape, 1)
    hits = [lane == idx[:, k:k + 1] for k in range(TOP_K)]
    sel = jnp.zeros(idx.shape, F32)
    for hit in hits:
        sel = jnp.where(hit, 1.0, sel)
    rows = lax.broadcasted_iota(jnp.int32, (tt, tt), 0)
    cols = lax.broadcasted_iota(jnp.int32, (tt, tt), 1)
    before = (rows > cols).astype(BF16)
    rank = jnp.dot(before, sel.astype(BF16), preferred_element_type=F32) + carry_ref[0:1, :]
    carry_ref[0:1, :] += jnp.sum(sel, axis=0, keepdims=True)
    dense = pstart_ref[...] + rank
    out = jnp.zeros(idx.shape, F32)
    for k, hit in enumerate(hits):
        dk = jnp.sum(jnp.where(hit, dense, 0.0), axis=-1, keepdims=True)
        out = jnp.where(lane == k, dk, out)
    dest_ref[...] = out.astype(jnp.int32)


def _route(idx_pad, pstart_row):
    t = idx_pad.shape[0]
    tt = min(ROW_TILE, t)
    return pl.pallas_call(
        _route_kernel,
        grid=(t // tt,),
        in_specs=[pl.BlockSpec((tt, LANES), lambda i: (i, 0)),
                  pl.BlockSpec((1, LANES), lambda i: (0, 0))],
        out_specs=pl.BlockSpec((tt, LANES), lambda i: (i, 0)),
        out_shape=jax.ShapeDtypeStruct((t, LANES), jnp.int32),
        scratch_shapes=[pltpu.VMEM((SUBLANES, LANES), F32)],
        compiler_params=_params(("arbitrary",)),
        name="route_rank",
    )(idx_pad, pstart_row)


def _invert_kernel(dest_ref, zeros_ref, rowtok_ref, sem):
    i = pl.program_id(0)
    per_step = dest_ref.shape[0] // TOP_K // pl.num_programs(0)

    @pl.when(i == 0)
    def _():
        fill = pltpu.make_async_copy(zeros_ref, rowtok_ref, sem)
        fill.start()
        fill.wait()

    def tok_body(n, carry):
        tok = i * per_step + n
        for k in range(TOP_K):
            rowtok_ref[dest_ref[tok * TOP_K + k]] = tok
        return carry
    lax.fori_loop(0, per_step, tok_body, 0, unroll=8)


def _invert(dest_flat, n_rows):
    t = dest_flat.shape[0] // TOP_K
    return pl.pallas_call(
        _invert_kernel,
        grid=(max(t // INVERT_TILE, 1),),
        in_specs=[pl.BlockSpec(memory_space=pltpu.SMEM), pl.BlockSpec(memory_space=pl.ANY)],
        out_specs=pl.BlockSpec(memory_space=pltpu.SMEM),
        out_shape=jax.ShapeDtypeStruct((n_rows,), jnp.int32),
        scratch_shapes=[pltpu.SemaphoreType.DMA(())],
        compiler_params=_params(("arbitrary",)),
        name="invert_route",
    )(dest_flat, jnp.zeros((n_rows,), jnp.int32))


def _expert_kernel(be_ref, nact_ref, rowtok_ref, h_ref, w1_ref, b1_ref, w2_ref, b2_ref, y_ref,
                   xbuf0_ref, xbuf1_ref, xbuf2_ref, w1b_ref, w2b_ref, sem):
    j = pl.program_id(0)
    n_act = nact_ref[0]
    active = j < n_act
    fresh = jnp.logical_or(j == 0, be_ref[j] != be_ref[jnp.maximum(j - 1, 0)])
    phase = j % GATHER_DEPTH
    bufs = (xbuf0_ref, xbuf1_ref, xbuf2_ref)

    def issue(block, s):
        base = jnp.minimum(block, n_act - 1) * MOE_ROWS
        for r in range(MOE_ROWS):
            pltpu.make_async_copy(h_ref.at[pl.ds(rowtok_ref[base + r], 1), :],
                                  bufs[s].at[pl.ds(r, 1), :], sem.at[s]).start()

    def wait(s):
        pltpu.make_async_copy(h_ref.at[pl.ds(0, MOE_ROWS), :], bufs[s], sem.at[s]).wait()

    @pl.when(j == 0)
    def _():
        for s in range(GATHER_DEPTH - 1):
            issue(s, s)

    @pl.when(jnp.logical_and(active, fresh))
    def _():
        w1b_ref[...] = w1_ref[0].astype(BF16)
        w2b_ref[...] = w2_ref[0].astype(BF16)

    for s in range(GATHER_DEPTH):
        @pl.when(jnp.logical_and(j < n_act + GATHER_DEPTH - 1, phase == s))
        def _(s=s):
            wait(s)

        @pl.when(jnp.logical_and(active, phase == s))
        def _(s=s):
            issue(j + GATHER_DEPTH - 1, (s + GATHER_DEPTH - 1) % GATHER_DEPTH)
            hb = jnp.dot(bufs[s][...].astype(BF16), w1b_ref[...], preferred_element_type=F32) + b1_ref[0]
            glu = jnp.minimum(hb[:, :D_FF], SWIGLU_LIMIT)
            lin = jnp.clip(hb[:, D_FF:], -SWIGLU_LIMIT, SWIGLU_LIMIT)
            act = glu * jax.nn.sigmoid(SWIGLU_ALPHA * glu) * (lin + 1.0)
            y_ref[...] = jnp.dot(act.astype(BF16), w2b_ref[...], preferred_element_type=F32) + b2_ref[0]


def _experts(block_e, n_act, row_tok, h2, w1, b1, w2, b2):
    n_rows = row_tok.shape[0]
    d = h2.shape[1]
    nb = block_e.shape[0]
    row_map = lambda j, be, na, rt: (jnp.maximum(jnp.minimum(j, na[0] - 1), 0), 0)
    exp_map = lambda j, be, na, rt: (be[j], 0, 0)
    return pl.pallas_call(
        _expert_kernel,
        grid_spec=pltpu.PrefetchScalarGridSpec(
            num_scalar_prefetch=3,
            grid=(nb,),
            in_specs=[pl.BlockSpec(memory_space=pl.ANY),
                      pl.BlockSpec((1, d, 2 * D_FF), exp_map),
                      pl.BlockSpec((1, 1, 2 * D_FF), exp_map),
                      pl.BlockSpec((1, D_FF, d), exp_map),
                      pl.BlockSpec((1, 1, d), exp_map)],
            out_specs=pl.BlockSpec((MOE_ROWS, d), row_map),
            scratch_shapes=[pltpu.VMEM((MOE_ROWS, d), F32)] * GATHER_DEPTH + [
                pltpu.VMEM((d, 2 * D_FF), BF16), pltpu.VMEM((D_FF, d), BF16),
                pltpu.SemaphoreType.DMA((GATHER_DEPTH,))],
        ),
        out_shape=jax.ShapeDtypeStruct((n_rows, d), F32),
        compiler_params=_params(("arbitrary",)),
        name="expert_ffn",
    )(block_e, n_act, row_tok, h2, w1, b1.reshape(N_EXPERTS, 1, 2 * D_FF), w2, b2.reshape(N_EXPERTS, 1, d))


def _combine_kernel(dest_ref, x1_ref, gate_ref, g2_ref, fw_ref, y_ref, o_ref, buf_ref, sem):
    i = pl.program_id(0)
    n = pl.num_programs(0)
    gt = x1_ref.shape[0]

    def row_copy(slot, k, j, src_row):
        return pltpu.make_async_copy(y_ref.at[pl.ds(src_row, 1), :],
                                     buf_ref.at[slot, k, pl.ds(j, 1), :], sem.at[slot])

    def issue(tile, slot):
        def body(j, carry):
            tok = tile * gt + j
            for k in range(TOP_K):
                row_copy(slot, k, j, dest_ref[tok * TOP_K + k]).start()
            return carry
        lax.fori_loop(0, gt, body, 0, unroll=4)

    @pl.when(i == 0)
    def _():
        issue(0, 0)

    @pl.when(i + 1 < n)
    def _():
        issue(i + 1, (i + 1) % 2)

    slot = i % 2

    for k in range(TOP_K):
        pltpu.make_async_copy(y_ref.at[pl.ds(0, gt), :], buf_ref.at[slot, k], sem.at[slot]).wait()

    gates = gate_ref[...]
    moe = gates[:, 0:1] * buf_ref[slot, 0]
    for k in range(1, TOP_K):
        moe = moe + gates[:, k:k + 1] * buf_ref[slot, k]
    x2 = x1_ref[...] + g2_ref[0] * moe
    ms = jnp.mean(x2 * x2, axis=-1, keepdims=True)
    o_ref[...] = x2 * lax.rsqrt(ms + EPS) * fw_ref[...]


def _combine(dest_flat, x1, gate_pad, g2, fw, yout, seqlen):
    t, d = x1.shape
    gt = min(GATHER_TILE, seqlen)
    per_seq = seqlen // gt
    return pl.pallas_call(
        _combine_kernel,
        grid_spec=pltpu.PrefetchScalarGridSpec(
            num_scalar_prefetch=1,
            grid=(t // gt,),
            in_specs=[pl.BlockSpec((gt, d), lambda i, dst: (i, 0)),
                      pl.BlockSpec((gt, LANES), lambda i, dst: (i, 0)),
                      pl.BlockSpec((1, 1, d), lambda i, dst: (i // per_seq, 0, 0)),
                      pl.BlockSpec((1, d), lambda i, dst: (0, 0)),
                      pl.BlockSpec(memory_space=pl.ANY)],
            out_specs=pl.BlockSpec((gt, d), lambda i, dst: (i, 0)),
            scratch_shapes=[pltpu.VMEM((2, TOP_K, gt, d), F32), pltpu.SemaphoreType.DMA((2,))],
        ),
        out_shape=jax.ShapeDtypeStruct((t, d), F32),
        compiler_params=_params(("arbitrary",)),
        name="combine_norm",
    )(dest_flat, x1, gate_pad, g2, fw, yout)


def _pad_lanes(v, fill=0.0):
    v = v.reshape(1, -1).astype(F32)
    return jnp.pad(v, ((0, 0), (0, LANES - v.shape[1])), constant_values=fill)


def kernel(x, c, ada_w, ada_b, norm1_w, w_in, hg_lb_logits, hg_norm_w, ssm_a_log, ssm_dt_bias, ssm_d,
           ssm_conv_w, ssm_conv_b, ssm_norm_w, w_out, norm2_w, router_w, router_b, exp_w1, exp_b1,
           exp_w2, exp_b2, final_norm_w):
    bsz, seqlen, d = x.shape
    t = bsz * seqlen
    l = 0

    c_pad = jnp.pad(c, ((0, SUBLANES - bsz), (0, 0)))
    mod = _ada_mod(c_pad, ada_w[l], ada_b[l])[:bsz]
    sh1, sc1, g1, sh2, sc2, g2 = [m.reshape(bsz, 1, d) for m in jnp.split(mod, N_MOD, axis=-1)]

    w_in_pad = jnp.pad(w_in[l], ((0, 0), (0, IN_COLS_PAD - IN_COLS))).astype(BF16)
    qs, kk, lf, vi, gg, zz, xbc, dt = _inproj(
        x, norm1_w[l].reshape(1, d), sh1, sc1, w_in_pad, hg_lb_logits, _pad_lanes(ssm_dt_bias[l]))

    o_hg = _hgrn(qs, kk, lf, vi, gg, hg_norm_w[l].reshape(1, HG_DV))
    y_ssd = _ssd(xbc, zz, dt, ssm_conv_w[l], ssm_conv_b[l].reshape(1, SSM_CONV_CH),
                 _pad_lanes(ssm_a_log[l]), jnp.repeat(ssm_d[l], SSM_HEAD_DIM).reshape(1, SSM_WIDTH),
                 ssm_norm_w[l].reshape(1, SSM_WIDTH))

    rw_pad = jnp.pad(router_w[l], ((0, 0), (0, LANES - N_EXPERTS)))
    rb_pad = _pad_lanes(router_b[l], NEG_BIG)
    x1, h2, idx_pad, gate_pad, counts = _outproj(
        x, o_hg, y_ssd, w_out[l].astype(BF16), g1, norm2_w[l].reshape(1, d), sh2, sc2, rw_pad, rb_pad)

    cnt = counts[0, :N_EXPERTS].astype(jnp.int32)
    blocks_e = (cnt + MOE_ROWS - 1) // MOE_ROWS
    blk_end = jnp.cumsum(blocks_e)
    pstart = (blk_end - blocks_e) * MOE_ROWS
    n_blocks = (t * TOP_K) // MOE_ROWS + N_EXPERTS
    n_rows = n_blocks * MOE_ROWS
    n_steps = n_blocks + GATHER_DEPTH - 1
    block_e = jnp.minimum(jnp.sum(blk_end[None, :] <= jnp.arange(n_steps)[:, None], axis=1),
                          N_EXPERTS - 1).astype(jnp.int32)
    n_act = blk_end[-1:].astype(jnp.int32)

    dest_pad = _route(idx_pad.reshape(t, LANES), _pad_lanes(pstart))
    dest_flat = dest_pad[:, :TOP_K].reshape(t * TOP_K)

    row_tok = _invert(dest_flat, n_rows)
    yout = _experts(block_e, n_act, row_tok, h2.reshape(t, d), exp_w1[l], exp_b1[l], exp_w2[l], exp_b2[l])
    out = _combine(dest_flat, x1.reshape(t, d), gate_pad.reshape(t, LANES), g2,
                   final_norm_w.reshape(1, d), yout, seqlen)
    return out.reshape(bsz, seqlen, d)
```

```python
import functools

import jax
import jax.numpy as jnp
from jax import lax
from jax.experimental import pallas as pl
from jax.experimental.pallas import tpu as pltpu

F32 = jnp.float32
BF16 = jnp.bfloat16
HIGHEST = lax.Precision.HIGHEST

EPS = 1e-6
D_MODEL = 1024
HG_HEADS = 4
HG_DK = 128
HG_DV = 128
HG_QF = HG_HEADS * HG_DK
HG_WIDTH = HG_HEADS * HG_DV
HG_CHUNK = 64
SSM_HEADS = 8
SSM_HEAD_DIM = 64
SSM_WIDTH = SSM_HEADS * SSM_HEAD_DIM
SSM_GROUPS = 2
SSM_GROUP_HEADS = SSM_HEADS // SSM_GROUPS
SSM_GROUP_WIDTH = SSM_WIDTH // SSM_GROUPS
SSM_STATE = 128
SSM_CONV = 4
SSM_CONV_CH = SSM_WIDTH + 2 * SSM_GROUPS * SSM_STATE
SSM_CHUNK = 128
IN_SPLITS = (HG_QF, HG_QF, HG_WIDTH, HG_WIDTH, SSM_WIDTH, SSM_CONV_CH, SSM_HEADS)
IN_COLS = sum(IN_SPLITS)
N_EXPERTS = 32
TOP_K = 4
D_FF = 1024
SWIGLU_LIMIT = 7.0
SWIGLU_ALPHA = 1.702
N_MOD = 6

LANES = 128
SUBLANES = 8
VMEM_LIMIT = 56 * 1024 * 1024

ROW_TILE = 512
SEQ_TILE = 512
MOE_ROWS = 256
GATHER_TILE = 128
INVERT_TILE = 1024
GATHER_DEPTH = 3
NEG_BIG = -1e30


def _silu(v):
    return v * jax.nn.sigmoid(v)


def _softplus(v):
    return jnp.maximum(v, 0.0) + jnp.log1p(jnp.exp(-jnp.abs(v)))


def _params(sem):
    return pltpu.CompilerParams(dimension_semantics=sem, vmem_limit_bytes=VMEM_LIMIT)


def _ada_kernel(c_ref, w_ref, b_ref, o_ref):
    o_ref[...] = jnp.dot(_silu(c_ref[...]), w_ref[...], precision=HIGHEST,
                         preferred_element_type=F32) + b_ref[...]


def _ada_mod(c_pad, ada_w, ada_b):
    n = ada_w.shape[1]
    tn = D_MODEL
    return pl.pallas_call(
        _ada_kernel,
        grid=(n // tn,),
        in_specs=[pl.BlockSpec((SUBLANES, D_MODEL), lambda j: (0, 0)),
                  pl.BlockSpec((D_MODEL, tn), lambda j: (0, j)),
                  pl.BlockSpec((1, tn), lambda j: (0, j))],
        out_specs=pl.BlockSpec((SUBLANES, tn), lambda j: (0, j)),
        out_shape=jax.ShapeDtypeStruct((SUBLANES, n), F32),
        compiler_params=_params(("arbitrary",)),
        name="ada_mod",
    )(c_pad, ada_w, ada_b.reshape(1, n))


_OFF = [0]
for _w in IN_SPLITS:
    _OFF.append(_OFF[-1] + _w)
IN_COLS_PAD = _OFF[6] + LANES


def _inproj_kernel(x_ref, nw_ref, sh_ref, sc_ref, w_ref, lbl_ref, dtb_ref,
                   qs_ref, kk_ref, lf_ref, vi_ref, gg_ref, zz_ref, xbc_ref, dt_ref):
    x = x_ref[0]
    ms = jnp.mean(x * x, axis=-1, keepdims=True)
    h = x * lax.rsqrt(ms + EPS) * nw_ref[...]
    h = h * (1.0 + sc_ref[0]) + sh_ref[0]
    hb = h.astype(BF16)

    def seg(k):
        return jnp.dot(hb, w_ref[:, _OFF[k]:_OFF[k] + (IN_SPLITS[k] if k < 6 else LANES)],
                       preferred_element_type=F32)

    lbl = lbl_ref[...]
    le = jnp.exp(lbl - jnp.max(lbl, axis=0, keepdims=True))
    lb = le[0:1, :] / jnp.sum(le, axis=0, keepdims=True)

    qs_ref[0] = _silu(seg(0)).astype(BF16)
    fg = lb + (1.0 - lb) * jax.nn.sigmoid(seg(1))
    kk_ref[0] = (1.0 - fg).astype(BF16)
    lf_ref[0] = jnp.log(fg)
    vi_ref[0] = seg(2).astype(BF16)
    gg_ref[0] = _silu(seg(3)).astype(BF16)
    zz_ref[0] = _silu(seg(4)).astype(BF16)
    xbc_ref[0] = seg(5).astype(BF16)
    dt_ref[0] = _softplus(seg(6) + dtb_ref[...])


def _inproj(x, nw, sh, sc, w_in_pad, lb_logits, dtb_pad):
    bsz, seqlen, d = x.shape
    tm = min(ROW_TILE, seqlen)
    nt = seqlen // tm
    tok = lambda w: pl.BlockSpec((1, tm, w), lambda b, i: (b, i, 0))
    full = lambda a: pl.BlockSpec(a.shape, lambda b, i: (0,) * a.ndim)
    mod = pl.BlockSpec((1, 1, d), lambda b, i: (b, 0, 0))
    shp = lambda w, dt: jax.ShapeDtypeStruct((bsz, seqlen, w), dt)
    return pl.pallas_call(
        _inproj_kernel,
        grid=(bsz, nt),
        in_specs=[tok(d), full(nw), mod, mod, full(w_in_pad), full(lb_logits), full(dtb_pad)],
        out_specs=[tok(HG_QF), tok(HG_QF), tok(HG_QF), tok(HG_WIDTH), tok(HG_WIDTH),
                   tok(SSM_WIDTH), tok(SSM_CONV_CH), tok(LANES)],
        out_shape=[shp(HG_QF, BF16), shp(HG_QF, BF16), shp(HG_QF, F32), shp(HG_WIDTH, BF16),
                   shp(HG_WIDTH, BF16), shp(SSM_WIDTH, BF16), shp(SSM_CONV_CH, BF16),
                   shp(LANES, F32)],
        compiler_params=_params(("arbitrary", "arbitrary")),
        name="inproj",
    )(x, nw, sh, sc, w_in_pad, lb_logits, dtb_pad)


def _hgrn_kernel(qs_ref, kk_ref, lf_ref, vi_ref, gg_ref, nw_ref, o_ref, st_ref):
    @pl.when(pl.program_id(2) == 0)
    def _():
        st_ref[...] = jnp.zeros_like(st_ref)

    c = HG_CHUNK
    rows = lax.broadcasted_iota(jnp.int32, (c, c), 0)
    cols = lax.broadcasted_iota(jnp.int32, (c, c), 1)
    causal = rows >= cols
    tril = causal.astype(F32)
    nw = nw_ref[...]
    tq = qs_ref.shape[1]
    nn = (((1,), (1,)), ((), ()))
    tn = (((0,), (0,)), ((), ()))
    for ci in range(tq // c):
        sl = pl.ds(ci * c, c)
        q = qs_ref[0, sl, :].astype(F32)
        k = kk_ref[0, sl, :].astype(F32)
        v = vi_ref[0, sl, :]
        b = jnp.dot(tril, lf_ref[0, sl, :], precision=HIGHEST, preferred_element_type=F32)
        b_mid = b[c // 2 - 1:c // 2, :]
        b_last = b[c - 1:c, :]
        qa = (q * jnp.exp(b - b_mid)).astype(BF16)
        ka = (k * jnp.exp(b_mid - b)).astype(BF16)
        att = lax.dot_general(qa, ka, nn, preferred_element_type=F32)
        att = jnp.where(causal, att, 0.0).astype(BF16)
        o = jnp.dot(att, v, preferred_element_type=F32)
        st = st_ref[...]
        qb = (q * jnp.exp(b)).astype(BF16)
        o = o + lax.dot_general(qb, st.astype(BF16), nn, preferred_element_type=F32)
        kd = (k * jnp.exp(b_last - b)).astype(BF16)
        st_ref[...] = st * jnp.exp(b_last) + lax.dot_general(v, kd, tn, preferred_element_type=F32)
        ms = jnp.mean(o * o, axis=-1, keepdims=True)
        y = o * lax.rsqrt(ms + EPS) * nw * gg_ref[0, sl, :].astype(F32)
        o_ref[0, sl, :] = y.astype(BF16)


def _hgrn(qs, kk, lf, vi, gg, nw):
    bsz, seqlen, _ = qs.shape
    tq = min(SEQ_TILE, seqlen)
    blk = pl.BlockSpec((1, tq, HG_DK), lambda b, h, i: (b, i, h))
    return pl.pallas_call(
        _hgrn_kernel,
        grid=(bsz, HG_HEADS, seqlen // tq),
        in_specs=[blk, blk, blk, blk, blk, pl.BlockSpec((1, HG_DV), lambda b, h, i: (0, 0))],
        out_specs=blk,
        out_shape=jax.ShapeDtypeStruct((bsz, seqlen, HG_WIDTH), BF16),
        scratch_shapes=[pltpu.VMEM((HG_DV, HG_DK), F32)],
        compiler_params=_params(("arbitrary", "arbitrary", "arbitrary")),
        name="hgrn2",
    )(qs, kk, lf, vi, gg, nw)


def _ssd_kernel(xbc_ref, zz_ref, dt_ref, cw_ref, cb_ref, alog_ref, dsk_ref, nw_ref,
                y_ref, buf_ref, xc_ref, yc_ref, st_ref):
    tq = xbc_ref.shape[1]
    halo = SUBLANES

    @pl.when(pl.program_id(1) == 0)
    def _():
        buf_ref[0:halo, :] = jnp.zeros((halo, SSM_CONV_CH), F32)
        st_ref[...] = jnp.zeros_like(st_ref)

    buf_ref[halo:halo + tq, :] = xbc_ref[0].astype(F32)
    conv = cb_ref[...] + buf_ref[halo - 3:halo - 3 + tq, :] * cw_ref[0:1, :]
    for w in range(1, SSM_CONV):
        conv = conv + buf_ref[halo - 3 + w:halo - 3 + w + tq, :] * cw_ref[w:w + 1, :]
    buf_ref[0:halo, :] = buf_ref[tq:tq + halo, :]
    xc_ref[...] = _silu(conv)

    c = SSM_CHUNK
    p = SSM_HEAD_DIM
    rows = lax.broadcasted_iota(jnp.int32, (c, c), 0)
    cols = lax.broadcasted_iota(jnp.int32, (c, c), 1)
    causal = rows >= cols
    tril = causal.astype(F32)
    a_neg = -jnp.exp(alog_ref[...])
    nn = (((1,), (1,)), ((), ()))
    tn = (((0,), (0,)), ((), ()))
    for ci in range(tq // c):
        sl = pl.ds(ci * c, c)
        dt = dt_ref[0, sl, :]
        acum = jnp.dot(tril, dt * a_neg, precision=HIGHEST, preferred_element_type=F32)
        acum_t = acum.T
        decay_out = jnp.exp(acum)
        decay_in = jnp.exp(acum[c - 1:c, :] - acum)
        for g in range(SSM_GROUPS):
            bm = xc_ref[sl, SSM_WIDTH + g * SSM_STATE:SSM_WIDTH + (g + 1) * SSM_STATE].astype(BF16)
            cm = xc_ref[sl, SSM_WIDTH + (SSM_GROUPS + g) * SSM_STATE:
                        SSM_WIDTH + (SSM_GROUPS + g + 1) * SSM_STATE].astype(BF16)
            cb = lax.dot_general(cm, bm, nn, preferred_element_type=F32)
            st = st_ref[g]
            y_inter = lax.dot_general(cm, st.astype(BF16), nn, preferred_element_type=F32)
            for hh in range(SSM_GROUP_HEADS):
                h = g * SSM_GROUP_HEADS + hh
                lo = h * p
                xdt = xc_ref[sl, lo:lo + p] * dt[:, h:h + 1]
                seg = jnp.exp(jnp.where(causal, acum[:, h:h + 1] - acum_t[h:h + 1, :], -jnp.inf))
                y = jnp.dot((cb * seg).astype(BF16), xdt.astype(BF16), preferred_element_type=F32)
                y = y + decay_out[:, h:h + 1] * y_inter[:, hh * p:(hh + 1) * p]
                yc_ref[:, lo:lo + p] = y
                d_st = lax.dot_general((xdt * decay_in[:, h:h + 1]).astype(BF16), bm, tn,
                                       preferred_element_type=F32)
                a_last = jnp.exp(acum_t[h:h + 1, c - 1:c])
                st_ref[g, hh * p:(hh + 1) * p, :] = st[hh * p:(hh + 1) * p, :] * a_last + d_st
        y = (yc_ref[...] + dsk_ref[...] * xc_ref[sl, 0:SSM_WIDTH]) * zz_ref[0, sl, :].astype(F32)
        for g in range(SSM_GROUPS):
            gs = slice(g * SSM_GROUP_WIDTH, (g + 1) * SSM_GROUP_WIDTH)
            yg = y[:, gs]
            ms = jnp.mean(yg * yg, axis=-1, keepdims=True)
            y_ref[0, sl, gs] = (yg * lax.rsqrt(ms + EPS) * nw_ref[:, gs]).astype(BF16)


def _ssd(xbc, zz, dt, conv_w, conv_b, alog_pad, dsk_wide, nw):
    bsz, seqlen, _ = xbc.shape
    tq = min(SEQ_TILE, seqlen)
    tok = lambda w: pl.BlockSpec((1, tq, w), lambda b, i: (b, i, 0))
    full = lambda a: pl.BlockSpec(a.shape, lambda b, i: (0,) * a.ndim)
    return pl.pallas_call(
        _ssd_kernel,
        grid=(bsz, seqlen // tq),
        in_specs=[tok(SSM_CONV_CH), tok(SSM_WIDTH), tok(LANES), full(conv_w), full(conv_b),
                  full(alog_pad), full(dsk_wide), full(nw)],
        out_specs=tok(SSM_WIDTH),
        out_shape=jax.ShapeDtypeStruct((bsz, seqlen, SSM_WIDTH), BF16),
        scratch_shapes=[pltpu.VMEM((tq + 2 * SUBLANES, SSM_CONV_CH), F32),
                        pltpu.VMEM((tq, SSM_CONV_CH), F32),
                        pltpu.VMEM((SSM_CHUNK, SSM_WIDTH), F32),
                        pltpu.VMEM((SSM_GROUPS, SSM_GROUP_WIDTH, SSM_STATE), F32)],
        compiler_params=_params(("arbitrary", "arbitrary")),
        name="ssd",
    )(xbc, zz, dt, conv_w, conv_b, alog_pad, dsk_wide, nw)


def _outproj_kernel(x_ref, oh_ref, ys_ref, wo_ref, g1_ref, nw_ref, sh_ref, sc_ref, rw_ref, rb_ref,
                    x1_ref, h2_ref, idx_ref, gate_ref, cnt_ref):
    first = jnp.logical_and(pl.program_id(0) == 0, pl.program_id(1) == 0)

    @pl.when(first)
    def _():
        cnt_ref[...] = jnp.zeros_like(cnt_ref)

    mix = jnp.dot(oh_ref[0], wo_ref[0:HG_WIDTH, :], preferred_element_type=F32)
    mix = mix + jnp.dot(ys_ref[0], wo_ref[HG_WIDTH:, :], preferred_element_type=F32)
    x1 = x_ref[0] + g1_ref[0] * mix
    x1_ref[0] = x1
    ms = jnp.mean(x1 * x1, axis=-1, keepdims=True)
    h2 = x1 * lax.rsqrt(ms + EPS) * nw_ref[...]
    h2 = h2 * (1.0 + sc_ref[0]) + sh_ref[0]
    h2_ref[0] = h2
    logits = jnp.dot(h2, rw_ref[...], precision=HIGHEST, preferred_element_type=F32) + rb_ref[...]
    lane = lax.broadcasted_iota(jnp.int32, logits.shape, 1).astype(F32)
    idx_out = jnp.zeros(logits.shape, F32)
    val_out = jnp.zeros(logits.shape, F32)
    sel = jnp.zeros(logits.shape, F32)
    work = logits
    top = None
    denom = None
    for k in range(TOP_K):
        m = jnp.max(work, axis=-1, keepdims=True)
        am = jnp.min(jnp.where(work == m, lane, float(LANES)), axis=-1, keepdims=True)
        hit = lane == am
        work = jnp.where(hit, -jnp.inf, work)
        sel = jnp.where(hit, 1.0, sel)
        if k == 0:
            top = m
        e = jnp.exp(m - top)
        denom = e if k == 0 else denom + e
        idx_out = jnp.where(lane == float(k), am, idx_out)
        val_out = jnp.where(lane == float(k), e, val_out)
    idx_ref[0] = idx_out.astype(jnp.int32)
    gate_ref[0] = val_out / denom
    cnt_ref[0:1, :] += jnp.sum(sel, axis=0, keepdims=True)


def _outproj(x, o_hg, y_ssd, w_out_b, g1, nw, sh, sc, rw_pad, rb_pad):
    bsz, seqlen, d = x.shape
    tm = min(ROW_TILE, seqlen)
    tok = lambda w: pl.BlockSpec((1, tm, w), lambda b, i: (b, i, 0))
    full = lambda a: pl.BlockSpec(a.shape, lambda b, i: (0,) * a.ndim)
    mod = pl.BlockSpec((1, 1, d), lambda b, i: (b, 0, 0))
    shp = lambda w, dt: jax.ShapeDtypeStruct((bsz, seqlen, w), dt)
    return pl.pallas_call(
        _outproj_kernel,
        grid=(bsz, seqlen // tm),
        in_specs=[tok(d), tok(HG_WIDTH), tok(SSM_WIDTH), full(w_out_b), mod, full(nw), mod, mod,
                  full(rw_pad), full(rb_pad)],
        out_specs=[tok(d), tok(d), tok(LANES), tok(LANES),
                   pl.BlockSpec((SUBLANES, LANES), lambda b, i: (0, 0))],
        out_shape=[shp(d, F32), shp(d, F32), shp(LANES, jnp.int32), shp(LANES, F32),
                   jax.ShapeDtypeStruct((SUBLANES, LANES), F32)],
        compiler_params=_params(("arbitrary", "arbitrary")),
        name="outproj_router",
    )(x, o_hg, y_ssd, w_out_b, g1, nw, sh, sc, rw_pad, rb_pad)


def _route_kernel(idx_ref, pstart_ref, dest_ref, carry_ref):
    @pl.when(pl.program_id(0) == 0)
    def _():
        carry_ref[...] = jnp.zeros_like(carry_ref)

    idx = idx_ref[...]
    tt = idx.shape[0]
    lane = lax.broadcasted_iota(jnp.int32, idx.shape, 1)
    hits = [lane == idx[:, k:k + 1] for k in range(TOP_K)]
    sel = jnp.zeros(idx.shape, F32)
    for hit in hits:
        sel = jnp.where(hit, 1.0, sel)
    rows = lax.broadcasted_iota(jnp.int32, (tt, tt), 0)
    cols = lax.broadcasted_iota(jnp.int32, (tt, tt), 1)
    before = (rows > cols).astype(BF16)
    rank = jnp.dot(before, sel.astype(BF16), preferred_element_type=F32) + carry_ref[0:1, :]
    carry_ref[0:1, :] += jnp.sum(sel, axis=0, keepdims=True)
    dense = pstart_ref[...] + rank
    out = jnp.zeros(idx.shape, F32)
    for k, hit in enumerate(hits):
        dk = jnp.sum(jnp.where(hit, dense, 0.0), axis=-1, keepdims=True)
        out = jnp.where(lane == k, dk, out)
    dest_ref[...] = out.astype(jnp.int32)


def _route(idx_pad, pstart_row):
    t = idx_pad.shape[0]
    tt = min(ROW_TILE, t)
    return pl.pallas_call(
        _route_kernel,
        grid=(t // tt,),
        in_specs=[pl.BlockSpec((tt, LANES), lambda i: (i, 0)),
                  pl.BlockSpec((1, LANES), lambda i: (0, 0))],
        out_specs=pl.BlockSpec((tt, LANES), lambda i: (i, 0)),
        out_shape=jax.ShapeDtypeStruct((t, LANES), jnp.int32),
        scratch_shapes=[pltpu.VMEM((SUBLANES, LANES), F32)],
        compiler_params=_params(("arbitrary",)),
        name="route_rank",
    )(idx_pad, pstart_row)


def _invert_kernel(dest_ref, zeros_ref, rowtok_ref, sem):
    i = pl.program_id(0)
    per_step = dest_ref.shape[0] // TOP_K // pl.num_programs(0)

    @pl.when(i == 0)
    def _():
        fill = pltpu.make_async_copy(zeros_ref, rowtok_ref, sem)
        fill.start()
        fill.wait()

    def tok_body(n, carry):
        tok = i * per_step + n
        for k in range(TOP_K):
            rowtok_ref[dest_ref[tok * TOP_K + k]] = tok
        return carry
    lax.fori_loop(0, per_step, tok_body, 0, unroll=8)


def _invert(dest_flat, n_rows):
    t = dest_flat.shape[0] // TOP_K
    return pl.pallas_call(
        _invert_kernel,
        grid=(max(t // INVERT_TILE, 1),),
        in_specs=[pl.BlockSpec(memory_space=pltpu.SMEM), pl.BlockSpec(memory_space=pl.ANY)],
        out_specs=pl.BlockSpec(memory_space=pltpu.SMEM),
        out_shape=jax.ShapeDtypeStruct((n_rows,), jnp.int32),
        scratch_shapes=[pltpu.SemaphoreType.DMA(())],
        compiler_params=_params(("arbitrary",)),
        name="invert_route",
    )(dest_flat, jnp.zeros((n_rows,), jnp.int32))


def _expert_kernel(be_ref, nact_ref, rowtok_ref, h_ref, w1_ref, b1_ref, w2_ref, b2_ref, y_ref,
                   xbuf0_ref, xbuf1_ref, xbuf2_ref, w1b_ref, w2b_ref, sem):
    j = pl.program_id(0)
    n_act = nact_ref[0]
    active = j < n_act
    fresh = jnp.logical_or(j == 0, be_ref[j] != be_ref[jnp.maximum(j - 1, 0)])
    phase = j % GATHER_DEPTH
    bufs = (xbuf0_ref, xbuf1_ref, xbuf2_ref)

    def issue(block, s):
        base = jnp.minimum(block, n_act - 1) * MOE_ROWS
        for r in range(MOE_ROWS):
            pltpu.make_async_copy(h_ref.at[pl.ds(rowtok_ref[base + r], 1), :],
                                  bufs[s].at[pl.ds(r, 1), :], sem.at[s]).start(priority=1)

    def wait(s):
        pltpu.make_async_copy(h_ref.at[pl.ds(0, MOE_ROWS), :], bufs[s], sem.at[s]).wait()

    @pl.when(j == 0)
    def _():
        for s in range(GATHER_DEPTH - 1):
            issue(s, s)

    @pl.when(jnp.logical_and(active, fresh))
    def _():
        w1b_ref[...] = w1_ref[0].astype(BF16)
        w2b_ref[...] = w2_ref[0].astype(BF16)

    for s in range(GATHER_DEPTH):
        @pl.when(jnp.logical_and(j < n_act + GATHER_DEPTH - 1, phase == s))
        def _(s=s):
            wait(s)

        @pl.when(jnp.logical_and(active, phase == s))
        def _(s=s):
            issue(j + GATHER_DEPTH - 1, (s + GATHER_DEPTH - 1) % GATHER_DEPTH)
            hb = jnp.dot(bufs[s][...].astype(BF16), w1b_ref[...], preferred_element_type=F32) + b1_ref[0]
            glu = jnp.minimum(hb[:, :D_FF], SWIGLU_LIMIT)
            lin = jnp.clip(hb[:, D_FF:], -SWIGLU_LIMIT, SWIGLU_LIMIT)
            act = glu * jax.nn.sigmoid(SWIGLU_ALPHA * glu) * (lin + 1.0)
            y_ref[...] = jnp.dot(act.astype(BF16), w2b_ref[...], preferred_element_type=F32) + b2_ref[0]


def _experts(block_e, n_act, row_tok, h2, w1, b1, w2, b2):
    n_rows = row_tok.shape[0]
    d = h2.shape[1]
    nb = block_e.shape[0]
    row_map = lambda j, be, na, rt: (jnp.maximum(jnp.minimum(j, na[0] - 1), 0), 0)
    exp_map = lambda j, be, na, rt: (be[j], 0, 0)
    return pl.pallas_call(
        _expert_kernel,
        grid_spec=pltpu.PrefetchScalarGridSpec(
            num_scalar_prefetch=3,
            grid=(nb,),
            in_specs=[pl.BlockSpec(memory_space=pl.ANY),
                      pl.BlockSpec((1, d, 2 * D_FF), exp_map),
                      pl.BlockSpec((1, 1, 2 * D_FF), exp_map),
                      pl.BlockSpec((1, D_FF, d), exp_map),
                      pl.BlockSpec((1, 1, d), exp_map)],
            out_specs=pl.BlockSpec((MOE_ROWS, d), row_map),
            scratch_shapes=[pltpu.VMEM((MOE_ROWS, d), F32)] * GATHER_DEPTH + [
                pltpu.VMEM((d, 2 * D_FF), BF16), pltpu.VMEM((D_FF, d), BF16),
                pltpu.SemaphoreType.DMA((GATHER_DEPTH,))],
        ),
        out_shape=jax.ShapeDtypeStruct((n_rows, d), F32),
        compiler_params=_params(("arbitrary",)),
        name="expert_ffn",
    )(block_e, n_act, row_tok, h2, w1, b1.reshape(N_EXPERTS, 1, 2 * D_FF), w2, b2.reshape(N_EXPERTS, 1, d))


def _combine_kernel(dest_ref, x1_ref, gate_ref, g2_ref, fw_ref, y_ref, o_ref, buf_ref, sem):
    i = pl.program_id(0)
    n = pl.num_programs(0)
    gt = x1_ref.shape[0]

    def row_copy(slot, k, j, src_row):
        return pltpu.make_async_copy(y_ref.at[pl.ds(src_row, 1), :],
                                     buf_ref.at[slot, k, pl.ds(j, 1), :], sem.at[slot])

    def issue(tile, slot):
        base = tile * (gt * TOP_K)
        for j in range(gt):
            for k in range(TOP_K):
                row_copy(slot, k, j, dest_ref[base + j * TOP_K + k]).start(priority=k % 2)

    @pl.when(i == 0)
    def _():
        issue(0, 0)

    for s in range(2):
        @pl.when(jnp.logical_and(i + 1 < n, (i + 1) % 2 == s))
        def _(s=s):
            issue(i + 1, s)

    slot = i % 2

    for k in range(TOP_K):
        pltpu.make_async_copy(y_ref.at[pl.ds(0, gt), :], buf_ref.at[slot, k], sem.at[slot]).wait()

    gates = gate_ref[...]
    moe = gates[:, 0:1] * buf_ref[slot, 0]
    for k in range(1, TOP_K):
        moe = moe + gates[:, k:k + 1] * buf_ref[slot, k]
    x2 = x1_ref[...] + g2_ref[0] * moe
    ms = jnp.mean(x2 * x2, axis=-1, keepdims=True)
    o_ref[...] = x2 * lax.rsqrt(ms + EPS) * fw_ref[...]


def _combine(dest_flat, x1, gate_pad, g2, fw, yout, seqlen):
    t, d = x1.shape
    gt = min(GATHER_TILE, seqlen)
    per_seq = seqlen // gt
    return pl.pallas_call(
        _combine_kernel,
        grid_spec=pltpu.PrefetchScalarGridSpec(
            num_scalar_prefetch=1,
            grid=(t // gt,),
            in_specs=[pl.BlockSpec((gt, d), lambda i, dst: (i, 0)),
                      pl.BlockSpec((gt, LANES), lambda i, dst: (i, 0)),
                      pl.BlockSpec((1, 1, d), lambda i, dst: (i // per_seq, 0, 0)),
                      pl.BlockSpec((1, d), lambda i, dst: (0, 0)),
                      pl.BlockSpec(memory_space=pl.ANY)],
            out_specs=pl.BlockSpec((gt, d), lambda i, dst: (i, 0)),
            scratch_shapes=[pltpu.VMEM((2, TOP_K, gt, d), F32), pltpu.SemaphoreType.DMA((2,))],
        ),
        out_shape=jax.ShapeDtypeStruct((t, d), F32),
        compiler_params=_params(("arbitrary",)),
        name="combine_norm",
    )(dest_flat, x1, gate_pad, g2, fw, yout)


def _pad_lanes(v, fill=0.0):
    v = v.reshape(1, -1).astype(F32)
    return jnp.pad(v, ((0, 0), (0, LANES - v.shape[1])), constant_values=fill)


def kernel(x, c, ada_w, ada_b, norm1_w, w_in, hg_lb_logits, hg_norm_w, ssm_a_log, ssm_dt_bias, ssm_d,
           ssm_conv_w, ssm_conv_b, ssm_norm_w, w_out, norm2_w, router_w, router_b, exp_w1, exp_b1,
           exp_w2, exp_b2, final_norm_w):
    bsz, seqlen, d = x.shape
    t = bsz * seqlen
    l = 0

    c_pad = jnp.pad(c, ((0, SUBLANES - bsz), (0, 0)))
    mod = _ada_mod(c_pad, ada_w[l], ada_b[l])[:bsz]
    sh1, sc1, g1, sh2, sc2, g2 = [m.reshape(bsz, 1, d) for m in jnp.split(mod, N_MOD, axis=-1)]

    w_in_pad = jnp.pad(w_in[l], ((0, 0), (0, IN_COLS_PAD - IN_COLS))).astype(BF16)
    qs, kk, lf, vi, gg, zz, xbc, dt = _inproj(
        x, norm1_w[l].reshape(1, d), sh1, sc1, w_in_pad, hg_lb_logits, _pad_lanes(ssm_dt_bias[l]))

    o_hg = _hgrn(qs, kk, lf, vi, gg, hg_norm_w[l].reshape(1, HG_DV))
    y_ssd = _ssd(xbc, zz, dt, ssm_conv_w[l], ssm_conv_b[l].reshape(1, SSM_CONV_CH),
                 _pad_lanes(ssm_a_log[l]), jnp.repeat(ssm_d[l], SSM_HEAD_DIM).reshape(1, SSM_WIDTH),
                 ssm_norm_w[l].reshape(1, SSM_WIDTH))

    rw_pad = jnp.pad(router_w[l], ((0, 0), (0, LANES - N_EXPERTS)))
    rb_pad = _pad_lanes(router_b[l], NEG_BIG)
    x1, h2, idx_pad, gate_pad, counts = _outproj(
        x, o_hg, y_ssd, w_out[l].astype(BF16), g1, norm2_w[l].reshape(1, d), sh2, sc2, rw_pad, rb_pad)

    cnt = counts[0, :N_EXPERTS].astype(jnp.int32)
    blocks_e = (cnt + MOE_ROWS - 1) // MOE_ROWS
    blk_end = jnp.cumsum(blocks_e)
    pstart = (blk_end - blocks_e) * MOE_ROWS
    n_blocks = (t * TOP_K) // MOE_ROWS + N_EXPERTS
    n_rows = n_blocks * MOE_ROWS
    n_steps = n_blocks + GATHER_DEPTH - 1
    block_e = jnp.minimum(jnp.sum(blk_end[None, :] <= jnp.arange(n_steps)[:, None], axis=1),
                          N_EXPERTS - 1).astype(jnp.int32)
    n_act = blk_end[-1:].astype(jnp.int32)

    dest_pad = _route(idx_pad.reshape(t, LANES), _pad_lanes(pstart))
    dest_flat = dest_pad[:, :TOP_K].reshape(t * TOP_K)

    row_tok = _invert(dest_flat, n_rows)
    yout = _experts(block_e, n_act, row_tok, h2.reshape(t, d), exp_w1[l], exp_b1[l], exp_w2[l], exp_b2[l])
    out = _combine(dest_flat, x1.reshape(t, d), gate_pad.reshape(t, LANES), g2,
                   final_norm_w.reshape(1, d), yout, seqlen)
    return out.reshape(bsz, seqlen, d)
```

```python
import functools

import jax
import jax.numpy as jnp
from jax import lax
from jax.experimental import pallas as pl
from jax.experimental.pallas import tpu as pltpu

F32 = jnp.float32
BF16 = jnp.bfloat16
HIGHEST = lax.Precision.HIGHEST

EPS = 1e-6
D_MODEL = 1024
HG_HEADS = 4
HG_DK = 128
HG_DV = 128
HG_QF = HG_HEADS * HG_DK
HG_WIDTH = HG_HEADS * HG_DV
HG_CHUNK = 64
SSM_HEADS = 8
SSM_HEAD_DIM = 64
SSM_WIDTH = SSM_HEADS * SSM_HEAD_DIM
SSM_GROUPS = 2
SSM_GROUP_HEADS = SSM_HEADS // SSM_GROUPS
SSM_GROUP_WIDTH = SSM_WIDTH // SSM_GROUPS
SSM_STATE = 128
SSM_CONV = 4
SSM_CONV_CH = SSM_WIDTH + 2 * SSM_GROUPS * SSM_STATE
SSM_CHUNK = 128
IN_SPLITS = (HG_QF, HG_QF, HG_WIDTH, HG_WIDTH, SSM_WIDTH, SSM_CONV_CH, SSM_HEADS)
IN_COLS = sum(IN_SPLITS)
N_EXPERTS = 32
TOP_K = 4
D_FF = 1024
SWIGLU_LIMIT = 7.0
SWIGLU_ALPHA = 1.702
N_MOD = 6

LANES = 128
SUBLANES = 8
VMEM_LIMIT = 56 * 1024 * 1024

ROW_TILE = 512
SEQ_TILE = 512
MOE_ROWS = 256
GATHER_TILE = 128
INVERT_TILE = 1024
GATHER_DEPTH = 3
NEG_BIG = -1e30


def _silu(v):
    return v * jax.nn.sigmoid(v)


def _softplus(v):
    return jnp.maximum(v, 0.0) + jnp.log1p(jnp.exp(-jnp.abs(v)))


def _params(sem):
    return pltpu.CompilerParams(dimension_semantics=sem, vmem_limit_bytes=VMEM_LIMIT)


def _ada_kernel(c_ref, w_ref, b_ref, o_ref):
    o_ref[...] = jnp.dot(_silu(c_ref[...]), w_ref[...], precision=HIGHEST,
                         preferred_element_type=F32) + b_ref[...]


def _ada_mod(c_pad, ada_w, ada_b):
    n = ada_w.shape[1]
    tn = D_MODEL
    return pl.pallas_call(
        _ada_kernel,
        grid=(n // tn,),
        in_specs=[pl.BlockSpec((SUBLANES, D_MODEL), lambda j: (0, 0)),
                  pl.BlockSpec((D_MODEL, tn), lambda j: (0, j)),
                  pl.BlockSpec((1, tn), lambda j: (0, j))],
        out_specs=pl.BlockSpec((SUBLANES, tn), lambda j: (0, j)),
        out_shape=jax.ShapeDtypeStruct((SUBLANES, n), F32),
        compiler_params=_params(("arbitrary",)),
        name="ada_mod",
    )(c_pad, ada_w, ada_b.reshape(1, n))


_OFF = [0]
for _w in IN_SPLITS:
    _OFF.append(_OFF[-1] + _w)
IN_COLS_PAD = _OFF[6] + LANES


def _inproj_kernel(x_ref, nw_ref, sh_ref, sc_ref, w_ref, lbl_ref, dtb_ref,
                   qs_ref, kk_ref, lf_ref, vi_ref, gg_ref, zz_ref, xbc_ref, dt_ref):
    x = x_ref[0]
    ms = jnp.mean(x * x, axis=-1, keepdims=True)
    h = x * lax.rsqrt(ms + EPS) * nw_ref[...]
    h = h * (1.0 + sc_ref[0]) + sh_ref[0]
    hb = h.astype(BF16)

    def seg(k):
        return jnp.dot(hb, w_ref[:, _OFF[k]:_OFF[k] + (IN_SPLITS[k] if k < 6 else LANES)],
                       preferred_element_type=F32)

    lbl = lbl_ref[...]
    le = jnp.exp(lbl - jnp.max(lbl, axis=0, keepdims=True))
    lb = le[0:1, :] / jnp.sum(le, axis=0, keepdims=True)

    qs_ref[0] = _silu(seg(0)).astype(BF16)
    fg = lb + (1.0 - lb) * jax.nn.sigmoid(seg(1))
    kk_ref[0] = (1.0 - fg).astype(BF16)
    lf_ref[0] = jnp.log(fg)
    vi_ref[0] = seg(2).astype(BF16)
    gg_ref[0] = _silu(seg(3)).astype(BF16)
    zz_ref[0] = _silu(seg(4)).astype(BF16)
    xbc_ref[0] = seg(5).astype(BF16)
    dt_ref[0] = _softplus(seg(6) + dtb_ref[...])


def _inproj(x, nw, sh, sc, w_in_pad, lb_logits, dtb_pad):
    bsz, seqlen, d = x.shape
    tm = min(ROW_TILE, seqlen)
    nt = seqlen // tm
    tok = lambda w: pl.BlockSpec((1, tm, w), lambda b, i: (b, i, 0))
    full = lambda a: pl.BlockSpec(a.shape, lambda b, i: (0,) * a.ndim)
    mod = pl.BlockSpec((1, 1, d), lambda b, i: (b, 0, 0))
    shp = lambda w, dt: jax.ShapeDtypeStruct((bsz, seqlen, w), dt)
    return pl.pallas_call(
        _inproj_kernel,
        grid=(bsz, nt),
        in_specs=[tok(d), full(nw), mod, mod, full(w_in_pad), full(lb_logits), full(dtb_pad)],
        out_specs=[tok(HG_QF), tok(HG_QF), tok(HG_QF), tok(HG_WIDTH), tok(HG_WIDTH),
                   tok(SSM_WIDTH), tok(SSM_CONV_CH), tok(LANES)],
        out_shape=[shp(HG_QF, BF16), shp(HG_QF, BF16), shp(HG_QF, F32), shp(HG_WIDTH, BF16),
                   shp(HG_WIDTH, BF16), shp(SSM_WIDTH, BF16), shp(SSM_CONV_CH, BF16),
                   shp(LANES, F32)],
        compiler_params=_params(("arbitrary", "arbitrary")),
        name="inproj",
    )(x, nw, sh, sc, w_in_pad, lb_logits, dtb_pad)


def _hgrn_kernel(qs_ref, kk_ref, lf_ref, vi_ref, gg_ref, nw_ref, o_ref, st_ref):
    @pl.when(pl.program_id(2) == 0)
    def _():
        st_ref[...] = jnp.zeros_like(st_ref)

    c = HG_CHUNK
    tq = qs_ref.shape[1]
    nc = tq // c
    rows = lax.broadcasted_iota(jnp.int32, (nc, c, c), 1)
    cols = lax.broadcasted_iota(jnp.int32, (nc, c, c), 2)
    causal = rows >= cols
    tril = causal.astype(F32)
    chunked = lambda ref: ref[0].reshape(nc, c, ref.shape[2])
    q = chunked(qs_ref).astype(F32)
    k = chunked(kk_ref).astype(F32)
    v = chunked(vi_ref)
    bmm = lambda a, bb, dims, **kw: lax.dot_general(a, bb, (dims, ((0,), (0,))),
                                                    preferred_element_type=F32, **kw)
    b = bmm(tril, chunked(lf_ref), ((2,), (1,)), precision=HIGHEST)
    b_mid = b[:, c // 2 - 1:c // 2, :]
    b_last = b[:, c - 1:c, :]
    qa = (q * jnp.exp(b - b_mid)).astype(BF16)
    ka = (k * jnp.exp(b_mid - b)).astype(BF16)
    att = bmm(qa, ka, ((2,), (2,)))
    att = jnp.where(causal, att, 0.0).astype(BF16)
    o = bmm(att, v, ((2,), (1,)))
    kd = (k * jnp.exp(b_last - b)).astype(BF16)
    d_st = bmm(v, kd, ((1,), (1,)))
    decay = jnp.exp(b_last)
    st = st_ref[...]
    entering = []
    for ci in range(nc):
        entering.append(st.astype(BF16))
        st = st * decay[ci] + d_st[ci]
    st_ref[...] = st
    qb = (q * jnp.exp(b)).astype(BF16)
    o = o + bmm(qb, jnp.stack(entering), ((2,), (2,)))
    ms = jnp.mean(o * o, axis=-1, keepdims=True)
    y = o * lax.rsqrt(ms + EPS) * nw_ref[...] * chunked(gg_ref).astype(F32)
    o_ref[0] = y.reshape(tq, o_ref.shape[2]).astype(BF16)


def _hgrn(qs, kk, lf, vi, gg, nw):
    bsz, seqlen, _ = qs.shape
    tq = min(SEQ_TILE, seqlen)
    blk = pl.BlockSpec((1, tq, HG_DK), lambda b, h, i: (b, i, h))
    return pl.pallas_call(
        _hgrn_kernel,
        grid=(bsz, HG_HEADS, seqlen // tq),
        in_specs=[blk, blk, blk, blk, blk, pl.BlockSpec((1, HG_DV), lambda b, h, i: (0, 0))],
        out_specs=blk,
        out_shape=jax.ShapeDtypeStruct((bsz, seqlen, HG_WIDTH), BF16),
        scratch_shapes=[pltpu.VMEM((HG_DV, HG_DK), F32)],
        compiler_params=_params(("arbitrary", "arbitrary", "arbitrary")),
        name="hgrn2",
    )(qs, kk, lf, vi, gg, nw)


def _ssd_kernel(xbc_ref, zz_ref, dt_ref, cw_ref, cb_ref, alog_ref, dsk_ref, nw_ref,
                y_ref, buf_ref, xc_ref, yc_ref, st_ref):
    tq = xbc_ref.shape[1]
    halo = SUBLANES

    @pl.when(pl.program_id(1) == 0)
    def _():
        buf_ref[0:halo, :] = jnp.zeros((halo, SSM_CONV_CH), F32)
        st_ref[...] = jnp.zeros_like(st_ref)

    buf_ref[halo:halo + tq, :] = xbc_ref[0].astype(F32)
    conv = cb_ref[...] + buf_ref[halo - 3:halo - 3 + tq, :] * cw_ref[0:1, :]
    for w in range(1, SSM_CONV):
        conv = conv + buf_ref[halo - 3 + w:halo - 3 + w + tq, :] * cw_ref[w:w + 1, :]
    buf_ref[0:halo, :] = buf_ref[tq:tq + halo, :]
    xc_ref[...] = _silu(conv)

    c = SSM_CHUNK
    p = SSM_HEAD_DIM
    nc = tq // c
    rows = lax.broadcasted_iota(jnp.int32, (nc, c, c), 1)
    cols = lax.broadcasted_iota(jnp.int32, (nc, c, c), 2)
    causal = rows >= cols
    tril = causal.astype(F32)
    triu = (rows <= cols).astype(F32)
    bmm = lambda a, bb, dims, **kw: lax.dot_general(a, bb, (dims, ((0,), (0,))),
                                                    preferred_element_type=F32, **kw)
    chunked = lambda val: val.reshape(nc, c, val.shape[-1])
    dt = chunked(dt_ref[0])
    da = dt * (-jnp.exp(alog_ref[...]))
    acum = bmm(tril, da, ((2,), (1,)), precision=HIGHEST)
    acum_t = bmm(da, triu, ((1,), (1,)), precision=HIGHEST)
    decay_out = jnp.exp(acum)
    decay_in = jnp.exp(acum[:, c - 1:c, :] - acum)
    for g in range(SSM_GROUPS):
        bm = chunked(xc_ref[:, SSM_WIDTH + g * SSM_STATE:SSM_WIDTH + (g + 1) * SSM_STATE]).astype(BF16)
        cm = chunked(xc_ref[:, SSM_WIDTH + (SSM_GROUPS + g) * SSM_STATE:
                            SSM_WIDTH + (SSM_GROUPS + g + 1) * SSM_STATE]).astype(BF16)
        cb = bmm(cm, bm, ((2,), (2,)))
        y_intra, d_st, a_last = [], [], []
        for hh in range(SSM_GROUP_HEADS):
            h = g * SSM_GROUP_HEADS + hh
            xdt = chunked(xc_ref[:, h * p:(h + 1) * p]) * dt[:, :, h:h + 1]
            seg = jnp.exp(jnp.where(causal, acum[:, :, h:h + 1] - acum_t[:, h:h + 1, :], -jnp.inf))
            y_intra.append(bmm((cb * seg).astype(BF16), xdt.astype(BF16), ((2,), (1,))))
            d_st.append(bmm((xdt * decay_in[:, :, h:h + 1]).astype(BF16), bm, ((1,), (1,))))
            a_last.append(jnp.exp(acum_t[:, h:h + 1, c - 1:c]))
        st = [st_ref[g, hh * p:(hh + 1) * p, :] for hh in range(SSM_GROUP_HEADS)]
        entering = []
        for ci in range(nc):
            entering.append(jnp.concatenate(st, axis=0).astype(BF16))
            st = [st[hh] * a_last[hh][ci] + d_st[hh][ci] for hh in range(SSM_GROUP_HEADS)]
        for hh in range(SSM_GROUP_HEADS):
            st_ref[g, hh * p:(hh + 1) * p, :] = st[hh]
        y_inter = bmm(cm, jnp.stack(entering), ((2,), (2,)))
        for hh in range(SSM_GROUP_HEADS):
            h = g * SSM_GROUP_HEADS + hh
            y = y_intra[hh] + decay_out[:, :, h:h + 1] * y_inter[:, :, hh * p:(hh + 1) * p]
            yc_ref[:, h * p:(h + 1) * p] = y.reshape(tq, p)
    y = (yc_ref[...] + dsk_ref[...] * xc_ref[:, 0:SSM_WIDTH]) * zz_ref[0].astype(F32)
    for g in range(SSM_GROUPS):
        gs = slice(g * SSM_GROUP_WIDTH, (g + 1) * SSM_GROUP_WIDTH)
        yg = y[:, gs]
        ms = jnp.mean(yg * yg, axis=-1, keepdims=True)
        y_ref[0, :, gs] = (yg * lax.rsqrt(ms + EPS) * nw_ref[:, gs]).astype(BF16)


def _ssd(xbc, zz, dt, conv_w, conv_b, alog_pad, dsk_wide, nw):
    bsz, seqlen, _ = xbc.shape
    tq = min(SEQ_TILE, seqlen)
    tok = lambda w: pl.BlockSpec((1, tq, w), lambda b, i: (b, i, 0))
    full = lambda a: pl.BlockSpec(a.shape, lambda b, i: (0,) * a.ndim)
    return pl.pallas_call(
        _ssd_kernel,
        grid=(bsz, seqlen // tq),
        in_specs=[tok(SSM_CONV_CH), tok(SSM_WIDTH), tok(LANES), full(conv_w), full(conv_b),
                  full(alog_pad), full(dsk_wide), full(nw)],
        out_specs=tok(SSM_WIDTH),
        out_shape=jax.ShapeDtypeStruct((bsz, seqlen, SSM_WIDTH), BF16),
        scratch_shapes=[pltpu.VMEM((tq + 2 * SUBLANES, SSM_CONV_CH), F32),
                        pltpu.VMEM((tq, SSM_CONV_CH), F32),
                        pltpu.VMEM((tq, SSM_WIDTH), F32),
                        pltpu.VMEM((SSM_GROUPS, SSM_GROUP_WIDTH, SSM_STATE), F32)],
        compiler_params=_params(("arbitrary", "arbitrary")),
        name="ssd",
    )(xbc, zz, dt, conv_w, conv_b, alog_pad, dsk_wide, nw)


def _outproj_kernel(x_ref, oh_ref, ys_ref, wo_ref, g1_ref, nw_ref, sh_ref, sc_ref, rw_ref, rb_ref,
                    x1_ref, h2_ref, idx_ref, gate_ref, cnt_ref):
    first = jnp.logical_and(pl.program_id(0) == 0, pl.program_id(1) == 0)

    @pl.when(first)
    def _():
        cnt_ref[...] = jnp.zeros_like(cnt_ref)

    mix = jnp.dot(oh_ref[0], wo_ref[0:HG_WIDTH, :], preferred_element_type=F32)
    mix = mix + jnp.dot(ys_ref[0], wo_ref[HG_WIDTH:, :], preferred_element_type=F32)
    x1 = x_ref[0] + g1_ref[0] * mix
    x1_ref[0] = x1
    ms = jnp.mean(x1 * x1, axis=-1, keepdims=True)
    h2 = x1 * lax.rsqrt(ms + EPS) * nw_ref[...]
    h2 = h2 * (1.0 + sc_ref[0]) + sh_ref[0]
    h2_ref[0] = h2
    logits = jnp.dot(h2, rw_ref[...], precision=HIGHEST, preferred_element_type=F32) + rb_ref[...]
    lane = lax.broadcasted_iota(jnp.int32, logits.shape, 1).astype(F32)
    idx_out = jnp.zeros(logits.shape, F32)
    val_out = jnp.zeros(logits.shape, F32)
    sel = jnp.zeros(logits.shape, F32)
    work = logits
    top = None
    denom = None
    for k in range(TOP_K):
        m = jnp.max(work, axis=-1, keepdims=True)
        am = jnp.min(jnp.where(work == m, lane, float(LANES)), axis=-1, keepdims=True)
        hit = lane == am
        work = jnp.where(hit, -jnp.inf, work)
        sel = jnp.where(hit, 1.0, sel)
        if k == 0:
            top = m
        e = jnp.exp(m - top)
        denom = e if k == 0 else denom + e
        idx_out = jnp.where(lane == float(k), am, idx_out)
        val_out = jnp.where(lane == float(k), e, val_out)
    idx_ref[0] = idx_out.astype(jnp.int32)
    gate_ref[0] = val_out / denom
    cnt_ref[0:1, :] += jnp.sum(sel, axis=0, keepdims=True)


def _outproj(x, o_hg, y_ssd, w_out_b, g1, nw, sh, sc, rw_pad, rb_pad):
    bsz, seqlen, d = x.shape
    tm = min(ROW_TILE, seqlen)
    tok = lambda w: pl.BlockSpec((1, tm, w), lambda b, i: (b, i, 0))
    full = lambda a: pl.BlockSpec(a.shape, lambda b, i: (0,) * a.ndim)
    mod = pl.BlockSpec((1, 1, d), lambda b, i: (b, 0, 0))
    shp = lambda w, dt: jax.ShapeDtypeStruct((bsz, seqlen, w), dt)
    return pl.pallas_call(
        _outproj_kernel,
        grid=(bsz, seqlen // tm),
        in_specs=[tok(d), tok(HG_WIDTH), tok(SSM_WIDTH), full(w_out_b), mod, full(nw), mod, mod,
                  full(rw_pad), full(rb_pad)],
        out_specs=[tok(d), tok(d), tok(LANES), tok(LANES),
                   pl.BlockSpec((SUBLANES, LANES), lambda b, i: (0, 0))],
        out_shape=[shp(d, F32), shp(d, F32), shp(LANES, jnp.int32), shp(LANES, F32),
                   jax.ShapeDtypeStruct((SUBLANES, LANES), F32)],
        compiler_params=_params(("arbitrary", "arbitrary")),
        name="outproj_router",
    )(x, o_hg, y_ssd, w_out_b, g1, nw, sh, sc, rw_pad, rb_pad)


def _route_kernel(idx_ref, pstart_ref, dest_ref, carry_ref):
    @pl.when(pl.program_id(0) == 0)
    def _():
        carry_ref[...] = jnp.zeros_like(carry_ref)

    idx = idx_ref[...]
    tt = idx.shape[0]
    lane = lax.broadcasted_iota(jnp.int32, idx.shape, 1)
    hits = [lane == idx[:, k:k + 1] for k in range(TOP_K)]
    sel = jnp.zeros(idx.shape, F32)
    for hit in hits:
        sel = jnp.where(hit, 1.0, sel)
    rows = lax.broadcasted_iota(jnp.int32, (tt, tt), 0)
    cols = lax.broadcasted_iota(jnp.int32, (tt, tt), 1)
    before = (rows > cols).astype(BF16)
    rank = jnp.dot(before, sel.astype(BF16), preferred_element_type=F32) + carry_ref[0:1, :]
    carry_ref[0:1, :] += jnp.sum(sel, axis=0, keepdims=True)
    dense = pstart_ref[...] + rank
    out = jnp.zeros(idx.shape, F32)
    for k, hit in enumerate(hits):
        dk = jnp.sum(jnp.where(hit, dense, 0.0), axis=-1, keepdims=True)
        out = jnp.where(lane == k, dk, out)
    dest_ref[...] = out.astype(jnp.int32)


def _route(idx_pad, pstart_row):
    t = idx_pad.shape[0]
    tt = min(ROW_TILE, t)
    return pl.pallas_call(
        _route_kernel,
        grid=(t // tt,),
        in_specs=[pl.BlockSpec((tt, LANES), lambda i: (i, 0)),
                  pl.BlockSpec((1, LANES), lambda i: (0, 0))],
        out_specs=pl.BlockSpec((tt, LANES), lambda i: (i, 0)),
        out_shape=jax.ShapeDtypeStruct((t, LANES), jnp.int32),
        scratch_shapes=[pltpu.VMEM((SUBLANES, LANES), F32)],
        compiler_params=_params(("arbitrary",)),
        name="route_rank",
    )(idx_pad, pstart_row)


def _invert_kernel(dest_ref, zeros_ref, rowtok_ref, sem):
    i = pl.program_id(0)
    per_step = dest_ref.shape[0] // TOP_K // pl.num_programs(0)

    @pl.when(i == 0)
    def _():
        fill = pltpu.make_async_copy(zeros_ref, rowtok_ref, sem)
        fill.start()
        fill.wait()

    def tok_body(n, carry):
        tok = i * per_step + n
        for k in range(TOP_K):
            rowtok_ref[dest_ref[tok * TOP_K + k]] = tok
        return carry
    lax.fori_loop(0, per_step, tok_body, 0, unroll=8)


def _invert(dest_flat, n_rows):
    t = dest_flat.shape[0] // TOP_K
    return pl.pallas_call(
        _invert_kernel,
        grid=(max(t // INVERT_TILE, 1),),
        in_specs=[pl.BlockSpec(memory_space=pltpu.SMEM), pl.BlockSpec(memory_space=pl.ANY)],
        out_specs=pl.BlockSpec(memory_space=pltpu.SMEM),
        out_shape=jax.ShapeDtypeStruct((n_rows,), jnp.int32),
        scratch_shapes=[pltpu.SemaphoreType.DMA(())],
        compiler_params=_params(("arbitrary",)),
        name="invert_route",
    )(dest_flat, jnp.zeros((n_rows,), jnp.int32))


def _expert_kernel(be_ref, nact_ref, rowtok_ref, h_ref, w1_ref, b1_ref, w2_ref, b2_ref, y_ref,
                   xbuf0_ref, xbuf1_ref, xbuf2_ref, w1b_ref, w2b_ref, sem):
    j = pl.program_id(0)
    n_act = nact_ref[0]
    active = j < n_act
    fresh = jnp.logical_or(j == 0, be_ref[j] != be_ref[jnp.maximum(j - 1, 0)])
    phase = j % GATHER_DEPTH
    bufs = (xbuf0_ref, xbuf1_ref, xbuf2_ref)

    def issue(block, s):
        base = jnp.minimum(block, n_act - 1) * MOE_ROWS
        for r in range(MOE_ROWS):
            pltpu.make_async_copy(h_ref.at[pl.ds(rowtok_ref[base + r], 1), :],
                                  bufs[s].at[pl.ds(r, 1), :], sem.at[s]).start(priority=1)

    def wait(s):
        pltpu.make_async_copy(h_ref.at[pl.ds(0, MOE_ROWS), :], bufs[s], sem.at[s]).wait()

    @pl.when(j == 0)
    def _():
        for s in range(GATHER_DEPTH - 1):
            issue(s, s)

    @pl.when(jnp.logical_and(active, fresh))
    def _():
        w1b_ref[...] = w1_ref[0].astype(BF16)
        w2b_ref[...] = w2_ref[0].astype(BF16)

    for s in range(GATHER_DEPTH):
        @pl.when(jnp.logical_and(j < n_act + GATHER_DEPTH - 1, phase == s))
        def _(s=s):
            wait(s)

        @pl.when(jnp.logical_and(active, phase == s))
        def _(s=s):
            issue(j + GATHER_DEPTH - 1, (s + GATHER_DEPTH - 1) % GATHER_DEPTH)
            hb = jnp.dot(bufs[s][...].astype(BF16), w1b_ref[...], preferred_element_type=F32) + b1_ref[0]
            glu = jnp.minimum(hb[:, :D_FF], SWIGLU_LIMIT)
            lin = jnp.clip(hb[:, D_FF:], -SWIGLU_LIMIT, SWIGLU_LIMIT)
            act = glu * jax.nn.sigmoid(SWIGLU_ALPHA * glu) * (lin + 1.0)
            y_ref[...] = jnp.dot(act.astype(BF16), w2b_ref[...], preferred_element_type=F32) + b2_ref[0]


def _experts(block_e, n_act, row_tok, h2, w1, b1, w2, b2):
    n_rows = row_tok.shape[0]
    d = h2.shape[1]
    nb = block_e.shape[0]
    row_map = lambda j, be, na, rt: (jnp.maximum(jnp.minimum(j, na[0] - 1), 0), 0)
    exp_map = lambda j, be, na, rt: (be[j], 0, 0)
    return pl.pallas_call(
        _expert_kernel,
        grid_spec=pltpu.PrefetchScalarGridSpec(
            num_scalar_prefetch=3,
            grid=(nb,),
            in_specs=[pl.BlockSpec(memory_space=pl.ANY),
                      pl.BlockSpec((1, d, 2 * D_FF), exp_map),
                      pl.BlockSpec((1, 1, 2 * D_FF), exp_map),
                      pl.BlockSpec((1, D_FF, d), exp_map),
                      pl.BlockSpec((1, 1, d), exp_map)],
            out_specs=pl.BlockSpec((MOE_ROWS, d), row_map),
            scratch_shapes=[pltpu.VMEM((MOE_ROWS, d), F32)] * GATHER_DEPTH + [
                pltpu.VMEM((d, 2 * D_FF), BF16), pltpu.VMEM((D_FF, d), BF16),
                pltpu.SemaphoreType.DMA((GATHER_DEPTH,))],
        ),
        out_shape=jax.ShapeDtypeStruct((n_rows, d), F32),
        compiler_params=_params(("arbitrary",)),
        name="expert_ffn",
    )(block_e, n_act, row_tok, h2, w1, b1.reshape(N_EXPERTS, 1, 2 * D_FF), w2, b2.reshape(N_EXPERTS, 1, d))


def _combine_kernel(dest_ref, x1_ref, gate_ref, g2_ref, fw_ref, y_ref, o_ref, buf_ref, sem):
    i = pl.program_id(0)
    n = pl.num_programs(0)
    gt = x1_ref.shape[0]

    def row_copy(slot, k, j, src_row):
        return pltpu.make_async_copy(y_ref.at[pl.ds(src_row, 1), :],
                                     buf_ref.at[slot, k, pl.ds(j, 1), :], sem.at[slot])

    def issue(tile, slot):
        base = tile * (gt * TOP_K)
        for j in range(gt):
            for k in range(TOP_K):
                row_copy(slot, k, j, dest_ref[base + j * TOP_K + k]).start(priority=k % 2)

    @pl.when(i == 0)
    def _():
        issue(0, 0)

    for s in range(2):
        @pl.when(jnp.logical_and(i + 1 < n, (i + 1) % 2 == s))
        def _(s=s):
            issue(i + 1, s)

    slot = i % 2

    for k in range(TOP_K):
        pltpu.make_async_copy(y_ref.at[pl.ds(0, gt), :], buf_ref.at[slot, k], sem.at[slot]).wait()

    gates = gate_ref[...]
    moe = gates[:, 0:1] * buf_ref[slot, 0]
    for k in range(1, TOP_K):
        moe = moe + gates[:, k:k + 1] * buf_ref[slot, k]
    x2 = x1_ref[...] + g2_ref[0] * moe
    ms = jnp.mean(x2 * x2, axis=-1, keepdims=True)
    o_ref[...] = x2 * lax.rsqrt(ms + EPS) * fw_ref[...]


def _combine(dest_flat, x1, gate_pad, g2, fw, yout, seqlen):
    t, d = x1.shape
    gt = min(GATHER_TILE, seqlen)
    per_seq = seqlen // gt
    return pl.pallas_call(
        _combine_kernel,
        grid_spec=pltpu.PrefetchScalarGridSpec(
            num_scalar_prefetch=1,
            grid=(t // gt,),
            in_specs=[pl.BlockSpec((gt, d), lambda i, dst: (i, 0)),
                      pl.BlockSpec((gt, LANES), lambda i, dst: (i, 0)),
                      pl.BlockSpec((1, 1, d), lambda i, dst: (i // per_seq, 0, 0)),
                      pl.BlockSpec((1, d), lambda i, dst: (0, 0)),
                      pl.BlockSpec(memory_space=pl.ANY)],
            out_specs=pl.BlockSpec((gt, d), lambda i, dst: (i, 0)),
            scratch_shapes=[pltpu.VMEM((2, TOP_K, gt, d), F32), pltpu.SemaphoreType.DMA((2,))],
        ),
        out_shape=jax.ShapeDtypeStruct((t, d), F32),
        compiler_params=_params(("arbitrary",)),
        name="combine_norm",
    )(dest_flat, x1, gate_pad, g2, fw, yout)


def _pad_lanes(v, fill=0.0):
    v = v.reshape(1, -1).astype(F32)
    return jnp.pad(v, ((0, 0), (0, LANES - v.shape[1])), constant_values=fill)


def kernel(x, c, ada_w, ada_b, norm1_w, w_in, hg_lb_logits, hg_norm_w, ssm_a_log, ssm_dt_bias, ssm_d,
           ssm_conv_w, ssm_conv_b, ssm_norm_w, w_out, norm2_w, router_w, router_b, exp_w1, exp_b1,
           exp_w2, exp_b2, final_norm_w):
    bsz, seqlen, d = x.shape
    t = bsz * seqlen
    l = 0

    c_pad = jnp.pad(c, ((0, SUBLANES - bsz), (0, 0)))
    mod = _ada_mod(c_pad, ada_w[l], ada_b[l])[:bsz]
    sh1, sc1, g1, sh2, sc2, g2 = [m.reshape(bsz, 1, d) for m in jnp.split(mod, N_MOD, axis=-1)]

    w_in_pad = jnp.pad(w_in[l], ((0, 0), (0, IN_COLS_PAD - IN_COLS))).astype(BF16)
    qs, kk, lf, vi, gg, zz, xbc, dt = _inproj(
        x, norm1_w[l].reshape(1, d), sh1, sc1, w_in_pad, hg_lb_logits, _pad_lanes(ssm_dt_bias[l]))

    o_hg = _hgrn(qs, kk, lf, vi, gg, hg_norm_w[l].reshape(1, HG_DV))
    y_ssd = _ssd(xbc, zz, dt, ssm_conv_w[l], ssm_conv_b[l].reshape(1, SSM_CONV_CH),
                 _pad_lanes(ssm_a_log[l]), jnp.repeat(ssm_d[l], SSM_HEAD_DIM).reshape(1, SSM_WIDTH),
                 ssm_norm_w[l].reshape(1, SSM_WIDTH))

    rw_pad = jnp.pad(router_w[l], ((0, 0), (0, LANES - N_EXPERTS)))
    rb_pad = _pad_lanes(router_b[l], NEG_BIG)
    x1, h2, idx_pad, gate_pad, counts = _outproj(
        x, o_hg, y_ssd, w_out[l].astype(BF16), g1, norm2_w[l].reshape(1, d), sh2, sc2, rw_pad, rb_pad)

    cnt = counts[0, :N_EXPERTS].astype(jnp.int32)
    blocks_e = (cnt + MOE_ROWS - 1) // MOE_ROWS
    blk_end = jnp.cumsum(blocks_e)
    pstart = (blk_end - blocks_e) * MOE_ROWS
    n_blocks = (t * TOP_K) // MOE_ROWS + N_EXPERTS
    n_rows = n_blocks * MOE_ROWS
    n_steps = n_blocks + GATHER_DEPTH - 1
    block_e = jnp.minimum(jnp.sum(blk_end[None, :] <= jnp.arange(n_steps)[:, None], axis=1),
                          N_EXPERTS - 1).astype(jnp.int32)
    n_act = blk_end[-1:].astype(jnp.int32)

    dest_pad = _route(idx_pad.reshape(t, LANES), _pad_lanes(pstart))
    dest_flat = dest_pad[:, :TOP_K].reshape(t * TOP_K)

    row_tok = _invert(dest_flat, n_rows)
    yout = _experts(block_e, n_act, row_tok, h2.reshape(t, d), exp_w1[l], exp_b1[l], exp_w2[l], exp_b2[l])
    out = _combine(dest_flat, x1.reshape(t, d), gate_pad.reshape(t, LANES), g2,
                   final_norm_w.reshape(1, d), yout, seqlen)
    return out.reshape(bsz, seqlen, d)
```

```python
import functools

import jax
import jax.numpy as jnp
from jax import lax
from jax.experimental import pallas as pl
from jax.experimental.pallas import tpu as pltpu

F32 = jnp.float32
BF16 = jnp.bfloat16
HIGHEST = lax.Precision.HIGHEST

EPS = 1e-6
D_MODEL = 1024
HG_HEADS = 4
HG_DK = 128
HG_DV = 128
HG_QF = HG_HEADS * HG_DK
HG_WIDTH = HG_HEADS * HG_DV
HG_CHUNK = 64
SSM_HEADS = 8
SSM_HEAD_DIM = 64
SSM_WIDTH = SSM_HEADS * SSM_HEAD_DIM
SSM_GROUPS = 2
SSM_GROUP_HEADS = SSM_HEADS // SSM_GROUPS
SSM_GROUP_WIDTH = SSM_WIDTH // SSM_GROUPS
SSM_STATE = 128
SSM_CONV = 4
SSM_CONV_CH = SSM_WIDTH + 2 * SSM_GROUPS * SSM_STATE
SSM_CHUNK = 128
IN_SPLITS = (HG_QF, HG_QF, HG_WIDTH, HG_WIDTH, SSM_WIDTH, SSM_CONV_CH, SSM_HEADS)
IN_COLS = sum(IN_SPLITS)
N_EXPERTS = 32
TOP_K = 4
D_FF = 1024
SWIGLU_LIMIT = 7.0
SWIGLU_ALPHA = 1.702
N_MOD = 6

LANES = 128
SUBLANES = 8
VMEM_LIMIT = 56 * 1024 * 1024

ROW_TILE = 512
SEQ_TILE = 512
MOE_ROWS = 256
GATHER_TILE = 128
INVERT_TILE = 1024
GATHER_DEPTH = 3
NEG_BIG = -1e30


def _silu(v):
    return v * jax.nn.sigmoid(v)


def _softplus(v):
    return jnp.maximum(v, 0.0) + jnp.log1p(jnp.exp(-jnp.abs(v)))


def _params(sem):
    return pltpu.CompilerParams(dimension_semantics=sem, vmem_limit_bytes=VMEM_LIMIT)


def _ada_kernel(c_ref, w_ref, b_ref, o_ref):
    o_ref[...] = jnp.dot(_silu(c_ref[...]), w_ref[...], precision=HIGHEST,
                         preferred_element_type=F32) + b_ref[...]


def _ada_mod(c_pad, ada_w, ada_b):
    n = ada_w.shape[1]
    tn = D_MODEL
    return pl.pallas_call(
        _ada_kernel,
        grid=(n // tn,),
        in_specs=[pl.BlockSpec((SUBLANES, D_MODEL), lambda j: (0, 0)),
                  pl.BlockSpec((D_MODEL, tn), lambda j: (0, j)),
                  pl.BlockSpec((1, tn), lambda j: (0, j))],
        out_specs=pl.BlockSpec((SUBLANES, tn), lambda j: (0, j)),
        out_shape=jax.ShapeDtypeStruct((SUBLANES, n), F32),
        compiler_params=_params(("arbitrary",)),
        name="ada_mod",
    )(c_pad, ada_w, ada_b.reshape(1, n))


_OFF = [0]
for _w in IN_SPLITS:
    _OFF.append(_OFF[-1] + _w)
IN_COLS_PAD = _OFF[6] + LANES


def _inproj_kernel(x_ref, nw_ref, sh_ref, sc_ref, w_ref, lbl_ref, dtb_ref,
                   qs_ref, kk_ref, lf_ref, vi_ref, gg_ref, zz_ref, xbc_ref, dt_ref):
    x = x_ref[0]
    ms = jnp.mean(x * x, axis=-1, keepdims=True)
    h = x * lax.rsqrt(ms + EPS) * nw_ref[...]
    h = h * (1.0 + sc_ref[0]) + sh_ref[0]
    hb = h.astype(BF16)

    def seg(k):
        return jnp.dot(hb, w_ref[:, _OFF[k]:_OFF[k] + (IN_SPLITS[k] if k < 6 else LANES)],
                       preferred_element_type=F32)

    lbl = lbl_ref[...]
    le = jnp.exp(lbl - jnp.max(lbl, axis=0, keepdims=True))
    lb = le[0:1, :] / jnp.sum(le, axis=0, keepdims=True)

    qs_ref[0] = _silu(seg(0)).astype(BF16)
    fg = lb + (1.0 - lb) * jax.nn.sigmoid(seg(1))
    kk_ref[0] = (1.0 - fg).astype(BF16)
    lf_ref[0] = jnp.log(fg)
    vi_ref[0] = seg(2).astype(BF16)
    gg_ref[0] = _silu(seg(3)).astype(BF16)
    zz_ref[0] = _silu(seg(4)).astype(BF16)
    xbc_ref[0] = seg(5).astype(BF16)
    dt_ref[0] = _softplus(seg(6) + dtb_ref[...])


def _inproj(x, nw, sh, sc, w_in_pad, lb_logits, dtb_pad):
    bsz, seqlen, d = x.shape
    tm = min(ROW_TILE, seqlen)
    nt = seqlen // tm
    tok = lambda w: pl.BlockSpec((1, tm, w), lambda b, i: (b, i, 0))
    full = lambda a: pl.BlockSpec(a.shape, lambda b, i: (0,) * a.ndim)
    mod = pl.BlockSpec((1, 1, d), lambda b, i: (b, 0, 0))
    shp = lambda w, dt: jax.ShapeDtypeStruct((bsz, seqlen, w), dt)
    return pl.pallas_call(
        _inproj_kernel,
        grid=(bsz, nt),
        in_specs=[tok(d), full(nw), mod, mod, full(w_in_pad), full(lb_logits), full(dtb_pad)],
        out_specs=[tok(HG_QF), tok(HG_QF), tok(HG_QF), tok(HG_WIDTH), tok(HG_WIDTH),
                   tok(SSM_WIDTH), tok(SSM_CONV_CH), tok(LANES)],
        out_shape=[shp(HG_QF, BF16), shp(HG_QF, BF16), shp(HG_QF, F32), shp(HG_WIDTH, BF16),
                   shp(HG_WIDTH, BF16), shp(SSM_WIDTH, BF16), shp(SSM_CONV_CH, BF16),
                   shp(LANES, F32)],
        compiler_params=_params(("arbitrary", "arbitrary")),
        name="inproj",
    )(x, nw, sh, sc, w_in_pad, lb_logits, dtb_pad)


def _hgrn_kernel(qs_ref, kk_ref, lf_ref, vi_ref, gg_ref, nw_ref, o_ref, st_ref):
    @pl.when(pl.program_id(2) == 0)
    def _():
        st_ref[...] = jnp.zeros_like(st_ref)

    c = HG_CHUNK
    tq = qs_ref.shape[1]
    nc = tq // c
    rows = lax.broadcasted_iota(jnp.int32, (nc, c, c), 1)
    cols = lax.broadcasted_iota(jnp.int32, (nc, c, c), 2)
    causal = rows >= cols
    tril = causal.astype(F32)
    chunked = lambda ref: ref[0].reshape(nc, c, ref.shape[2])
    q = chunked(qs_ref).astype(F32)
    k = chunked(kk_ref).astype(F32)
    v = chunked(vi_ref)
    bmm = lambda a, bb, dims, **kw: lax.dot_general(a, bb, (dims, ((0,), (0,))),
                                                    preferred_element_type=F32, **kw)
    b = bmm(tril, chunked(lf_ref), ((2,), (1,)), precision=HIGHEST)
    b_mid = b[:, c // 2 - 1:c // 2, :]
    b_last = b[:, c - 1:c, :]
    qa = (q * jnp.exp(b - b_mid)).astype(BF16)
    ka = (k * jnp.exp(b_mid - b)).astype(BF16)
    att = bmm(qa, ka, ((2,), (2,)))
    att = jnp.where(causal, att, 0.0).astype(BF16)
    o = bmm(att, v, ((2,), (1,)))
    kd = (k * jnp.exp(b_last - b)).astype(BF16)
    d_st = bmm(v, kd, ((1,), (1,)))
    decay = jnp.exp(b_last)
    st = st_ref[...]
    entering = []
    for ci in range(nc):
        entering.append(st.astype(BF16))
        st = st * decay[ci] + d_st[ci]
    st_ref[...] = st
    qb = (q * jnp.exp(b)).astype(BF16)
    o = o + bmm(qb, jnp.stack(entering), ((2,), (2,)))
    ms = jnp.mean(o * o, axis=-1, keepdims=True)
    y = o * lax.rsqrt(ms + EPS) * nw_ref[...] * chunked(gg_ref).astype(F32)
    o_ref[0] = y.reshape(tq, o_ref.shape[2]).astype(BF16)


def _hgrn(qs, kk, lf, vi, gg, nw):
    bsz, seqlen, _ = qs.shape
    tq = min(SEQ_TILE, seqlen)
    blk = pl.BlockSpec((1, tq, HG_DK), lambda b, h, i: (b, i, h))
    return pl.pallas_call(
        _hgrn_kernel,
        grid=(bsz, HG_HEADS, seqlen // tq),
        in_specs=[blk, blk, blk, blk, blk, pl.BlockSpec((1, HG_DV), lambda b, h, i: (0, 0))],
        out_specs=blk,
        out_shape=jax.ShapeDtypeStruct((bsz, seqlen, HG_WIDTH), BF16),
        scratch_shapes=[pltpu.VMEM((HG_DV, HG_DK), F32)],
        compiler_params=_params(("arbitrary", "arbitrary", "arbitrary")),
        name="hgrn2",
    )(qs, kk, lf, vi, gg, nw)


def _ssd_kernel(xbc_ref, zz_ref, dt_ref, cw_ref, cb_ref, alog_ref, dsk_ref, nw_ref,
                y_ref, buf_ref, xc_ref, yc_ref, st_ref):
    tq = xbc_ref.shape[1]
    halo = SUBLANES

    @pl.when(pl.program_id(1) == 0)
    def _():
        buf_ref[0:halo, :] = jnp.zeros((halo, SSM_CONV_CH), F32)
        st_ref[...] = jnp.zeros_like(st_ref)

    buf_ref[halo:halo + tq, :] = xbc_ref[0].astype(F32)
    conv = cb_ref[...] + buf_ref[halo - 3:halo - 3 + tq, :] * cw_ref[0:1, :]
    for w in range(1, SSM_CONV):
        conv = conv + buf_ref[halo - 3 + w:halo - 3 + w + tq, :] * cw_ref[w:w + 1, :]
    buf_ref[0:halo, :] = buf_ref[tq:tq + halo, :]
    xc_ref[...] = _silu(conv)

    c = SSM_CHUNK
    p = SSM_HEAD_DIM
    nc = tq // c
    rows = lax.broadcasted_iota(jnp.int32, (nc, c, c), 1)
    cols = lax.broadcasted_iota(jnp.int32, (nc, c, c), 2)
    causal = rows >= cols
    tril = causal.astype(F32)
    triu = (rows <= cols).astype(F32)
    bmm = lambda a, bb, dims, **kw: lax.dot_general(a, bb, (dims, ((0,), (0,))),
                                                    preferred_element_type=F32, **kw)
    chunked = lambda val: val.reshape(nc, c, val.shape[-1])
    dt = chunked(dt_ref[0])
    da = dt * (-jnp.exp(alog_ref[...]))
    acum = bmm(tril, da, ((2,), (1,)), precision=HIGHEST)
    acum_t = bmm(da, triu, ((1,), (1,)), precision=HIGHEST)
    decay_out = jnp.exp(acum)
    decay_in = jnp.exp(acum[:, c - 1:c, :] - acum)
    for g in range(SSM_GROUPS):
        bm = chunked(xc_ref[:, SSM_WIDTH + g * SSM_STATE:SSM_WIDTH + (g + 1) * SSM_STATE]).astype(BF16)
        cm = chunked(xc_ref[:, SSM_WIDTH + (SSM_GROUPS + g) * SSM_STATE:
                            SSM_WIDTH + (SSM_GROUPS + g + 1) * SSM_STATE]).astype(BF16)
        cb = bmm(cm, bm, ((2,), (2,)))
        y_intra, d_st, a_last = [], [], []
        for hh in range(SSM_GROUP_HEADS):
            h = g * SSM_GROUP_HEADS + hh
            xdt = chunked(xc_ref[:, h * p:(h + 1) * p]) * dt[:, :, h:h + 1]
            seg = jnp.exp(jnp.where(causal, acum[:, :, h:h + 1] - acum_t[:, h:h + 1, :], -jnp.inf))
            y_intra.append(bmm((cb * seg).astype(BF16), xdt.astype(BF16), ((2,), (1,))))
            d_st.append(bmm((xdt * decay_in[:, :, h:h + 1]).astype(BF16), bm, ((1,), (1,))))
            a_last.append(jnp.exp(acum_t[:, h:h + 1, c - 1:c]))
        st = [st_ref[g, hh * p:(hh + 1) * p, :] for hh in range(SSM_GROUP_HEADS)]
        entering = []
        for ci in range(nc):
            entering.append(jnp.concatenate(st, axis=0).astype(BF16))
            st = [st[hh] * a_last[hh][ci] + d_st[hh][ci] for hh in range(SSM_GROUP_HEADS)]
        for hh in range(SSM_GROUP_HEADS):
            st_ref[g, hh * p:(hh + 1) * p, :] = st[hh]
        y_inter = bmm(cm, jnp.stack(entering), ((2,), (2,)))
        for hh in range(SSM_GROUP_HEADS):
            h = g * SSM_GROUP_HEADS + hh
            y = y_intra[hh] + decay_out[:, :, h:h + 1] * y_inter[:, :, hh * p:(hh + 1) * p]
            yc_ref[:, h * p:(h + 1) * p] = y.reshape(tq, p)
    y = (yc_ref[...] + dsk_ref[...] * xc_ref[:, 0:SSM_WIDTH]) * zz_ref[0].astype(F32)
    for g in range(SSM_GROUPS):
        gs = slice(g * SSM_GROUP_WIDTH, (g + 1) * SSM_GROUP_WIDTH)
        yg = y[:, gs]
        ms = jnp.mean(yg * yg, axis=-1, keepdims=True)
        y_ref[0, :, gs] = (yg * lax.rsqrt(ms + EPS) * nw_ref[:, gs]).astype(BF16)


def _ssd(xbc, zz, dt, conv_w, conv_b, alog_pad, dsk_wide, nw):
    bsz, seqlen, _ = xbc.shape
    tq = min(SEQ_TILE, seqlen)
    tok = lambda w: pl.BlockSpec((1, tq, w), lambda b, i: (b, i, 0))
    full = lambda a: pl.BlockSpec(a.shape, lambda b, i: (0,) * a.ndim)
    return pl.pallas_call(
        _ssd_kernel,
        grid=(bsz, seqlen // tq),
        in_specs=[tok(SSM_CONV_CH), tok(SSM_WIDTH), tok(LANES), full(conv_w), full(conv_b),
                  full(alog_pad), full(dsk_wide), full(nw)],
        out_specs=tok(SSM_WIDTH),
        out_shape=jax.ShapeDtypeStruct((bsz, seqlen, SSM_WIDTH), BF16),
        scratch_shapes=[pltpu.VMEM((tq + 2 * SUBLANES, SSM_CONV_CH), F32),
                        pltpu.VMEM((tq, SSM_CONV_CH), F32),
                        pltpu.VMEM((tq, SSM_WIDTH), F32),
                        pltpu.VMEM((SSM_GROUPS, SSM_GROUP_WIDTH, SSM_STATE), F32)],
        compiler_params=_params(("arbitrary", "arbitrary")),
        name="ssd",
    )(xbc, zz, dt, conv_w, conv_b, alog_pad, dsk_wide, nw)


def _outproj_kernel(x_ref, oh_ref, ys_ref, wo_ref, g1_ref, nw_ref, sh_ref, sc_ref, rw_ref, rb_ref,
                    x1_ref, h2_ref, idx_ref, gate_ref, cnt_ref):
    first = jnp.logical_and(pl.program_id(0) == 0, pl.program_id(1) == 0)

    @pl.when(first)
    def _():
        cnt_ref[...] = jnp.zeros_like(cnt_ref)

    mix = jnp.dot(oh_ref[0], wo_ref[0:HG_WIDTH, :], preferred_element_type=F32)
    mix = mix + jnp.dot(ys_ref[0], wo_ref[HG_WIDTH:, :], preferred_element_type=F32)
    x1 = x_ref[0] + g1_ref[0] * mix
    x1_ref[0] = x1
    ms = jnp.mean(x1 * x1, axis=-1, keepdims=True)
    h2 = x1 * lax.rsqrt(ms + EPS) * nw_ref[...]
    h2 = h2 * (1.0 + sc_ref[0]) + sh_ref[0]
    h2_ref[0] = h2
    logits = jnp.dot(h2, rw_ref[...], precision=HIGHEST, preferred_element_type=F32) + rb_ref[...]
    lane = lax.broadcasted_iota(jnp.int32, logits.shape, 1).astype(F32)
    idx_out = jnp.zeros(logits.shape, F32)
    val_out = jnp.zeros(logits.shape, F32)
    sel = jnp.zeros(logits.shape, F32)
    work = logits
    top = None
    denom = None
    for k in range(TOP_K):
        m = jnp.max(work, axis=-1, keepdims=True)
        am = jnp.min(jnp.where(work == m, lane, float(LANES)), axis=-1, keepdims=True)
        hit = lane == am
        work = jnp.where(hit, -jnp.inf, work)
        sel = jnp.where(hit, 1.0, sel)
        if k == 0:
            top = m
        e = jnp.exp(m - top)
        denom = e if k == 0 else denom + e
        idx_out = jnp.where(lane == float(k), am, idx_out)
        val_out = jnp.where(lane == float(k), e, val_out)
    idx_ref[0] = idx_out.astype(jnp.int32)
    gate_ref[0] = val_out / denom
    cnt_ref[0:1, :] += jnp.sum(sel, axis=0, keepdims=True)


def _outproj(x, o_hg, y_ssd, w_out_b, g1, nw, sh, sc, rw_pad, rb_pad):
    bsz, seqlen, d = x.shape
    tm = min(ROW_TILE, seqlen)
    tok = lambda w: pl.BlockSpec((1, tm, w), lambda b, i: (b, i, 0))
    full = lambda a: pl.BlockSpec(a.shape, lambda b, i: (0,) * a.ndim)
    mod = pl.BlockSpec((1, 1, d), lambda b, i: (b, 0, 0))
    shp = lambda w, dt: jax.ShapeDtypeStruct((bsz, seqlen, w), dt)
    return pl.pallas_call(
        _outproj_kernel,
        grid=(bsz, seqlen // tm),
        in_specs=[tok(d), tok(HG_WIDTH), tok(SSM_WIDTH), full(w_out_b), mod, full(nw), mod, mod,
                  full(rw_pad), full(rb_pad)],
        out_specs=[tok(d), tok(d), tok(LANES), tok(LANES),
                   pl.BlockSpec((SUBLANES, LANES), lambda b, i: (0, 0))],
        out_shape=[shp(d, F32), shp(d, F32), shp(LANES, jnp.int32), shp(LANES, F32),
                   jax.ShapeDtypeStruct((SUBLANES, LANES), F32)],
        compiler_params=_params(("arbitrary", "arbitrary")),
        name="outproj_router",
    )(x, o_hg, y_ssd, w_out_b, g1, nw, sh, sc, rw_pad, rb_pad)


def _route_kernel(idx_ref, pstart_ref, dest_ref, carry_ref):
    @pl.when(pl.program_id(0) == 0)
    def _():
        carry_ref[...] = jnp.zeros_like(carry_ref)

    idx = idx_ref[...]
    tt = idx.shape[0]
    lane = lax.broadcasted_iota(jnp.int32, idx.shape, 1)
    hits = [lane == idx[:, k:k + 1] for k in range(TOP_K)]
    sel = jnp.zeros(idx.shape, F32)
    for hit in hits:
        sel = jnp.where(hit, 1.0, sel)
    rows = lax.broadcasted_iota(jnp.int32, (tt, tt), 0)
    cols = lax.broadcasted_iota(jnp.int32, (tt, tt), 1)
    before = (rows > cols).astype(BF16)
    rank = jnp.dot(before, sel.astype(BF16), preferred_element_type=F32) + carry_ref[0:1, :]
    carry_ref[0:1, :] += jnp.sum(sel, axis=0, keepdims=True)
    dense = pstart_ref[...] + rank
    out = jnp.zeros(idx.shape, F32)
    for k, hit in enumerate(hits):
        dk = jnp.sum(jnp.where(hit, dense, 0.0), axis=-1, keepdims=True)
        out = jnp.where(lane == k, dk, out)
    dest_ref[...] = out.astype(jnp.int32)


def _route(idx_pad, pstart_row):
    t = idx_pad.shape[0]
    tt = min(ROW_TILE, t)
    return pl.pallas_call(
        _route_kernel,
        grid=(t // tt,),
        in_specs=[pl.BlockSpec((tt, LANES), lambda i: (i, 0)),
                  pl.BlockSpec((1, LANES), lambda i: (0, 0))],
        out_specs=pl.BlockSpec((tt, LANES), lambda i: (i, 0)),
        out_shape=jax.ShapeDtypeStruct((t, LANES), jnp.int32),
        scratch_shapes=[pltpu.VMEM((SUBLANES, LANES), F32)],
        compiler_params=_params(("arbitrary",)),
        name="route_rank",
    )(idx_pad, pstart_row)


def _invert_kernel(dest_ref, zeros_ref, rowtok_ref, sem):
    i = pl.program_id(0)
    per_step = dest_ref.shape[0] // TOP_K // pl.num_programs(0)

    @pl.when(i == 0)
    def _():
        fill = pltpu.make_async_copy(zeros_ref, rowtok_ref, sem)
        fill.start()
        fill.wait()

    def tok_body(n, carry):
        tok = i * per_step + n
        for k in range(TOP_K):
            rowtok_ref[dest_ref[tok * TOP_K + k]] = tok
        return carry
    lax.fori_loop(0, per_step, tok_body, 0, unroll=8)


def _invert(dest_flat, n_rows):
    t = dest_flat.shape[0] // TOP_K
    return pl.pallas_call(
        _invert_kernel,
        grid=(max(t // INVERT_TILE, 1),),
        in_specs=[pl.BlockSpec(memory_space=pltpu.SMEM), pl.BlockSpec(memory_space=pl.ANY)],
        out_specs=pl.BlockSpec(memory_space=pltpu.SMEM),
        out_shape=jax.ShapeDtypeStruct((n_rows,), jnp.int32),
        scratch_shapes=[pltpu.SemaphoreType.DMA(())],
        compiler_params=_params(("arbitrary",)),
        name="invert_route",
    )(dest_flat, jnp.zeros((n_rows,), jnp.int32))


def _expert_kernel(be_ref, nact_ref, nxt_ref, rowtok_ref, h_ref, w1_ref, b1_ref, w2_ref, b2_ref, y_ref,
                   xbuf0_ref, xbuf1_ref, xbuf2_ref, w1s_ref, w2s_ref, w1b_ref, w2b_ref, sem, wsem):
    j = pl.program_id(0)
    n_act = nact_ref[0]
    active = j < n_act
    expert = be_ref[j]
    fresh = jnp.logical_or(j == 0, expert != be_ref[jnp.maximum(j - 1, 0)])
    phase = j % GATHER_DEPTH
    bufs = (xbuf0_ref, xbuf1_ref, xbuf2_ref)

    def issue(block, s):
        base = jnp.minimum(block, n_act - 1) * MOE_ROWS
        for r in range(MOE_ROWS):
            pltpu.make_async_copy(h_ref.at[pl.ds(rowtok_ref[base + r], 1), :],
                                  bufs[s].at[pl.ds(r, 1), :], sem.at[s]).start(priority=1)

    def wait(s):
        pltpu.make_async_copy(h_ref.at[pl.ds(0, MOE_ROWS), :], bufs[s], sem.at[s]).wait()

    def weight_copies(e):
        return (pltpu.make_async_copy(w1_ref.at[e], w1s_ref, wsem.at[0]),
                pltpu.make_async_copy(w2_ref.at[e], w2s_ref, wsem.at[1]))

    @pl.when(j == 0)
    def _():
        for cp in weight_copies(expert):
            cp.start()
        for s in range(GATHER_DEPTH - 1):
            issue(s, s)

    @pl.when(jnp.logical_and(active, fresh))
    def _():
        for cp in weight_copies(expert):
            cp.wait()
        w1b_ref[...] = w1s_ref[...].astype(BF16)
        w2b_ref[...] = w2s_ref[...].astype(BF16)

        @pl.when(nxt_ref[expert] < N_EXPERTS)
        def _():
            for cp in weight_copies(nxt_ref[expert]):
                cp.start()

    for s in range(GATHER_DEPTH):
        @pl.when(jnp.logical_and(j < n_act + GATHER_DEPTH - 1, phase == s))
        def _(s=s):
            wait(s)

        @pl.when(jnp.logical_and(active, phase == s))
        def _(s=s):
            issue(j + GATHER_DEPTH - 1, (s + GATHER_DEPTH - 1) % GATHER_DEPTH)
            hb = jnp.dot(bufs[s][...].astype(BF16), w1b_ref[...], preferred_element_type=F32) + b1_ref[0]
            glu = jnp.minimum(hb[:, :D_FF], SWIGLU_LIMIT)
            lin = jnp.clip(hb[:, D_FF:], -SWIGLU_LIMIT, SWIGLU_LIMIT)
            act = glu * jax.nn.sigmoid(SWIGLU_ALPHA * glu) * (lin + 1.0)
            y_ref[...] = jnp.dot(act.astype(BF16), w2b_ref[...], preferred_element_type=F32) + b2_ref[0]


def _experts(block_e, n_act, next_e, row_tok, h2, w1, b1, w2, b2):
    n_rows = row_tok.shape[0]
    d = h2.shape[1]
    nb = block_e.shape[0]
    row_map = lambda j, be, na, nx, rt: (jnp.maximum(jnp.minimum(j, na[0] - 1), 0), 0)
    exp_map = lambda j, be, na, nx, rt: (be[j], 0, 0)
    return pl.pallas_call(
        _expert_kernel,
        grid_spec=pltpu.PrefetchScalarGridSpec(
            num_scalar_prefetch=4,
            grid=(nb,),
            in_specs=[pl.BlockSpec(memory_space=pl.ANY),
                      pl.BlockSpec(memory_space=pl.ANY),
                      pl.BlockSpec((1, 1, 2 * D_FF), exp_map),
                      pl.BlockSpec(memory_space=pl.ANY),
                      pl.BlockSpec((1, 1, d), exp_map)],
            out_specs=pl.BlockSpec((MOE_ROWS, d), row_map),
            scratch_shapes=[pltpu.VMEM((MOE_ROWS, d), F32)] * GATHER_DEPTH + [
                pltpu.VMEM((d, 2 * D_FF), F32), pltpu.VMEM((D_FF, d), F32),
                pltpu.VMEM((d, 2 * D_FF), BF16), pltpu.VMEM((D_FF, d), BF16),
                pltpu.SemaphoreType.DMA((GATHER_DEPTH,)), pltpu.SemaphoreType.DMA((2,))],
        ),
        out_shape=jax.ShapeDtypeStruct((n_rows, d), F32),
        compiler_params=_params(("arbitrary",)),
        name="expert_ffn",
    )(block_e, n_act, next_e, row_tok, h2, w1, b1.reshape(N_EXPERTS, 1, 2 * D_FF), w2,
      b2.reshape(N_EXPERTS, 1, d))


def _combine_kernel(dest_ref, x1_ref, gate_ref, g2_ref, fw_ref, y_ref, o_ref, buf_ref, sem):
    i = pl.program_id(0)
    n = pl.num_programs(0)
    gt = x1_ref.shape[0]

    def row_copy(slot, k, j, src_row):
        return pltpu.make_async_copy(y_ref.at[pl.ds(src_row, 1), :],
                                     buf_ref.at[slot, k, pl.ds(j, 1), :], sem.at[slot])

    def issue(tile, slot):
        base = tile * (gt * TOP_K)
        for j in range(gt):
            for k in range(TOP_K):
                row_copy(slot, k, j, dest_ref[base + j * TOP_K + k]).start(priority=k % 2)

    @pl.when(i == 0)
    def _():
        issue(0, 0)

    for s in range(2):
        @pl.when(jnp.logical_and(i + 1 < n, (i + 1) % 2 == s))
        def _(s=s):
            issue(i + 1, s)

    slot = i % 2

    for k in range(TOP_K):
        pltpu.make_async_copy(y_ref.at[pl.ds(0, gt), :], buf_ref.at[slot, k], sem.at[slot]).wait()

    gates = gate_ref[...]
    moe = gates[:, 0:1] * buf_ref[slot, 0]
    for k in range(1, TOP_K):
        moe = moe + gates[:, k:k + 1] * buf_ref[slot, k]
    x2 = x1_ref[...] + g2_ref[0] * moe
    ms = jnp.mean(x2 * x2, axis=-1, keepdims=True)
    o_ref[...] = x2 * lax.rsqrt(ms + EPS) * fw_ref[...]


def _combine(dest_flat, x1, gate_pad, g2, fw, yout, seqlen):
    t, d = x1.shape
    gt = min(GATHER_TILE, seqlen)
    per_seq = seqlen // gt
    return pl.pallas_call(
        _combine_kernel,
        grid_spec=pltpu.PrefetchScalarGridSpec(
            num_scalar_prefetch=1,
            grid=(t // gt,),
            in_specs=[pl.BlockSpec((gt, d), lambda i, dst: (i, 0)),
                      pl.BlockSpec((gt, LANES), lambda i, dst: (i, 0)),
                      pl.BlockSpec((1, 1, d), lambda i, dst: (i // per_seq, 0, 0)),
                      pl.BlockSpec((1, d), lambda i, dst: (0, 0)),
                      pl.BlockSpec(memory_space=pl.ANY)],
            out_specs=pl.BlockSpec((gt, d), lambda i, dst: (i, 0)),
            scratch_shapes=[pltpu.VMEM((2, TOP_K, gt, d), F32), pltpu.SemaphoreType.DMA((2,))],
        ),
        out_shape=jax.ShapeDtypeStruct((t, d), F32),
        compiler_params=_params(("arbitrary",)),
        name="combine_norm",
    )(dest_flat, x1, gate_pad, g2, fw, yout)


def _pad_lanes(v, fill=0.0):
    v = v.reshape(1, -1).astype(F32)
    return jnp.pad(v, ((0, 0), (0, LANES - v.shape[1])), constant_values=fill)


def kernel(x, c, ada_w, ada_b, norm1_w, w_in, hg_lb_logits, hg_norm_w, ssm_a_log, ssm_dt_bias, ssm_d,
           ssm_conv_w, ssm_conv_b, ssm_norm_w, w_out, norm2_w, router_w, router_b, exp_w1, exp_b1,
           exp_w2, exp_b2, final_norm_w):
    bsz, seqlen, d = x.shape
    t = bsz * seqlen
    l = 0

    c_pad = jnp.pad(c, ((0, SUBLANES - bsz), (0, 0)))
    mod = _ada_mod(c_pad, ada_w[l], ada_b[l])[:bsz]
    sh1, sc1, g1, sh2, sc2, g2 = [m.reshape(bsz, 1, d) for m in jnp.split(mod, N_MOD, axis=-1)]

    w_in_pad = jnp.pad(w_in[l], ((0, 0), (0, IN_COLS_PAD - IN_COLS))).astype(BF16)
    qs, kk, lf, vi, gg, zz, xbc, dt = _inproj(
        x, norm1_w[l].reshape(1, d), sh1, sc1, w_in_pad, hg_lb_logits, _pad_lanes(ssm_dt_bias[l]))

    o_hg = _hgrn(qs, kk, lf, vi, gg, hg_norm_w[l].reshape(1, HG_DV))
    y_ssd = _ssd(xbc, zz, dt, ssm_conv_w[l], ssm_conv_b[l].reshape(1, SSM_CONV_CH),
                 _pad_lanes(ssm_a_log[l]), jnp.repeat(ssm_d[l], SSM_HEAD_DIM).reshape(1, SSM_WIDTH),
                 ssm_norm_w[l].reshape(1, SSM_WIDTH))

    rw_pad = jnp.pad(router_w[l], ((0, 0), (0, LANES - N_EXPERTS)))
    rb_pad = _pad_lanes(router_b[l], NEG_BIG)
    x1, h2, idx_pad, gate_pad, counts = _outproj(
        x, o_hg, y_ssd, w_out[l].astype(BF16), g1, norm2_w[l].reshape(1, d), sh2, sc2, rw_pad, rb_pad)

    cnt = counts[0, :N_EXPERTS].astype(jnp.int32)
    blocks_e = (cnt + MOE_ROWS - 1) // MOE_ROWS
    blk_end = jnp.cumsum(blocks_e)
    pstart = (blk_end - blocks_e) * MOE_ROWS
    n_blocks = (t * TOP_K) // MOE_ROWS + N_EXPERTS
    n_rows = n_blocks * MOE_ROWS
    n_steps = n_blocks + GATHER_DEPTH - 1
    block_e = jnp.minimum(jnp.sum(blk_end[None, :] <= jnp.arange(n_steps)[:, None], axis=1),
                          N_EXPERTS - 1).astype(jnp.int32)
    n_act = blk_end[-1:].astype(jnp.int32)
    owner = jnp.where(blocks_e > 0, jnp.arange(N_EXPERTS), N_EXPERTS)
    next_e = jnp.concatenate([lax.cummin(owner, reverse=True)[1:],
                              jnp.full((1,), N_EXPERTS)]).astype(jnp.int32)

    dest_pad = _route(idx_pad.reshape(t, LANES), _pad_lanes(pstart))
    dest_flat = dest_pad[:, :TOP_K].reshape(t * TOP_K)

    row_tok = _invert(dest_flat, n_rows)
    yout = _experts(block_e, n_act, next_e, row_tok, h2.reshape(t, d), exp_w1[l], exp_b1[l], exp_w2[l], exp_b2[l])
    out = _combine(dest_flat, x1.reshape(t, d), gate_pad.reshape(t, LANES), g2,
                   final_norm_w.reshape(1, d), yout, seqlen)
    return out.reshape(bsz, seqlen, d)
```

```python
import functools

import jax
import jax.numpy as jnp
from jax import lax
from jax.experimental import pallas as pl
from jax.experimental.pallas import tpu as pltpu

F32 = jnp.float32
BF16 = jnp.bfloat16
HIGHEST = lax.Precision.HIGHEST

EPS = 1e-6
D_MODEL = 1024
HG_HEADS = 4
HG_DK = 128
HG_DV = 128
HG_QF = HG_HEADS * HG_DK
HG_WIDTH = HG_HEADS * HG_DV
HG_CHUNK = 64
SSM_HEADS = 8
SSM_HEAD_DIM = 64
SSM_WIDTH = SSM_HEADS * SSM_HEAD_DIM
SSM_GROUPS = 2
SSM_GROUP_HEADS = SSM_HEADS // SSM_GROUPS
SSM_GROUP_WIDTH = SSM_WIDTH // SSM_GROUPS
SSM_STATE = 128
SSM_CONV = 4
SSM_CONV_CH = SSM_WIDTH + 2 * SSM_GROUPS * SSM_STATE
SSM_CHUNK = 128
IN_SPLITS = (HG_QF, HG_QF, HG_WIDTH, HG_WIDTH, SSM_WIDTH, SSM_CONV_CH, SSM_HEADS)
IN_COLS = sum(IN_SPLITS)
N_EXPERTS = 32
TOP_K = 4
D_FF = 1024
SWIGLU_LIMIT = 7.0
SWIGLU_ALPHA = 1.702
N_MOD = 6

LANES = 128
SUBLANES = 8
VMEM_LIMIT = 56 * 1024 * 1024

ROW_TILE = 512
SEQ_TILE = 512
MOE_ROWS = 256
GATHER_TILE = 128
INVERT_TILE = 1024
GATHER_DEPTH = 3
NEG_BIG = -1e30


def _silu(v):
    return v * jax.nn.sigmoid(v)


def _softplus(v):
    return jnp.maximum(v, 0.0) + jnp.log1p(jnp.exp(-jnp.abs(v)))


def _split3(v):
    hi = v.astype(BF16)
    r1 = v - hi.astype(F32)
    mid = r1.astype(BF16)
    lo = (r1 - mid.astype(F32)).astype(BF16)
    return jnp.concatenate([hi, mid, lo], axis=-1)


def _params(sem):
    return pltpu.CompilerParams(dimension_semantics=sem, vmem_limit_bytes=VMEM_LIMIT)


def _ada_kernel(c_ref, w_ref, b_ref, o_ref):
    o_ref[...] = jnp.dot(_silu(c_ref[...]), w_ref[...], precision=HIGHEST,
                         preferred_element_type=F32) + b_ref[...]


def _ada_mod(c_pad, ada_w, ada_b):
    n = ada_w.shape[1]
    tn = D_MODEL
    return pl.pallas_call(
        _ada_kernel,
        grid=(n // tn,),
        in_specs=[pl.BlockSpec((SUBLANES, D_MODEL), lambda j: (0, 0)),
                  pl.BlockSpec((D_MODEL, tn), lambda j: (0, j)),
                  pl.BlockSpec((1, tn), lambda j: (0, j))],
        out_specs=pl.BlockSpec((SUBLANES, tn), lambda j: (0, j)),
        out_shape=jax.ShapeDtypeStruct((SUBLANES, n), F32),
        compiler_params=_params(("arbitrary",)),
        name="ada_mod",
    )(c_pad, ada_w, ada_b.reshape(1, n))


_OFF = [0]
for _w in IN_SPLITS:
    _OFF.append(_OFF[-1] + _w)
IN_COLS_PAD = _OFF[6] + LANES


def _inproj_kernel(x_ref, nw_ref, sh_ref, sc_ref, w_ref, lbl_ref, dtb_ref,
                   qs_ref, kk_ref, lf_ref, vi_ref, gg_ref, zz_ref, xbc_ref, dt_ref):
    x = x_ref[0]
    ms = jnp.mean(x * x, axis=-1, keepdims=True)
    h = x * lax.rsqrt(ms + EPS) * nw_ref[...]
    h = h * (1.0 + sc_ref[0]) + sh_ref[0]
    hb = h.astype(BF16)

    def seg(k):
        return jnp.dot(hb, w_ref[:, _OFF[k]:_OFF[k] + (IN_SPLITS[k] if k < 6 else LANES)],
                       preferred_element_type=F32)

    lbl = lbl_ref[...]
    le = jnp.exp(lbl - jnp.max(lbl, axis=0, keepdims=True))
    lb = le[0:1, :] / jnp.sum(le, axis=0, keepdims=True)

    qs_ref[0] = _silu(seg(0)).astype(BF16)
    fg = lb + (1.0 - lb) * jax.nn.sigmoid(seg(1))
    kk_ref[0] = (1.0 - fg).astype(BF16)
    lf_ref[0] = jnp.log(fg)
    vi_ref[0] = seg(2).astype(BF16)
    gg_ref[0] = _silu(seg(3)).astype(BF16)
    zz_ref[0] = _silu(seg(4)).astype(BF16)
    xbc_ref[0] = seg(5).astype(BF16)
    dt_ref[0] = _softplus(seg(6) + dtb_ref[...])


def _inproj(x, nw, sh, sc, w_in_pad, lb_logits, dtb_pad):
    bsz, seqlen, d = x.shape
    tm = min(ROW_TILE, seqlen)
    nt = seqlen // tm
    tok = lambda w: pl.BlockSpec((1, tm, w), lambda b, i: (b, i, 0))
    full = lambda a: pl.BlockSpec(a.shape, lambda b, i: (0,) * a.ndim)
    mod = pl.BlockSpec((1, 1, d), lambda b, i: (b, 0, 0))
    shp = lambda w, dt: jax.ShapeDtypeStruct((bsz, seqlen, w), dt)
    return pl.pallas_call(
        _inproj_kernel,
        grid=(bsz, nt),
        in_specs=[tok(d), full(nw), mod, mod, full(w_in_pad), full(lb_logits), full(dtb_pad)],
        out_specs=[tok(HG_QF), tok(HG_QF), tok(HG_QF), tok(HG_WIDTH), tok(HG_WIDTH),
                   tok(SSM_WIDTH), tok(SSM_CONV_CH), tok(LANES)],
        out_shape=[shp(HG_QF, BF16), shp(HG_QF, BF16), shp(HG_QF, F32), shp(HG_WIDTH, BF16),
                   shp(HG_WIDTH, BF16), shp(SSM_WIDTH, BF16), shp(SSM_CONV_CH, BF16),
                   shp(LANES, F32)],
        compiler_params=_params(("arbitrary", "arbitrary")),
        name="inproj",
    )(x, nw, sh, sc, w_in_pad, lb_logits, dtb_pad)


def _hgrn_kernel(qs_ref, kk_ref, lf_ref, vi_ref, gg_ref, nw_ref, o_ref, st_ref):
    @pl.when(pl.program_id(2) == 0)
    def _():
        st_ref[...] = jnp.zeros_like(st_ref)

    c = HG_CHUNK
    tq = qs_ref.shape[1]
    nc = tq // c
    rows = lax.broadcasted_iota(jnp.int32, (nc, c, c), 1)
    cols = lax.broadcasted_iota(jnp.int32, (nc, c, c), 2)
    causal = rows >= cols
    chunked = lambda ref: ref[0].reshape(nc, c, ref.shape[2])
    q = chunked(qs_ref).astype(F32)
    k = chunked(kk_ref).astype(F32)
    v = chunked(vi_ref)
    bmm = lambda a, bb, dims, **kw: lax.dot_general(a, bb, (dims, ((0,), (0,))),
                                                    preferred_element_type=F32, **kw)
    b3 = bmm(causal.astype(BF16), _split3(chunked(lf_ref)), ((2,), (1,)))
    b = b3[:, :, :HG_DK] + b3[:, :, HG_DK:2 * HG_DK] + b3[:, :, 2 * HG_DK:]
    b_mid = b[:, c // 2 - 1:c // 2, :]
    b_last = b[:, c - 1:c, :]
    qa = (q * jnp.exp(b - b_mid)).astype(BF16)
    ka = (k * jnp.exp(b_mid - b)).astype(BF16)
    att = bmm(qa, ka, ((2,), (2,)))
    att = jnp.where(causal, att, 0.0).astype(BF16)
    o = bmm(att, v, ((2,), (1,)))
    kd = (k * jnp.exp(b_last - b)).astype(BF16)
    d_st = bmm(v, kd, ((1,), (1,)))
    decay = jnp.exp(b_last)
    st = st_ref[...]
    entering = []
    for ci in range(nc):
        entering.append(st.astype(BF16))
        st = st * decay[ci] + d_st[ci]
    st_ref[...] = st
    qb = (q * jnp.exp(b)).astype(BF16)
    o = o + bmm(qb, jnp.stack(entering), ((2,), (2,)))
    ms = jnp.mean(o * o, axis=-1, keepdims=True)
    y = o * lax.rsqrt(ms + EPS) * nw_ref[...] * chunked(gg_ref).astype(F32)
    o_ref[0] = y.reshape(tq, o_ref.shape[2]).astype(BF16)


def _hgrn(qs, kk, lf, vi, gg, nw):
    bsz, seqlen, _ = qs.shape
    tq = min(SEQ_TILE, seqlen)
    blk = pl.BlockSpec((1, tq, HG_DK), lambda b, h, i: (b, i, h))
    return pl.pallas_call(
        _hgrn_kernel,
        grid=(bsz, HG_HEADS, seqlen // tq),
        in_specs=[blk, blk, blk, blk, blk, pl.BlockSpec((1, HG_DV), lambda b, h, i: (0, 0))],
        out_specs=blk,
        out_shape=jax.ShapeDtypeStruct((bsz, seqlen, HG_WIDTH), BF16),
        scratch_shapes=[pltpu.VMEM((HG_DV, HG_DK), F32)],
        compiler_params=_params(("arbitrary", "arbitrary", "arbitrary")),
        name="hgrn2",
    )(qs, kk, lf, vi, gg, nw)


def _ssd_kernel(xbc_ref, zz_ref, dt_ref, cw_ref, cb_ref, alog_ref, dsk_ref, nw_ref,
                y_ref, buf_ref, xc_ref, yc_ref, st_ref):
    tq = xbc_ref.shape[1]
    halo = SUBLANES

    @pl.when(pl.program_id(1) == 0)
    def _():
        buf_ref[0:halo, :] = jnp.zeros((halo, SSM_CONV_CH), F32)
        st_ref[...] = jnp.zeros_like(st_ref)

    buf_ref[halo:halo + tq, :] = xbc_ref[0].astype(F32)
    conv = cb_ref[...] + buf_ref[halo - 3:halo - 3 + tq, :] * cw_ref[0:1, :]
    for w in range(1, SSM_CONV):
        conv = conv + buf_ref[halo - 3 + w:halo - 3 + w + tq, :] * cw_ref[w:w + 1, :]
    buf_ref[0:halo, :] = buf_ref[tq:tq + halo, :]
    xc_ref[...] = _silu(conv)

    c = SSM_CHUNK
    p = SSM_HEAD_DIM
    nc = tq // c
    rows = lax.broadcasted_iota(jnp.int32, (nc, c, c), 1)
    cols = lax.broadcasted_iota(jnp.int32, (nc, c, c), 2)
    causal = rows >= cols
    tril = causal.astype(BF16)
    bmm = lambda a, bb, dims, **kw: lax.dot_general(a, bb, (dims, ((0,), (0,))),
                                                    preferred_element_type=F32, **kw)
    chunked = lambda val: val.reshape(nc, c, val.shape[-1])
    dt = chunked(dt_ref[0])
    da = dt * (-jnp.exp(alog_ref[...]))
    da3 = _split3(da)
    a3 = bmm(tril, da3, ((2,), (1,)))
    acum = a3[:, :, :LANES] + a3[:, :, LANES:2 * LANES] + a3[:, :, 2 * LANES:]
    at3 = bmm(da3, tril, ((1,), (2,)))
    acum_t = at3[:, :LANES, :] + at3[:, LANES:2 * LANES, :] + at3[:, 2 * LANES:, :]
    decay_out = jnp.exp(acum)
    decay_in = jnp.exp(acum[:, c - 1:c, :] - acum)
    for g in range(SSM_GROUPS):
        bm = chunked(xc_ref[:, SSM_WIDTH + g * SSM_STATE:SSM_WIDTH + (g + 1) * SSM_STATE]).astype(BF16)
        cm = chunked(xc_ref[:, SSM_WIDTH + (SSM_GROUPS + g) * SSM_STATE:
                            SSM_WIDTH + (SSM_GROUPS + g + 1) * SSM_STATE]).astype(BF16)
        cb = bmm(cm, bm, ((2,), (2,)))
        y_intra, d_st, a_last = [], [], []
        for hh in range(SSM_GROUP_HEADS):
            h = g * SSM_GROUP_HEADS + hh
            xdt = chunked(xc_ref[:, h * p:(h + 1) * p]) * dt[:, :, h:h + 1]
            seg = jnp.exp(jnp.where(causal, acum[:, :, h:h + 1] - acum_t[:, h:h + 1, :], -jnp.inf))
            y_intra.append(bmm((cb * seg).astype(BF16), xdt.astype(BF16), ((2,), (1,))))
            d_st.append(bmm((xdt * decay_in[:, :, h:h + 1]).astype(BF16), bm, ((1,), (1,))))
            a_last.append(jnp.exp(acum_t[:, h:h + 1, c - 1:c]))
        st = [st_ref[g, hh * p:(hh + 1) * p, :] for hh in range(SSM_GROUP_HEADS)]
        entering = []
        for ci in range(nc):
            entering.append(jnp.concatenate(st, axis=0).astype(BF16))
            st = [st[hh] * a_last[hh][ci] + d_st[hh][ci] for hh in range(SSM_GROUP_HEADS)]
        for hh in range(SSM_GROUP_HEADS):
            st_ref[g, hh * p:(hh + 1) * p, :] = st[hh]
        y_inter = bmm(cm, jnp.stack(entering), ((2,), (2,)))
        for hh in range(SSM_GROUP_HEADS):
            h = g * SSM_GROUP_HEADS + hh
            y = y_intra[hh] + decay_out[:, :, h:h + 1] * y_inter[:, :, hh * p:(hh + 1) * p]
            yc_ref[:, h * p:(h + 1) * p] = y.reshape(tq, p)
    y = (yc_ref[...] + dsk_ref[...] * xc_ref[:, 0:SSM_WIDTH]) * zz_ref[0].astype(F32)
    for g in range(SSM_GROUPS):
        gs = slice(g * SSM_GROUP_WIDTH, (g + 1) * SSM_GROUP_WIDTH)
        yg = y[:, gs]
        ms = jnp.mean(yg * yg, axis=-1, keepdims=True)
        y_ref[0, :, gs] = (yg * lax.rsqrt(ms + EPS) * nw_ref[:, gs]).astype(BF16)


def _ssd(xbc, zz, dt, conv_w, conv_b, alog_pad, dsk_wide, nw):
    bsz, seqlen, _ = xbc.shape
    tq = min(SEQ_TILE, seqlen)
    tok = lambda w: pl.BlockSpec((1, tq, w), lambda b, i: (b, i, 0))
    full = lambda a: pl.BlockSpec(a.shape, lambda b, i: (0,) * a.ndim)
    return pl.pallas_call(
        _ssd_kernel,
        grid=(bsz, seqlen // tq),
        in_specs=[tok(SSM_CONV_CH), tok(SSM_WIDTH), tok(LANES), full(conv_w), full(conv_b),
                  full(alog_pad), full(dsk_wide), full(nw)],
        out_specs=tok(SSM_WIDTH),
        out_shape=jax.ShapeDtypeStruct((bsz, seqlen, SSM_WIDTH), BF16),
        scratch_shapes=[pltpu.VMEM((tq + 2 * SUBLANES, SSM_CONV_CH), F32),
                        pltpu.VMEM((tq, SSM_CONV_CH), F32),
                        pltpu.VMEM((tq, SSM_WIDTH), F32),
                        pltpu.VMEM((SSM_GROUPS, SSM_GROUP_WIDTH, SSM_STATE), F32)],
        compiler_params=_params(("arbitrary", "arbitrary")),
        name="ssd",
    )(xbc, zz, dt, conv_w, conv_b, alog_pad, dsk_wide, nw)


def _outproj_kernel(x_ref, oh_ref, ys_ref, wo_ref, g1_ref, nw_ref, sh_ref, sc_ref, rw_ref, rb_ref,
                    x1_ref, h2_ref, idx_ref, gate_ref, cnt_ref):
    first = jnp.logical_and(pl.program_id(0) == 0, pl.program_id(1) == 0)

    @pl.when(first)
    def _():
        cnt_ref[...] = jnp.zeros_like(cnt_ref)

    mix = jnp.dot(oh_ref[0], wo_ref[0:HG_WIDTH, :], preferred_element_type=F32)
    mix = mix + jnp.dot(ys_ref[0], wo_ref[HG_WIDTH:, :], preferred_element_type=F32)
    x1 = x_ref[0] + g1_ref[0] * mix
    x1_ref[0] = x1
    ms = jnp.mean(x1 * x1, axis=-1, keepdims=True)
    h2 = x1 * lax.rsqrt(ms + EPS) * nw_ref[...]
    h2 = h2 * (1.0 + sc_ref[0]) + sh_ref[0]
    h2_ref[0] = h2
    h_hi = h2.astype(BF16)
    h_lo = (h2 - h_hi.astype(F32)).astype(BF16)
    part = jnp.dot(h_hi, rw_ref[...], preferred_element_type=F32)
    logits = (part[:, :LANES] + part[:, LANES:]
              + jnp.dot(h_lo, rw_ref[:, :LANES], preferred_element_type=F32)) + rb_ref[...]
    lane = lax.broadcasted_iota(jnp.int32, logits.shape, 1).astype(F32)
    idx_out = jnp.zeros(logits.shape, F32)
    val_out = jnp.zeros(logits.shape, F32)
    sel = jnp.zeros(logits.shape, F32)
    work = logits
    top = None
    denom = None
    for k in range(TOP_K):
        m = jnp.max(work, axis=-1, keepdims=True)
        am = jnp.min(jnp.where(work == m, lane, float(LANES)), axis=-1, keepdims=True)
        hit = lane == am
        work = jnp.where(hit, -jnp.inf, work)
        sel = jnp.where(hit, 1.0, sel)
        if k == 0:
            top = m
        e = jnp.exp(m - top)
        denom = e if k == 0 else denom + e
        idx_out = jnp.where(lane == float(k), am, idx_out)
        val_out = jnp.where(lane == float(k), e, val_out)
    idx_ref[0] = idx_out.astype(jnp.int32)
    gate_ref[0] = val_out / denom
    cnt_ref[0:1, :] += jnp.sum(sel, axis=0, keepdims=True)


def _outproj(x, o_hg, y_ssd, w_out_b, g1, nw, sh, sc, rw_pad, rb_pad):
    bsz, seqlen, d = x.shape
    tm = min(ROW_TILE, seqlen)
    tok = lambda w: pl.BlockSpec((1, tm, w), lambda b, i: (b, i, 0))
    full = lambda a: pl.BlockSpec(a.shape, lambda b, i: (0,) * a.ndim)
    mod = pl.BlockSpec((1, 1, d), lambda b, i: (b, 0, 0))
    shp = lambda w, dt: jax.ShapeDtypeStruct((bsz, seqlen, w), dt)
    return pl.pallas_call(
        _outproj_kernel,
        grid=(bsz, seqlen // tm),
        in_specs=[tok(d), tok(HG_WIDTH), tok(SSM_WIDTH), full(w_out_b), mod, full(nw), mod, mod,
                  full(rw_pad), full(rb_pad)],
        out_specs=[tok(d), tok(d), tok(LANES), tok(LANES),
                   pl.BlockSpec((SUBLANES, LANES), lambda b, i: (0, 0))],
        out_shape=[shp(d, F32), shp(d, F32), shp(LANES, jnp.int32), shp(LANES, F32),
                   jax.ShapeDtypeStruct((SUBLANES, LANES), F32)],
        compiler_params=_params(("arbitrary", "arbitrary")),
        name="outproj_router",
    )(x, o_hg, y_ssd, w_out_b, g1, nw, sh, sc, rw_pad, rb_pad)


def _route_kernel(idx_ref, pstart_ref, dest_ref, carry_ref):
    @pl.when(pl.program_id(0) == 0)
    def _():
        carry_ref[...] = jnp.zeros_like(carry_ref)

    idx = idx_ref[...]
    tt = idx.shape[0]
    lane = lax.broadcasted_iota(jnp.int32, idx.shape, 1)
    hits = [lane == idx[:, k:k + 1] for k in range(TOP_K)]
    sel = jnp.zeros(idx.shape, F32)
    for hit in hits:
        sel = jnp.where(hit, 1.0, sel)
    rows = lax.broadcasted_iota(jnp.int32, (tt, tt), 0)
    cols = lax.broadcasted_iota(jnp.int32, (tt, tt), 1)
    before = (rows > cols).astype(BF16)
    rank = jnp.dot(before, sel.astype(BF16), preferred_element_type=F32) + carry_ref[0:1, :]
    carry_ref[0:1, :] += jnp.sum(sel, axis=0, keepdims=True)
    dense = pstart_ref[...] + rank
    out = jnp.zeros(idx.shape, F32)
    for k, hit in enumerate(hits):
        dk = jnp.sum(jnp.where(hit, dense, 0.0), axis=-1, keepdims=True)
        out = jnp.where(lane == k, dk, out)
    dest_ref[...] = out.astype(jnp.int32)


def _route(idx_pad, pstart_row):
    t = idx_pad.shape[0]
    tt = min(ROW_TILE, t)
    return pl.pallas_call(
        _route_kernel,
        grid=(t // tt,),
        in_specs=[pl.BlockSpec((tt, LANES), lambda i: (i, 0)),
                  pl.BlockSpec((1, LANES), lambda i: (0, 0))],
        out_specs=pl.BlockSpec((tt, LANES), lambda i: (i, 0)),
        out_shape=jax.ShapeDtypeStruct((t, LANES), jnp.int32),
        scratch_shapes=[pltpu.VMEM((SUBLANES, LANES), F32)],
        compiler_params=_params(("arbitrary",)),
        name="route_rank",
    )(idx_pad, pstart_row)


def _invert_kernel(dest_ref, zeros_ref, rowtok_ref, sem):
    i = pl.program_id(0)
    per_step = dest_ref.shape[0] // TOP_K // pl.num_programs(0)

    @pl.when(i == 0)
    def _():
        fill = pltpu.make_async_copy(zeros_ref, rowtok_ref, sem)
        fill.start()
        fill.wait()

    def tok_body(n, carry):
        tok = i * per_step + n
        for k in range(TOP_K):
            rowtok_ref[dest_ref[tok * TOP_K + k]] = tok
        return carry
    lax.fori_loop(0, per_step, tok_body, 0, unroll=8)


def _invert(dest_flat, n_rows):
    t = dest_flat.shape[0] // TOP_K
    return pl.pallas_call(
        _invert_kernel,
        grid=(max(t // INVERT_TILE, 1),),
        in_specs=[pl.BlockSpec(memory_space=pltpu.SMEM), pl.BlockSpec(memory_space=pl.ANY)],
        out_specs=pl.BlockSpec(memory_space=pltpu.SMEM),
        out_shape=jax.ShapeDtypeStruct((n_rows,), jnp.int32),
        scratch_shapes=[pltpu.SemaphoreType.DMA(())],
        compiler_params=_params(("arbitrary",)),
        name="invert_route",
    )(dest_flat, jnp.zeros((n_rows,), jnp.int32))


def _expert_kernel(be_ref, nact_ref, nxt_ref, rowtok_ref, h_ref, w1_ref, b1_ref, w2_ref, b2_ref, y_ref,
                   xbuf0_ref, xbuf1_ref, xbuf2_ref, w1s_ref, w2s_ref, w1b_ref, w2b_ref, sem, wsem):
    j = pl.program_id(0)
    n_act = nact_ref[0]
    active = j < n_act
    expert = be_ref[j]
    fresh = jnp.logical_or(j == 0, expert != be_ref[jnp.maximum(j - 1, 0)])
    phase = j % GATHER_DEPTH
    bufs = (xbuf0_ref, xbuf1_ref, xbuf2_ref)

    def issue(block, s):
        base = jnp.minimum(block, n_act - 1) * MOE_ROWS
        for r in range(MOE_ROWS):
            pltpu.make_async_copy(h_ref.at[pl.ds(rowtok_ref[base + r], 1), :],
                                  bufs[s].at[pl.ds(r, 1), :], sem.at[s]).start(priority=1)

    def wait(s):
        pltpu.make_async_copy(h_ref.at[pl.ds(0, MOE_ROWS), :], bufs[s], sem.at[s]).wait()

    def weight_copies(e):
        return (pltpu.make_async_copy(w1_ref.at[e], w1s_ref, wsem.at[0]),
                pltpu.make_async_copy(w2_ref.at[e], w2s_ref, wsem.at[1]))

    @pl.when(j == 0)
    def _():
        for cp in weight_copies(expert):
            cp.start()
        for s in range(GATHER_DEPTH - 1):
            issue(s, s)

    @pl.when(jnp.logical_and(active, fresh))
    def _():
        for cp in weight_copies(expert):
            cp.wait()
        w1b_ref[...] = w1s_ref[...].astype(BF16)
        w2b_ref[...] = w2s_ref[...].astype(BF16)

        @pl.when(nxt_ref[expert] < N_EXPERTS)
        def _():
            for cp in weight_copies(nxt_ref[expert]):
                cp.start()

    for s in range(GATHER_DEPTH):
        @pl.when(jnp.logical_and(j < n_act + GATHER_DEPTH - 1, phase == s))
        def _(s=s):
            wait(s)

        @pl.when(jnp.logical_and(active, phase == s))
        def _(s=s):
            issue(j + GATHER_DEPTH - 1, (s + GATHER_DEPTH - 1) % GATHER_DEPTH)
            hb = jnp.dot(bufs[s][...].astype(BF16), w1b_ref[...], preferred_element_type=F32) + b1_ref[0]
            glu = jnp.minimum(hb[:, :D_FF], SWIGLU_LIMIT)
            lin = jnp.clip(hb[:, D_FF:], -SWIGLU_LIMIT, SWIGLU_LIMIT)
            act = glu * jax.nn.sigmoid(SWIGLU_ALPHA * glu) * (lin + 1.0)
            y_ref[...] = jnp.dot(act.astype(BF16), w2b_ref[...], preferred_element_type=F32) + b2_ref[0]


def _experts(block_e, n_act, next_e, row_tok, h2, w1, b1, w2, b2):
    n_rows = row_tok.shape[0]
    d = h2.shape[1]
    nb = block_e.shape[0]
    row_map = lambda j, be, na, nx, rt: (jnp.maximum(jnp.minimum(j, na[0] - 1), 0), 0)
    exp_map = lambda j, be, na, nx, rt: (be[j], 0, 0)
    return pl.pallas_call(
        _expert_kernel,
        grid_spec=pltpu.PrefetchScalarGridSpec(
            num_scalar_prefetch=4,
            grid=(nb,),
            in_specs=[pl.BlockSpec(memory_space=pl.ANY),
                      pl.BlockSpec(memory_space=pl.ANY),
                      pl.BlockSpec((1, 1, 2 * D_FF), exp_map),
                      pl.BlockSpec(memory_space=pl.ANY),
                      pl.BlockSpec((1, 1, d), exp_map)],
            out_specs=pl.BlockSpec((MOE_ROWS, d), row_map),
            scratch_shapes=[pltpu.VMEM((MOE_ROWS, d), F32)] * GATHER_DEPTH + [
                pltpu.VMEM((d, 2 * D_FF), F32), pltpu.VMEM((D_FF, d), F32),
                pltpu.VMEM((d, 2 * D_FF), BF16), pltpu.VMEM((D_FF, d), BF16),
                pltpu.SemaphoreType.DMA((GATHER_DEPTH,)), pltpu.SemaphoreType.DMA((2,))],
        ),
        out_shape=jax.ShapeDtypeStruct((n_rows, d), F32),
        compiler_params=_params(("arbitrary",)),
        name="expert_ffn",
    )(block_e, n_act, next_e, row_tok, h2, w1, b1.reshape(N_EXPERTS, 1, 2 * D_FF), w2,
      b2.reshape(N_EXPERTS, 1, d))


def _combine_kernel(dest_ref, x1_ref, gate_ref, g2_ref, fw_ref, y_ref, o_ref, buf_ref, sem):
    i = pl.program_id(0)
    n = pl.num_programs(0)
    gt = x1_ref.shape[0]

    def row_copy(slot, k, j, src_row):
        return pltpu.make_async_copy(y_ref.at[pl.ds(src_row, 1), :],
                                     buf_ref.at[slot, k, pl.ds(j, 1), :], sem.at[slot])

    def issue(tile, slot):
        base = tile * (gt * TOP_K)
        for j in range(gt):
            for k in range(TOP_K):
                row_copy(slot, k, j, dest_ref[base + j * TOP_K + k]).start(priority=k % 2)

    @pl.when(i == 0)
    def _():
        issue(0, 0)

    for s in range(2):
        @pl.when(jnp.logical_and(i + 1 < n, (i + 1) % 2 == s))
        def _(s=s):
            issue(i + 1, s)

    slot = i % 2

    for k in range(TOP_K):
        pltpu.make_async_copy(y_ref.at[pl.ds(0, gt), :], buf_ref.at[slot, k], sem.at[slot]).wait()

    gates = gate_ref[...]
    moe = gates[:, 0:1] * buf_ref[slot, 0]
    for k in range(1, TOP_K):
        moe = moe + gates[:, k:k + 1] * buf_ref[slot, k]
    x2 = x1_ref[...] + g2_ref[0] * moe
    ms = jnp.mean(x2 * x2, axis=-1, keepdims=True)
    o_ref[...] = x2 * lax.rsqrt(ms + EPS) * fw_ref[...]


def _combine(dest_flat, x1, gate_pad, g2, fw, yout, seqlen):
    t, d = x1.shape
    gt = min(GATHER_TILE, seqlen)
    per_seq = seqlen // gt
    return pl.pallas_call(
        _combine_kernel,
        grid_spec=pltpu.PrefetchScalarGridSpec(
            num_scalar_prefetch=1,
            grid=(t // gt,),
            in_specs=[pl.BlockSpec((gt, d), lambda i, dst: (i, 0)),
                      pl.BlockSpec((gt, LANES), lambda i, dst: (i, 0)),
                      pl.BlockSpec((1, 1, d), lambda i, dst: (i // per_seq, 0, 0)),
                      pl.BlockSpec((1, d), lambda i, dst: (0, 0)),
                      pl.BlockSpec(memory_space=pl.ANY)],
            out_specs=pl.BlockSpec((gt, d), lambda i, dst: (i, 0)),
            scratch_shapes=[pltpu.VMEM((2, TOP_K, gt, d), F32), pltpu.SemaphoreType.DMA((2,))],
        ),
        out_shape=jax.ShapeDtypeStruct((t, d), F32),
        compiler_params=_params(("arbitrary",)),
        name="combine_norm",
    )(dest_flat, x1, gate_pad, g2, fw, yout)


def _pad_lanes(v, fill=0.0):
    v = v.reshape(1, -1).astype(F32)
    return jnp.pad(v, ((0, 0), (0, LANES - v.shape[1])), constant_values=fill)


def kernel(x, c, ada_w, ada_b, norm1_w, w_in, hg_lb_logits, hg_norm_w, ssm_a_log, ssm_dt_bias, ssm_d,
           ssm_conv_w, ssm_conv_b, ssm_norm_w, w_out, norm2_w, router_w, router_b, exp_w1, exp_b1,
           exp_w2, exp_b2, final_norm_w):
    bsz, seqlen, d = x.shape
    t = bsz * seqlen
    l = 0

    c_pad = jnp.pad(c, ((0, SUBLANES - bsz), (0, 0)))
    mod = _ada_mod(c_pad, ada_w[l], ada_b[l])[:bsz]
    sh1, sc1, g1, sh2, sc2, g2 = [m.reshape(bsz, 1, d) for m in jnp.split(mod, N_MOD, axis=-1)]

    w_in_pad = jnp.pad(w_in[l], ((0, 0), (0, IN_COLS_PAD - IN_COLS))).astype(BF16)
    qs, kk, lf, vi, gg, zz, xbc, dt = _inproj(
        x, norm1_w[l].reshape(1, d), sh1, sc1, w_in_pad, hg_lb_logits, _pad_lanes(ssm_dt_bias[l]))

    o_hg = _hgrn(qs, kk, lf, vi, gg, hg_norm_w[l].reshape(1, HG_DV))
    y_ssd = _ssd(xbc, zz, dt, ssm_conv_w[l], ssm_conv_b[l].reshape(1, SSM_CONV_CH),
                 _pad_lanes(ssm_a_log[l]), jnp.repeat(ssm_d[l], SSM_HEAD_DIM).reshape(1, SSM_WIDTH),
                 ssm_norm_w[l].reshape(1, SSM_WIDTH))

    rw_pad = jnp.pad(router_w[l], ((0, 0), (0, LANES - N_EXPERTS)))
    rw_hi = rw_pad.astype(BF16)
    rw_pad = jnp.concatenate([rw_hi, (rw_pad - rw_hi.astype(F32)).astype(BF16)], axis=1)
    rb_pad = _pad_lanes(router_b[l], NEG_BIG)
    x1, h2, idx_pad, gate_pad, counts = _outproj(
        x, o_hg, y_ssd, w_out[l].astype(BF16), g1, norm2_w[l].reshape(1, d), sh2, sc2, rw_pad, rb_pad)

    cnt = counts[0, :N_EXPERTS].astype(jnp.int32)
    blocks_e = (cnt + MOE_ROWS - 1) // MOE_ROWS
    blk_end = jnp.cumsum(blocks_e)
    pstart = (blk_end - blocks_e) * MOE_ROWS
    n_blocks = (t * TOP_K) // MOE_ROWS + N_EXPERTS
    n_rows = n_blocks * MOE_ROWS
    n_steps = n_blocks + GATHER_DEPTH - 1
    block_e = jnp.minimum(jnp.sum(blk_end[None, :] <= jnp.arange(n_steps)[:, None], axis=1),
                          N_EXPERTS - 1).astype(jnp.int32)
    n_act = blk_end[-1:].astype(jnp.int32)
    owner = jnp.where(blocks_e > 0, jnp.arange(N_EXPERTS), N_EXPERTS)
    next_e = jnp.concatenate([lax.cummin(owner, reverse=True)[1:],
                              jnp.full((1,), N_EXPERTS)]).astype(jnp.int32)

    dest_pad = _route(idx_pad.reshape(t, LANES), _pad_lanes(pstart))
    dest_flat = dest_pad[:, :TOP_K].reshape(t * TOP_K)

    row_tok = _invert(dest_flat, n_rows)
    yout = _experts(block_e, n_act, next_e, row_tok, h2.reshape(t, d), exp_w1[l], exp_b1[l], exp_w2[l], exp_b2[l])
    out = _combine(dest_flat, x1.reshape(t, d), gate_pad.reshape(t, LANES), g2,
                   final_norm_w.reshape(1, d), yout, seqlen)
    return out.reshape(bsz, seqlen, d)
```

```python
import functools

import jax
import jax.numpy as jnp
from jax import lax
from jax.experimental import pallas as pl
from jax.experimental.pallas import tpu as pltpu

F32 = jnp.float32
BF16 = jnp.bfloat16
HIGHEST = lax.Precision.HIGHEST

EPS = 1e-6
D_MODEL = 1024
HG_HEADS = 4
HG_DK = 128
HG_DV = 128
HG_QF = HG_HEADS * HG_DK
HG_WIDTH = HG_HEADS * HG_DV
HG_CHUNK = 64
SSM_HEADS = 8
SSM_HEAD_DIM = 64
SSM_WIDTH = SSM_HEADS * SSM_HEAD_DIM
SSM_GROUPS = 2
SSM_GROUP_HEADS = SSM_HEADS // SSM_GROUPS
SSM_GROUP_WIDTH = SSM_WIDTH // SSM_GROUPS
SSM_STATE = 128
SSM_CONV = 4
SSM_CONV_CH = SSM_WIDTH + 2 * SSM_GROUPS * SSM_STATE
SSM_CHUNK = 128
IN_SPLITS = (HG_QF, HG_QF, HG_WIDTH, HG_WIDTH, SSM_WIDTH, SSM_CONV_CH, SSM_HEADS)
IN_COLS = sum(IN_SPLITS)
N_EXPERTS = 32
TOP_K = 4
D_FF = 1024
SWIGLU_LIMIT = 7.0
SWIGLU_ALPHA = 1.702
N_MOD = 6

LANES = 128
SUBLANES = 8
VMEM_LIMIT = 56 * 1024 * 1024

ROW_TILE = 512
SEQ_TILE = 1024
MOE_ROWS = 256
GATHER_TILE = 128
INVERT_TILE = 1024
GATHER_DEPTH = 3
NEG_BIG = -1e30


def _silu(v):
    return v * jax.nn.sigmoid(v)


def _softplus(v):
    return jnp.maximum(v, 0.0) + jnp.log1p(jnp.exp(-jnp.abs(v)))


def _split3(v):
    hi = v.astype(BF16)
    r1 = v - hi.astype(F32)
    mid = r1.astype(BF16)
    lo = (r1 - mid.astype(F32)).astype(BF16)
    return jnp.concatenate([hi, mid, lo], axis=-1)


def _params(sem):
    return pltpu.CompilerParams(dimension_semantics=sem, vmem_limit_bytes=VMEM_LIMIT)


def _ada_kernel(c_ref, w_ref, b_ref, o_ref):
    o_ref[...] = jnp.dot(_silu(c_ref[...]), w_ref[...], precision=HIGHEST,
                         preferred_element_type=F32) + b_ref[...]


def _ada_mod(c_pad, ada_w, ada_b):
    n = ada_w.shape[1]
    tn = D_MODEL
    return pl.pallas_call(
        _ada_kernel,
        grid=(n // tn,),
        in_specs=[pl.BlockSpec((SUBLANES, D_MODEL), lambda j: (0, 0)),
                  pl.BlockSpec((D_MODEL, tn), lambda j: (0, j)),
                  pl.BlockSpec((1, tn), lambda j: (0, j))],
        out_specs=pl.BlockSpec((SUBLANES, tn), lambda j: (0, j)),
        out_shape=jax.ShapeDtypeStruct((SUBLANES, n), F32),
        compiler_params=_params(("arbitrary",)),
        name="ada_mod",
    )(c_pad, ada_w, ada_b.reshape(1, n))


_OFF = [0]
for _w in IN_SPLITS:
    _OFF.append(_OFF[-1] + _w)
IN_COLS_PAD = _OFF[6] + LANES


def _inproj_kernel(x_ref, nw_ref, sh_ref, sc_ref, w_ref, lbl_ref, dtb_ref,
                   qs_ref, kk_ref, lf_ref, vi_ref, gg_ref, zz_ref, xbc_ref, dt_ref):
    x = x_ref[0]
    ms = jnp.mean(x * x, axis=-1, keepdims=True)
    h = x * lax.rsqrt(ms + EPS) * nw_ref[...]
    h = h * (1.0 + sc_ref[0]) + sh_ref[0]
    hb = h.astype(BF16)

    def seg(k):
        return jnp.dot(hb, w_ref[:, _OFF[k]:_OFF[k] + (IN_SPLITS[k] if k < 6 else LANES)],
                       preferred_element_type=F32)

    lbl = lbl_ref[...]
    le = jnp.exp(lbl - jnp.max(lbl, axis=0, keepdims=True))
    lb = le[0:1, :] / jnp.sum(le, axis=0, keepdims=True)

    qs_ref[0] = _silu(seg(0)).astype(BF16)
    fg = lb + (1.0 - lb) * jax.nn.sigmoid(seg(1))
    kk_ref[0] = (1.0 - fg).astype(BF16)
    lf_ref[0] = jnp.log(fg)
    vi_ref[0] = seg(2).astype(BF16)
    gg_ref[0] = _silu(seg(3)).astype(BF16)
    zz_ref[0] = _silu(seg(4)).astype(BF16)
    xbc_ref[0] = seg(5).astype(BF16)
    dt_ref[0] = _softplus(seg(6) + dtb_ref[...])


def _inproj(x, nw, sh, sc, w_in_pad, lb_logits, dtb_pad):
    bsz, seqlen, d = x.shape
    tm = min(ROW_TILE, seqlen)
    nt = seqlen // tm
    tok = lambda w: pl.BlockSpec((1, tm, w), lambda b, i: (b, i, 0))
    full = lambda a: pl.BlockSpec(a.shape, lambda b, i: (0,) * a.ndim)
    mod = pl.BlockSpec((1, 1, d), lambda b, i: (b, 0, 0))
    shp = lambda w, dt: jax.ShapeDtypeStruct((bsz, seqlen, w), dt)
    return pl.pallas_call(
        _inproj_kernel,
        grid=(bsz, nt),
        in_specs=[tok(d), full(nw), mod, mod, full(w_in_pad), full(lb_logits), full(dtb_pad)],
        out_specs=[tok(HG_QF), tok(HG_QF), tok(HG_QF), tok(HG_WIDTH), tok(HG_WIDTH),
                   tok(SSM_WIDTH), tok(SSM_CONV_CH), tok(LANES)],
        out_shape=[shp(HG_QF, BF16), shp(HG_QF, BF16), shp(HG_QF, F32), shp(HG_WIDTH, BF16),
                   shp(HG_WIDTH, BF16), shp(SSM_WIDTH, BF16), shp(SSM_CONV_CH, BF16),
                   shp(LANES, F32)],
        compiler_params=_params(("arbitrary", "arbitrary")),
        name="inproj",
    )(x, nw, sh, sc, w_in_pad, lb_logits, dtb_pad)


def _hgrn_kernel(qs_ref, kk_ref, lf_ref, vi_ref, gg_ref, nw_ref, o_ref, st_ref):
    @pl.when(pl.program_id(2) == 0)
    def _():
        st_ref[...] = jnp.zeros_like(st_ref)

    c = HG_CHUNK
    tq = qs_ref.shape[1]
    nc = tq // c
    rows = lax.broadcasted_iota(jnp.int32, (nc, c, c), 1)
    cols = lax.broadcasted_iota(jnp.int32, (nc, c, c), 2)
    causal = rows >= cols
    chunked = lambda ref: ref[0].reshape(nc, c, ref.shape[2])
    q = chunked(qs_ref).astype(F32)
    k = chunked(kk_ref).astype(F32)
    v = chunked(vi_ref)
    bmm = lambda a, bb, dims, **kw: lax.dot_general(a, bb, (dims, ((0,), (0,))),
                                                    preferred_element_type=F32, **kw)
    b3 = bmm(causal.astype(BF16), _split3(chunked(lf_ref)), ((2,), (1,)))
    b = b3[:, :, :HG_DK] + b3[:, :, HG_DK:2 * HG_DK] + b3[:, :, 2 * HG_DK:]
    b_mid = b[:, c // 2 - 1:c // 2, :]
    b_last = b[:, c - 1:c, :]
    qa = (q * jnp.exp(b - b_mid)).astype(BF16)
    ka = (k * jnp.exp(b_mid - b)).astype(BF16)
    att = bmm(qa, ka, ((2,), (2,)))
    att = jnp.where(causal, att, 0.0).astype(BF16)
    o = bmm(att, v, ((2,), (1,)))
    kd = (k * jnp.exp(b_last - b)).astype(BF16)
    d_st = bmm(v, kd, ((1,), (1,)))
    decay = jnp.exp(b_last)
    st = st_ref[...]
    entering = []
    for ci in range(nc):
        entering.append(st.astype(BF16))
        st = st * decay[ci] + d_st[ci]
    st_ref[...] = st
    qb = (q * jnp.exp(b)).astype(BF16)
    o = o + bmm(qb, jnp.stack(entering), ((2,), (2,)))
    ms = jnp.mean(o * o, axis=-1, keepdims=True)
    y = o * lax.rsqrt(ms + EPS) * nw_ref[...] * chunked(gg_ref).astype(F32)
    o_ref[0] = y.reshape(tq, o_ref.shape[2]).astype(BF16)


def _hgrn(qs, kk, lf, vi, gg, nw):
    bsz, seqlen, _ = qs.shape
    tq = min(SEQ_TILE, seqlen)
    blk = pl.BlockSpec((1, tq, HG_DK), lambda b, h, i: (b, i, h))
    return pl.pallas_call(
        _hgrn_kernel,
        grid=(bsz, HG_HEADS, seqlen // tq),
        in_specs=[blk, blk, blk, blk, blk, pl.BlockSpec((1, HG_DV), lambda b, h, i: (0, 0))],
        out_specs=blk,
        out_shape=jax.ShapeDtypeStruct((bsz, seqlen, HG_WIDTH), BF16),
        scratch_shapes=[pltpu.VMEM((HG_DV, HG_DK), F32)],
        compiler_params=_params(("arbitrary", "arbitrary", "arbitrary")),
        name="hgrn2",
    )(qs, kk, lf, vi, gg, nw)


def _ssd_kernel(xbc_ref, zz_ref, dt_ref, cw_ref, cb_ref, alog_ref, dsk_ref, nw_ref, ex_ref,
                y_ref, buf_ref, xc_ref, yc_ref, st_ref):
    tq = xbc_ref.shape[1]
    halo = SUBLANES

    @pl.when(pl.program_id(1) == 0)
    def _():
        buf_ref[0:halo, :] = jnp.zeros((halo, SSM_CONV_CH), F32)
        st_ref[...] = jnp.zeros_like(st_ref)

    buf_ref[halo:halo + tq, :] = xbc_ref[0].astype(F32)
    conv = cb_ref[...] + buf_ref[halo - 3:halo - 3 + tq, :] * cw_ref[0:1, :]
    for w in range(1, SSM_CONV):
        conv = conv + buf_ref[halo - 3 + w:halo - 3 + w + tq, :] * cw_ref[w:w + 1, :]
    buf_ref[0:halo, :] = buf_ref[tq:tq + halo, :]
    xc_ref[...] = _silu(conv)

    c = SSM_CHUNK
    p = SSM_HEAD_DIM
    nc = tq // c
    rows = lax.broadcasted_iota(jnp.int32, (nc, c, c), 1)
    cols = lax.broadcasted_iota(jnp.int32, (nc, c, c), 2)
    causal = rows >= cols
    tril = causal.astype(BF16)
    bmm = lambda a, bb, dims, **kw: lax.dot_general(a, bb, (dims, ((0,), (0,))),
                                                    preferred_element_type=F32, **kw)
    chunked = lambda val: val.reshape(nc, c, val.shape[-1])
    dt = chunked(dt_ref[0])
    da = dt * (-jnp.exp(alog_ref[...]))
    da3 = _split3(da)
    a3 = bmm(tril, da3, ((2,), (1,)))
    acum = a3[:, :, :LANES] + a3[:, :, LANES:2 * LANES] + a3[:, :, 2 * LANES:]
    at3 = bmm(da3, tril, ((1,), (2,)))
    acum_t = at3[:, :LANES, :] + at3[:, LANES:2 * LANES, :] + at3[:, 2 * LANES:, :]
    widen = lambda v: jnp.dot(_split3(v.reshape(tq, LANES)), ex_ref[...], preferred_element_type=F32)
    dt_w = widen(dt)
    decay_out_w = widen(jnp.exp(acum))
    decay_in_w = widen(jnp.exp(acum[:, c - 1:c, :] - acum))
    xdt_w = xc_ref[:, 0:SSM_WIDTH] * dt_w
    xdd_w = xdt_w * decay_in_w
    y_inter_groups = []
    for g in range(SSM_GROUPS):
        gs = slice(g * SSM_GROUP_WIDTH, (g + 1) * SSM_GROUP_WIDTH)
        bm = chunked(xc_ref[:, SSM_WIDTH + g * SSM_STATE:SSM_WIDTH + (g + 1) * SSM_STATE]).astype(BF16)
        cm = chunked(xc_ref[:, SSM_WIDTH + (SSM_GROUPS + g) * SSM_STATE:
                            SSM_WIDTH + (SSM_GROUPS + g + 1) * SSM_STATE]).astype(BF16)
        cb = bmm(cm, bm, ((2,), (2,)))
        d_st = bmm(chunked(xdd_w[:, gs]).astype(BF16), bm, ((1,), (1,)))
        a_last = []
        for hh in range(SSM_GROUP_HEADS):
            h = g * SSM_GROUP_HEADS + hh
            hs = slice(h * p, (h + 1) * p)
            seg = jnp.exp(jnp.where(causal, acum[:, :, h:h + 1] - acum_t[:, h:h + 1, :], -jnp.inf))
            y = bmm((cb * seg).astype(BF16), chunked(xdt_w[:, hs]).astype(BF16), ((2,), (1,)))
            yc_ref[:, hs] = y.reshape(tq, p)
            a_last.append(jnp.exp(acum_t[:, h:h + 1, c - 1:c]))
        st = [st_ref[g, hh * p:(hh + 1) * p, :] for hh in range(SSM_GROUP_HEADS)]
        entering = []
        for ci in range(nc):
            entering.append(jnp.concatenate(st, axis=0).astype(BF16))
            st = [st[hh] * a_last[hh][ci] + d_st[ci, hh * p:(hh + 1) * p, :] for hh in range(SSM_GROUP_HEADS)]
        for hh in range(SSM_GROUP_HEADS):
            st_ref[g, hh * p:(hh + 1) * p, :] = st[hh]
        y_inter_groups.append(bmm(cm, jnp.stack(entering), ((2,), (2,))).reshape(tq, SSM_GROUP_WIDTH))
    y_inter = jnp.concatenate(y_inter_groups, axis=-1)
    y = yc_ref[...] + decay_out_w * y_inter
    y = (y + dsk_ref[...] * xc_ref[:, 0:SSM_WIDTH]) * zz_ref[0].astype(F32)
    for g in range(SSM_GROUPS):
        gs = slice(g * SSM_GROUP_WIDTH, (g + 1) * SSM_GROUP_WIDTH)
        yg = y[:, gs]
        ms = jnp.mean(yg * yg, axis=-1, keepdims=True)
        y_ref[0, :, gs] = (yg * lax.rsqrt(ms + EPS) * nw_ref[:, gs]).astype(BF16)


def _ssd(xbc, zz, dt, conv_w, conv_b, alog_pad, dsk_wide, nw, expand3):
    bsz, seqlen, _ = xbc.shape
    tq = min(SEQ_TILE, seqlen)
    tok = lambda w: pl.BlockSpec((1, tq, w), lambda b, i: (b, i, 0))
    full = lambda a: pl.BlockSpec(a.shape, lambda b, i: (0,) * a.ndim)
    return pl.pallas_call(
        _ssd_kernel,
        grid=(bsz, seqlen // tq),
        in_specs=[tok(SSM_CONV_CH), tok(SSM_WIDTH), tok(LANES), full(conv_w), full(conv_b),
                  full(alog_pad), full(dsk_wide), full(nw), full(expand3)],
        out_specs=tok(SSM_WIDTH),
        out_shape=jax.ShapeDtypeStruct((bsz, seqlen, SSM_WIDTH), BF16),
        scratch_shapes=[pltpu.VMEM((tq + 2 * SUBLANES, SSM_CONV_CH), F32),
                        pltpu.VMEM((tq, SSM_CONV_CH), F32),
                        pltpu.VMEM((tq, SSM_WIDTH), F32),
                        pltpu.VMEM((SSM_GROUPS, SSM_GROUP_WIDTH, SSM_STATE), F32)],
        compiler_params=_params(("arbitrary", "arbitrary")),
        name="ssd",
    )(xbc, zz, dt, conv_w, conv_b, alog_pad, dsk_wide, nw, expand3)


def _outproj_kernel(x_ref, oh_ref, ys_ref, wo_ref, g1_ref, nw_ref, sh_ref, sc_ref, rw_ref, rb_ref,
                    x1_ref, h2_ref, idx_ref, gate_ref, cnt_ref):
    first = jnp.logical_and(pl.program_id(0) == 0, pl.program_id(1) == 0)

    @pl.when(first)
    def _():
        cnt_ref[...] = jnp.zeros_like(cnt_ref)

    mix = jnp.dot(oh_ref[0], wo_ref[0:HG_WIDTH, :], preferred_element_type=F32)
    mix = mix + jnp.dot(ys_ref[0], wo_ref[HG_WIDTH:, :], preferred_element_type=F32)
    x1 = x_ref[0] + g1_ref[0] * mix
    x1_ref[0] = x1
    ms = jnp.mean(x1 * x1, axis=-1, keepdims=True)
    h2 = x1 * lax.rsqrt(ms + EPS) * nw_ref[...]
    h2 = h2 * (1.0 + sc_ref[0]) + sh_ref[0]
    h2_ref[0] = h2
    h_hi = h2.astype(BF16)
    h_lo = (h2 - h_hi.astype(F32)).astype(BF16)
    part = jnp.dot(h_hi, rw_ref[...], preferred_element_type=F32)
    logits = (part[:, :LANES] + part[:, LANES:]
              + jnp.dot(h_lo, rw_ref[:, :LANES], preferred_element_type=F32)) + rb_ref[...]
    lane = lax.broadcasted_iota(jnp.int32, logits.shape, 1).astype(F32)
    idx_out = jnp.zeros(logits.shape, F32)
    val_out = jnp.zeros(logits.shape, F32)
    sel = jnp.zeros(logits.shape, F32)
    work = logits
    top = None
    denom = None
    for k in range(TOP_K):
        m = jnp.max(work, axis=-1, keepdims=True)
        am = jnp.min(jnp.where(work == m, lane, float(LANES)), axis=-1, keepdims=True)
        hit = lane == am
        work = jnp.where(hit, -jnp.inf, work)
        sel = jnp.where(hit, 1.0, sel)
        if k == 0:
            top = m
        e = jnp.exp(m - top)
        denom = e if k == 0 else denom + e
        idx_out = jnp.where(lane == float(k), am, idx_out)
        val_out = jnp.where(lane == float(k), e, val_out)
    idx_ref[0] = idx_out.astype(jnp.int32)
    gate_ref[0] = val_out / denom
    cnt_ref[0:1, :] += jnp.sum(sel, axis=0, keepdims=True)


def _outproj(x, o_hg, y_ssd, w_out_b, g1, nw, sh, sc, rw_pad, rb_pad):
    bsz, seqlen, d = x.shape
    tm = min(ROW_TILE, seqlen)
    tok = lambda w: pl.BlockSpec((1, tm, w), lambda b, i: (b, i, 0))
    full = lambda a: pl.BlockSpec(a.shape, lambda b, i: (0,) * a.ndim)
    mod = pl.BlockSpec((1, 1, d), lambda b, i: (b, 0, 0))
    shp = lambda w, dt: jax.ShapeDtypeStruct((bsz, seqlen, w), dt)
    return pl.pallas_call(
        _outproj_kernel,
        grid=(bsz, seqlen // tm),
        in_specs=[tok(d), tok(HG_WIDTH), tok(SSM_WIDTH), full(w_out_b), mod, full(nw), mod, mod,
                  full(rw_pad), full(rb_pad)],
        out_specs=[tok(d), tok(d), tok(LANES), tok(LANES),
                   pl.BlockSpec((SUBLANES, LANES), lambda b, i: (0, 0))],
        out_shape=[shp(d, F32), shp(d, F32), shp(LANES, jnp.int32), shp(LANES, F32),
                   jax.ShapeDtypeStruct((SUBLANES, LANES), F32)],
        compiler_params=_params(("arbitrary", "arbitrary")),
        name="outproj_router",
    )(x, o_hg, y_ssd, w_out_b, g1, nw, sh, sc, rw_pad, rb_pad)


def _route_kernel(idx_ref, pstart_ref, dest_ref, carry_ref):
    @pl.when(pl.program_id(0) == 0)
    def _():
        carry_ref[...] = jnp.zeros_like(carry_ref)

    idx = idx_ref[...]
    tt = idx.shape[0]
    lane = lax.broadcasted_iota(jnp.int32, idx.shape, 1)
    hits = [lane == idx[:, k:k + 1] for k in range(TOP_K)]
    sel = jnp.zeros(idx.shape, F32)
    for hit in hits:
        sel = jnp.where(hit, 1.0, sel)
    rows = lax.broadcasted_iota(jnp.int32, (tt, tt), 0)
    cols = lax.broadcasted_iota(jnp.int32, (tt, tt), 1)
    before = (rows > cols).astype(BF16)
    rank = jnp.dot(before, sel.astype(BF16), preferred_element_type=F32) + carry_ref[0:1, :]
    carry_ref[0:1, :] += jnp.sum(sel, axis=0, keepdims=True)
    dense = pstart_ref[...] + rank
    out = jnp.zeros(idx.shape, F32)
    for k, hit in enumerate(hits):
        dk = jnp.sum(jnp.where(hit, dense, 0.0), axis=-1, keepdims=True)
        out = jnp.where(lane == k, dk, out)
    dest_ref[...] = out.astype(jnp.int32)


def _route(idx_pad, pstart_row):
    t = idx_pad.shape[0]
    tt = min(ROW_TILE, t)
    return pl.pallas_call(
        _route_kernel,
        grid=(t // tt,),
        in_specs=[pl.BlockSpec((tt, LANES), lambda i: (i, 0)),
                  pl.BlockSpec((1, LANES), lambda i: (0, 0))],
        out_specs=pl.BlockSpec((tt, LANES), lambda i: (i, 0)),
        out_shape=jax.ShapeDtypeStruct((t, LANES), jnp.int32),
        scratch_shapes=[pltpu.VMEM((SUBLANES, LANES), F32)],
        compiler_params=_params(("arbitrary",)),
        name="route_rank",
    )(idx_pad, pstart_row)


def _invert_kernel(dest_ref, zeros_ref, rowtok_ref, sem):
    i = pl.program_id(0)
    per_step = dest_ref.shape[0] // TOP_K // pl.num_programs(0)

    @pl.when(i == 0)
    def _():
        fill = pltpu.make_async_copy(zeros_ref, rowtok_ref, sem)
        fill.start()
        fill.wait()

    def tok_body(n, carry):
        tok = i * per_step + n
        for k in range(TOP_K):
            rowtok_ref[dest_ref[tok * TOP_K + k]] = tok
        return carry
    lax.fori_loop(0, per_step, tok_body, 0, unroll=8)


def _invert(dest_flat, n_rows):
    t = dest_flat.shape[0] // TOP_K
    return pl.pallas_call(
        _invert_kernel,
        grid=(max(t // INVERT_TILE, 1),),
        in_specs=[pl.BlockSpec(memory_space=pltpu.SMEM), pl.BlockSpec(memory_space=pl.ANY)],
        out_specs=pl.BlockSpec(memory_space=pltpu.SMEM),
        out_shape=jax.ShapeDtypeStruct((n_rows,), jnp.int32),
        scratch_shapes=[pltpu.SemaphoreType.DMA(())],
        compiler_params=_params(("arbitrary",)),
        name="invert_route",
    )(dest_flat, jnp.zeros((n_rows,), jnp.int32))


def _expert_kernel(be_ref, nact_ref, nxt_ref, rowtok_ref, h_ref, w1_ref, b1_ref, w2_ref, b2_ref, y_ref,
                   xbuf0_ref, xbuf1_ref, xbuf2_ref, w1s_ref, w2s_ref, w1b_ref, w2b_ref, sem, wsem):
    j = pl.program_id(0)
    n_act = nact_ref[0]
    active = j < n_act
    expert = be_ref[j]
    fresh = jnp.logical_or(j == 0, expert != be_ref[jnp.maximum(j - 1, 0)])
    phase = j % GATHER_DEPTH
    bufs = (xbuf0_ref, xbuf1_ref, xbuf2_ref)

    def issue(block, s):
        base = jnp.minimum(block, n_act - 1) * MOE_ROWS
        for r in range(MOE_ROWS):
            pltpu.make_async_copy(h_ref.at[pl.ds(rowtok_ref[base + r], 1), :],
                                  bufs[s].at[pl.ds(r, 1), :], sem.at[s]).start(priority=1)

    def wait(s):
        pltpu.make_async_copy(h_ref.at[pl.ds(0, MOE_ROWS), :], bufs[s], sem.at[s]).wait()

    def weight_copies(e):
        return (pltpu.make_async_copy(w1_ref.at[e], w1s_ref, wsem.at[0]),
                pltpu.make_async_copy(w2_ref.at[e], w2s_ref, wsem.at[1]))

    @pl.when(j == 0)
    def _():
        for cp in weight_copies(expert):
            cp.start()
        for s in range(GATHER_DEPTH - 1):
            issue(s, s)

    @pl.when(jnp.logical_and(active, fresh))
    def _():
        for cp in weight_copies(expert):
            cp.wait()
        w1b_ref[...] = w1s_ref[...].astype(BF16)
        w2b_ref[...] = w2s_ref[...].astype(BF16)

        @pl.when(nxt_ref[expert] < N_EXPERTS)
        def _():
            for cp in weight_copies(nxt_ref[expert]):
                cp.start()

    for s in range(GATHER_DEPTH):
        @pl.when(jnp.logical_and(j < n_act + GATHER_DEPTH - 1, phase == s))
        def _(s=s):
            wait(s)

        @pl.when(jnp.logical_and(active, phase == s))
        def _(s=s):
            issue(j + GATHER_DEPTH - 1, (s + GATHER_DEPTH - 1) % GATHER_DEPTH)
            hb = jnp.dot(bufs[s][...].astype(BF16), w1b_ref[...], preferred_element_type=F32) + b1_ref[0]
            glu = jnp.minimum(hb[:, :D_FF], SWIGLU_LIMIT)
            lin = jnp.clip(hb[:, D_FF:], -SWIGLU_LIMIT, SWIGLU_LIMIT)
            act = glu * jax.nn.sigmoid(SWIGLU_ALPHA * glu) * (lin + 1.0)
            y_ref[...] = jnp.dot(act.astype(BF16), w2b_ref[...], preferred_element_type=F32) + b2_ref[0]


def _experts(block_e, n_act, next_e, row_tok, h2, w1, b1, w2, b2):
    n_rows = row_tok.shape[0]
    d = h2.shape[1]
    nb = block_e.shape[0]
    row_map = lambda j, be, na, nx, rt: (jnp.maximum(jnp.minimum(j, na[0] - 1), 0), 0)
    exp_map = lambda j, be, na, nx, rt: (be[j], 0, 0)
    return pl.pallas_call(
        _expert_kernel,
        grid_spec=pltpu.PrefetchScalarGridSpec(
            num_scalar_prefetch=4,
            grid=(nb,),
            in_specs=[pl.BlockSpec(memory_space=pl.ANY),
                      pl.BlockSpec(memory_space=pl.ANY),
                      pl.BlockSpec((1, 1, 2 * D_FF), exp_map),
                      pl.BlockSpec(memory_space=pl.ANY),
                      pl.BlockSpec((1, 1, d), exp_map)],
            out_specs=pl.BlockSpec((MOE_ROWS, d), row_map),
            scratch_shapes=[pltpu.VMEM((MOE_ROWS, d), F32)] * GATHER_DEPTH + [
                pltpu.VMEM((d, 2 * D_FF), F32), pltpu.VMEM((D_FF, d), F32),
                pltpu.VMEM((d, 2 * D_FF), BF16), pltpu.VMEM((D_FF, d), BF16),
                pltpu.SemaphoreType.DMA((GATHER_DEPTH,)), pltpu.SemaphoreType.DMA((2,))],
        ),
        out_shape=jax.ShapeDtypeStruct((n_rows, d), F32),
        compiler_params=_params(("arbitrary",)),
        name="expert_ffn",
    )(block_e, n_act, next_e, row_tok, h2, w1, b1.reshape(N_EXPERTS, 1, 2 * D_FF), w2,
      b2.reshape(N_EXPERTS, 1, d))


def _combine_kernel(dest_ref, x1_ref, gate_ref, g2_ref, fw_ref, y_ref, o_ref, buf_ref, sem):
    i = pl.program_id(0)
    n = pl.num_programs(0)
    gt = x1_ref.shape[0]

    def row_copy(slot, k, j, src_row):
        return pltpu.make_async_copy(y_ref.at[pl.ds(src_row, 1), :],
                                     buf_ref.at[slot, k, pl.ds(j, 1), :], sem.at[slot])

    def issue(tile, slot):
        base = tile * (gt * TOP_K)
        for j in range(gt):
            for k in range(TOP_K):
                row_copy(slot, k, j, dest_ref[base + j * TOP_K + k]).start(priority=k % 2)

    @pl.when(i == 0)
    def _():
        issue(0, 0)

    for s in range(2):
        @pl.when(jnp.logical_and(i + 1 < n, (i + 1) % 2 == s))
        def _(s=s):
            issue(i + 1, s)

    slot = i % 2

    for k in range(TOP_K):
        pltpu.make_async_copy(y_ref.at[pl.ds(0, gt), :], buf_ref.at[slot, k], sem.at[slot]).wait()

    gates = gate_ref[...]
    moe = gates[:, 0:1] * buf_ref[slot, 0]
    for k in range(1, TOP_K):
        moe = moe + gates[:, k:k + 1] * buf_ref[slot, k]
    x2 = x1_ref[...] + g2_ref[0] * moe
    ms = jnp.mean(x2 * x2, axis=-1, keepdims=True)
    o_ref[...] = x2 * lax.rsqrt(ms + EPS) * fw_ref[...]


def _combine(dest_flat, x1, gate_pad, g2, fw, yout, seqlen):
    t, d = x1.shape
    gt = min(GATHER_TILE, seqlen)
    per_seq = seqlen // gt
    return pl.pallas_call(
        _combine_kernel,
        grid_spec=pltpu.PrefetchScalarGridSpec(
            num_scalar_prefetch=1,
            grid=(t // gt,),
            in_specs=[pl.BlockSpec((gt, d), lambda i, dst: (i, 0)),
                      pl.BlockSpec((gt, LANES), lambda i, dst: (i, 0)),
                      pl.BlockSpec((1, 1, d), lambda i, dst: (i // per_seq, 0, 0)),
                      pl.BlockSpec((1, d), lambda i, dst: (0, 0)),
                      pl.BlockSpec(memory_space=pl.ANY)],
            out_specs=pl.BlockSpec((gt, d), lambda i, dst: (i, 0)),
            scratch_shapes=[pltpu.VMEM((2, TOP_K, gt, d), F32), pltpu.SemaphoreType.DMA((2,))],
        ),
        out_shape=jax.ShapeDtypeStruct((t, d), F32),
        compiler_params=_params(("arbitrary",)),
        name="combine_norm",
    )(dest_flat, x1, gate_pad, g2, fw, yout)


def _head_expand():
    head_of_lane = jnp.arange(SSM_WIDTH) // SSM_HEAD_DIM
    one = (jnp.arange(LANES)[:, None] == head_of_lane[None, :]).astype(BF16)
    return jnp.tile(one, (3, 1))


def _pad_lanes(v, fill=0.0):
    v = v.reshape(1, -1).astype(F32)
    return jnp.pad(v, ((0, 0), (0, LANES - v.shape[1])), constant_values=fill)


def kernel(x, c, ada_w, ada_b, norm1_w, w_in, hg_lb_logits, hg_norm_w, ssm_a_log, ssm_dt_bias, ssm_d,
           ssm_conv_w, ssm_conv_b, ssm_norm_w, w_out, norm2_w, router_w, router_b, exp_w1, exp_b1,
           exp_w2, exp_b2, final_norm_w):
    bsz, seqlen, d = x.shape
    t = bsz * seqlen
    l = 0

    c_pad = jnp.pad(c, ((0, SUBLANES - bsz), (0, 0)))
    mod = _ada_mod(c_pad, ada_w[l], ada_b[l])[:bsz]
    sh1, sc1, g1, sh2, sc2, g2 = [m.reshape(bsz, 1, d) for m in jnp.split(mod, N_MOD, axis=-1)]

    w_in_pad = jnp.pad(w_in[l], ((0, 0), (0, IN_COLS_PAD - IN_COLS))).astype(BF16)
    qs, kk, lf, vi, gg, zz, xbc, dt = _inproj(
        x, norm1_w[l].reshape(1, d), sh1, sc1, w_in_pad, hg_lb_logits, _pad_lanes(ssm_dt_bias[l]))

    o_hg = _hgrn(qs, kk, lf, vi, gg, hg_norm_w[l].reshape(1, HG_DV))
    y_ssd = _ssd(xbc, zz, dt, ssm_conv_w[l], ssm_conv_b[l].reshape(1, SSM_CONV_CH),
                 _pad_lanes(ssm_a_log[l]), jnp.repeat(ssm_d[l], SSM_HEAD_DIM).reshape(1, SSM_WIDTH),
                 ssm_norm_w[l].reshape(1, SSM_WIDTH), _head_expand())

    rw_pad = jnp.pad(router_w[l], ((0, 0), (0, LANES - N_EXPERTS)))
    rw_hi = rw_pad.astype(BF16)
    rw_pad = jnp.concatenate([rw_hi, (rw_pad - rw_hi.astype(F32)).astype(BF16)], axis=1)
    rb_pad = _pad_lanes(router_b[l], NEG_BIG)
    x1, h2, idx_pad, gate_pad, counts = _outproj(
        x, o_hg, y_ssd, w_out[l].astype(BF16), g1, norm2_w[l].reshape(1, d), sh2, sc2, rw_pad, rb_pad)

    cnt = counts[0, :N_EXPERTS].astype(jnp.int32)
    blocks_e = (cnt + MOE_ROWS - 1) // MOE_ROWS
    blk_end = jnp.cumsum(blocks_e)
    pstart = (blk_end - blocks_e) * MOE_ROWS
    n_blocks = (t * TOP_K) // MOE_ROWS + N_EXPERTS
    n_rows = n_blocks * MOE_ROWS
    n_steps = n_blocks + GATHER_DEPTH - 1
    block_e = jnp.minimum(jnp.sum(blk_end[None, :] <= jnp.arange(n_steps)[:, None], axis=1),
                          N_EXPERTS - 1).astype(jnp.int32)
    n_act = blk_end[-1:].astype(jnp.int32)
    owner = jnp.where(blocks_e > 0, jnp.arange(N_EXPERTS), N_EXPERTS)
    next_e = jnp.concatenate([lax.cummin(owner, reverse=True)[1:],
                              jnp.full((1,), N_EXPERTS)]).astype(jnp.int32)

    dest_pad = _route(idx_pad.reshape(t, LANES), _pad_lanes(pstart))
    dest_flat = dest_pad[:, :TOP_K].reshape(t * TOP_K)

    row_tok = _invert(dest_flat, n_rows)
    yout = _experts(block_e, n_act, next_e, row_tok, h2.reshape(t, d), exp_w1[l], exp_b1[l], exp_w2[l], exp_b2[l])
    out = _combine(dest_flat, x1.reshape(t, d), gate_pad.reshape(t, LANES), g2,
                   final_norm_w.reshape(1, d), yout, seqlen)
    return out.reshape(bsz, seqlen, d)
```

```python
import functools

import jax
import jax.numpy as jnp
from jax import lax
from jax.experimental import pallas as pl
from jax.experimental.pallas import tpu as pltpu

F32 = jnp.float32
BF16 = jnp.bfloat16
HIGHEST = lax.Precision.HIGHEST

EPS = 1e-6
D_MODEL = 1024
HG_HEADS = 4
HG_DK = 128
HG_DV = 128
HG_QF = HG_HEADS * HG_DK
HG_WIDTH = HG_HEADS * HG_DV
HG_CHUNK = 64
SSM_HEADS = 8
SSM_HEAD_DIM = 64
SSM_WIDTH = SSM_HEADS * SSM_HEAD_DIM
SSM_GROUPS = 2
SSM_GROUP_HEADS = SSM_HEADS // SSM_GROUPS
SSM_GROUP_WIDTH = SSM_WIDTH // SSM_GROUPS
SSM_STATE = 128
SSM_CONV = 4
SSM_CONV_CH = SSM_WIDTH + 2 * SSM_GROUPS * SSM_STATE
SSM_CHUNK = 128
IN_SPLITS = (HG_QF, HG_QF, HG_WIDTH, HG_WIDTH, SSM_WIDTH, SSM_CONV_CH, SSM_HEADS)
IN_COLS = sum(IN_SPLITS)
N_EXPERTS = 32
TOP_K = 4
D_FF = 1024
SWIGLU_LIMIT = 7.0
SWIGLU_ALPHA = 1.702
N_MOD = 6

LANES = 128
SUBLANES = 8
VMEM_LIMIT = 56 * 1024 * 1024

ROW_TILE = 1024
ROUTE_TILE = 512
SEQ_TILE = 1024
MOE_ROWS = 256
GATHER_TILE = 128
INVERT_TILE = 1024
GATHER_DEPTH = 3
NEG_BIG = -1e30


def _silu(v):
    return v * jax.nn.sigmoid(v)


def _softplus(v):
    return jnp.maximum(v, 0.0) + jnp.log1p(jnp.exp(-jnp.abs(v)))


def _split3(v):
    hi = v.astype(BF16)
    r1 = v - hi.astype(F32)
    mid = r1.astype(BF16)
    lo = (r1 - mid.astype(F32)).astype(BF16)
    return jnp.concatenate([hi, mid, lo], axis=-1)


def _params(sem):
    return pltpu.CompilerParams(dimension_semantics=sem, vmem_limit_bytes=VMEM_LIMIT)


def _ada_kernel(c_ref, w_ref, b_ref, o_ref):
    o_ref[...] = jnp.dot(_silu(c_ref[...]), w_ref[...], precision=HIGHEST,
                         preferred_element_type=F32) + b_ref[...]


def _ada_mod(c_pad, ada_w, ada_b):
    n = ada_w.shape[1]
    tn = D_MODEL
    return pl.pallas_call(
        _ada_kernel,
        grid=(n // tn,),
        in_specs=[pl.BlockSpec((SUBLANES, D_MODEL), lambda j: (0, 0)),
                  pl.BlockSpec((D_MODEL, tn), lambda j: (0, j)),
                  pl.BlockSpec((1, tn), lambda j: (0, j))],
        out_specs=pl.BlockSpec((SUBLANES, tn), lambda j: (0, j)),
        out_shape=jax.ShapeDtypeStruct((SUBLANES, n), F32),
        compiler_params=_params(("arbitrary",)),
        name="ada_mod",
    )(c_pad, ada_w, ada_b.reshape(1, n))


_OFF = [0]
for _w in IN_SPLITS:
    _OFF.append(_OFF[-1] + _w)
IN_COLS_PAD = _OFF[6] + LANES


def _inproj_kernel(x_ref, nw_ref, sh_ref, sc_ref, w_ref, lbl_ref, dtb_ref,
                   qs_ref, kk_ref, lf_ref, vi_ref, gg_ref, zz_ref, xbc_ref, dt_ref):
    x = x_ref[0]
    ms = jnp.mean(x * x, axis=-1, keepdims=True)
    h = x * lax.rsqrt(ms + EPS) * nw_ref[...]
    h = h * (1.0 + sc_ref[0]) + sh_ref[0]
    hb = h.astype(BF16)

    def seg(k):
        return jnp.dot(hb, w_ref[:, _OFF[k]:_OFF[k] + (IN_SPLITS[k] if k < 6 else LANES)],
                       preferred_element_type=F32)

    lbl = lbl_ref[...]
    le = jnp.exp(lbl - jnp.max(lbl, axis=0, keepdims=True))
    lb = le[0:1, :] / jnp.sum(le, axis=0, keepdims=True)

    qs_ref[0] = _silu(seg(0)).astype(BF16)
    fg = lb + (1.0 - lb) * jax.nn.sigmoid(seg(1))
    kk_ref[0] = (1.0 - fg).astype(BF16)
    lf_ref[0] = jnp.log(fg)
    vi_ref[0] = seg(2).astype(BF16)
    gg_ref[0] = _silu(seg(3)).astype(BF16)
    zz_ref[0] = _silu(seg(4)).astype(BF16)
    xbc_ref[0] = seg(5).astype(BF16)
    dt_ref[0] = _softplus(seg(6) + dtb_ref[...])


def _inproj(x, nw, sh, sc, w_in_pad, lb_logits, dtb_pad):
    bsz, seqlen, d = x.shape
    tm = min(ROW_TILE, seqlen)
    nt = seqlen // tm
    tok = lambda w: pl.BlockSpec((1, tm, w), lambda b, i: (b, i, 0))
    full = lambda a: pl.BlockSpec(a.shape, lambda b, i: (0,) * a.ndim)
    mod = pl.BlockSpec((1, 1, d), lambda b, i: (b, 0, 0))
    shp = lambda w, dt: jax.ShapeDtypeStruct((bsz, seqlen, w), dt)
    return pl.pallas_call(
        _inproj_kernel,
        grid=(bsz, nt),
        in_specs=[tok(d), full(nw), mod, mod, full(w_in_pad), full(lb_logits), full(dtb_pad)],
        out_specs=[tok(HG_QF), tok(HG_QF), tok(HG_QF), tok(HG_WIDTH), tok(HG_WIDTH),
                   tok(SSM_WIDTH), tok(SSM_CONV_CH), tok(LANES)],
        out_shape=[shp(HG_QF, BF16), shp(HG_QF, BF16), shp(HG_QF, F32), shp(HG_WIDTH, BF16),
                   shp(HG_WIDTH, BF16), shp(SSM_WIDTH, BF16), shp(SSM_CONV_CH, BF16),
                   shp(LANES, F32)],
        compiler_params=_params(("arbitrary", "arbitrary")),
        name="inproj",
    )(x, nw, sh, sc, w_in_pad, lb_logits, dtb_pad)


def _hgrn_kernel(qs_ref, kk_ref, lf_ref, vi_ref, gg_ref, nw_ref, o_ref, st_ref):
    @pl.when(pl.program_id(2) == 0)
    def _():
        st_ref[...] = jnp.zeros_like(st_ref)

    c = HG_CHUNK
    tq = qs_ref.shape[1]
    nc = tq // c
    rows = lax.broadcasted_iota(jnp.int32, (nc, c, c), 1)
    cols = lax.broadcasted_iota(jnp.int32, (nc, c, c), 2)
    causal = rows >= cols
    chunked = lambda ref: ref[0].reshape(nc, c, ref.shape[2])
    q = chunked(qs_ref).astype(F32)
    k = chunked(kk_ref).astype(F32)
    v = chunked(vi_ref)
    bmm = lambda a, bb, dims, **kw: lax.dot_general(a, bb, (dims, ((0,), (0,))),
                                                    preferred_element_type=F32, **kw)
    b3 = bmm(causal.astype(BF16), _split3(chunked(lf_ref)), ((2,), (1,)))
    b = b3[:, :, :HG_DK] + b3[:, :, HG_DK:2 * HG_DK] + b3[:, :, 2 * HG_DK:]
    b_mid = b[:, c // 2 - 1:c // 2, :]
    b_last = b[:, c - 1:c, :]
    qa = (q * jnp.exp(b - b_mid)).astype(BF16)
    ka = (k * jnp.exp(b_mid - b)).astype(BF16)
    att = bmm(qa, ka, ((2,), (2,)))
    att = jnp.where(causal, att, 0.0).astype(BF16)
    o = bmm(att, v, ((2,), (1,)))
    kd = (k * jnp.exp(b_last - b)).astype(BF16)
    d_st = bmm(v, kd, ((1,), (1,)))
    decay = jnp.exp(b_last)
    st = st_ref[...]
    entering = []
    for ci in range(nc):
        entering.append(st.astype(BF16))
        st = st * decay[ci] + d_st[ci]
    st_ref[...] = st
    qb = (q * jnp.exp(b)).astype(BF16)
    o = o + bmm(qb, jnp.stack(entering), ((2,), (2,)))
    ms = jnp.mean(o * o, axis=-1, keepdims=True)
    y = o * lax.rsqrt(ms + EPS) * nw_ref[...] * chunked(gg_ref).astype(F32)
    o_ref[0] = y.reshape(tq, o_ref.shape[2]).astype(BF16)


def _hgrn(qs, kk, lf, vi, gg, nw):
    bsz, seqlen, _ = qs.shape
    tq = min(SEQ_TILE, seqlen)
    blk = pl.BlockSpec((1, tq, HG_DK), lambda b, h, i: (b, i, h))
    return pl.pallas_call(
        _hgrn_kernel,
        grid=(bsz, HG_HEADS, seqlen // tq),
        in_specs=[blk, blk, blk, blk, blk, pl.BlockSpec((1, HG_DV), lambda b, h, i: (0, 0))],
        out_specs=blk,
        out_shape=jax.ShapeDtypeStruct((bsz, seqlen, HG_WIDTH), BF16),
        scratch_shapes=[pltpu.VMEM((HG_DV, HG_DK), F32)],
        compiler_params=_params(("arbitrary", "arbitrary", "arbitrary")),
        name="hgrn2",
    )(qs, kk, lf, vi, gg, nw)


def _ssd_kernel(xbc_ref, zz_ref, dt_ref, cw_ref, cb_ref, alog_ref, dsk_ref, nw_ref, ex_ref,
                y_ref, buf_ref, xc_ref, yc_ref, st_ref):
    tq = xbc_ref.shape[1]
    halo = SUBLANES

    @pl.when(pl.program_id(1) == 0)
    def _():
        buf_ref[0:halo, :] = jnp.zeros((halo, SSM_CONV_CH), F32)
        st_ref[...] = jnp.zeros_like(st_ref)

    buf_ref[halo:halo + tq, :] = xbc_ref[0].astype(F32)
    conv = cb_ref[...] + buf_ref[halo - 3:halo - 3 + tq, :] * cw_ref[0:1, :]
    for w in range(1, SSM_CONV):
        conv = conv + buf_ref[halo - 3 + w:halo - 3 + w + tq, :] * cw_ref[w:w + 1, :]
    buf_ref[0:halo, :] = buf_ref[tq:tq + halo, :]
    xc_ref[...] = _silu(conv)

    c = SSM_CHUNK
    p = SSM_HEAD_DIM
    nc = tq // c
    rows = lax.broadcasted_iota(jnp.int32, (nc, c, c), 1)
    cols = lax.broadcasted_iota(jnp.int32, (nc, c, c), 2)
    causal = rows >= cols
    tril = causal.astype(BF16)
    bmm = lambda a, bb, dims, **kw: lax.dot_general(a, bb, (dims, ((0,), (0,))),
                                                    preferred_element_type=F32, **kw)
    chunked = lambda val: val.reshape(nc, c, val.shape[-1])
    dt = chunked(dt_ref[0])
    da = dt * (-jnp.exp(alog_ref[...]))
    da3 = _split3(da)
    a3 = bmm(tril, da3, ((2,), (1,)))
    acum = a3[:, :, :LANES] + a3[:, :, LANES:2 * LANES] + a3[:, :, 2 * LANES:]
    at3 = bmm(da3, tril, ((1,), (2,)))
    acum_t = at3[:, :LANES, :] + at3[:, LANES:2 * LANES, :] + at3[:, 2 * LANES:, :]
    widen = lambda v: jnp.dot(_split3(v.reshape(tq, LANES)), ex_ref[...], preferred_element_type=F32)
    dt_w = widen(dt)
    decay_out_w = widen(jnp.exp(acum))
    decay_in_w = widen(jnp.exp(acum[:, c - 1:c, :] - acum))
    xdt_w = xc_ref[:, 0:SSM_WIDTH] * dt_w
    xdd_w = xdt_w * decay_in_w
    y_inter_groups = []
    for g in range(SSM_GROUPS):
        gs = slice(g * SSM_GROUP_WIDTH, (g + 1) * SSM_GROUP_WIDTH)
        bm = chunked(xc_ref[:, SSM_WIDTH + g * SSM_STATE:SSM_WIDTH + (g + 1) * SSM_STATE]).astype(BF16)
        cm = chunked(xc_ref[:, SSM_WIDTH + (SSM_GROUPS + g) * SSM_STATE:
                            SSM_WIDTH + (SSM_GROUPS + g + 1) * SSM_STATE]).astype(BF16)
        cb = bmm(cm, bm, ((2,), (2,)))
        d_st = bmm(chunked(xdd_w[:, gs]).astype(BF16), bm, ((1,), (1,)))
        a_last = []
        for hh in range(SSM_GROUP_HEADS):
            h = g * SSM_GROUP_HEADS + hh
            hs = slice(h * p, (h + 1) * p)
            seg = jnp.exp(jnp.where(causal, acum[:, :, h:h + 1] - acum_t[:, h:h + 1, :], -jnp.inf))
            y = bmm((cb * seg).astype(BF16), chunked(xdt_w[:, hs]).astype(BF16), ((2,), (1,)))
            yc_ref[:, hs] = y.reshape(tq, p)
            a_last.append(jnp.exp(acum_t[:, h:h + 1, c - 1:c]))
        st = [st_ref[g, hh * p:(hh + 1) * p, :] for hh in range(SSM_GROUP_HEADS)]
        entering = []
        for ci in range(nc):
            entering.append(jnp.concatenate(st, axis=0).astype(BF16))
            st = [st[hh] * a_last[hh][ci] + d_st[ci, hh * p:(hh + 1) * p, :] for hh in range(SSM_GROUP_HEADS)]
        for hh in range(SSM_GROUP_HEADS):
            st_ref[g, hh * p:(hh + 1) * p, :] = st[hh]
        y_inter_groups.append(bmm(cm, jnp.stack(entering), ((2,), (2,))).reshape(tq, SSM_GROUP_WIDTH))
    y_inter = jnp.concatenate(y_inter_groups, axis=-1)
    y = yc_ref[...] + decay_out_w * y_inter
    y = (y + dsk_ref[...] * xc_ref[:, 0:SSM_WIDTH]) * zz_ref[0].astype(F32)
    for g in range(SSM_GROUPS):
        gs = slice(g * SSM_GROUP_WIDTH, (g + 1) * SSM_GROUP_WIDTH)
        yg = y[:, gs]
        ms = jnp.mean(yg * yg, axis=-1, keepdims=True)
        y_ref[0, :, gs] = (yg * lax.rsqrt(ms + EPS) * nw_ref[:, gs]).astype(BF16)


def _ssd(xbc, zz, dt, conv_w, conv_b, alog_pad, dsk_wide, nw, expand3):
    bsz, seqlen, _ = xbc.shape
    tq = min(SEQ_TILE, seqlen)
    tok = lambda w: pl.BlockSpec((1, tq, w), lambda b, i: (b, i, 0))
    full = lambda a: pl.BlockSpec(a.shape, lambda b, i: (0,) * a.ndim)
    return pl.pallas_call(
        _ssd_kernel,
        grid=(bsz, seqlen // tq),
        in_specs=[tok(SSM_CONV_CH), tok(SSM_WIDTH), tok(LANES), full(conv_w), full(conv_b),
                  full(alog_pad), full(dsk_wide), full(nw), full(expand3)],
        out_specs=tok(SSM_WIDTH),
        out_shape=jax.ShapeDtypeStruct((bsz, seqlen, SSM_WIDTH), BF16),
        scratch_shapes=[pltpu.VMEM((tq + 2 * SUBLANES, SSM_CONV_CH), F32),
                        pltpu.VMEM((tq, SSM_CONV_CH), F32),
                        pltpu.VMEM((tq, SSM_WIDTH), F32),
                        pltpu.VMEM((SSM_GROUPS, SSM_GROUP_WIDTH, SSM_STATE), F32)],
        compiler_params=_params(("arbitrary", "arbitrary")),
        name="ssd",
    )(xbc, zz, dt, conv_w, conv_b, alog_pad, dsk_wide, nw, expand3)


def _outproj_kernel(x_ref, oh_ref, ys_ref, wo_ref, g1_ref, nw_ref, sh_ref, sc_ref, rw_ref, rb_ref,
                    x1_ref, h2_ref, idx_ref, gate_ref, cnt_ref):
    first = jnp.logical_and(pl.program_id(0) == 0, pl.program_id(1) == 0)

    @pl.when(first)
    def _():
        cnt_ref[...] = jnp.zeros_like(cnt_ref)

    mix = jnp.dot(oh_ref[0], wo_ref[0:HG_WIDTH, :], preferred_element_type=F32)
    mix = mix + jnp.dot(ys_ref[0], wo_ref[HG_WIDTH:, :], preferred_element_type=F32)
    x1 = x_ref[0] + g1_ref[0] * mix
    x1_ref[0] = x1
    ms = jnp.mean(x1 * x1, axis=-1, keepdims=True)
    h2 = x1 * lax.rsqrt(ms + EPS) * nw_ref[...]
    h2 = h2 * (1.0 + sc_ref[0]) + sh_ref[0]
    h2_ref[0] = h2
    h_hi = h2.astype(BF16)
    h_lo = (h2 - h_hi.astype(F32)).astype(BF16)
    part = jnp.dot(h_hi, rw_ref[...], preferred_element_type=F32)
    logits = (part[:, :LANES] + part[:, LANES:]
              + jnp.dot(h_lo, rw_ref[:, :LANES], preferred_element_type=F32)) + rb_ref[...]
    lane = lax.broadcasted_iota(jnp.int32, logits.shape, 1).astype(F32)
    idx_out = jnp.zeros(logits.shape, F32)
    val_out = jnp.zeros(logits.shape, F32)
    sel = jnp.zeros(logits.shape, F32)
    work = logits
    top = None
    denom = None
    for k in range(TOP_K):
        m = jnp.max(work, axis=-1, keepdims=True)
        am = jnp.min(jnp.where(work == m, lane, float(LANES)), axis=-1, keepdims=True)
        hit = lane == am
        work = jnp.where(hit, -jnp.inf, work)
        sel = jnp.where(hit, 1.0, sel)
        if k == 0:
            top = m
        e = jnp.exp(m - top)
        denom = e if k == 0 else denom + e
        idx_out = jnp.where(lane == float(k), am, idx_out)
        val_out = jnp.where(lane == float(k), e, val_out)
    idx_ref[0] = idx_out.astype(jnp.int32)
    gate_ref[0] = val_out / denom
    cnt_ref[0:1, :] += jnp.sum(sel, axis=0, keepdims=True)


def _outproj(x, o_hg, y_ssd, w_out_b, g1, nw, sh, sc, rw_pad, rb_pad):
    bsz, seqlen, d = x.shape
    tm = min(ROW_TILE, seqlen)
    tok = lambda w: pl.BlockSpec((1, tm, w), lambda b, i: (b, i, 0))
    full = lambda a: pl.BlockSpec(a.shape, lambda b, i: (0,) * a.ndim)
    mod = pl.BlockSpec((1, 1, d), lambda b, i: (b, 0, 0))
    shp = lambda w, dt: jax.ShapeDtypeStruct((bsz, seqlen, w), dt)
    return pl.pallas_call(
        _outproj_kernel,
        grid=(bsz, seqlen // tm),
        in_specs=[tok(d), tok(HG_WIDTH), tok(SSM_WIDTH), full(w_out_b), mod, full(nw), mod, mod,
                  full(rw_pad), full(rb_pad)],
        out_specs=[tok(d), tok(d), tok(LANES), tok(LANES),
                   pl.BlockSpec((SUBLANES, LANES), lambda b, i: (0, 0))],
        out_shape=[shp(d, F32), shp(d, F32), shp(LANES, jnp.int32), shp(LANES, F32),
                   jax.ShapeDtypeStruct((SUBLANES, LANES), F32)],
        compiler_params=_params(("arbitrary", "arbitrary")),
        name="outproj_router",
    )(x, o_hg, y_ssd, w_out_b, g1, nw, sh, sc, rw_pad, rb_pad)


def _route_kernel(idx_ref, pstart_ref, dest_ref, carry_ref):
    @pl.when(pl.program_id(0) == 0)
    def _():
        carry_ref[...] = jnp.zeros_like(carry_ref)

    idx = idx_ref[...]
    tt = idx.shape[0]
    lane = lax.broadcasted_iota(jnp.int32, idx.shape, 1)
    hits = [lane == idx[:, k:k + 1] for k in range(TOP_K)]
    sel = jnp.zeros(idx.shape, F32)
    for hit in hits:
        sel = jnp.where(hit, 1.0, sel)
    rows = lax.broadcasted_iota(jnp.int32, (tt, tt), 0)
    cols = lax.broadcasted_iota(jnp.int32, (tt, tt), 1)
    before = (rows > cols).astype(BF16)
    rank = jnp.dot(before, sel.astype(BF16), preferred_element_type=F32) + carry_ref[0:1, :]
    carry_ref[0:1, :] += jnp.sum(sel, axis=0, keepdims=True)
    dense = pstart_ref[...] + rank
    out = jnp.zeros(idx.shape, F32)
    for k, hit in enumerate(hits):
        dk = jnp.sum(jnp.where(hit, dense, 0.0), axis=-1, keepdims=True)
        out = jnp.where(lane == k, dk, out)
    dest_ref[...] = out.astype(jnp.int32)


def _route(idx_pad, pstart_row):
    t = idx_pad.shape[0]
    tt = min(ROUTE_TILE, t)
    return pl.pallas_call(
        _route_kernel,
        grid=(t // tt,),
        in_specs=[pl.BlockSpec((tt, LANES), lambda i: (i, 0)),
                  pl.BlockSpec((1, LANES), lambda i: (0, 0))],
        out_specs=pl.BlockSpec((tt, LANES), lambda i: (i, 0)),
        out_shape=jax.ShapeDtypeStruct((t, LANES), jnp.int32),
        scratch_shapes=[pltpu.VMEM((SUBLANES, LANES), F32)],
        compiler_params=_params(("arbitrary",)),
        name="route_rank",
    )(idx_pad, pstart_row)


def _invert_kernel(dest_ref, zeros_ref, rowtok_ref, sem):
    i = pl.program_id(0)
    per_step = dest_ref.shape[0] // TOP_K // pl.num_programs(0)

    @pl.when(i == 0)
    def _():
        fill = pltpu.make_async_copy(zeros_ref, rowtok_ref, sem)
        fill.start()
        fill.wait()

    def tok_body(n, carry):
        tok = i * per_step + n
        for k in range(TOP_K):
            rowtok_ref[dest_ref[tok * TOP_K + k]] = tok
        return carry
    lax.fori_loop(0, per_step, tok_body, 0, unroll=8)


def _invert(dest_flat, n_rows):
    t = dest_flat.shape[0] // TOP_K
    return pl.pallas_call(
        _invert_kernel,
        grid=(max(t // INVERT_TILE, 1),),
        in_specs=[pl.BlockSpec(memory_space=pltpu.SMEM), pl.BlockSpec(memory_space=pl.ANY)],
        out_specs=pl.BlockSpec(memory_space=pltpu.SMEM),
        out_shape=jax.ShapeDtypeStruct((n_rows,), jnp.int32),
        scratch_shapes=[pltpu.SemaphoreType.DMA(())],
        compiler_params=_params(("arbitrary",)),
        name="invert_route",
    )(dest_flat, jnp.zeros((n_rows,), jnp.int32))


def _expert_kernel(be_ref, nact_ref, nxt_ref, rowtok_ref, h_ref, w1_ref, b1_ref, w2_ref, b2_ref, y_ref,
                   xbuf0_ref, xbuf1_ref, xbuf2_ref, w1s_ref, w2s_ref, w1b_ref, w2b_ref, sem, wsem):
    j = pl.program_id(0)
    n_act = nact_ref[0]
    active = j < n_act
    expert = be_ref[j]
    fresh = jnp.logical_or(j == 0, expert != be_ref[jnp.maximum(j - 1, 0)])
    phase = j % GATHER_DEPTH
    bufs = (xbuf0_ref, xbuf1_ref, xbuf2_ref)

    def issue(block, s):
        base = jnp.minimum(block, n_act - 1) * MOE_ROWS
        for r in range(MOE_ROWS):
            pltpu.make_async_copy(h_ref.at[pl.ds(rowtok_ref[base + r], 1), :],
                                  bufs[s].at[pl.ds(r, 1), :], sem.at[s]).start(priority=1)

    def wait(s):
        pltpu.make_async_copy(h_ref.at[pl.ds(0, MOE_ROWS), :], bufs[s], sem.at[s]).wait()

    def weight_copies(e):
        return (pltpu.make_async_copy(w1_ref.at[e], w1s_ref, wsem.at[0]),
                pltpu.make_async_copy(w2_ref.at[e], w2s_ref, wsem.at[1]))

    @pl.when(j == 0)
    def _():
        for cp in weight_copies(expert):
            cp.start()
        for s in range(GATHER_DEPTH - 1):
            issue(s, s)

    @pl.when(jnp.logical_and(active, fresh))
    def _():
        for cp in weight_copies(expert):
            cp.wait()
        w1b_ref[...] = w1s_ref[...].astype(BF16)
        w2b_ref[...] = w2s_ref[...].astype(BF16)

        @pl.when(nxt_ref[expert] < N_EXPERTS)
        def _():
            for cp in weight_copies(nxt_ref[expert]):
                cp.start()

    for s in range(GATHER_DEPTH):
        @pl.when(jnp.logical_and(j < n_act + GATHER_DEPTH - 1, phase == s))
        def _(s=s):
            wait(s)

        @pl.when(jnp.logical_and(active, phase == s))
        def _(s=s):
            issue(j + GATHER_DEPTH - 1, (s + GATHER_DEPTH - 1) % GATHER_DEPTH)
            hb = jnp.dot(bufs[s][...].astype(BF16), w1b_ref[...], preferred_element_type=F32) + b1_ref[0]
            glu = jnp.minimum(hb[:, :D_FF], SWIGLU_LIMIT)
            lin = jnp.clip(hb[:, D_FF:], -SWIGLU_LIMIT, SWIGLU_LIMIT)
            act = glu * jax.nn.sigmoid(SWIGLU_ALPHA * glu) * (lin + 1.0)
            y_ref[...] = jnp.dot(act.astype(BF16), w2b_ref[...], preferred_element_type=F32) + b2_ref[0]


def _experts(block_e, n_act, next_e, row_tok, h2, w1, b1, w2, b2):
    n_rows = row_tok.shape[0]
    d = h2.shape[1]
    nb = block_e.shape[0]
    row_map = lambda j, be, na, nx, rt: (jnp.maximum(jnp.minimum(j, na[0] - 1), 0), 0)
    exp_map = lambda j, be, na, nx, rt: (be[j], 0, 0)
    return pl.pallas_call(
        _expert_kernel,
        grid_spec=pltpu.PrefetchScalarGridSpec(
            num_scalar_prefetch=4,
            grid=(nb,),
            in_specs=[pl.BlockSpec(memory_space=pl.ANY),
                      pl.BlockSpec(memory_space=pl.ANY),
                      pl.BlockSpec((1, 1, 2 * D_FF), exp_map),
                      pl.BlockSpec(memory_space=pl.ANY),
                      pl.BlockSpec((1, 1, d), exp_map)],
            out_specs=pl.BlockSpec((MOE_ROWS, d), row_map),
            scratch_shapes=[pltpu.VMEM((MOE_ROWS, d), F32)] * GATHER_DEPTH + [
                pltpu.VMEM((d, 2 * D_FF), F32), pltpu.VMEM((D_FF, d), F32),
                pltpu.VMEM((d, 2 * D_FF), BF16), pltpu.VMEM((D_FF, d), BF16),
                pltpu.SemaphoreType.DMA((GATHER_DEPTH,)), pltpu.SemaphoreType.DMA((2,))],
        ),
        out_shape=jax.ShapeDtypeStruct((n_rows, d), F32),
        compiler_params=_params(("arbitrary",)),
        name="expert_ffn",
    )(block_e, n_act, next_e, row_tok, h2, w1, b1.reshape(N_EXPERTS, 1, 2 * D_FF), w2,
      b2.reshape(N_EXPERTS, 1, d))


def _combine_kernel(dest_ref, x1_ref, gate_ref, g2_ref, fw_ref, y_ref, o_ref, buf_ref, sem):
    i = pl.program_id(0)
    n = pl.num_programs(0)
    gt = x1_ref.shape[0]

    def row_copy(slot, k, j, src_row):
        return pltpu.make_async_copy(y_ref.at[pl.ds(src_row, 1), :],
                                     buf_ref.at[slot, k, pl.ds(j, 1), :], sem.at[slot])

    def issue(tile, slot):
        base = tile * (gt * TOP_K)
        for j in range(gt):
            for k in range(TOP_K):
                row_copy(slot, k, j, dest_ref[base + j * TOP_K + k]).start(priority=k % 2)

    @pl.when(i == 0)
    def _():
        issue(0, 0)

    for s in range(2):
        @pl.when(jnp.logical_and(i + 1 < n, (i + 1) % 2 == s))
        def _(s=s):
            issue(i + 1, s)

    slot = i % 2

    for k in range(TOP_K):
        pltpu.make_async_copy(y_ref.at[pl.ds(0, gt), :], buf_ref.at[slot, k], sem.at[slot]).wait()

    gates = gate_ref[...]
    moe = gates[:, 0:1] * buf_ref[slot, 0]
    for k in range(1, TOP_K):
        moe = moe + gates[:, k:k + 1] * buf_ref[slot, k]
    x2 = x1_ref[...] + g2_ref[0] * moe
    ms = jnp.mean(x2 * x2, axis=-1, keepdims=True)
    o_ref[...] = x2 * lax.rsqrt(ms + EPS) * fw_ref[...]


def _combine(dest_flat, x1, gate_pad, g2, fw, yout, seqlen):
    t, d = x1.shape
    gt = min(GATHER_TILE, seqlen)
    per_seq = seqlen // gt
    return pl.pallas_call(
        _combine_kernel,
        grid_spec=pltpu.PrefetchScalarGridSpec(
            num_scalar_prefetch=1,
            grid=(t // gt,),
            in_specs=[pl.BlockSpec((gt, d), lambda i, dst: (i, 0)),
                      pl.BlockSpec((gt, LANES), lambda i, dst: (i, 0)),
                      pl.BlockSpec((1, 1, d), lambda i, dst: (i // per_seq, 0, 0)),
                      pl.BlockSpec((1, d), lambda i, dst: (0, 0)),
                      pl.BlockSpec(memory_space=pl.ANY)],
            out_specs=pl.BlockSpec((gt, d), lambda i, dst: (i, 0)),
            scratch_shapes=[pltpu.VMEM((2, TOP_K, gt, d), F32), pltpu.SemaphoreType.DMA((2,))],
        ),
        out_shape=jax.ShapeDtypeStruct((t, d), F32),
        compiler_params=_params(("arbitrary",)),
        name="combine_norm",
    )(dest_flat, x1, gate_pad, g2, fw, yout)


def _head_expand():
    head_of_lane = jnp.arange(SSM_WIDTH) // SSM_HEAD_DIM
    one = (jnp.arange(LANES)[:, None] == head_of_lane[None, :]).astype(BF16)
    return jnp.tile(one, (3, 1))


def _pad_lanes(v, fill=0.0):
    v = v.reshape(1, -1).astype(F32)
    return jnp.pad(v, ((0, 0), (0, LANES - v.shape[1])), constant_values=fill)


def kernel(x, c, ada_w, ada_b, norm1_w, w_in, hg_lb_logits, hg_norm_w, ssm_a_log, ssm_dt_bias, ssm_d,
           ssm_conv_w, ssm_conv_b, ssm_norm_w, w_out, norm2_w, router_w, router_b, exp_w1, exp_b1,
           exp_w2, exp_b2, final_norm_w):
    bsz, seqlen, d = x.shape
    t = bsz * seqlen
    l = 0

    c_pad = jnp.pad(c, ((0, SUBLANES - bsz), (0, 0)))
    mod = _ada_mod(c_pad, ada_w[l], ada_b[l])[:bsz]
    sh1, sc1, g1, sh2, sc2, g2 = [m.reshape(bsz, 1, d) for m in jnp.split(mod, N_MOD, axis=-1)]

    w_in_pad = jnp.pad(w_in[l], ((0, 0), (0, IN_COLS_PAD - IN_COLS))).astype(BF16)
    qs, kk, lf, vi, gg, zz, xbc, dt = _inproj(
        x, norm1_w[l].reshape(1, d), sh1, sc1, w_in_pad, hg_lb_logits, _pad_lanes(ssm_dt_bias[l]))

    o_hg = _hgrn(qs, kk, lf, vi, gg, hg_norm_w[l].reshape(1, HG_DV))
    y_ssd = _ssd(xbc, zz, dt, ssm_conv_w[l], ssm_conv_b[l].reshape(1, SSM_CONV_CH),
                 _pad_lanes(ssm_a_log[l]), jnp.repeat(ssm_d[l], SSM_HEAD_DIM).reshape(1, SSM_WIDTH),
                 ssm_norm_w[l].reshape(1, SSM_WIDTH), _head_expand())

    rw_pad = jnp.pad(router_w[l], ((0, 0), (0, LANES - N_EXPERTS)))
    rw_hi = rw_pad.astype(BF16)
    rw_pad = jnp.concatenate([rw_hi, (rw_pad - rw_hi.astype(F32)).astype(BF16)], axis=1)
    rb_pad = _pad_lanes(router_b[l], NEG_BIG)
    x1, h2, idx_pad, gate_pad, counts = _outproj(
        x, o_hg, y_ssd, w_out[l].astype(BF16), g1, norm2_w[l].reshape(1, d), sh2, sc2, rw_pad, rb_pad)

    cnt = counts[0, :N_EXPERTS].astype(jnp.int32)
    blocks_e = (cnt + MOE_ROWS - 1) // MOE_ROWS
    blk_end = jnp.cumsum(blocks_e)
    pstart = (blk_end - blocks_e) * MOE_ROWS
    n_blocks = (t * TOP_K) // MOE_ROWS + N_EXPERTS
    n_rows = n_blocks * MOE_ROWS
    n_steps = n_blocks + GATHER_DEPTH - 1
    block_e = jnp.minimum(jnp.sum(blk_end[None, :] <= jnp.arange(n_steps)[:, None], axis=1),
                          N_EXPERTS - 1).astype(jnp.int32)
    n_act = blk_end[-1:].astype(jnp.int32)
    owner = jnp.where(blocks_e > 0, jnp.arange(N_EXPERTS), N_EXPERTS)
    next_e = jnp.concatenate([lax.cummin(owner, reverse=True)[1:],
                              jnp.full((1,), N_EXPERTS)]).astype(jnp.int32)

    dest_pad = _route(idx_pad.reshape(t, LANES), _pad_lanes(pstart))
    dest_flat = dest_pad[:, :TOP_K].reshape(t * TOP_K)

    row_tok = _invert(dest_flat, n_rows)
    yout = _experts(block_e, n_act, next_e, row_tok, h2.reshape(t, d), exp_w1[l], exp_b1[l], exp_w2[l], exp_b2[l])
    out = _combine(dest_flat, x1.reshape(t, d), gate_pad.reshape(t, LANES), g2,
                   final_norm_w.reshape(1, d), yout, seqlen)
    return out.reshape(bsz, seqlen, d)
```

```python
import functools

import jax
import jax.numpy as jnp
from jax import lax
from jax.experimental import pallas as pl
from jax.experimental.pallas import tpu as pltpu

F32 = jnp.float32
BF16 = jnp.bfloat16
HIGHEST = lax.Precision.HIGHEST

EPS = 1e-6
D_MODEL = 1024
HG_HEADS = 4
HG_DK = 128
HG_DV = 128
HG_QF = HG_HEADS * HG_DK
HG_WIDTH = HG_HEADS * HG_DV
HG_CHUNK = 64
SSM_HEADS = 8
SSM_HEAD_DIM = 64
SSM_WIDTH = SSM_HEADS * SSM_HEAD_DIM
SSM_GROUPS = 2
SSM_GROUP_HEADS = SSM_HEADS // SSM_GROUPS
SSM_GROUP_WIDTH = SSM_WIDTH // SSM_GROUPS
SSM_STATE = 128
SSM_CONV = 4
SSM_CONV_CH = SSM_WIDTH + 2 * SSM_GROUPS * SSM_STATE
SSM_CHUNK = 128
IN_SPLITS = (HG_QF, HG_QF, HG_WIDTH, HG_WIDTH, SSM_WIDTH, SSM_CONV_CH, SSM_HEADS)
IN_COLS = sum(IN_SPLITS)
N_EXPERTS = 32
TOP_K = 4
D_FF = 1024
SWIGLU_LIMIT = 7.0
SWIGLU_ALPHA = 1.702
N_MOD = 6

LANES = 128
SUBLANES = 8
VMEM_LIMIT = 56 * 1024 * 1024

ROW_TILE = 1024
ROUTE_TILE = 512
SEQ_TILE = 1024
MOE_ROWS = 256
GATHER_TILE = 256
INVERT_TILE = 1024
GATHER_DEPTH = 3
NEG_BIG = -1e30


def _silu(v):
    return v * jax.nn.sigmoid(v)


def _softplus(v):
    return jnp.maximum(v, 0.0) + jnp.log1p(jnp.exp(-jnp.abs(v)))


def _split3(v):
    hi = v.astype(BF16)
    r1 = v - hi.astype(F32)
    mid = r1.astype(BF16)
    lo = (r1 - mid.astype(F32)).astype(BF16)
    return jnp.concatenate([hi, mid, lo], axis=-1)


def _params(sem):
    return pltpu.CompilerParams(dimension_semantics=sem, vmem_limit_bytes=VMEM_LIMIT)


def _ada_kernel(c_ref, w_ref, b_ref, o_ref):
    o_ref[...] = jnp.dot(_silu(c_ref[...]), w_ref[...], precision=HIGHEST,
                         preferred_element_type=F32) + b_ref[...]


def _ada_mod(c_pad, ada_w, ada_b):
    n = ada_w.shape[1]
    tn = D_MODEL
    return pl.pallas_call(
        _ada_kernel,
        grid=(n // tn,),
        in_specs=[pl.BlockSpec((SUBLANES, D_MODEL), lambda j: (0, 0)),
                  pl.BlockSpec((D_MODEL, tn), lambda j: (0, j)),
                  pl.BlockSpec((1, tn), lambda j: (0, j))],
        out_specs=pl.BlockSpec((SUBLANES, tn), lambda j: (0, j)),
        out_shape=jax.ShapeDtypeStruct((SUBLANES, n), F32),
        compiler_params=_params(("arbitrary",)),
        name="ada_mod",
    )(c_pad, ada_w, ada_b.reshape(1, n))


_OFF = [0]
for _w in IN_SPLITS:
    _OFF.append(_OFF[-1] + _w)
IN_COLS_PAD = _OFF[6] + LANES


def _inproj_kernel(x_ref, nw_ref, sh_ref, sc_ref, w_ref, lbl_ref, dtb_ref,
                   qs_ref, kk_ref, lf_ref, vi_ref, gg_ref, zz_ref, xbc_ref, dt_ref):
    x = x_ref[0]
    ms = jnp.mean(x * x, axis=-1, keepdims=True)
    h = x * lax.rsqrt(ms + EPS) * nw_ref[...]
    h = h * (1.0 + sc_ref[0]) + sh_ref[0]
    hb = h.astype(BF16)

    def seg(k):
        cols = w_ref[_OFF[k]:_OFF[k] + (IN_SPLITS[k] if k < 6 else LANES), :]
        return lax.dot_general(hb, cols, (((1,), (1,)), ((), ())), preferred_element_type=F32)

    lbl = lbl_ref[...]
    le = jnp.exp(lbl - jnp.max(lbl, axis=0, keepdims=True))
    lb = le[0:1, :] / jnp.sum(le, axis=0, keepdims=True)

    qs_ref[0] = _silu(seg(0)).astype(BF16)
    fg = lb + (1.0 - lb) * jax.nn.sigmoid(seg(1))
    kk_ref[0] = (1.0 - fg).astype(BF16)
    lf_ref[0] = jnp.log(fg)
    vi_ref[0] = seg(2).astype(BF16)
    gg_ref[0] = _silu(seg(3)).astype(BF16)
    zz_ref[0] = _silu(seg(4)).astype(BF16)
    xbc_ref[0] = seg(5).astype(BF16)
    dt_ref[0] = _softplus(seg(6) + dtb_ref[...])


def _inproj(x, nw, sh, sc, w_in_pad, lb_logits, dtb_pad):
    bsz, seqlen, d = x.shape
    tm = min(ROW_TILE, seqlen)
    nt = seqlen // tm
    tok = lambda w: pl.BlockSpec((1, tm, w), lambda b, i: (b, i, 0))
    full = lambda a: pl.BlockSpec(a.shape, lambda b, i: (0,) * a.ndim)
    mod = pl.BlockSpec((1, 1, d), lambda b, i: (b, 0, 0))
    shp = lambda w, dt: jax.ShapeDtypeStruct((bsz, seqlen, w), dt)
    return pl.pallas_call(
        _inproj_kernel,
        grid=(bsz, nt),
        in_specs=[tok(d), full(nw), mod, mod, full(w_in_pad), full(lb_logits), full(dtb_pad)],
        out_specs=[tok(HG_QF), tok(HG_QF), tok(HG_QF), tok(HG_WIDTH), tok(HG_WIDTH),
                   tok(SSM_WIDTH), tok(SSM_CONV_CH), tok(LANES)],
        out_shape=[shp(HG_QF, BF16), shp(HG_QF, BF16), shp(HG_QF, F32), shp(HG_WIDTH, BF16),
                   shp(HG_WIDTH, BF16), shp(SSM_WIDTH, BF16), shp(SSM_CONV_CH, BF16),
                   shp(LANES, F32)],
        compiler_params=_params(("arbitrary", "arbitrary")),
        name="inproj",
    )(x, nw, sh, sc, w_in_pad, lb_logits, dtb_pad)


def _hgrn_kernel(qs_ref, kk_ref, lf_ref, vi_ref, gg_ref, nw_ref, o_ref, st_ref):
    @pl.when(pl.program_id(2) == 0)
    def _():
        st_ref[...] = jnp.zeros_like(st_ref)

    c = HG_CHUNK
    tq = qs_ref.shape[1]
    nc = tq // c
    rows = lax.broadcasted_iota(jnp.int32, (nc, c, c), 1)
    cols = lax.broadcasted_iota(jnp.int32, (nc, c, c), 2)
    causal = rows >= cols
    chunked = lambda ref: ref[0].reshape(nc, c, ref.shape[2])
    q = chunked(qs_ref).astype(F32)
    k = chunked(kk_ref).astype(F32)
    v = chunked(vi_ref)
    bmm = lambda a, bb, dims, **kw: lax.dot_general(a, bb, (dims, ((0,), (0,))),
                                                    preferred_element_type=F32, **kw)
    b3 = bmm(causal.astype(BF16), _split3(chunked(lf_ref)), ((2,), (1,)))
    b = b3[:, :, :HG_DK] + b3[:, :, HG_DK:2 * HG_DK] + b3[:, :, 2 * HG_DK:]
    b_mid = b[:, c // 2 - 1:c // 2, :]
    b_last = b[:, c - 1:c, :]
    qa = (q * jnp.exp(b - b_mid)).astype(BF16)
    ka = (k * jnp.exp(b_mid - b)).astype(BF16)
    att = bmm(qa, ka, ((2,), (2,)))
    att = jnp.where(causal, att, 0.0).astype(BF16)
    o = bmm(att, v, ((2,), (1,)))
    kd = (k * jnp.exp(b_last - b)).astype(BF16)
    d_st = bmm(v, kd, ((1,), (1,)))
    decay = jnp.exp(b_last)
    st = st_ref[...]
    entering = []
    for ci in range(nc):
        entering.append(st.astype(BF16))
        st = st * decay[ci] + d_st[ci]
    st_ref[...] = st
    qb = (q * jnp.exp(b)).astype(BF16)
    o = o + bmm(qb, jnp.stack(entering), ((2,), (2,)))
    ms = jnp.mean(o * o, axis=-1, keepdims=True)
    y = o * lax.rsqrt(ms + EPS) * nw_ref[...] * chunked(gg_ref).astype(F32)
    o_ref[0] = y.reshape(tq, o_ref.shape[2]).astype(BF16)


def _hgrn(qs, kk, lf, vi, gg, nw):
    bsz, seqlen, _ = qs.shape
    tq = min(SEQ_TILE, seqlen)
    blk = pl.BlockSpec((1, tq, HG_DK), lambda b, h, i: (b, i, h))
    return pl.pallas_call(
        _hgrn_kernel,
        grid=(bsz, HG_HEADS, seqlen // tq),
        in_specs=[blk, blk, blk, blk, blk, pl.BlockSpec((1, HG_DV), lambda b, h, i: (0, 0))],
        out_specs=blk,
        out_shape=jax.ShapeDtypeStruct((bsz, seqlen, HG_WIDTH), BF16),
        scratch_shapes=[pltpu.VMEM((HG_DV, HG_DK), F32)],
        compiler_params=_params(("arbitrary", "arbitrary", "arbitrary")),
        name="hgrn2",
    )(qs, kk, lf, vi, gg, nw)


def _ssd_kernel(xbc_ref, zz_ref, dt_ref, cw_ref, cb_ref, alog_ref, dsk_ref, nw_ref, ex_ref,
                y_ref, buf_ref, xc_ref, yc_ref, st_ref):
    tq = xbc_ref.shape[1]
    halo = SUBLANES

    @pl.when(pl.program_id(1) == 0)
    def _():
        buf_ref[0:halo, :] = jnp.zeros((halo, SSM_CONV_CH), F32)
        st_ref[...] = jnp.zeros_like(st_ref)

    buf_ref[halo:halo + tq, :] = xbc_ref[0].astype(F32)
    conv = cb_ref[...] + buf_ref[halo - 3:halo - 3 + tq, :] * cw_ref[0:1, :]
    for w in range(1, SSM_CONV):
        conv = conv + buf_ref[halo - 3 + w:halo - 3 + w + tq, :] * cw_ref[w:w + 1, :]
    buf_ref[0:halo, :] = buf_ref[tq:tq + halo, :]
    xc_ref[...] = _silu(conv)

    c = SSM_CHUNK
    p = SSM_HEAD_DIM
    nc = tq // c
    rows = lax.broadcasted_iota(jnp.int32, (nc, c, c), 1)
    cols = lax.broadcasted_iota(jnp.int32, (nc, c, c), 2)
    causal = rows >= cols
    tril = causal.astype(BF16)
    bmm = lambda a, bb, dims, **kw: lax.dot_general(a, bb, (dims, ((0,), (0,))),
                                                    preferred_element_type=F32, **kw)
    chunked = lambda val: val.reshape(nc, c, val.shape[-1])
    dt = chunked(dt_ref[0])
    da = dt * (-jnp.exp(alog_ref[...]))
    da3 = _split3(da)
    a3 = bmm(tril, da3, ((2,), (1,)))
    acum = a3[:, :, :LANES] + a3[:, :, LANES:2 * LANES] + a3[:, :, 2 * LANES:]
    at3 = bmm(da3, tril, ((1,), (2,)))
    acum_t = at3[:, :LANES, :] + at3[:, LANES:2 * LANES, :] + at3[:, 2 * LANES:, :]
    widen = lambda v: jnp.dot(_split3(v.reshape(tq, LANES)), ex_ref[...], preferred_element_type=F32)
    dt_w = widen(dt)
    decay_out_w = widen(jnp.exp(acum))
    decay_in_w = widen(jnp.exp(acum[:, c - 1:c, :] - acum))
    xdt_w = xc_ref[:, 0:SSM_WIDTH] * dt_w
    xdd_w = xdt_w * decay_in_w
    y_inter_groups = []
    for g in range(SSM_GROUPS):
        gs = slice(g * SSM_GROUP_WIDTH, (g + 1) * SSM_GROUP_WIDTH)
        bm = chunked(xc_ref[:, SSM_WIDTH + g * SSM_STATE:SSM_WIDTH + (g + 1) * SSM_STATE]).astype(BF16)
        cm = chunked(xc_ref[:, SSM_WIDTH + (SSM_GROUPS + g) * SSM_STATE:
                            SSM_WIDTH + (SSM_GROUPS + g + 1) * SSM_STATE]).astype(BF16)
        cb = bmm(cm, bm, ((2,), (2,)))
        d_st = bmm(chunked(xdd_w[:, gs]).astype(BF16), bm, ((1,), (1,)))
        a_last = []
        for hh in range(SSM_GROUP_HEADS):
            h = g * SSM_GROUP_HEADS + hh
            hs = slice(h * p, (h + 1) * p)
            seg = jnp.exp(jnp.where(causal, acum[:, :, h:h + 1] - acum_t[:, h:h + 1, :], -jnp.inf))
            y = bmm((cb * seg).astype(BF16), chunked(xdt_w[:, hs]).astype(BF16), ((2,), (1,)))
            yc_ref[:, hs] = y.reshape(tq, p)
            a_last.append(jnp.exp(acum_t[:, h:h + 1, c - 1:c]))
        st = [st_ref[g, hh * p:(hh + 1) * p, :] for hh in range(SSM_GROUP_HEADS)]
        entering = []
        for ci in range(nc):
            entering.append(jnp.concatenate(st, axis=0).astype(BF16))
            st = [st[hh] * a_last[hh][ci] + d_st[ci, hh * p:(hh + 1) * p, :] for hh in range(SSM_GROUP_HEADS)]
        for hh in range(SSM_GROUP_HEADS):
            st_ref[g, hh * p:(hh + 1) * p, :] = st[hh]
        y_inter_groups.append(bmm(cm, jnp.stack(entering), ((2,), (2,))).reshape(tq, SSM_GROUP_WIDTH))
    y_inter = jnp.concatenate(y_inter_groups, axis=-1)
    y = yc_ref[...] + decay_out_w * y_inter
    y = (y + dsk_ref[...] * xc_ref[:, 0:SSM_WIDTH]) * zz_ref[0].astype(F32)
    for g in range(SSM_GROUPS):
        gs = slice(g * SSM_GROUP_WIDTH, (g + 1) * SSM_GROUP_WIDTH)
        yg = y[:, gs]
        ms = jnp.mean(yg * yg, axis=-1, keepdims=True)
        y_ref[0, :, gs] = (yg * lax.rsqrt(ms + EPS) * nw_ref[:, gs]).astype(BF16)


def _ssd(xbc, zz, dt, conv_w, conv_b, alog_pad, dsk_wide, nw, expand3):
    bsz, seqlen, _ = xbc.shape
    tq = min(SEQ_TILE, seqlen)
    tok = lambda w: pl.BlockSpec((1, tq, w), lambda b, i: (b, i, 0))
    full = lambda a: pl.BlockSpec(a.shape, lambda b, i: (0,) * a.ndim)
    return pl.pallas_call(
        _ssd_kernel,
        grid=(bsz, seqlen // tq),
        in_specs=[tok(SSM_CONV_CH), tok(SSM_WIDTH), tok(LANES), full(conv_w), full(conv_b),
                  full(alog_pad), full(dsk_wide), full(nw), full(expand3)],
        out_specs=tok(SSM_WIDTH),
        out_shape=jax.ShapeDtypeStruct((bsz, seqlen, SSM_WIDTH), BF16),
        scratch_shapes=[pltpu.VMEM((tq + 2 * SUBLANES, SSM_CONV_CH), F32),
                        pltpu.VMEM((tq, SSM_CONV_CH), F32),
                        pltpu.VMEM((tq, SSM_WIDTH), F32),
                        pltpu.VMEM((SSM_GROUPS, SSM_GROUP_WIDTH, SSM_STATE), F32)],
        compiler_params=_params(("arbitrary", "arbitrary")),
        name="ssd",
    )(xbc, zz, dt, conv_w, conv_b, alog_pad, dsk_wide, nw, expand3)


def _outproj_kernel(x_ref, oh_ref, ys_ref, wo_ref, g1_ref, nw_ref, sh_ref, sc_ref, rw_ref, rb_ref,
                    x1_ref, h2_ref, idx_ref, gate_ref, cnt_ref):
    first = jnp.logical_and(pl.program_id(0) == 0, pl.program_id(1) == 0)

    @pl.when(first)
    def _():
        cnt_ref[...] = jnp.zeros_like(cnt_ref)

    mix = jnp.dot(oh_ref[0], wo_ref[0:HG_WIDTH, :], preferred_element_type=F32)
    mix = mix + jnp.dot(ys_ref[0], wo_ref[HG_WIDTH:, :], preferred_element_type=F32)
    x1 = x_ref[0] + g1_ref[0] * mix
    x1_ref[0] = x1
    ms = jnp.mean(x1 * x1, axis=-1, keepdims=True)
    h2 = x1 * lax.rsqrt(ms + EPS) * nw_ref[...]
    h2 = h2 * (1.0 + sc_ref[0]) + sh_ref[0]
    h2_ref[0] = h2
    h_hi = h2.astype(BF16)
    h_lo = (h2 - h_hi.astype(F32)).astype(BF16)
    part = jnp.dot(h_hi, rw_ref[...], preferred_element_type=F32)
    logits = (part[:, :LANES] + part[:, LANES:]
              + jnp.dot(h_lo, rw_ref[:, :LANES], preferred_element_type=F32)) + rb_ref[...]
    lane = lax.broadcasted_iota(jnp.int32, logits.shape, 1).astype(F32)
    idx_out = jnp.zeros(logits.shape, F32)
    val_out = jnp.zeros(logits.shape, F32)
    sel = jnp.zeros(logits.shape, F32)
    work = logits
    top = None
    denom = None
    for k in range(TOP_K):
        m = jnp.max(work, axis=-1, keepdims=True)
        am = jnp.min(jnp.where(work == m, lane, float(LANES)), axis=-1, keepdims=True)
        hit = lane == am
        work = jnp.where(hit, -jnp.inf, work)
        sel = jnp.where(hit, 1.0, sel)
        if k == 0:
            top = m
        e = jnp.exp(m - top)
        denom = e if k == 0 else denom + e
        idx_out = jnp.where(lane == float(k), am, idx_out)
        val_out = jnp.where(lane == float(k), e, val_out)
    idx_ref[0] = idx_out.astype(jnp.int32)
    gate_ref[0] = val_out / denom
    cnt_ref[0:1, :] += jnp.sum(sel, axis=0, keepdims=True)


def _outproj(x, o_hg, y_ssd, w_out_b, g1, nw, sh, sc, rw_pad, rb_pad):
    bsz, seqlen, d = x.shape
    tm = min(ROW_TILE, seqlen)
    tok = lambda w: pl.BlockSpec((1, tm, w), lambda b, i: (b, i, 0))
    full = lambda a: pl.BlockSpec(a.shape, lambda b, i: (0,) * a.ndim)
    mod = pl.BlockSpec((1, 1, d), lambda b, i: (b, 0, 0))
    shp = lambda w, dt: jax.ShapeDtypeStruct((bsz, seqlen, w), dt)
    return pl.pallas_call(
        _outproj_kernel,
        grid=(bsz, seqlen // tm),
        in_specs=[tok(d), tok(HG_WIDTH), tok(SSM_WIDTH), full(w_out_b), mod, full(nw), mod, mod,
                  full(rw_pad), full(rb_pad)],
        out_specs=[tok(d), tok(d), tok(LANES), tok(LANES),
                   pl.BlockSpec((SUBLANES, LANES), lambda b, i: (0, 0))],
        out_shape=[shp(d, F32), shp(d, F32), shp(LANES, jnp.int32), shp(LANES, F32),
                   jax.ShapeDtypeStruct((SUBLANES, LANES), F32)],
        compiler_params=_params(("arbitrary", "arbitrary")),
        name="outproj_router",
    )(x, o_hg, y_ssd, w_out_b, g1, nw, sh, sc, rw_pad, rb_pad)


def _route_kernel(idx_ref, pstart_ref, dest_ref, carry_ref):
    @pl.when(pl.program_id(0) == 0)
    def _():
        carry_ref[...] = jnp.zeros_like(carry_ref)

    idx = idx_ref[...]
    tt = idx.shape[0]
    lane = lax.broadcasted_iota(jnp.int32, idx.shape, 1)
    hits = [lane == idx[:, k:k + 1] for k in range(TOP_K)]
    sel = jnp.zeros(idx.shape, F32)
    for hit in hits:
        sel = jnp.where(hit, 1.0, sel)
    rows = lax.broadcasted_iota(jnp.int32, (tt, tt), 0)
    cols = lax.broadcasted_iota(jnp.int32, (tt, tt), 1)
    before = (rows > cols).astype(BF16)
    rank = jnp.dot(before, sel.astype(BF16), preferred_element_type=F32) + carry_ref[0:1, :]
    carry_ref[0:1, :] += jnp.sum(sel, axis=0, keepdims=True)
    dense = pstart_ref[...] + rank
    pack = LANES // TOP_K
    slot = (lax.broadcasted_iota(jnp.int32, idx.shape, 0) % pack) * TOP_K
    spread = jnp.zeros(idx.shape, F32)
    for k, hit in enumerate(hits):
        dk = jnp.sum(jnp.where(hit, dense, 0.0), axis=-1, keepdims=True)
        spread = jnp.where(lane == slot + k, dk, spread)
    out_rows = tt // pack
    merge = (lax.broadcasted_iota(jnp.int32, (out_rows, tt), 0)
             == lax.broadcasted_iota(jnp.int32, (out_rows, tt), 1) // pack).astype(BF16)
    m3 = jnp.dot(merge, _split3(spread), preferred_element_type=F32)
    dest_ref[...] = (m3[:, :LANES] + m3[:, LANES:2 * LANES] + m3[:, 2 * LANES:]).astype(jnp.int32)


def _route(idx_pad, pstart_row):
    t = idx_pad.shape[0]
    tt = min(ROUTE_TILE, t)
    return pl.pallas_call(
        _route_kernel,
        grid=(t // tt,),
        in_specs=[pl.BlockSpec((tt, LANES), lambda i: (i, 0)),
                  pl.BlockSpec((1, LANES), lambda i: (0, 0))],
        out_specs=pl.BlockSpec((tt * TOP_K // LANES, LANES), lambda i: (i, 0)),
        out_shape=jax.ShapeDtypeStruct((t * TOP_K // LANES, LANES), jnp.int32),
        scratch_shapes=[pltpu.VMEM((SUBLANES, LANES), F32)],
        compiler_params=_params(("arbitrary",)),
        name="route_rank",
    )(idx_pad, pstart_row)


def _invert_kernel(dest_ref, zeros_ref, rowtok_ref, sem):
    i = pl.program_id(0)
    per_step = dest_ref.shape[0] // TOP_K // pl.num_programs(0)

    @pl.when(i == 0)
    def _():
        fill = pltpu.make_async_copy(zeros_ref, rowtok_ref, sem)
        fill.start()
        fill.wait()

    def tok_body(n, carry):
        tok = i * per_step + n
        for k in range(TOP_K):
            rowtok_ref[dest_ref[tok * TOP_K + k]] = tok
        return carry
    lax.fori_loop(0, per_step, tok_body, 0, unroll=8)


def _invert(dest_flat, n_rows):
    t = dest_flat.shape[0] // TOP_K
    return pl.pallas_call(
        _invert_kernel,
        grid=(max(t // INVERT_TILE, 1),),
        in_specs=[pl.BlockSpec(memory_space=pltpu.SMEM), pl.BlockSpec(memory_space=pl.ANY)],
        out_specs=pl.BlockSpec(memory_space=pltpu.SMEM),
        out_shape=jax.ShapeDtypeStruct((n_rows,), jnp.int32),
        scratch_shapes=[pltpu.SemaphoreType.DMA(())],
        compiler_params=_params(("arbitrary",)),
        name="invert_route",
    )(dest_flat, jnp.zeros((n_rows,), jnp.int32))


def _expert_kernel(be_ref, nact_ref, nxt_ref, rowtok_ref, h_ref, w1_ref, b1_ref, w2_ref, b2_ref, y_ref,
                   xbuf0_ref, xbuf1_ref, xbuf2_ref, w1s_ref, w2s_ref, w1b_ref, w2b_ref, sem, wsem):
    j = pl.program_id(0)
    n_act = nact_ref[0]
    active = j < n_act
    expert = be_ref[j]
    fresh = jnp.logical_or(j == 0, expert != be_ref[jnp.maximum(j - 1, 0)])
    phase = j % GATHER_DEPTH
    bufs = (xbuf0_ref, xbuf1_ref, xbuf2_ref)

    def issue(block, s):
        base = jnp.minimum(block, n_act - 1) * MOE_ROWS
        for r in range(MOE_ROWS):
            pltpu.make_async_copy(h_ref.at[pl.ds(rowtok_ref[base + r], 1), :],
                                  bufs[s].at[pl.ds(r, 1), :], sem.at[s]).start(priority=1)

    def wait(s):
        pltpu.make_async_copy(h_ref.at[pl.ds(0, MOE_ROWS), :], bufs[s], sem.at[s]).wait()

    def weight_copies(e):
        return (pltpu.make_async_copy(w1_ref.at[e], w1s_ref, wsem.at[0]),
                pltpu.make_async_copy(w2_ref.at[e], w2s_ref, wsem.at[1]))

    @pl.when(j == 0)
    def _():
        for cp in weight_copies(expert):
            cp.start()
        for s in range(GATHER_DEPTH - 1):
            issue(s, s)

    @pl.when(jnp.logical_and(active, fresh))
    def _():
        for cp in weight_copies(expert):
            cp.wait()
        w1b_ref[...] = w1s_ref[...].astype(BF16)
        w2b_ref[...] = w2s_ref[...].astype(BF16)

        @pl.when(nxt_ref[expert] < N_EXPERTS)
        def _():
            for cp in weight_copies(nxt_ref[expert]):
                cp.start()

    for s in range(GATHER_DEPTH):
        @pl.when(jnp.logical_and(j < n_act + GATHER_DEPTH - 1, phase == s))
        def _(s=s):
            wait(s)

        @pl.when(jnp.logical_and(active, phase == s))
        def _(s=s):
            issue(j + GATHER_DEPTH - 1, (s + GATHER_DEPTH - 1) % GATHER_DEPTH)
            hb = jnp.dot(bufs[s][...].astype(BF16), w1b_ref[...], preferred_element_type=F32) + b1_ref[0]
            glu = jnp.minimum(hb[:, :D_FF], SWIGLU_LIMIT)
            lin = jnp.clip(hb[:, D_FF:], -SWIGLU_LIMIT, SWIGLU_LIMIT)
            act = glu * jax.nn.sigmoid(SWIGLU_ALPHA * glu) * (lin + 1.0)
            y_ref[...] = jnp.dot(act.astype(BF16), w2b_ref[...], preferred_element_type=F32) + b2_ref[0]


def _experts(block_e, n_act, next_e, row_tok, h2, w1, b1, w2, b2):
    n_rows = row_tok.shape[0]
    d = h2.shape[1]
    nb = block_e.shape[0]
    row_map = lambda j, be, na, nx, rt: (jnp.maximum(jnp.minimum(j, na[0] - 1), 0), 0)
    exp_map = lambda j, be, na, nx, rt: (be[j], 0, 0)
    return pl.pallas_call(
        _expert_kernel,
        grid_spec=pltpu.PrefetchScalarGridSpec(
            num_scalar_prefetch=4,
            grid=(nb,),
            in_specs=[pl.BlockSpec(memory_space=pl.ANY),
                      pl.BlockSpec(memory_space=pl.ANY),
                      pl.BlockSpec((1, 1, 2 * D_FF), exp_map),
                      pl.BlockSpec(memory_space=pl.ANY),
                      pl.BlockSpec((1, 1, d), exp_map)],
            out_specs=pl.BlockSpec((MOE_ROWS, d), row_map),
            scratch_shapes=[pltpu.VMEM((MOE_ROWS, d), F32)] * GATHER_DEPTH + [
                pltpu.VMEM((d, 2 * D_FF), F32), pltpu.VMEM((D_FF, d), F32),
                pltpu.VMEM((d, 2 * D_FF), BF16), pltpu.VMEM((D_FF, d), BF16),
                pltpu.SemaphoreType.DMA((GATHER_DEPTH,)), pltpu.SemaphoreType.DMA((2,))],
        ),
        out_shape=jax.ShapeDtypeStruct((n_rows, d), F32),
        compiler_params=_params(("arbitrary",)),
        name="expert_ffn",
    )(block_e, n_act, next_e, row_tok, h2, w1, b1.reshape(N_EXPERTS, 1, 2 * D_FF), w2,
      b2.reshape(N_EXPERTS, 1, d))


def _combine_kernel(dest_ref, x1_ref, gate_ref, g2_ref, fw_ref, y_ref, o_ref, buf_ref, sem):
    i = pl.program_id(0)
    n = pl.num_programs(0)
    gt = x1_ref.shape[0]

    def row_copy(slot, k, j, src_row):
        return pltpu.make_async_copy(y_ref.at[pl.ds(src_row, 1), :],
                                     buf_ref.at[slot, k, pl.ds(j, 1), :], sem.at[slot])

    def issue(tile, slot):
        base = tile * (gt * TOP_K)
        for j in range(gt):
            for k in range(TOP_K):
                row_copy(slot, k, j, dest_ref[base + j * TOP_K + k]).start(priority=k % 2)

    @pl.when(i == 0)
    def _():
        issue(0, 0)

    for s in range(2):
        @pl.when(jnp.logical_and(i + 1 < n, (i + 1) % 2 == s))
        def _(s=s):
            issue(i + 1, s)

    slot = i % 2

    for k in range(TOP_K):
        pltpu.make_async_copy(y_ref.at[pl.ds(0, gt), :], buf_ref.at[slot, k], sem.at[slot]).wait()

    gates = gate_ref[...]
    moe = gates[:, 0:1] * buf_ref[slot, 0]
    for k in range(1, TOP_K):
        moe = moe + gates[:, k:k + 1] * buf_ref[slot, k]
    x2 = x1_ref[...] + g2_ref[0] * moe
    ms = jnp.mean(x2 * x2, axis=-1, keepdims=True)
    o_ref[...] = x2 * lax.rsqrt(ms + EPS) * fw_ref[...]


def _combine(dest_flat, x1, gate_pad, g2, fw, yout, seqlen):
    t, d = x1.shape
    gt = min(GATHER_TILE, seqlen)
    per_seq = seqlen // gt
    return pl.pallas_call(
        _combine_kernel,
        grid_spec=pltpu.PrefetchScalarGridSpec(
            num_scalar_prefetch=1,
            grid=(t // gt,),
            in_specs=[pl.BlockSpec((gt, d), lambda i, dst: (i, 0)),
                      pl.BlockSpec((gt, LANES), lambda i, dst: (i, 0)),
                      pl.BlockSpec((1, 1, d), lambda i, dst: (i // per_seq, 0, 0)),
                      pl.BlockSpec((1, d), lambda i, dst: (0, 0)),
                      pl.BlockSpec(memory_space=pl.ANY)],
            out_specs=pl.BlockSpec((gt, d), lambda i, dst: (i, 0)),
            scratch_shapes=[pltpu.VMEM((2, TOP_K, gt, d), F32), pltpu.SemaphoreType.DMA((2,))],
        ),
        out_shape=jax.ShapeDtypeStruct((t, d), F32),
        compiler_params=_params(("arbitrary",)),
        name="combine_norm",
    )(dest_flat, x1, gate_pad, g2, fw, yout)


def _head_expand():
    head_of_lane = jnp.arange(SSM_WIDTH) // SSM_HEAD_DIM
    one = (jnp.arange(LANES)[:, None] == head_of_lane[None, :]).astype(BF16)
    return jnp.tile(one, (3, 1))


def _pad_lanes(v, fill=0.0):
    v = v.reshape(1, -1).astype(F32)
    return jnp.pad(v, ((0, 0), (0, LANES - v.shape[1])), constant_values=fill)


def kernel(x, c, ada_w, ada_b, norm1_w, w_in, hg_lb_logits, hg_norm_w, ssm_a_log, ssm_dt_bias, ssm_d,
           ssm_conv_w, ssm_conv_b, ssm_norm_w, w_out, norm2_w, router_w, router_b, exp_w1, exp_b1,
           exp_w2, exp_b2, final_norm_w):
    bsz, seqlen, d = x.shape
    t = bsz * seqlen
    l = 0

    c_pad = jnp.pad(c, ((0, SUBLANES - bsz), (0, 0)))
    mod = _ada_mod(c_pad, ada_w[l], ada_b[l])[:bsz]
    sh1, sc1, g1, sh2, sc2, g2 = [m.reshape(bsz, 1, d) for m in jnp.split(mod, N_MOD, axis=-1)]

    w_in_pad = jnp.pad(jnp.swapaxes(w_in[l], 0, 1).astype(BF16), ((0, IN_COLS_PAD - IN_COLS), (0, 0)))
    qs, kk, lf, vi, gg, zz, xbc, dt = _inproj(
        x, norm1_w[l].reshape(1, d), sh1, sc1, w_in_pad, hg_lb_logits, _pad_lanes(ssm_dt_bias[l]))

    o_hg = _hgrn(qs, kk, lf, vi, gg, hg_norm_w[l].reshape(1, HG_DV))
    y_ssd = _ssd(xbc, zz, dt, ssm_conv_w[l], ssm_conv_b[l].reshape(1, SSM_CONV_CH),
                 _pad_lanes(ssm_a_log[l]), jnp.repeat(ssm_d[l], SSM_HEAD_DIM).reshape(1, SSM_WIDTH),
                 ssm_norm_w[l].reshape(1, SSM_WIDTH), _head_expand())

    rw_pad = jnp.pad(router_w[l], ((0, 0), (0, LANES - N_EXPERTS)))
    rw_hi = rw_pad.astype(BF16)
    rw_pad = jnp.concatenate([rw_hi, (rw_pad - rw_hi.astype(F32)).astype(BF16)], axis=1)
    rb_pad = _pad_lanes(router_b[l], NEG_BIG)
    x1, h2, idx_pad, gate_pad, counts = _outproj(
        x, o_hg, y_ssd, w_out[l].astype(BF16), g1, norm2_w[l].reshape(1, d), sh2, sc2, rw_pad, rb_pad)

    cnt = counts[0, :N_EXPERTS].astype(jnp.int32)
    blocks_e = (cnt + MOE_ROWS - 1) // MOE_ROWS
    blk_end = jnp.cumsum(blocks_e)
    pstart = (blk_end - blocks_e) * MOE_ROWS
    n_blocks = (t * TOP_K) // MOE_ROWS + N_EXPERTS
    n_rows = n_blocks * MOE_ROWS
    n_steps = n_blocks + GATHER_DEPTH - 1
    block_e = jnp.minimum(jnp.sum(blk_end[None, :] <= jnp.arange(n_steps)[:, None], axis=1),
                          N_EXPERTS - 1).astype(jnp.int32)
    n_act = blk_end[-1:].astype(jnp.int32)
    owner = jnp.where(blocks_e > 0, jnp.arange(N_EXPERTS), N_EXPERTS)
    next_e = jnp.concatenate([lax.cummin(owner, reverse=True)[1:],
                              jnp.full((1,), N_EXPERTS)]).astype(jnp.int32)

    dest_flat = _route(idx_pad.reshape(t, LANES), _pad_lanes(pstart)).reshape(t * TOP_K)

    row_tok = _invert(dest_flat, n_rows)
    yout = _experts(block_e, n_act, next_e, row_tok, h2.reshape(t, d), exp_w1[l], exp_b1[l], exp_w2[l], exp_b2[l])
    out = _combine(dest_flat, x1.reshape(t, d), gate_pad.reshape(t, LANES), g2,
                   final_norm_w.reshape(1, d), yout, seqlen)
    return out.reshape(bsz, seqlen, d)
```

```python
import functools

import jax
import jax.numpy as jnp
from jax import lax
from jax.experimental import pallas as pl
from jax.experimental.pallas import tpu as pltpu

F32 = jnp.float32
BF16 = jnp.bfloat16
HIGHEST = lax.Precision.HIGHEST

EPS = 1e-6
D_MODEL = 1024
HG_HEADS = 4
HG_DK = 128
HG_DV = 128
HG_QF = HG_HEADS * HG_DK
HG_WIDTH = HG_HEADS * HG_DV
HG_CHUNK = 64
SSM_HEADS = 8
SSM_HEAD_DIM = 64
SSM_WIDTH = SSM_HEADS * SSM_HEAD_DIM
SSM_GROUPS = 2
SSM_GROUP_HEADS = SSM_HEADS // SSM_GROUPS
SSM_GROUP_WIDTH = SSM_WIDTH // SSM_GROUPS
SSM_STATE = 128
SSM_CONV = 4
SSM_CONV_CH = SSM_WIDTH + 2 * SSM_GROUPS * SSM_STATE
SSM_CHUNK = 128
IN_SPLITS = (HG_QF, HG_QF, HG_WIDTH, HG_WIDTH, SSM_WIDTH, SSM_CONV_CH, SSM_HEADS)
IN_COLS = sum(IN_SPLITS)
N_EXPERTS = 32
TOP_K = 4
D_FF = 1024
SWIGLU_LIMIT = 7.0
SWIGLU_ALPHA = 1.702
N_MOD = 6

LANES = 128
SUBLANES = 8
VMEM_LIMIT = 56 * 1024 * 1024

ROW_TILE = 1024
ROUTE_TILE = 512
SEQ_TILE = 1024
MOE_ROWS = 256
GATHER_TILE = 256
INVERT_TILE = 1024
GATHER_DEPTH = 3
NEG_BIG = -1e30


def _silu(v):
    return v * jax.nn.sigmoid(v)


def _softplus(v):
    return jnp.maximum(v, 0.0) + jnp.log1p(jnp.exp(-jnp.abs(v)))


def _split3(v):
    hi = v.astype(BF16)
    r1 = v - hi.astype(F32)
    mid = r1.astype(BF16)
    lo = (r1 - mid.astype(F32)).astype(BF16)
    return jnp.concatenate([hi, mid, lo], axis=-1)


def _params(sem):
    return pltpu.CompilerParams(dimension_semantics=sem, vmem_limit_bytes=VMEM_LIMIT)


def _ada_kernel(c_ref, w_ref, b_ref, o_ref):
    o_ref[...] = jnp.dot(_silu(c_ref[...]), w_ref[...], precision=HIGHEST,
                         preferred_element_type=F32) + b_ref[...]


def _ada_mod(c_pad, ada_w, ada_b):
    n = ada_w.shape[1]
    tn = D_MODEL
    return pl.pallas_call(
        _ada_kernel,
        grid=(n // tn,),
        in_specs=[pl.BlockSpec((SUBLANES, D_MODEL), lambda j: (0, 0)),
                  pl.BlockSpec((D_MODEL, tn), lambda j: (0, j)),
                  pl.BlockSpec((1, tn), lambda j: (0, j))],
        out_specs=pl.BlockSpec((SUBLANES, tn), lambda j: (0, j)),
        out_shape=jax.ShapeDtypeStruct((SUBLANES, n), F32),
        compiler_params=_params(("arbitrary",)),
        name="ada_mod",
    )(c_pad, ada_w, ada_b.reshape(1, n))


_OFF = [0]
for _w in IN_SPLITS:
    _OFF.append(_OFF[-1] + _w)


def _inproj_kernel(x_ref, nw_ref, sh_ref, sc_ref, w_ref, wdt_ref, lbl_ref, dtb_ref,
                   qs_ref, kk_ref, lf_ref, vi_ref, gg_ref, zz_ref, xbc_ref, dt_ref):
    x = x_ref[0]
    ms = jnp.mean(x * x, axis=-1, keepdims=True)
    h = x * lax.rsqrt(ms + EPS) * nw_ref[...]
    h = h * (1.0 + sc_ref[0]) + sh_ref[0]
    hb = h.astype(BF16)

    def seg(k):
        cols = w_ref[_OFF[k]:_OFF[k + 1], :] if k < 6 else wdt_ref[...]
        return lax.dot_general(hb, cols, (((1,), (1,)), ((), ())), preferred_element_type=F32)

    lbl = lbl_ref[...]
    le = jnp.exp(lbl - jnp.max(lbl, axis=0, keepdims=True))
    lb = le[0:1, :] / jnp.sum(le, axis=0, keepdims=True)

    qs_ref[0] = _silu(seg(0)).astype(BF16)
    fg = lb + (1.0 - lb) * jax.nn.sigmoid(seg(1))
    kk_ref[0] = (1.0 - fg).astype(BF16)
    lf_ref[0] = jnp.log(fg)
    vi_ref[0] = seg(2).astype(BF16)
    gg_ref[0] = _silu(seg(3)).astype(BF16)
    zz_ref[0] = _silu(seg(4)).astype(BF16)
    xbc_ref[0] = seg(5).astype(BF16)
    dt_ref[0] = _softplus(seg(6) + dtb_ref[...])


def _inproj(x, nw, sh, sc, w_main, w_dt, lb_logits, dtb_pad):
    bsz, seqlen, d = x.shape
    tm = min(ROW_TILE, seqlen)
    nt = seqlen // tm
    tok = lambda w: pl.BlockSpec((1, tm, w), lambda b, i: (b, i, 0))
    full = lambda a: pl.BlockSpec(a.shape, lambda b, i: (0,) * a.ndim)
    mod = pl.BlockSpec((1, 1, d), lambda b, i: (b, 0, 0))
    shp = lambda w, dt: jax.ShapeDtypeStruct((bsz, seqlen, w), dt)
    return pl.pallas_call(
        _inproj_kernel,
        grid=(bsz, nt),
        in_specs=[tok(d), full(nw), mod, mod, full(w_main), full(w_dt), full(lb_logits), full(dtb_pad)],
        out_specs=[tok(HG_QF), tok(HG_QF), tok(HG_QF), tok(HG_WIDTH), tok(HG_WIDTH),
                   tok(SSM_WIDTH), tok(SSM_CONV_CH), tok(LANES)],
        out_shape=[shp(HG_QF, BF16), shp(HG_QF, BF16), shp(HG_QF, F32), shp(HG_WIDTH, BF16),
                   shp(HG_WIDTH, BF16), shp(SSM_WIDTH, BF16), shp(SSM_CONV_CH, BF16),
                   shp(LANES, F32)],
        compiler_params=_params(("arbitrary", "arbitrary")),
        name="inproj",
    )(x, nw, sh, sc, w_main, w_dt, lb_logits, dtb_pad)


def _hgrn_kernel(qs_ref, kk_ref, lf_ref, vi_ref, gg_ref, nw_ref, o_ref, st_ref):
    @pl.when(pl.program_id(2) == 0)
    def _():
        st_ref[...] = jnp.zeros_like(st_ref)

    c = HG_CHUNK
    tq = qs_ref.shape[1]
    nc = tq // c
    rows = lax.broadcasted_iota(jnp.int32, (nc, c, c), 1)
    cols = lax.broadcasted_iota(jnp.int32, (nc, c, c), 2)
    causal = rows >= cols
    chunked = lambda ref: ref[0].reshape(nc, c, ref.shape[2])
    q = chunked(qs_ref).astype(F32)
    k = chunked(kk_ref).astype(F32)
    v = chunked(vi_ref)
    bmm = lambda a, bb, dims, **kw: lax.dot_general(a, bb, (dims, ((0,), (0,))),
                                                    preferred_element_type=F32, **kw)
    b3 = bmm(causal.astype(BF16), _split3(chunked(lf_ref)), ((2,), (1,)))
    b = b3[:, :, :HG_DK] + b3[:, :, HG_DK:2 * HG_DK] + b3[:, :, 2 * HG_DK:]
    b_mid = b[:, c // 2 - 1:c // 2, :]
    b_last = b[:, c - 1:c, :]
    qa = (q * jnp.exp(b - b_mid)).astype(BF16)
    ka = (k * jnp.exp(b_mid - b)).astype(BF16)
    att = bmm(qa, ka, ((2,), (2,)))
    att = jnp.where(causal, att, 0.0).astype(BF16)
    o = bmm(att, v, ((2,), (1,)))
    kd = (k * jnp.exp(b_last - b)).astype(BF16)
    d_st = bmm(v, kd, ((1,), (1,)))
    decay = jnp.exp(b_last)
    st = st_ref[...]
    entering = []
    for ci in range(nc):
        entering.append(st.astype(BF16))
        st = st * decay[ci] + d_st[ci]
    st_ref[...] = st
    qb = (q * jnp.exp(b)).astype(BF16)
    o = o + bmm(qb, jnp.stack(entering), ((2,), (2,)))
    ms = jnp.mean(o * o, axis=-1, keepdims=True)
    y = o * lax.rsqrt(ms + EPS) * nw_ref[...] * chunked(gg_ref).astype(F32)
    o_ref[0] = y.reshape(tq, o_ref.shape[2]).astype(BF16)


def _hgrn(qs, kk, lf, vi, gg, nw):
    bsz, seqlen, _ = qs.shape
    tq = min(SEQ_TILE, seqlen)
    blk = pl.BlockSpec((1, tq, HG_DK), lambda b, h, i: (b, i, h))
    return pl.pallas_call(
        _hgrn_kernel,
        grid=(bsz, HG_HEADS, seqlen // tq),
        in_specs=[blk, blk, blk, blk, blk, pl.BlockSpec((1, HG_DV), lambda b, h, i: (0, 0))],
        out_specs=blk,
        out_shape=jax.ShapeDtypeStruct((bsz, seqlen, HG_WIDTH), BF16),
        scratch_shapes=[pltpu.VMEM((HG_DV, HG_DK), F32)],
        compiler_params=_params(("arbitrary", "arbitrary", "arbitrary")),
        name="hgrn2",
    )(qs, kk, lf, vi, gg, nw)


def _ssd_kernel(xbc_ref, zz_ref, dt_ref, cw_ref, cb_ref, alog_ref, dsk_ref, nw_ref, ex_ref,
                y_ref, buf_ref, xc_ref, yc_ref, st_ref):
    tq = xbc_ref.shape[1]
    halo = SUBLANES

    @pl.when(pl.program_id(1) == 0)
    def _():
        buf_ref[0:halo, :] = jnp.zeros((halo, SSM_CONV_CH), F32)
        st_ref[...] = jnp.zeros_like(st_ref)

    buf_ref[halo:halo + tq, :] = xbc_ref[0].astype(F32)
    conv = cb_ref[...] + buf_ref[halo - 3:halo - 3 + tq, :] * cw_ref[0:1, :]
    for w in range(1, SSM_CONV):
        conv = conv + buf_ref[halo - 3 + w:halo - 3 + w + tq, :] * cw_ref[w:w + 1, :]
    buf_ref[0:halo, :] = buf_ref[tq:tq + halo, :]
    xc_ref[...] = _silu(conv)

    c = SSM_CHUNK
    p = SSM_HEAD_DIM
    nc = tq // c
    rows = lax.broadcasted_iota(jnp.int32, (nc, c, c), 1)
    cols = lax.broadcasted_iota(jnp.int32, (nc, c, c), 2)
    causal = rows >= cols
    tril = causal.astype(BF16)
    bmm = lambda a, bb, dims, **kw: lax.dot_general(a, bb, (dims, ((0,), (0,))),
                                                    preferred_element_type=F32, **kw)
    chunked = lambda val: val.reshape(nc, c, val.shape[-1])
    dt = chunked(dt_ref[0])
    da = dt * (-jnp.exp(alog_ref[...]))
    da3 = _split3(da)
    a3 = bmm(tril, da3, ((2,), (1,)))
    acum = a3[:, :, :LANES] + a3[:, :, LANES:2 * LANES] + a3[:, :, 2 * LANES:]
    at3 = bmm(da3, tril, ((1,), (2,)))
    acum_t = at3[:, :LANES, :] + at3[:, LANES:2 * LANES, :] + at3[:, 2 * LANES:, :]
    widen = lambda v: jnp.dot(_split3(v.reshape(tq, LANES)), ex_ref[...], preferred_element_type=F32)
    dt_w = widen(dt)
    decay_out_w = widen(jnp.exp(acum))
    decay_in_w = widen(jnp.exp(acum[:, c - 1:c, :] - acum))
    xdt_w = xc_ref[:, 0:SSM_WIDTH] * dt_w
    xdd_w = xdt_w * decay_in_w
    y_inter_groups = []
    for g in range(SSM_GROUPS):
        gs = slice(g * SSM_GROUP_WIDTH, (g + 1) * SSM_GROUP_WIDTH)
        bm = chunked(xc_ref[:, SSM_WIDTH + g * SSM_STATE:SSM_WIDTH + (g + 1) * SSM_STATE]).astype(BF16)
        cm = chunked(xc_ref[:, SSM_WIDTH + (SSM_GROUPS + g) * SSM_STATE:
                            SSM_WIDTH + (SSM_GROUPS + g + 1) * SSM_STATE]).astype(BF16)
        cb = bmm(cm, bm, ((2,), (2,)))
        d_st = bmm(chunked(xdd_w[:, gs]).astype(BF16), bm, ((1,), (1,)))
        a_last = []
        for hh in range(SSM_GROUP_HEADS):
            h = g * SSM_GROUP_HEADS + hh
            hs = slice(h * p, (h + 1) * p)
            seg = jnp.exp(jnp.where(causal, acum[:, :, h:h + 1] - acum_t[:, h:h + 1, :], -jnp.inf))
            y = bmm((cb * seg).astype(BF16), chunked(xdt_w[:, hs]).astype(BF16), ((2,), (1,)))
            yc_ref[:, hs] = y.reshape(tq, p)
            a_last.append(jnp.exp(acum_t[:, h:h + 1, c - 1:c]))
        st = [st_ref[g, hh * p:(hh + 1) * p, :] for hh in range(SSM_GROUP_HEADS)]
        entering = []
        for ci in range(nc):
            entering.append(jnp.concatenate(st, axis=0).astype(BF16))
            st = [st[hh] * a_last[hh][ci] + d_st[ci, hh * p:(hh + 1) * p, :] for hh in range(SSM_GROUP_HEADS)]
        for hh in range(SSM_GROUP_HEADS):
            st_ref[g, hh * p:(hh + 1) * p, :] = st[hh]
        y_inter_groups.append(bmm(cm, jnp.stack(entering), ((2,), (2,))).reshape(tq, SSM_GROUP_WIDTH))
    y_inter = jnp.concatenate(y_inter_groups, axis=-1)
    y = yc_ref[...] + decay_out_w * y_inter
    y = (y + dsk_ref[...] * xc_ref[:, 0:SSM_WIDTH]) * zz_ref[0].astype(F32)
    for g in range(SSM_GROUPS):
        gs = slice(g * SSM_GROUP_WIDTH, (g + 1) * SSM_GROUP_WIDTH)
        yg = y[:, gs]
        ms = jnp.mean(yg * yg, axis=-1, keepdims=True)
        y_ref[0, :, gs] = (yg * lax.rsqrt(ms + EPS) * nw_ref[:, gs]).astype(BF16)


def _ssd(xbc, zz, dt, conv_w, conv_b, alog_pad, dsk_wide, nw, expand3):
    bsz, seqlen, _ = xbc.shape
    tq = min(SEQ_TILE, seqlen)
    tok = lambda w: pl.BlockSpec((1, tq, w), lambda b, i: (b, i, 0))
    full = lambda a: pl.BlockSpec(a.shape, lambda b, i: (0,) * a.ndim)
    return pl.pallas_call(
        _ssd_kernel,
        grid=(bsz, seqlen // tq),
        in_specs=[tok(SSM_CONV_CH), tok(SSM_WIDTH), tok(LANES), full(conv_w), full(conv_b),
                  full(alog_pad), full(dsk_wide), full(nw), full(expand3)],
        out_specs=tok(SSM_WIDTH),
        out_shape=jax.ShapeDtypeStruct((bsz, seqlen, SSM_WIDTH), BF16),
        scratch_shapes=[pltpu.VMEM((tq + 2 * SUBLANES, SSM_CONV_CH), F32),
                        pltpu.VMEM((tq, SSM_CONV_CH), F32),
                        pltpu.VMEM((tq, SSM_WIDTH), F32),
                        pltpu.VMEM((SSM_GROUPS, SSM_GROUP_WIDTH, SSM_STATE), F32)],
        compiler_params=_params(("arbitrary", "arbitrary")),
        name="ssd",
    )(xbc, zz, dt, conv_w, conv_b, alog_pad, dsk_wide, nw, expand3)


def _outproj_kernel(x_ref, oh_ref, ys_ref, wo_ref, g1_ref, nw_ref, sh_ref, sc_ref, rw_ref, rb_ref,
                    x1_ref, h2_ref, idx_ref, gate_ref, cnt_ref):
    first = jnp.logical_and(pl.program_id(0) == 0, pl.program_id(1) == 0)

    @pl.when(first)
    def _():
        cnt_ref[...] = jnp.zeros_like(cnt_ref)

    mix = jnp.dot(oh_ref[0], wo_ref[0:HG_WIDTH, :], preferred_element_type=F32)
    mix = mix + jnp.dot(ys_ref[0], wo_ref[HG_WIDTH:, :], preferred_element_type=F32)
    x1 = x_ref[0] + g1_ref[0] * mix
    x1_ref[0] = x1
    ms = jnp.mean(x1 * x1, axis=-1, keepdims=True)
    h2 = x1 * lax.rsqrt(ms + EPS) * nw_ref[...]
    h2 = h2 * (1.0 + sc_ref[0]) + sh_ref[0]
    h2_ref[0] = h2
    h_hi = h2.astype(BF16)
    h_lo = (h2 - h_hi.astype(F32)).astype(BF16)
    part = jnp.dot(h_hi, rw_ref[...], preferred_element_type=F32)
    logits = (part[:, :LANES] + part[:, LANES:]
              + jnp.dot(h_lo, rw_ref[:, :LANES], preferred_element_type=F32)) + rb_ref[...]
    lane = lax.broadcasted_iota(jnp.int32, logits.shape, 1).astype(F32)
    idx_out = jnp.zeros(logits.shape, F32)
    val_out = jnp.zeros(logits.shape, F32)
    sel = jnp.zeros(logits.shape, F32)
    work = logits
    top = None
    denom = None
    for k in range(TOP_K):
        m = jnp.max(work, axis=-1, keepdims=True)
        am = jnp.min(jnp.where(work == m, lane, float(LANES)), axis=-1, keepdims=True)
        hit = lane == am
        work = jnp.where(hit, -jnp.inf, work)
        sel = jnp.where(hit, 1.0, sel)
        if k == 0:
            top = m
        e = jnp.exp(m - top)
        denom = e if k == 0 else denom + e
        idx_out = jnp.where(lane == float(k), am, idx_out)
        val_out = jnp.where(lane == float(k), e, val_out)
    idx_ref[0] = idx_out.astype(jnp.int32)
    gate_ref[0] = val_out / denom
    cnt_ref[0:1, :] += jnp.sum(sel, axis=0, keepdims=True)


def _outproj(x, o_hg, y_ssd, w_out_b, g1, nw, sh, sc, rw_pad, rb_pad):
    bsz, seqlen, d = x.shape
    tm = min(ROW_TILE, seqlen)
    tok = lambda w: pl.BlockSpec((1, tm, w), lambda b, i: (b, i, 0))
    full = lambda a: pl.BlockSpec(a.shape, lambda b, i: (0,) * a.ndim)
    mod = pl.BlockSpec((1, 1, d), lambda b, i: (b, 0, 0))
    shp = lambda w, dt: jax.ShapeDtypeStruct((bsz, seqlen, w), dt)
    return pl.pallas_call(
        _outproj_kernel,
        grid=(bsz, seqlen // tm),
        in_specs=[tok(d), tok(HG_WIDTH), tok(SSM_WIDTH), full(w_out_b), mod, full(nw), mod, mod,
                  full(rw_pad), full(rb_pad)],
        out_specs=[tok(d), tok(d), tok(LANES), tok(LANES),
                   pl.BlockSpec((SUBLANES, LANES), lambda b, i: (0, 0))],
        out_shape=[shp(d, F32), shp(d, F32), shp(LANES, jnp.int32), shp(LANES, F32),
                   jax.ShapeDtypeStruct((SUBLANES, LANES), F32)],
        compiler_params=_params(("arbitrary", "arbitrary")),
        name="outproj_router",
    )(x, o_hg, y_ssd, w_out_b, g1, nw, sh, sc, rw_pad, rb_pad)


def _route_kernel(idx_ref, pstart_ref, dest_ref, carry_ref):
    @pl.when(pl.program_id(0) == 0)
    def _():
        carry_ref[...] = jnp.zeros_like(carry_ref)

    idx = idx_ref[...]
    tt = idx.shape[0]
    lane = lax.broadcasted_iota(jnp.int32, idx.shape, 1)
    hits = [lane == idx[:, k:k + 1] for k in range(TOP_K)]
    sel = jnp.zeros(idx.shape, F32)
    for hit in hits:
        sel = jnp.where(hit, 1.0, sel)
    rows = lax.broadcasted_iota(jnp.int32, (tt, tt), 0)
    cols = lax.broadcasted_iota(jnp.int32, (tt, tt), 1)
    before = (rows > cols).astype(BF16)
    rank = jnp.dot(before, sel.astype(BF16), preferred_element_type=F32) + carry_ref[0:1, :]
    carry_ref[0:1, :] += jnp.sum(sel, axis=0, keepdims=True)
    dense = pstart_ref[...] + rank
    pack = LANES // TOP_K
    slot = (lax.broadcasted_iota(jnp.int32, idx.shape, 0) % pack) * TOP_K
    spread = jnp.zeros(idx.shape, F32)
    for k, hit in enumerate(hits):
        dk = jnp.sum(jnp.where(hit, dense, 0.0), axis=-1, keepdims=True)
        spread = jnp.where(lane == slot + k, dk, spread)
    out_rows = tt // pack
    merge = (lax.broadcasted_iota(jnp.int32, (out_rows, tt), 0)
             == lax.broadcasted_iota(jnp.int32, (out_rows, tt), 1) // pack).astype(BF16)
    m3 = jnp.dot(merge, _split3(spread), preferred_element_type=F32)
    dest_ref[...] = (m3[:, :LANES] + m3[:, LANES:2 * LANES] + m3[:, 2 * LANES:]).astype(jnp.int32)


def _route(idx_pad, pstart_row):
    t = idx_pad.shape[0]
    tt = min(ROUTE_TILE, t)
    return pl.pallas_call(
        _route_kernel,
        grid=(t // tt,),
        in_specs=[pl.BlockSpec((tt, LANES), lambda i: (i, 0)),
                  pl.BlockSpec((1, LANES), lambda i: (0, 0))],
        out_specs=pl.BlockSpec((tt * TOP_K // LANES, LANES), lambda i: (i, 0)),
        out_shape=jax.ShapeDtypeStruct((t * TOP_K // LANES, LANES), jnp.int32),
        scratch_shapes=[pltpu.VMEM((SUBLANES, LANES), F32)],
        compiler_params=_params(("arbitrary",)),
        name="route_rank",
    )(idx_pad, pstart_row)


def _invert_kernel(dest_ref, zeros_ref, rowtok_ref, sem):
    i = pl.program_id(0)
    per_step = dest_ref.shape[0] // TOP_K // pl.num_programs(0)

    @pl.when(i == 0)
    def _():
        fill = pltpu.make_async_copy(zeros_ref, rowtok_ref, sem)
        fill.start()
        fill.wait()

    def tok_body(n, carry):
        tok = i * per_step + n
        for k in range(TOP_K):
            rowtok_ref[dest_ref[tok * TOP_K + k]] = tok
        return carry
    lax.fori_loop(0, per_step, tok_body, 0, unroll=8)


def _invert(dest_flat, n_rows):
    t = dest_flat.shape[0] // TOP_K
    return pl.pallas_call(
        _invert_kernel,
        grid=(max(t // INVERT_TILE, 1),),
        in_specs=[pl.BlockSpec(memory_space=pltpu.SMEM), pl.BlockSpec(memory_space=pl.ANY)],
        out_specs=pl.BlockSpec(memory_space=pltpu.SMEM),
        out_shape=jax.ShapeDtypeStruct((n_rows,), jnp.int32),
        scratch_shapes=[pltpu.SemaphoreType.DMA(())],
        compiler_params=_params(("arbitrary",)),
        name="invert_route",
    )(dest_flat, jnp.zeros((n_rows,), jnp.int32))


def _expert_kernel(be_ref, nact_ref, nxt_ref, rowtok_ref, h_ref, w1_ref, b1_ref, w2_ref, b2_ref, y_ref,
                   xbuf0_ref, xbuf1_ref, xbuf2_ref, w1s_ref, w2s_ref, w1b_ref, w2b_ref, sem, wsem):
    j = pl.program_id(0)
    n_act = nact_ref[0]
    active = j < n_act
    expert = be_ref[j]
    fresh = jnp.logical_or(j == 0, expert != be_ref[jnp.maximum(j - 1, 0)])
    phase = j % GATHER_DEPTH
    bufs = (xbuf0_ref, xbuf1_ref, xbuf2_ref)

    def issue(block, s):
        base = jnp.minimum(block, n_act - 1) * MOE_ROWS
        for r in range(MOE_ROWS):
            pltpu.make_async_copy(h_ref.at[pl.ds(rowtok_ref[base + r], 1), :],
                                  bufs[s].at[pl.ds(r, 1), :], sem.at[s]).start(priority=1)

    def wait(s):
        pltpu.make_async_copy(h_ref.at[pl.ds(0, MOE_ROWS), :], bufs[s], sem.at[s]).wait()

    def weight_copies(e):
        return (pltpu.make_async_copy(w1_ref.at[e], w1s_ref, wsem.at[0]),
                pltpu.make_async_copy(w2_ref.at[e], w2s_ref, wsem.at[1]))

    @pl.when(j == 0)
    def _():
        for cp in weight_copies(expert):
            cp.start()
        for s in range(GATHER_DEPTH - 1):
            issue(s, s)

    @pl.when(jnp.logical_and(active, fresh))
    def _():
        for cp in weight_copies(expert):
            cp.wait()
        w1b_ref[...] = w1s_ref[...].astype(BF16)
        w2b_ref[...] = w2s_ref[...].astype(BF16)

        @pl.when(nxt_ref[expert] < N_EXPERTS)
        def _():
            for cp in weight_copies(nxt_ref[expert]):
                cp.start()

    for s in range(GATHER_DEPTH):
        @pl.when(jnp.logical_and(j < n_act + GATHER_DEPTH - 1, phase == s))
        def _(s=s):
            wait(s)

        @pl.when(jnp.logical_and(active, phase == s))
        def _(s=s):
            issue(j + GATHER_DEPTH - 1, (s + GATHER_DEPTH - 1) % GATHER_DEPTH)
            hb = jnp.dot(bufs[s][...].astype(BF16), w1b_ref[...], preferred_element_type=F32) + b1_ref[0]
            glu = jnp.minimum(hb[:, :D_FF], SWIGLU_LIMIT)
            lin = jnp.clip(hb[:, D_FF:], -SWIGLU_LIMIT, SWIGLU_LIMIT)
            act = glu * jax.nn.sigmoid(SWIGLU_ALPHA * glu) * (lin + 1.0)
            y_ref[...] = jnp.dot(act.astype(BF16), w2b_ref[...], preferred_element_type=F32) + b2_ref[0]


def _experts(block_e, n_act, next_e, row_tok, h2, w1, b1, w2, b2):
    n_rows = row_tok.shape[0]
    d = h2.shape[1]
    nb = block_e.shape[0]
    row_map = lambda j, be, na, nx, rt: (jnp.maximum(jnp.minimum(j, na[0] - 1), 0), 0)
    exp_map = lambda j, be, na, nx, rt: (be[j], 0, 0)
    return pl.pallas_call(
        _expert_kernel,
        grid_spec=pltpu.PrefetchScalarGridSpec(
            num_scalar_prefetch=4,
            grid=(nb,),
            in_specs=[pl.BlockSpec(memory_space=pl.ANY),
                      pl.BlockSpec(memory_space=pl.ANY),
                      pl.BlockSpec((1, 1, 2 * D_FF), exp_map),
                      pl.BlockSpec(memory_space=pl.ANY),
                      pl.BlockSpec((1, 1, d), exp_map)],
            out_specs=pl.BlockSpec((MOE_ROWS, d), row_map),
            scratch_shapes=[pltpu.VMEM((MOE_ROWS, d), F32)] * GATHER_DEPTH + [
                pltpu.VMEM((d, 2 * D_FF), F32), pltpu.VMEM((D_FF, d), F32),
                pltpu.VMEM((d, 2 * D_FF), BF16), pltpu.VMEM((D_FF, d), BF16),
                pltpu.SemaphoreType.DMA((GATHER_DEPTH,)), pltpu.SemaphoreType.DMA((2,))],
        ),
        out_shape=jax.ShapeDtypeStruct((n_rows, d), F32),
        compiler_params=_params(("arbitrary",)),
        name="expert_ffn",
    )(block_e, n_act, next_e, row_tok, h2, w1, b1.reshape(N_EXPERTS, 1, 2 * D_FF), w2,
      b2.reshape(N_EXPERTS, 1, d))


def _combine_kernel(dest_ref, x1_ref, gate_ref, g2_ref, fw_ref, y_ref, o_ref, buf_ref, sem):
    i = pl.program_id(0)
    n = pl.num_programs(0)
    gt = x1_ref.shape[0]

    def row_copy(slot, k, j, src_row):
        return pltpu.make_async_copy(y_ref.at[pl.ds(src_row, 1), :],
                                     buf_ref.at[slot, k, pl.ds(j, 1), :], sem.at[slot])

    def issue(tile, slot):
        base = tile * (gt * TOP_K)
        for j in range(gt):
            for k in range(TOP_K):
                row_copy(slot, k, j, dest_ref[base + j * TOP_K + k]).start(priority=k % 2)

    @pl.when(i == 0)
    def _():
        issue(0, 0)

    for s in range(2):
        @pl.when(jnp.logical_and(i + 1 < n, (i + 1) % 2 == s))
        def _(s=s):
            issue(i + 1, s)

    slot = i % 2

    for k in range(TOP_K):
        pltpu.make_async_copy(y_ref.at[pl.ds(0, gt), :], buf_ref.at[slot, k], sem.at[slot]).wait()

    gates = gate_ref[...]
    moe = gates[:, 0:1] * buf_ref[slot, 0]
    for k in range(1, TOP_K):
        moe = moe + gates[:, k:k + 1] * buf_ref[slot, k]
    x2 = x1_ref[...] + g2_ref[0] * moe
    ms = jnp.mean(x2 * x2, axis=-1, keepdims=True)
    o_ref[...] = x2 * lax.rsqrt(ms + EPS) * fw_ref[...]


def _combine(dest_flat, x1, gate_pad, g2, fw, yout, seqlen):
    t, d = x1.shape
    gt = min(GATHER_TILE, seqlen)
    per_seq = seqlen // gt
    return pl.pallas_call(
        _combine_kernel,
        grid_spec=pltpu.PrefetchScalarGridSpec(
            num_scalar_prefetch=1,
            grid=(t // gt,),
            in_specs=[pl.BlockSpec((gt, d), lambda i, dst: (i, 0)),
                      pl.BlockSpec((gt, LANES), lambda i, dst: (i, 0)),
                      pl.BlockSpec((1, 1, d), lambda i, dst: (i // per_seq, 0, 0)),
                      pl.BlockSpec((1, d), lambda i, dst: (0, 0)),
                      pl.BlockSpec(memory_space=pl.ANY)],
            out_specs=pl.BlockSpec((gt, d), lambda i, dst: (i, 0)),
            scratch_shapes=[pltpu.VMEM((2, TOP_K, gt, d), F32), pltpu.SemaphoreType.DMA((2,))],
        ),
        out_shape=jax.ShapeDtypeStruct((t, d), F32),
        compiler_params=_params(("arbitrary",)),
        name="combine_norm",
    )(dest_flat, x1, gate_pad, g2, fw, yout)


def _head_expand():
    head_of_lane = jnp.arange(SSM_WIDTH) // SSM_HEAD_DIM
    one = (jnp.arange(LANES)[:, None] == head_of_lane[None, :]).astype(BF16)
    return jnp.tile(one, (3, 1))


def _pad_lanes(v, fill=0.0):
    v = v.reshape(1, -1).astype(F32)
    return jnp.pad(v, ((0, 0), (0, LANES - v.shape[1])), constant_values=fill)


def kernel(x, c, ada_w, ada_b, norm1_w, w_in, hg_lb_logits, hg_norm_w, ssm_a_log, ssm_dt_bias, ssm_d,
           ssm_conv_w, ssm_conv_b, ssm_norm_w, w_out, norm2_w, router_w, router_b, exp_w1, exp_b1,
           exp_w2, exp_b2, final_norm_w):
    bsz, seqlen, d = x.shape
    t = bsz * seqlen
    l = 0

    c_pad = jnp.pad(c, ((0, SUBLANES - bsz), (0, 0)))
    mod = _ada_mod(c_pad, ada_w[l], ada_b[l])[:bsz]
    sh1, sc1, g1, sh2, sc2, g2 = [m.reshape(bsz, 1, d) for m in jnp.split(mod, N_MOD, axis=-1)]

    w_in_t = jnp.swapaxes(w_in[l], 0, 1)
    w_main = w_in_t[:_OFF[6]].astype(BF16)
    w_dt = jnp.pad(w_in_t[_OFF[6]:].astype(BF16), ((0, LANES - SSM_HEADS), (0, 0)))
    qs, kk, lf, vi, gg, zz, xbc, dt = _inproj(
        x, norm1_w[l].reshape(1, d), sh1, sc1, w_main, w_dt, hg_lb_logits, _pad_lanes(ssm_dt_bias[l]))

    o_hg = _hgrn(qs, kk, lf, vi, gg, hg_norm_w[l].reshape(1, HG_DV))
    y_ssd = _ssd(xbc, zz, dt, ssm_conv_w[l], ssm_conv_b[l].reshape(1, SSM_CONV_CH),
                 _pad_lanes(ssm_a_log[l]), jnp.repeat(ssm_d[l], SSM_HEAD_DIM).reshape(1, SSM_WIDTH),
                 ssm_norm_w[l].reshape(1, SSM_WIDTH), _head_expand())

    rw_pad = jnp.pad(router_w[l], ((0, 0), (0, LANES - N_EXPERTS)))
    rw_hi = rw_pad.astype(BF16)
    rw_pad = jnp.concatenate([rw_hi, (rw_pad - rw_hi.astype(F32)).astype(BF16)], axis=1)
    rb_pad = _pad_lanes(router_b[l], NEG_BIG)
    x1, h2, idx_pad, gate_pad, counts = _outproj(
        x, o_hg, y_ssd, w_out[l].astype(BF16), g1, norm2_w[l].reshape(1, d), sh2, sc2, rw_pad, rb_pad)

    cnt = counts[0, :N_EXPERTS].astype(jnp.int32)
    blocks_e = (cnt + MOE_ROWS - 1) // MOE_ROWS
    blk_end = jnp.cumsum(blocks_e)
    pstart = (blk_end - blocks_e) * MOE_ROWS
    n_blocks = (t * TOP_K) // MOE_ROWS + N_EXPERTS
    n_rows = n_blocks * MOE_ROWS
    n_steps = n_blocks + GATHER_DEPTH - 1
    block_e = jnp.minimum(jnp.sum(blk_end[None, :] <= jnp.arange(n_steps)[:, None], axis=1),
                          N_EXPERTS - 1).astype(jnp.int32)
    n_act = blk_end[-1:].astype(jnp.int32)
    owner = jnp.where(blocks_e > 0, jnp.arange(N_EXPERTS), N_EXPERTS)
    next_e = jnp.concatenate([lax.cummin(owner, reverse=True)[1:],
                              jnp.full((1,), N_EXPERTS)]).astype(jnp.int32)

    dest_flat = _route(idx_pad.reshape(t, LANES), _pad_lanes(pstart)).reshape(t * TOP_K)

    row_tok = _invert(dest_flat, n_rows)
    yout = _experts(block_e, n_act, next_e, row_tok, h2.reshape(t, d), exp_w1[l], exp_b1[l], exp_w2[l], exp_b2[l])
    out = _combine(dest_flat, x1.reshape(t, d), gate_pad.reshape(t, LANES), g2,
                   final_norm_w.reshape(1, d), yout, seqlen)
    return out.reshape(bsz, seqlen, d)
```

```python
import functools

import jax
import jax.numpy as jnp
from jax import lax
from jax.experimental import pallas as pl
from jax.experimental.pallas import tpu as pltpu

F32 = jnp.float32
BF16 = jnp.bfloat16
HIGHEST = lax.Precision.HIGHEST

EPS = 1e-6
D_MODEL = 1024
HG_HEADS = 4
HG_DK = 128
HG_DV = 128
HG_QF = HG_HEADS * HG_DK
HG_WIDTH = HG_HEADS * HG_DV
HG_CHUNK = 64
SSM_HEADS = 8
SSM_HEAD_DIM = 64
SSM_WIDTH = SSM_HEADS * SSM_HEAD_DIM
SSM_GROUPS = 2
SSM_GROUP_HEADS = SSM_HEADS // SSM_GROUPS
SSM_GROUP_WIDTH = SSM_WIDTH // SSM_GROUPS
SSM_STATE = 128
SSM_CONV = 4
SSM_CONV_CH = SSM_WIDTH + 2 * SSM_GROUPS * SSM_STATE
SSM_CHUNK = 128
IN_SPLITS = (HG_QF, HG_QF, HG_WIDTH, HG_WIDTH, SSM_WIDTH, SSM_CONV_CH, SSM_HEADS)
IN_COLS = sum(IN_SPLITS)
N_EXPERTS = 32
TOP_K = 4
D_FF = 1024
SWIGLU_LIMIT = 7.0
SWIGLU_ALPHA = 1.702
N_MOD = 6

LANES = 128
SUBLANES = 8
VMEM_LIMIT = 56 * 1024 * 1024

ROW_TILE = 1024
ROUTE_TILE = 512
SEQ_TILE = 1024
MOE_ROWS = 256
GATHER_TILE = 256
INVERT_TILE = 1024
GATHER_DEPTH = 3
NEG_BIG = -1e30


def _silu(v):
    return v * jax.nn.sigmoid(v)


def _softplus(v):
    return jnp.maximum(v, 0.0) + jnp.log1p(jnp.exp(-jnp.abs(v)))


def _split3(v):
    hi = v.astype(BF16)
    r1 = v - hi.astype(F32)
    mid = r1.astype(BF16)
    lo = (r1 - mid.astype(F32)).astype(BF16)
    return jnp.concatenate([hi, mid, lo], axis=-1)


def _params(sem):
    return pltpu.CompilerParams(dimension_semantics=sem, vmem_limit_bytes=VMEM_LIMIT)


def _ada_kernel(c_ref, w_ref, b_ref, o_ref):
    o_ref[...] = jnp.dot(_silu(c_ref[...]), w_ref[...], precision=HIGHEST,
                         preferred_element_type=F32) + b_ref[...]


def _ada_mod(c_pad, ada_w, ada_b):
    n = ada_w.shape[1]
    tn = D_MODEL
    return pl.pallas_call(
        _ada_kernel,
        grid=(n // tn,),
        in_specs=[pl.BlockSpec((SUBLANES, D_MODEL), lambda j: (0, 0)),
                  pl.BlockSpec((D_MODEL, tn), lambda j: (0, j)),
                  pl.BlockSpec((1, tn), lambda j: (0, j))],
        out_specs=pl.BlockSpec((SUBLANES, tn), lambda j: (0, j)),
        out_shape=jax.ShapeDtypeStruct((SUBLANES, n), F32),
        compiler_params=_params(("arbitrary",)),
        name="ada_mod",
    )(c_pad, ada_w, ada_b.reshape(1, n))


_OFF = [0]
for _w in IN_SPLITS:
    _OFF.append(_OFF[-1] + _w)


def _inproj_kernel(x_ref, nw_ref, sh_ref, sc_ref, w_ref, wdt_ref, lbl_ref, dtb_ref,
                   qs_ref, kk_ref, lf_ref, vi_ref, gg_ref, zz_ref, xbc_ref, dt_ref):
    x = x_ref[0]
    ms = jnp.mean(x * x, axis=-1, keepdims=True)
    h = x * lax.rsqrt(ms + EPS) * nw_ref[...]
    h = h * (1.0 + sc_ref[0]) + sh_ref[0]
    hb = h.astype(BF16)

    def seg(k):
        cols = w_ref[_OFF[k]:_OFF[k + 1], :] if k < 6 else wdt_ref[...]
        return lax.dot_general(hb, cols, (((1,), (1,)), ((), ())), preferred_element_type=F32)

    lbl = lbl_ref[...]
    le = jnp.exp(lbl - jnp.max(lbl, axis=0, keepdims=True))
    lb = le[0:1, :] / jnp.sum(le, axis=0, keepdims=True)

    qs_ref[0] = _silu(seg(0)).astype(BF16)
    fg = lb + (1.0 - lb) * jax.nn.sigmoid(seg(1))
    kk_ref[0] = (1.0 - fg).astype(BF16)
    lf_ref[0] = jnp.log(fg)
    vi_ref[0] = seg(2).astype(BF16)
    gg_ref[0] = _silu(seg(3)).astype(BF16)
    zz_ref[0] = _silu(seg(4)).astype(BF16)
    xbc_ref[0] = seg(5).astype(BF16)
    dt_ref[0] = _softplus(seg(6) + dtb_ref[...])


def _inproj(x, nw, sh, sc, w_main, w_dt, lb_logits, dtb_pad):
    bsz, seqlen, d = x.shape
    tm = min(ROW_TILE, seqlen)
    nt = seqlen // tm
    tok = lambda w: pl.BlockSpec((1, tm, w), lambda b, i: (b, i, 0))
    full = lambda a: pl.BlockSpec(a.shape, lambda b, i: (0,) * a.ndim)
    mod = pl.BlockSpec((1, 1, d), lambda b, i: (b, 0, 0))
    shp = lambda w, dt: jax.ShapeDtypeStruct((bsz, seqlen, w), dt)
    return pl.pallas_call(
        _inproj_kernel,
        grid=(bsz, nt),
        in_specs=[tok(d), full(nw), mod, mod, full(w_main), full(w_dt), full(lb_logits), full(dtb_pad)],
        out_specs=[tok(HG_QF), tok(HG_QF), tok(HG_QF), tok(HG_WIDTH), tok(HG_WIDTH),
                   tok(SSM_WIDTH), tok(SSM_CONV_CH), tok(LANES)],
        out_shape=[shp(HG_QF, BF16), shp(HG_QF, BF16), shp(HG_QF, F32), shp(HG_WIDTH, BF16),
                   shp(HG_WIDTH, BF16), shp(SSM_WIDTH, BF16), shp(SSM_CONV_CH, BF16),
                   shp(LANES, F32)],
        compiler_params=_params(("arbitrary", "arbitrary")),
        name="inproj",
    )(x, nw, sh, sc, w_main, w_dt, lb_logits, dtb_pad)


def _hgrn_kernel(qs_ref, kk_ref, lf_ref, vi_ref, gg_ref, nw_ref, o_ref, st_ref):
    @pl.when(pl.program_id(1) == 0)
    def _():
        st_ref[...] = jnp.zeros_like(st_ref)

    c = HG_CHUNK
    tq = qs_ref.shape[1]
    nc = tq // c
    rows = lax.broadcasted_iota(jnp.int32, (nc, c, c), 1)
    cols = lax.broadcasted_iota(jnp.int32, (nc, c, c), 2)
    causal = rows >= cols
    tril = causal.astype(BF16)
    bmm = lambda a, bb, dims: lax.dot_general(a, bb, (dims, ((0,), (0,))), preferred_element_type=F32)
    for h in range(HG_HEADS):
        hs = slice(h * HG_DK, (h + 1) * HG_DK)
        chunked = lambda ref: ref[0, :, hs].reshape(nc, c, HG_DK)
        q = chunked(qs_ref).astype(F32)
        k = chunked(kk_ref).astype(F32)
        v = chunked(vi_ref)
        b3 = bmm(tril, _split3(chunked(lf_ref)), ((2,), (1,)))
        b = b3[:, :, :HG_DK] + b3[:, :, HG_DK:2 * HG_DK] + b3[:, :, 2 * HG_DK:]
        b_mid = b[:, c // 2 - 1:c // 2, :]
        b_last = b[:, c - 1:c, :]
        qa = (q * jnp.exp(b - b_mid)).astype(BF16)
        ka = (k * jnp.exp(b_mid - b)).astype(BF16)
        att = bmm(qa, ka, ((2,), (2,)))
        att = jnp.where(causal, att, 0.0).astype(BF16)
        o = bmm(att, v, ((2,), (1,)))
        kd = (k * jnp.exp(b_last - b)).astype(BF16)
        d_st = bmm(v, kd, ((1,), (1,)))
        decay = jnp.exp(b_last)
        st = st_ref[h]
        entering = []
        for ci in range(nc):
            entering.append(st.astype(BF16))
            st = st * decay[ci] + d_st[ci]
        st_ref[h] = st
        qb = (q * jnp.exp(b)).astype(BF16)
        o = o + bmm(qb, jnp.stack(entering), ((2,), (2,)))
        ms = jnp.mean(o * o, axis=-1, keepdims=True)
        y = o * lax.rsqrt(ms + EPS) * nw_ref[...] * chunked(gg_ref).astype(F32)
        o_ref[0, :, hs] = y.reshape(tq, HG_DV).astype(BF16)


def _hgrn(qs, kk, lf, vi, gg, nw):
    bsz, seqlen, width = qs.shape
    tq = min(SEQ_TILE, seqlen)
    blk = pl.BlockSpec((1, tq, width), lambda b, i: (b, i, 0))
    return pl.pallas_call(
        _hgrn_kernel,
        grid=(bsz, seqlen // tq),
        in_specs=[blk, blk, blk, blk, blk, pl.BlockSpec((1, HG_DV), lambda b, i: (0, 0))],
        out_specs=blk,
        out_shape=jax.ShapeDtypeStruct((bsz, seqlen, HG_WIDTH), BF16),
        scratch_shapes=[pltpu.VMEM((HG_HEADS, HG_DV, HG_DK), F32)],
        compiler_params=_params(("arbitrary", "arbitrary")),
        name="hgrn2",
    )(qs, kk, lf, vi, gg, nw)


def _ssd_kernel(xbc_ref, zz_ref, dt_ref, cw_ref, cb_ref, alog_ref, dsk_ref, nw_ref, ex_ref,
                y_ref, buf_ref, xc_ref, yc_ref, st_ref):
    tq = xbc_ref.shape[1]
    halo = SUBLANES

    @pl.when(pl.program_id(1) == 0)
    def _():
        buf_ref[0:halo, :] = jnp.zeros((halo, SSM_CONV_CH), F32)
        st_ref[...] = jnp.zeros_like(st_ref)

    buf_ref[halo:halo + tq, :] = xbc_ref[0].astype(F32)
    conv = cb_ref[...] + buf_ref[halo - 3:halo - 3 + tq, :] * cw_ref[0:1, :]
    for w in range(1, SSM_CONV):
        conv = conv + buf_ref[halo - 3 + w:halo - 3 + w + tq, :] * cw_ref[w:w + 1, :]
    buf_ref[0:halo, :] = buf_ref[tq:tq + halo, :]
    xc_ref[...] = _silu(conv)

    c = SSM_CHUNK
    p = SSM_HEAD_DIM
    nc = tq // c
    rows = lax.broadcasted_iota(jnp.int32, (nc, c, c), 1)
    cols = lax.broadcasted_iota(jnp.int32, (nc, c, c), 2)
    causal = rows >= cols
    tril = causal.astype(BF16)
    bmm = lambda a, bb, dims, **kw: lax.dot_general(a, bb, (dims, ((0,), (0,))),
                                                    preferred_element_type=F32, **kw)
    chunked = lambda val: val.reshape(nc, c, val.shape[-1])
    dt = chunked(dt_ref[0])
    da = dt * (-jnp.exp(alog_ref[...]))
    da3 = _split3(da)
    a3 = bmm(tril, da3, ((2,), (1,)))
    acum = a3[:, :, :LANES] + a3[:, :, LANES:2 * LANES] + a3[:, :, 2 * LANES:]
    at3 = bmm(da3, tril, ((1,), (2,)))
    acum_t = at3[:, :LANES, :] + at3[:, LANES:2 * LANES, :] + at3[:, 2 * LANES:, :]
    widen = lambda v: jnp.dot(_split3(v.reshape(tq, LANES)), ex_ref[...], preferred_element_type=F32)
    dt_w = widen(dt)
    decay_out_w = widen(jnp.exp(acum))
    decay_in_w = widen(jnp.exp(acum[:, c - 1:c, :] - acum))
    xdt_w = xc_ref[:, 0:SSM_WIDTH] * dt_w
    xdd_w = xdt_w * decay_in_w
    y_inter_groups = []
    for g in range(SSM_GROUPS):
        gs = slice(g * SSM_GROUP_WIDTH, (g + 1) * SSM_GROUP_WIDTH)
        bm = chunked(xc_ref[:, SSM_WIDTH + g * SSM_STATE:SSM_WIDTH + (g + 1) * SSM_STATE]).astype(BF16)
        cm = chunked(xc_ref[:, SSM_WIDTH + (SSM_GROUPS + g) * SSM_STATE:
                            SSM_WIDTH + (SSM_GROUPS + g + 1) * SSM_STATE]).astype(BF16)
        cb = bmm(cm, bm, ((2,), (2,)))
        d_st = bmm(chunked(xdd_w[:, gs]).astype(BF16), bm, ((1,), (1,)))
        a_last = []
        for hh in range(SSM_GROUP_HEADS):
            h = g * SSM_GROUP_HEADS + hh
            hs = slice(h * p, (h + 1) * p)
            seg = jnp.exp(jnp.where(causal, acum[:, :, h:h + 1] - acum_t[:, h:h + 1, :], -jnp.inf))
            y = bmm((cb * seg).astype(BF16), chunked(xdt_w[:, hs]).astype(BF16), ((2,), (1,)))
            yc_ref[:, hs] = y.reshape(tq, p)
            a_last.append(jnp.exp(acum_t[:, h:h + 1, c - 1:c]))
        st = [st_ref[g, hh * p:(hh + 1) * p, :] for hh in range(SSM_GROUP_HEADS)]
        entering = []
        for ci in range(nc):
            entering.append(jnp.concatenate(st, axis=0).astype(BF16))
            st = [st[hh] * a_last[hh][ci] + d_st[ci, hh * p:(hh + 1) * p, :] for hh in range(SSM_GROUP_HEADS)]
        for hh in range(SSM_GROUP_HEADS):
            st_ref[g, hh * p:(hh + 1) * p, :] = st[hh]
        y_inter_groups.append(bmm(cm, jnp.stack(entering), ((2,), (2,))).reshape(tq, SSM_GROUP_WIDTH))
    y_inter = jnp.concatenate(y_inter_groups, axis=-1)
    y = yc_ref[...] + decay_out_w * y_inter
    y = (y + dsk_ref[...] * xc_ref[:, 0:SSM_WIDTH]) * zz_ref[0].astype(F32)
    for g in range(SSM_GROUPS):
        gs = slice(g * SSM_GROUP_WIDTH, (g + 1) * SSM_GROUP_WIDTH)
        yg = y[:, gs]
        ms = jnp.mean(yg * yg, axis=-1, keepdims=True)
        y_ref[0, :, gs] = (yg * lax.rsqrt(ms + EPS) * nw_ref[:, gs]).astype(BF16)


def _ssd(xbc, zz, dt, conv_w, conv_b, alog_pad, dsk_wide, nw, expand3):
    bsz, seqlen, _ = xbc.shape
    tq = min(SEQ_TILE, seqlen)
    tok = lambda w: pl.BlockSpec((1, tq, w), lambda b, i: (b, i, 0))
    full = lambda a: pl.BlockSpec(a.shape, lambda b, i: (0,) * a.ndim)
    return pl.pallas_call(
        _ssd_kernel,
        grid=(bsz, seqlen // tq),
        in_specs=[tok(SSM_CONV_CH), tok(SSM_WIDTH), tok(LANES), full(conv_w), full(conv_b),
                  full(alog_pad), full(dsk_wide), full(nw), full(expand3)],
        out_specs=tok(SSM_WIDTH),
        out_shape=jax.ShapeDtypeStruct((bsz, seqlen, SSM_WIDTH), BF16),
        scratch_shapes=[pltpu.VMEM((tq + 2 * SUBLANES, SSM_CONV_CH), F32),
                        pltpu.VMEM((tq, SSM_CONV_CH), F32),
                        pltpu.VMEM((tq, SSM_WIDTH), F32),
                        pltpu.VMEM((SSM_GROUPS, SSM_GROUP_WIDTH, SSM_STATE), F32)],
        compiler_params=_params(("arbitrary", "arbitrary")),
        name="ssd",
    )(xbc, zz, dt, conv_w, conv_b, alog_pad, dsk_wide, nw, expand3)


def _outproj_kernel(x_ref, oh_ref, ys_ref, wo_ref, g1_ref, nw_ref, sh_ref, sc_ref, rw_ref, rb_ref,
                    x1_ref, h2_ref, idx_ref, gate_ref, cnt_ref):
    first = jnp.logical_and(pl.program_id(0) == 0, pl.program_id(1) == 0)

    @pl.when(first)
    def _():
        cnt_ref[...] = jnp.zeros_like(cnt_ref)

    mix = jnp.dot(oh_ref[0], wo_ref[0:HG_WIDTH, :], preferred_element_type=F32)
    mix = mix + jnp.dot(ys_ref[0], wo_ref[HG_WIDTH:, :], preferred_element_type=F32)
    x1 = x_ref[0] + g1_ref[0] * mix
    x1_ref[0] = x1
    ms = jnp.mean(x1 * x1, axis=-1, keepdims=True)
    h2 = x1 * lax.rsqrt(ms + EPS) * nw_ref[...]
    h2 = h2 * (1.0 + sc_ref[0]) + sh_ref[0]
    h2_ref[0] = h2
    h_hi = h2.astype(BF16)
    h_lo = (h2 - h_hi.astype(F32)).astype(BF16)
    part = jnp.dot(h_hi, rw_ref[...], preferred_element_type=F32)
    logits = (part[:, :LANES] + part[:, LANES:]
              + jnp.dot(h_lo, rw_ref[:, :LANES], preferred_element_type=F32)) + rb_ref[...]
    lane = lax.broadcasted_iota(jnp.int32, logits.shape, 1).astype(F32)
    idx_out = jnp.zeros(logits.shape, F32)
    val_out = jnp.zeros(logits.shape, F32)
    sel = jnp.zeros(logits.shape, F32)
    work = logits
    top = None
    denom = None
    for k in range(TOP_K):
        m = jnp.max(work, axis=-1, keepdims=True)
        am = jnp.min(jnp.where(work == m, lane, float(LANES)), axis=-1, keepdims=True)
        hit = lane == am
        work = jnp.where(hit, -jnp.inf, work)
        sel = jnp.where(hit, 1.0, sel)
        if k == 0:
            top = m
        e = jnp.exp(m - top)
        denom = e if k == 0 else denom + e
        idx_out = jnp.where(lane == float(k), am, idx_out)
        val_out = jnp.where(lane == float(k), e, val_out)
    idx_ref[0] = idx_out.astype(jnp.int32)
    gate_ref[0] = val_out / denom
    cnt_ref[0:1, :] += jnp.sum(sel, axis=0, keepdims=True)


def _outproj(x, o_hg, y_ssd, w_out_b, g1, nw, sh, sc, rw_pad, rb_pad):
    bsz, seqlen, d = x.shape
    tm = min(ROW_TILE, seqlen)
    tok = lambda w: pl.BlockSpec((1, tm, w), lambda b, i: (b, i, 0))
    full = lambda a: pl.BlockSpec(a.shape, lambda b, i: (0,) * a.ndim)
    mod = pl.BlockSpec((1, 1, d), lambda b, i: (b, 0, 0))
    shp = lambda w, dt: jax.ShapeDtypeStruct((bsz, seqlen, w), dt)
    return pl.pallas_call(
        _outproj_kernel,
        grid=(bsz, seqlen // tm),
        in_specs=[tok(d), tok(HG_WIDTH), tok(SSM_WIDTH), full(w_out_b), mod, full(nw), mod, mod,
                  full(rw_pad), full(rb_pad)],
        out_specs=[tok(d), tok(d), tok(LANES), tok(LANES),
                   pl.BlockSpec((SUBLANES, LANES), lambda b, i: (0, 0))],
        out_shape=[shp(d, F32), shp(d, F32), shp(LANES, jnp.int32), shp(LANES, F32),
                   jax.ShapeDtypeStruct((SUBLANES, LANES), F32)],
        compiler_params=_params(("arbitrary", "arbitrary")),
        name="outproj_router",
    )(x, o_hg, y_ssd, w_out_b, g1, nw, sh, sc, rw_pad, rb_pad)


def _route_kernel(idx_ref, pstart_ref, dest_ref, carry_ref):
    @pl.when(pl.program_id(0) == 0)
    def _():
        carry_ref[...] = jnp.zeros_like(carry_ref)

    idx = idx_ref[...]
    tt = idx.shape[0]
    lane = lax.broadcasted_iota(jnp.int32, idx.shape, 1)
    hits = [lane == idx[:, k:k + 1] for k in range(TOP_K)]
    sel = jnp.zeros(idx.shape, F32)
    for hit in hits:
        sel = jnp.where(hit, 1.0, sel)
    rows = lax.broadcasted_iota(jnp.int32, (tt, tt), 0)
    cols = lax.broadcasted_iota(jnp.int32, (tt, tt), 1)
    before = (rows > cols).astype(BF16)
    rank = jnp.dot(before, sel.astype(BF16), preferred_element_type=F32) + carry_ref[0:1, :]
    carry_ref[0:1, :] += jnp.sum(sel, axis=0, keepdims=True)
    dense = pstart_ref[...] + rank
    pack = LANES // TOP_K
    slot = (lax.broadcasted_iota(jnp.int32, idx.shape, 0) % pack) * TOP_K
    spread = jnp.zeros(idx.shape, F32)
    for k, hit in enumerate(hits):
        dk = jnp.sum(jnp.where(hit, dense, 0.0), axis=-1, keepdims=True)
        spread = jnp.where(lane == slot + k, dk, spread)
    out_rows = tt // pack
    merge = (lax.broadcasted_iota(jnp.int32, (out_rows, tt), 0)
             == lax.broadcasted_iota(jnp.int32, (out_rows, tt), 1) // pack).astype(BF16)
    m3 = jnp.dot(merge, _split3(spread), preferred_element_type=F32)
    dest_ref[...] = (m3[:, :LANES] + m3[:, LANES:2 * LANES] + m3[:, 2 * LANES:]).astype(jnp.int32)


def _route(idx_pad, pstart_row):
    t = idx_pad.shape[0]
    tt = min(ROUTE_TILE, t)
    return pl.pallas_call(
        _route_kernel,
        grid=(t // tt,),
        in_specs=[pl.BlockSpec((tt, LANES), lambda i: (i, 0)),
                  pl.BlockSpec((1, LANES), lambda i: (0, 0))],
        out_specs=pl.BlockSpec((tt * TOP_K // LANES, LANES), lambda i: (i, 0)),
        out_shape=jax.ShapeDtypeStruct((t * TOP_K // LANES, LANES), jnp.int32),
        scratch_shapes=[pltpu.VMEM((SUBLANES, LANES), F32)],
        compiler_params=_params(("arbitrary",)),
        name="route_rank",
    )(idx_pad, pstart_row)


def _invert_kernel(dest_ref, zeros_ref, rowtok_ref, sem):
    i = pl.program_id(0)
    per_step = dest_ref.shape[0] // TOP_K // pl.num_programs(0)

    @pl.when(i == 0)
    def _():
        fill = pltpu.make_async_copy(zeros_ref, rowtok_ref, sem)
        fill.start()
        fill.wait()

    def tok_body(n, carry):
        tok = i * per_step + n
        for k in range(TOP_K):
            rowtok_ref[dest_ref[tok * TOP_K + k]] = tok
        return carry
    lax.fori_loop(0, per_step, tok_body, 0, unroll=8)


def _invert(dest_flat, n_rows):
    t = dest_flat.shape[0] // TOP_K
    return pl.pallas_call(
        _invert_kernel,
        grid=(max(t // INVERT_TILE, 1),),
        in_specs=[pl.BlockSpec(memory_space=pltpu.SMEM), pl.BlockSpec(memory_space=pl.ANY)],
        out_specs=pl.BlockSpec(memory_space=pltpu.SMEM),
        out_shape=jax.ShapeDtypeStruct((n_rows,), jnp.int32),
        scratch_shapes=[pltpu.SemaphoreType.DMA(())],
        compiler_params=_params(("arbitrary",)),
        name="invert_route",
    )(dest_flat, jnp.zeros((n_rows,), jnp.int32))


def _expert_kernel(be_ref, nact_ref, nxt_ref, rowtok_ref, h_ref, w1_ref, b1_ref, w2_ref, b2_ref, y_ref,
                   xbuf0_ref, xbuf1_ref, xbuf2_ref, w1s_ref, w2s_ref, w1b_ref, w2b_ref, sem, wsem):
    j = pl.program_id(0)
    n_act = nact_ref[0]
    active = j < n_act
    expert = be_ref[j]
    fresh = jnp.logical_or(j == 0, expert != be_ref[jnp.maximum(j - 1, 0)])
    phase = j % GATHER_DEPTH
    bufs = (xbuf0_ref, xbuf1_ref, xbuf2_ref)

    def issue(block, s):
        base = jnp.minimum(block, n_act - 1) * MOE_ROWS
        for r in range(MOE_ROWS):
            pltpu.make_async_copy(h_ref.at[pl.ds(rowtok_ref[base + r], 1), :],
                                  bufs[s].at[pl.ds(r, 1), :], sem.at[s]).start(priority=1)

    def wait(s):
        pltpu.make_async_copy(h_ref.at[pl.ds(0, MOE_ROWS), :], bufs[s], sem.at[s]).wait()

    def weight_copies(e):
        return (pltpu.make_async_copy(w1_ref.at[e], w1s_ref, wsem.at[0]),
                pltpu.make_async_copy(w2_ref.at[e], w2s_ref, wsem.at[1]))

    @pl.when(j == 0)
    def _():
        for cp in weight_copies(expert):
            cp.start()
        for s in range(GATHER_DEPTH - 1):
            issue(s, s)

    @pl.when(jnp.logical_and(active, fresh))
    def _():
        for cp in weight_copies(expert):
            cp.wait()
        w1b_ref[...] = w1s_ref[...].astype(BF16)
        w2b_ref[...] = w2s_ref[...].astype(BF16)

        @pl.when(nxt_ref[expert] < N_EXPERTS)
        def _():
            for cp in weight_copies(nxt_ref[expert]):
                cp.start()

    for s in range(GATHER_DEPTH):
        @pl.when(jnp.logical_and(j < n_act + GATHER_DEPTH - 1, phase == s))
        def _(s=s):
            wait(s)

        @pl.when(jnp.logical_and(active, phase == s))
        def _(s=s):
            issue(j + GATHER_DEPTH - 1, (s + GATHER_DEPTH - 1) % GATHER_DEPTH)
            hb = jnp.dot(bufs[s][...].astype(BF16), w1b_ref[...], preferred_element_type=F32) + b1_ref[0]
            glu = jnp.minimum(hb[:, :D_FF], SWIGLU_LIMIT)
            lin = jnp.clip(hb[:, D_FF:], -SWIGLU_LIMIT, SWIGLU_LIMIT)
            act = glu * jax.nn.sigmoid(SWIGLU_ALPHA * glu) * (lin + 1.0)
            y_ref[...] = jnp.dot(act.astype(BF16), w2b_ref[...], preferred_element_type=F32) + b2_ref[0]


def _experts(block_e, n_act, next_e, row_tok, h2, w1, b1, w2, b2):
    n_rows = row_tok.shape[0]
    d = h2.shape[1]
    nb = block_e.shape[0]
    row_map = lambda j, be, na, nx, rt: (jnp.maximum(jnp.minimum(j, na[0] - 1), 0), 0)
    exp_map = lambda j, be, na, nx, rt: (be[j], 0, 0)
    return pl.pallas_call(
        _expert_kernel,
        grid_spec=pltpu.PrefetchScalarGridSpec(
            num_scalar_prefetch=4,
            grid=(nb,),
            in_specs=[pl.BlockSpec(memory_space=pl.ANY),
                      pl.BlockSpec(memory_space=pl.ANY),
                      pl.BlockSpec((1, 1, 2 * D_FF), exp_map),
                      pl.BlockSpec(memory_space=pl.ANY),
                      pl.BlockSpec((1, 1, d), exp_map)],
            out_specs=pl.BlockSpec((MOE_ROWS, d), row_map),
            scratch_shapes=[pltpu.VMEM((MOE_ROWS, d), F32)] * GATHER_DEPTH + [
                pltpu.VMEM((d, 2 * D_FF), F32), pltpu.VMEM((D_FF, d), F32),
                pltpu.VMEM((d, 2 * D_FF), BF16), pltpu.VMEM((D_FF, d), BF16),
                pltpu.SemaphoreType.DMA((GATHER_DEPTH,)), pltpu.SemaphoreType.DMA((2,))],
        ),
        out_shape=jax.ShapeDtypeStruct((n_rows, d), F32),
        compiler_params=_params(("arbitrary",)),
        name="expert_ffn",
    )(block_e, n_act, next_e, row_tok, h2, w1, b1.reshape(N_EXPERTS, 1, 2 * D_FF), w2,
      b2.reshape(N_EXPERTS, 1, d))


def _combine_kernel(dest_ref, x1_ref, gate_ref, g2_ref, fw_ref, y_ref, o_ref, buf_ref, sem):
    i = pl.program_id(0)
    n = pl.num_programs(0)
    gt = x1_ref.shape[0]

    def row_copy(slot, k, j, src_row):
        return pltpu.make_async_copy(y_ref.at[pl.ds(src_row, 1), :],
                                     buf_ref.at[slot, k, pl.ds(j, 1), :], sem.at[slot])

    def issue(tile, slot):
        base = tile * (gt * TOP_K)
        for j in range(gt):
            for k in range(TOP_K):
                row_copy(slot, k, j, dest_ref[base + j * TOP_K + k]).start(priority=k % 2)

    @pl.when(i == 0)
    def _():
        issue(0, 0)

    for s in range(2):
        @pl.when(jnp.logical_and(i + 1 < n, (i + 1) % 2 == s))
        def _(s=s):
            issue(i + 1, s)

    slot = i % 2

    for k in range(TOP_K):
        pltpu.make_async_copy(y_ref.at[pl.ds(0, gt), :], buf_ref.at[slot, k], sem.at[slot]).wait()

    gates = gate_ref[...]
    moe = gates[:, 0:1] * buf_ref[slot, 0]
    for k in range(1, TOP_K):
        moe = moe + gates[:, k:k + 1] * buf_ref[slot, k]
    x2 = x1_ref[...] + g2_ref[0] * moe
    ms = jnp.mean(x2 * x2, axis=-1, keepdims=True)
    o_ref[...] = x2 * lax.rsqrt(ms + EPS) * fw_ref[...]


def _combine(dest_flat, x1, gate_pad, g2, fw, yout, seqlen):
    t, d = x1.shape
    gt = min(GATHER_TILE, seqlen)
    per_seq = seqlen // gt
    return pl.pallas_call(
        _combine_kernel,
        grid_spec=pltpu.PrefetchScalarGridSpec(
            num_scalar_prefetch=1,
            grid=(t // gt,),
            in_specs=[pl.BlockSpec((gt, d), lambda i, dst: (i, 0)),
                      pl.BlockSpec((gt, LANES), lambda i, dst: (i, 0)),
                      pl.BlockSpec((1, 1, d), lambda i, dst: (i // per_seq, 0, 0)),
                      pl.BlockSpec((1, d), lambda i, dst: (0, 0)),
                      pl.BlockSpec(memory_space=pl.ANY)],
            out_specs=pl.BlockSpec((gt, d), lambda i, dst: (i, 0)),
            scratch_shapes=[pltpu.VMEM((2, TOP_K, gt, d), F32), pltpu.SemaphoreType.DMA((2,))],
        ),
        out_shape=jax.ShapeDtypeStruct((t, d), F32),
        compiler_params=_params(("arbitrary",)),
        name="combine_norm",
    )(dest_flat, x1, gate_pad, g2, fw, yout)


def _head_expand():
    head_of_lane = jnp.arange(SSM_WIDTH) // SSM_HEAD_DIM
    one = (jnp.arange(LANES)[:, None] == head_of_lane[None, :]).astype(BF16)
    return jnp.tile(one, (3, 1))


def _pad_lanes(v, fill=0.0):
    v = v.reshape(1, -1).astype(F32)
    return jnp.pad(v, ((0, 0), (0, LANES - v.shape[1])), constant_values=fill)


def kernel(x, c, ada_w, ada_b, norm1_w, w_in, hg_lb_logits, hg_norm_w, ssm_a_log, ssm_dt_bias, ssm_d,
           ssm_conv_w, ssm_conv_b, ssm_norm_w, w_out, norm2_w, router_w, router_b, exp_w1, exp_b1,
           exp_w2, exp_b2, final_norm_w):
    bsz, seqlen, d = x.shape
    t = bsz * seqlen
    l = 0

    c_pad = jnp.pad(c, ((0, SUBLANES - bsz), (0, 0)))
    mod = _ada_mod(c_pad, ada_w[l], ada_b[l])[:bsz]
    sh1, sc1, g1, sh2, sc2, g2 = [m.reshape(bsz, 1, d) for m in jnp.split(mod, N_MOD, axis=-1)]

    w_in_t = jnp.swapaxes(w_in[l], 0, 1)
    w_main = w_in_t[:_OFF[6]].astype(BF16)
    w_dt = jnp.pad(w_in_t[_OFF[6]:].astype(BF16), ((0, LANES - SSM_HEADS), (0, 0)))
    qs, kk, lf, vi, gg, zz, xbc, dt = _inproj(
        x, norm1_w[l].reshape(1, d), sh1, sc1, w_main, w_dt, hg_lb_logits, _pad_lanes(ssm_dt_bias[l]))

    o_hg = _hgrn(qs, kk, lf, vi, gg, hg_norm_w[l].reshape(1, HG_DV))
    y_ssd = _ssd(xbc, zz, dt, ssm_conv_w[l], ssm_conv_b[l].reshape(1, SSM_CONV_CH),
                 _pad_lanes(ssm_a_log[l]), jnp.repeat(ssm_d[l], SSM_HEAD_DIM).reshape(1, SSM_WIDTH),
                 ssm_norm_w[l].reshape(1, SSM_WIDTH), _head_expand())

    rw_pad = jnp.pad(router_w[l], ((0, 0), (0, LANES - N_EXPERTS)))
    rw_hi = rw_pad.astype(BF16)
    rw_pad = jnp.concatenate([rw_hi, (rw_pad - rw_hi.astype(F32)).astype(BF16)], axis=1)
    rb_pad = _pad_lanes(router_b[l], NEG_BIG)
    x1, h2, idx_pad, gate_pad, counts = _outproj(
        x, o_hg, y_ssd, w_out[l].astype(BF16), g1, norm2_w[l].reshape(1, d), sh2, sc2, rw_pad, rb_pad)

    cnt = counts[0, :N_EXPERTS].astype(jnp.int32)
    blocks_e = (cnt + MOE_ROWS - 1) // MOE_ROWS
    blk_end = jnp.cumsum(blocks_e)
    pstart = (blk_end - blocks_e) * MOE_ROWS
    n_blocks = (t * TOP_K) // MOE_ROWS + N_EXPERTS
    n_rows = n_blocks * MOE_ROWS
    n_steps = n_blocks + GATHER_DEPTH - 1
    block_e = jnp.minimum(jnp.sum(blk_end[None, :] <= jnp.arange(n_steps)[:, None], axis=1),
                          N_EXPERTS - 1).astype(jnp.int32)
    n_act = blk_end[-1:].astype(jnp.int32)
    owner = jnp.where(blocks_e > 0, jnp.arange(N_EXPERTS), N_EXPERTS)
    next_e = jnp.concatenate([lax.cummin(owner, reverse=True)[1:],
                              jnp.full((1,), N_EXPERTS)]).astype(jnp.int32)

    dest_flat = _route(idx_pad.reshape(t, LANES), _pad_lanes(pstart)).reshape(t * TOP_K)

    row_tok = _invert(dest_flat, n_rows)
    yout = _experts(block_e, n_act, next_e, row_tok, h2.reshape(t, d), exp_w1[l], exp_b1[l], exp_w2[l], exp_b2[l])
    out = _combine(dest_flat, x1.reshape(t, d), gate_pad.reshape(t, LANES), g2,
                   final_norm_w.reshape(1, d), yout, seqlen)
    return out.reshape(bsz, seqlen, d)
```

```python
import functools

import jax
import jax.numpy as jnp
from jax import lax
from jax.experimental import pallas as pl
from jax.experimental.pallas import tpu as pltpu

F32 = jnp.float32
BF16 = jnp.bfloat16
HIGHEST = lax.Precision.HIGHEST

EPS = 1e-6
D_MODEL = 1024
HG_HEADS = 4
HG_DK = 128
HG_DV = 128
HG_QF = HG_HEADS * HG_DK
HG_WIDTH = HG_HEADS * HG_DV
HG_CHUNK = 64
SSM_HEADS = 8
SSM_HEAD_DIM = 64
SSM_WIDTH = SSM_HEADS * SSM_HEAD_DIM
SSM_GROUPS = 2
SSM_GROUP_HEADS = SSM_HEADS // SSM_GROUPS
SSM_GROUP_WIDTH = SSM_WIDTH // SSM_GROUPS
SSM_STATE = 128
SSM_CONV = 4
SSM_CONV_CH = SSM_WIDTH + 2 * SSM_GROUPS * SSM_STATE
SSM_CHUNK = 128
IN_SPLITS = (HG_QF, HG_QF, HG_WIDTH, HG_WIDTH, SSM_WIDTH, SSM_CONV_CH, SSM_HEADS)
IN_COLS = sum(IN_SPLITS)
N_EXPERTS = 32
TOP_K = 4
D_FF = 1024
SWIGLU_LIMIT = 7.0
SWIGLU_ALPHA = 1.702
N_MOD = 6

LANES = 128
SUBLANES = 8
VMEM_LIMIT = 56 * 1024 * 1024

ROW_TILE = 1024
ROUTE_TILE = 512
SEQ_TILE = 1024
MOE_ROWS = 256
GATHER_TILE = 256
INVERT_TILE = 1024
GATHER_DEPTH = 3
NEG_BIG = -1e30


def _silu(v):
    return v * jax.nn.sigmoid(v)


def _softplus(v):
    return jnp.maximum(v, 0.0) + jnp.log1p(jnp.exp(-jnp.abs(v)))


def _split3(v):
    hi = v.astype(BF16)
    r1 = v - hi.astype(F32)
    mid = r1.astype(BF16)
    lo = (r1 - mid.astype(F32)).astype(BF16)
    return jnp.concatenate([hi, mid, lo], axis=-1)


def _params(sem):
    return pltpu.CompilerParams(dimension_semantics=sem, vmem_limit_bytes=VMEM_LIMIT)


def _ada_kernel(c_ref, w_ref, b_ref, o_ref):
    o_ref[...] = jnp.dot(_silu(c_ref[...]), w_ref[...], precision=HIGHEST,
                         preferred_element_type=F32) + b_ref[...]


def _ada_mod(c_pad, ada_w, ada_b):
    n = ada_w.shape[1]
    tn = D_MODEL
    return pl.pallas_call(
        _ada_kernel,
        grid=(n // tn,),
        in_specs=[pl.BlockSpec((SUBLANES, D_MODEL), lambda j: (0, 0)),
                  pl.BlockSpec((D_MODEL, tn), lambda j: (0, j)),
                  pl.BlockSpec((1, tn), lambda j: (0, j))],
        out_specs=pl.BlockSpec((SUBLANES, tn), lambda j: (0, j)),
        out_shape=jax.ShapeDtypeStruct((SUBLANES, n), F32),
        compiler_params=_params(("arbitrary",)),
        name="ada_mod",
    )(c_pad, ada_w, ada_b.reshape(1, n))


_OFF = [0]
for _w in IN_SPLITS:
    _OFF.append(_OFF[-1] + _w)


def _inproj_kernel(x_ref, nw_ref, sh_ref, sc_ref, w_ref, wdt_ref, lbl_ref, dtb_ref,
                   qs_ref, kk_ref, lf_ref, vi_ref, gg_ref, zz_ref, xbc_ref, dt_ref):
    x = x_ref[0]
    ms = jnp.mean(x * x, axis=-1, keepdims=True)
    h = x * lax.rsqrt(ms + EPS) * nw_ref[...]
    h = h * (1.0 + sc_ref[0]) + sh_ref[0]
    hb = h.astype(BF16)

    def seg(k):
        cols = w_ref[_OFF[k]:_OFF[k + 1], :] if k < 6 else wdt_ref[...]
        return lax.dot_general(hb, cols, (((1,), (1,)), ((), ())), preferred_element_type=F32)

    lbl = lbl_ref[...]
    le = jnp.exp(lbl - jnp.max(lbl, axis=0, keepdims=True))
    lb = le[0:1, :] / jnp.sum(le, axis=0, keepdims=True)

    qs_ref[0] = _silu(seg(0)).astype(BF16)
    fg = lb + (1.0 - lb) * jax.nn.sigmoid(seg(1))
    kk_ref[0] = (1.0 - fg).astype(BF16)
    lf_ref[0] = jnp.log(fg)
    vi_ref[0] = seg(2).astype(BF16)
    gg_ref[0] = _silu(seg(3)).astype(BF16)
    zz_ref[0] = _silu(seg(4)).astype(BF16)
    xbc_ref[0] = seg(5).astype(BF16)
    dt_ref[0] = _softplus(seg(6) + dtb_ref[...])


def _inproj(x, nw, sh, sc, w_main, w_dt, lb_logits, dtb_pad):
    bsz, seqlen, d = x.shape
    tm = min(ROW_TILE, seqlen)
    nt = seqlen // tm
    tok = lambda w: pl.BlockSpec((1, tm, w), lambda b, i: (b, i, 0))
    full = lambda a: pl.BlockSpec(a.shape, lambda b, i: (0,) * a.ndim)
    mod = pl.BlockSpec((1, 1, d), lambda b, i: (b, 0, 0))
    shp = lambda w, dt: jax.ShapeDtypeStruct((bsz, seqlen, w), dt)
    return pl.pallas_call(
        _inproj_kernel,
        grid=(bsz, nt),
        in_specs=[tok(d), full(nw), mod, mod, full(w_main), full(w_dt), full(lb_logits), full(dtb_pad)],
        out_specs=[tok(HG_QF), tok(HG_QF), tok(HG_QF), tok(HG_WIDTH), tok(HG_WIDTH),
                   tok(SSM_WIDTH), tok(SSM_CONV_CH), tok(LANES)],
        out_shape=[shp(HG_QF, BF16), shp(HG_QF, BF16), shp(HG_QF, F32), shp(HG_WIDTH, BF16),
                   shp(HG_WIDTH, BF16), shp(SSM_WIDTH, BF16), shp(SSM_CONV_CH, BF16),
                   shp(LANES, F32)],
        compiler_params=_params(("arbitrary", "arbitrary")),
        name="inproj",
    )(x, nw, sh, sc, w_main, w_dt, lb_logits, dtb_pad)


def _hgrn_kernel(qs_ref, kk_ref, lf_ref, vi_ref, gg_ref, nw_ref, o_ref, st_ref):
    @pl.when(pl.program_id(1) == 0)
    def _():
        st_ref[...] = jnp.zeros_like(st_ref)

    c = HG_CHUNK
    tq = qs_ref.shape[1]
    nc = tq // c
    rows = lax.broadcasted_iota(jnp.int32, (nc, c, c), 1)
    cols = lax.broadcasted_iota(jnp.int32, (nc, c, c), 2)
    causal = rows >= cols
    tril = causal.astype(BF16)
    bmm = lambda a, bb, dims: lax.dot_general(a, bb, (dims, ((0,), (0,))), preferred_element_type=F32)
    for h in range(HG_HEADS):
        hs = slice(h * HG_DK, (h + 1) * HG_DK)
        chunked = lambda ref: ref[0, :, hs].reshape(nc, c, HG_DK)
        q = chunked(qs_ref).astype(F32)
        k = chunked(kk_ref).astype(F32)
        v = chunked(vi_ref)
        b3 = bmm(tril, _split3(chunked(lf_ref)), ((2,), (1,)))
        b = b3[:, :, :HG_DK] + b3[:, :, HG_DK:2 * HG_DK] + b3[:, :, 2 * HG_DK:]
        b_mid = b[:, c // 2 - 1:c // 2, :]
        b_last = b[:, c - 1:c, :]
        qa = (q * jnp.exp(b - b_mid)).astype(BF16)
        ka = (k * jnp.exp(b_mid - b)).astype(BF16)
        att = bmm(qa, ka, ((2,), (2,)))
        att = jnp.where(causal, att, 0.0).astype(BF16)
        o = bmm(att, v, ((2,), (1,)))
        kd = (k * jnp.exp(b_last - b)).astype(BF16)
        d_st = bmm(v, kd, ((1,), (1,)))
        decay = jnp.exp(b_last)
        st = st_ref[h]
        entering = []
        for ci in range(nc):
            entering.append(st.astype(BF16))
            st = st * decay[ci] + d_st[ci]
        st_ref[h] = st
        qb = (q * jnp.exp(b)).astype(BF16)
        o = o + bmm(qb, jnp.stack(entering), ((2,), (2,)))
        ms = jnp.mean(o * o, axis=-1, keepdims=True)
        y = o * lax.rsqrt(ms + EPS) * nw_ref[...] * chunked(gg_ref).astype(F32)
        o_ref[0, :, hs] = y.reshape(tq, HG_DV).astype(BF16)


def _hgrn(qs, kk, lf, vi, gg, nw):
    bsz, seqlen, width = qs.shape
    tq = min(SEQ_TILE, seqlen)
    blk = pl.BlockSpec((1, tq, width), lambda b, i: (b, i, 0))
    return pl.pallas_call(
        _hgrn_kernel,
        grid=(bsz, seqlen // tq),
        in_specs=[blk, blk, blk, blk, blk, pl.BlockSpec((1, HG_DV), lambda b, i: (0, 0))],
        out_specs=blk,
        out_shape=jax.ShapeDtypeStruct((bsz, seqlen, HG_WIDTH), BF16),
        scratch_shapes=[pltpu.VMEM((HG_HEADS, HG_DV, HG_DK), F32)],
        compiler_params=_params(("arbitrary", "arbitrary")),
        name="hgrn2",
    )(qs, kk, lf, vi, gg, nw)


def _ssd_kernel(xbc_ref, zz_ref, dt_ref, cw_ref, cb_ref, alog_ref, dsk_ref, nw_ref, ex_ref,
                y_ref, buf_ref, xc_ref, yc_ref, st_ref):
    tq = xbc_ref.shape[1]
    halo = SUBLANES

    @pl.when(pl.program_id(1) == 0)
    def _():
        buf_ref[0:halo, :] = jnp.zeros((halo, SSM_CONV_CH), F32)
        st_ref[...] = jnp.zeros_like(st_ref)

    buf_ref[halo:halo + tq, :] = xbc_ref[0].astype(F32)
    conv = cb_ref[...] + buf_ref[halo - 3:halo - 3 + tq, :] * cw_ref[0:1, :]
    for w in range(1, SSM_CONV):
        conv = conv + buf_ref[halo - 3 + w:halo - 3 + w + tq, :] * cw_ref[w:w + 1, :]
    buf_ref[0:halo, :] = buf_ref[tq:tq + halo, :]
    xc_ref[...] = _silu(conv)

    c = SSM_CHUNK
    p = SSM_HEAD_DIM
    nc = tq // c
    rows = lax.broadcasted_iota(jnp.int32, (nc, c, c), 1)
    cols = lax.broadcasted_iota(jnp.int32, (nc, c, c), 2)
    causal = rows >= cols
    tril = causal.astype(BF16)
    bmm = lambda a, bb, dims, **kw: lax.dot_general(a, bb, (dims, ((0,), (0,))),
                                                    preferred_element_type=F32, **kw)
    chunked = lambda val: val.reshape(nc, c, val.shape[-1])
    dt = chunked(dt_ref[0])
    da = dt * (-jnp.exp(alog_ref[...]))
    da3 = _split3(da)
    a3 = bmm(tril, da3, ((2,), (1,)))
    acum = a3[:, :, :LANES] + a3[:, :, LANES:2 * LANES] + a3[:, :, 2 * LANES:]
    at3 = bmm(da3, tril, ((1,), (2,)))
    acum_t = at3[:, :LANES, :] + at3[:, LANES:2 * LANES, :] + at3[:, 2 * LANES:, :]
    widen = lambda v: jnp.dot(_split3(v.reshape(tq, LANES)), ex_ref[...], preferred_element_type=F32)
    dt_w = widen(dt)
    decay_out_w = widen(jnp.exp(acum))
    decay_in_w = widen(jnp.exp(acum[:, c - 1:c, :] - acum))
    xdt_w = xc_ref[:, 0:SSM_WIDTH] * dt_w
    xdd_w = xdt_w * decay_in_w
    y_inter_groups = []
    for g in range(SSM_GROUPS):
        gs = slice(g * SSM_GROUP_WIDTH, (g + 1) * SSM_GROUP_WIDTH)
        bm = chunked(xc_ref[:, SSM_WIDTH + g * SSM_STATE:SSM_WIDTH + (g + 1) * SSM_STATE]).astype(BF16)
        cm = chunked(xc_ref[:, SSM_WIDTH + (SSM_GROUPS + g) * SSM_STATE:
                            SSM_WIDTH + (SSM_GROUPS + g + 1) * SSM_STATE]).astype(BF16)
        cb = bmm(cm, bm, ((2,), (2,)))
        d_st = bmm(chunked(xdd_w[:, gs]).astype(BF16), bm, ((1,), (1,)))
        a_last = []
        for hh in range(SSM_GROUP_HEADS):
            h = g * SSM_GROUP_HEADS + hh
            hs = slice(h * p, (h + 1) * p)
            seg = jnp.exp(jnp.where(causal, acum[:, :, h:h + 1] - acum_t[:, h:h + 1, :], -jnp.inf))
            y = bmm((cb * seg).astype(BF16), chunked(xdt_w[:, hs]).astype(BF16), ((2,), (1,)))
            yc_ref[:, hs] = y.reshape(tq, p)
            a_last.append(jnp.exp(acum_t[:, h:h + 1, c - 1:c]))
        st = [st_ref[g, hh * p:(hh + 1) * p, :] for hh in range(SSM_GROUP_HEADS)]
        entering = []
        for ci in range(nc):
            entering.append(jnp.concatenate(st, axis=0).astype(BF16))
            st = [st[hh] * a_last[hh][ci] + d_st[ci, hh * p:(hh + 1) * p, :] for hh in range(SSM_GROUP_HEADS)]
        for hh in range(SSM_GROUP_HEADS):
            st_ref[g, hh * p:(hh + 1) * p, :] = st[hh]
        y_inter_groups.append(bmm(cm, jnp.stack(entering), ((2,), (2,))).reshape(tq, SSM_GROUP_WIDTH))
    y_inter = jnp.concatenate(y_inter_groups, axis=-1)
    y = yc_ref[...] + decay_out_w * y_inter
    y = (y + dsk_ref[...] * xc_ref[:, 0:SSM_WIDTH]) * zz_ref[0].astype(F32)
    for g in range(SSM_GROUPS):
        gs = slice(g * SSM_GROUP_WIDTH, (g + 1) * SSM_GROUP_WIDTH)
        yg = y[:, gs]
        ms = jnp.mean(yg * yg, axis=-1, keepdims=True)
        y_ref[0, :, gs] = (yg * lax.rsqrt(ms + EPS) * nw_ref[:, gs]).astype(BF16)


def _ssd(xbc, zz, dt, conv_w, conv_b, alog_pad, dsk_wide, nw, expand3):
    bsz, seqlen, _ = xbc.shape
    tq = min(SEQ_TILE, seqlen)
    tok = lambda w: pl.BlockSpec((1, tq, w), lambda b, i: (b, i, 0))
    full = lambda a: pl.BlockSpec(a.shape, lambda b, i: (0,) * a.ndim)
    return pl.pallas_call(
        _ssd_kernel,
        grid=(bsz, seqlen // tq),
        in_specs=[tok(SSM_CONV_CH), tok(SSM_WIDTH), tok(LANES), full(conv_w), full(conv_b),
                  full(alog_pad), full(dsk_wide), full(nw), full(expand3)],
        out_specs=tok(SSM_WIDTH),
        out_shape=jax.ShapeDtypeStruct((bsz, seqlen, SSM_WIDTH), BF16),
        scratch_shapes=[pltpu.VMEM((tq + 2 * SUBLANES, SSM_CONV_CH), F32),
                        pltpu.VMEM((tq, SSM_CONV_CH), F32),
                        pltpu.VMEM((tq, SSM_WIDTH), F32),
                        pltpu.VMEM((SSM_GROUPS, SSM_GROUP_WIDTH, SSM_STATE), F32)],
        compiler_params=_params(("arbitrary", "arbitrary")),
        name="ssd",
    )(xbc, zz, dt, conv_w, conv_b, alog_pad, dsk_wide, nw, expand3)


def _outproj_kernel(x_ref, oh_ref, ys_ref, wo_ref, g1_ref, nw_ref, sh_ref, sc_ref, rw_ref, rb_ref,
                    x1_ref, h2_ref, idx_ref, gate_ref, cnt_ref):
    first = jnp.logical_and(pl.program_id(0) == 0, pl.program_id(1) == 0)

    @pl.when(first)
    def _():
        cnt_ref[...] = jnp.zeros_like(cnt_ref)

    mix = jnp.dot(oh_ref[0], wo_ref[0:HG_WIDTH, :], preferred_element_type=F32)
    mix = mix + jnp.dot(ys_ref[0], wo_ref[HG_WIDTH:, :], preferred_element_type=F32)
    x1 = x_ref[0] + g1_ref[0] * mix
    x1_ref[0] = x1
    ms = jnp.mean(x1 * x1, axis=-1, keepdims=True)
    h2 = x1 * lax.rsqrt(ms + EPS) * nw_ref[...]
    h2 = h2 * (1.0 + sc_ref[0]) + sh_ref[0]
    h2_ref[0] = h2
    h_hi = h2.astype(BF16)
    h_lo = (h2 - h_hi.astype(F32)).astype(BF16)
    part = jnp.dot(h_hi, rw_ref[...], preferred_element_type=F32)
    logits = (part[:, :LANES] + part[:, LANES:]
              + jnp.dot(h_lo, rw_ref[:, :LANES], preferred_element_type=F32)) + rb_ref[...]
    lane = lax.broadcasted_iota(jnp.int32, logits.shape, 1).astype(F32)
    idx_out = jnp.zeros(logits.shape, F32)
    val_out = jnp.zeros(logits.shape, F32)
    sel = jnp.zeros(logits.shape, F32)
    work = logits
    top = None
    denom = None
    for k in range(TOP_K):
        m = jnp.max(work, axis=-1, keepdims=True)
        am = jnp.min(jnp.where(work == m, lane, float(LANES)), axis=-1, keepdims=True)
        hit = lane == am
        work = jnp.where(hit, -jnp.inf, work)
        sel = jnp.where(hit, 1.0, sel)
        if k == 0:
            top = m
        e = jnp.exp(m - top)
        denom = e if k == 0 else denom + e
        idx_out = jnp.where(lane == float(k), am, idx_out)
        val_out = jnp.where(lane == float(k), e, val_out)
    idx_ref[0] = idx_out.astype(jnp.int32)
    gate_ref[0] = val_out / denom
    cnt_ref[0:1, :] += jnp.sum(sel, axis=0, keepdims=True)


def _outproj(x, o_hg, y_ssd, w_out_b, g1, nw, sh, sc, rw_pad, rb_pad):
    bsz, seqlen, d = x.shape
    tm = min(ROW_TILE, seqlen)
    tok = lambda w: pl.BlockSpec((1, tm, w), lambda b, i: (b, i, 0))
    full = lambda a: pl.BlockSpec(a.shape, lambda b, i: (0,) * a.ndim)
    mod = pl.BlockSpec((1, 1, d), lambda b, i: (b, 0, 0))
    shp = lambda w, dt: jax.ShapeDtypeStruct((bsz, seqlen, w), dt)
    return pl.pallas_call(
        _outproj_kernel,
        grid=(bsz, seqlen // tm),
        in_specs=[tok(d), tok(HG_WIDTH), tok(SSM_WIDTH), full(w_out_b), mod, full(nw), mod, mod,
                  full(rw_pad), full(rb_pad)],
        out_specs=[tok(d), tok(d), tok(LANES), tok(LANES),
                   pl.BlockSpec((SUBLANES, LANES), lambda b, i: (0, 0))],
        out_shape=[shp(d, F32), shp(d, F32), shp(LANES, jnp.int32), shp(LANES, F32),
                   jax.ShapeDtypeStruct((SUBLANES, LANES), F32)],
        compiler_params=_params(("arbitrary", "arbitrary")),
        name="outproj_router",
    )(x, o_hg, y_ssd, w_out_b, g1, nw, sh, sc, rw_pad, rb_pad)


def _route_kernel(idx_ref, pstart_ref, dest_ref, carry_ref):
    @pl.when(pl.program_id(0) == 0)
    def _():
        carry_ref[...] = jnp.zeros_like(carry_ref)

    idx = idx_ref[...]
    tt = idx.shape[0]
    lane = lax.broadcasted_iota(jnp.int32, idx.shape, 1)
    hits = [lane == idx[:, k:k + 1] for k in range(TOP_K)]
    sel = jnp.zeros(idx.shape, F32)
    for hit in hits:
        sel = jnp.where(hit, 1.0, sel)
    rows = lax.broadcasted_iota(jnp.int32, (tt, tt), 0)
    cols = lax.broadcasted_iota(jnp.int32, (tt, tt), 1)
    before = (rows > cols).astype(BF16)
    rank = jnp.dot(before, sel.astype(BF16), preferred_element_type=F32) + carry_ref[0:1, :]
    carry_ref[0:1, :] += jnp.sum(sel, axis=0, keepdims=True)
    dense = pstart_ref[...] + rank
    pack = LANES // TOP_K
    slot = (lax.broadcasted_iota(jnp.int32, idx.shape, 0) % pack) * TOP_K
    spread = jnp.zeros(idx.shape, F32)
    for k, hit in enumerate(hits):
        dk = jnp.sum(jnp.where(hit, dense, 0.0), axis=-1, keepdims=True)
        spread = jnp.where(lane == slot + k, dk, spread)
    out_rows = tt // pack
    merge = (lax.broadcasted_iota(jnp.int32, (out_rows, tt), 0)
             == lax.broadcasted_iota(jnp.int32, (out_rows, tt), 1) // pack).astype(BF16)
    m3 = jnp.dot(merge, _split3(spread), preferred_element_type=F32)
    dest_ref[...] = (m3[:, :LANES] + m3[:, LANES:2 * LANES] + m3[:, 2 * LANES:]).astype(jnp.int32)


def _route(idx_pad, pstart_row):
    t = idx_pad.shape[0]
    tt = min(ROUTE_TILE, t)
    return pl.pallas_call(
        _route_kernel,
        grid=(t // tt,),
        in_specs=[pl.BlockSpec((tt, LANES), lambda i: (i, 0)),
                  pl.BlockSpec((1, LANES), lambda i: (0, 0))],
        out_specs=pl.BlockSpec((tt * TOP_K // LANES, LANES), lambda i: (i, 0)),
        out_shape=jax.ShapeDtypeStruct((t * TOP_K // LANES, LANES), jnp.int32),
        scratch_shapes=[pltpu.VMEM((SUBLANES, LANES), F32)],
        compiler_params=_params(("arbitrary",)),
        name="route_rank",
    )(idx_pad, pstart_row)


def _invert_kernel(dest_ref, zeros_ref, rowtok_ref, sem):
    i = pl.program_id(0)
    per_step = dest_ref.shape[0] // TOP_K // pl.num_programs(0)

    @pl.when(i == 0)
    def _():
        fill = pltpu.make_async_copy(zeros_ref, rowtok_ref, sem)
        fill.start()
        fill.wait()

    def tok_body(n, carry):
        tok = i * per_step + n
        for k in range(TOP_K):
            rowtok_ref[dest_ref[tok * TOP_K + k]] = tok
        return carry
    lax.fori_loop(0, per_step, tok_body, 0, unroll=8)


def _invert(dest_flat, n_rows):
    t = dest_flat.shape[0] // TOP_K
    return pl.pallas_call(
        _invert_kernel,
        grid=(max(t // INVERT_TILE, 1),),
        in_specs=[pl.BlockSpec(memory_space=pltpu.SMEM), pl.BlockSpec(memory_space=pl.ANY)],
        out_specs=pl.BlockSpec(memory_space=pltpu.SMEM),
        out_shape=jax.ShapeDtypeStruct((n_rows,), jnp.int32),
        scratch_shapes=[pltpu.SemaphoreType.DMA(())],
        compiler_params=_params(("arbitrary",)),
        name="invert_route",
    )(dest_flat, jnp.zeros((n_rows,), jnp.int32))


def _expert_kernel(be_ref, nact_ref, nxt_ref, rowtok_ref, h_ref, w1_ref, b1_ref, w2_ref, b2_ref, y_ref,
                   xbuf0_ref, xbuf1_ref, xbuf2_ref, w1s_ref, w2s_ref, w1b_ref, w2b_ref, sem, wsem):
    j = pl.program_id(0)
    n_act = nact_ref[0]
    active = j < n_act
    expert = be_ref[j]
    fresh = jnp.logical_or(j == 0, expert != be_ref[jnp.maximum(j - 1, 0)])
    phase = j % GATHER_DEPTH
    bufs = (xbuf0_ref, xbuf1_ref, xbuf2_ref)

    def issue(block, s):
        base = jnp.minimum(block, n_act - 1) * MOE_ROWS
        for r in range(MOE_ROWS):
            pltpu.make_async_copy(h_ref.at[pl.ds(rowtok_ref[base + r], 1), :],
                                  bufs[s].at[pl.ds(r, 1), :], sem.at[s]).start(priority=1)

    def wait(s):
        pltpu.make_async_copy(h_ref.at[pl.ds(0, MOE_ROWS), :], bufs[s], sem.at[s]).wait()

    def weight_copies(e):
        return (pltpu.make_async_copy(w1_ref.at[e], w1s_ref, wsem.at[0]),
                pltpu.make_async_copy(w2_ref.at[e], w2s_ref, wsem.at[1]))

    @pl.when(j == 0)
    def _():
        for cp in weight_copies(expert):
            cp.start()
        for s in range(GATHER_DEPTH - 1):
            issue(s, s)

    @pl.when(jnp.logical_and(active, fresh))
    def _():
        for cp in weight_copies(expert):
            cp.wait()
        w1b_ref[...] = w1s_ref[...].astype(BF16)
        w2b_ref[...] = w2s_ref[...].astype(BF16)

        @pl.when(nxt_ref[expert] < N_EXPERTS)
        def _():
            for cp in weight_copies(nxt_ref[expert]):
                cp.start()

    for s in range(GATHER_DEPTH):
        @pl.when(jnp.logical_and(j < n_act + GATHER_DEPTH - 1, phase == s))
        def _(s=s):
            wait(s)

        @pl.when(jnp.logical_and(active, phase == s))
        def _(s=s):
            issue(j + GATHER_DEPTH - 1, (s + GATHER_DEPTH - 1) % GATHER_DEPTH)
            hb = jnp.dot(bufs[s][...].astype(BF16), w1b_ref[...], preferred_element_type=F32) + b1_ref[0]
            glu = jnp.minimum(hb[:, :D_FF], SWIGLU_LIMIT)
            lin = jnp.clip(hb[:, D_FF:], -SWIGLU_LIMIT, SWIGLU_LIMIT)
            act = glu * jax.nn.sigmoid(SWIGLU_ALPHA * glu) * (lin + 1.0)
            y_ref[...] = jnp.dot(act.astype(BF16), w2b_ref[...], preferred_element_type=F32) + b2_ref[0]


def _experts(block_e, n_act, next_e, row_tok, h2, w1, b1, w2, b2):
    n_rows = row_tok.shape[0]
    d = h2.shape[1]
    nb = block_e.shape[0]
    row_map = lambda j, be, na, nx, rt: (jnp.maximum(jnp.minimum(j, na[0] - 1), 0), 0)
    exp_map = lambda j, be, na, nx, rt: (be[j], 0, 0)
    return pl.pallas_call(
        _expert_kernel,
        grid_spec=pltpu.PrefetchScalarGridSpec(
            num_scalar_prefetch=4,
            grid=(nb,),
            in_specs=[pl.BlockSpec(memory_space=pl.ANY),
                      pl.BlockSpec(memory_space=pl.ANY),
                      pl.BlockSpec((1, 1, 2 * D_FF), exp_map),
                      pl.BlockSpec(memory_space=pl.ANY),
                      pl.BlockSpec((1, 1, d), exp_map)],
            out_specs=pl.BlockSpec((MOE_ROWS, d), row_map),
            scratch_shapes=[pltpu.VMEM((MOE_ROWS, d), F32)] * GATHER_DEPTH + [
                pltpu.VMEM((d, 2 * D_FF), F32), pltpu.VMEM((D_FF, d), F32),
                pltpu.VMEM((d, 2 * D_FF), BF16), pltpu.VMEM((D_FF, d), BF16),
                pltpu.SemaphoreType.DMA((GATHER_DEPTH,)), pltpu.SemaphoreType.DMA((2,))],
        ),
        out_shape=jax.ShapeDtypeStruct((n_rows, d), F32),
        compiler_params=_params(("arbitrary",)),
        name="expert_ffn",
    )(block_e, n_act, next_e, row_tok, h2, w1, b1.reshape(N_EXPERTS, 1, 2 * D_FF), w2,
      b2.reshape(N_EXPERTS, 1, d))


def _combine_kernel(dest_ref, x1_ref, gate_ref, g2_ref, fw_ref, y_ref, o_ref, buf0_ref, buf1_ref, sem):
    i = pl.program_id(0)
    n = pl.num_programs(0)
    gt = x1_ref.shape[0]
    bufs = (buf0_ref, buf1_ref)

    def issue(tile, s):
        base = tile * (gt * TOP_K)
        for j in range(gt):
            for k in range(TOP_K):
                pltpu.make_async_copy(y_ref.at[pl.ds(dest_ref[base + j * TOP_K + k], 1), :],
                                      bufs[s].at[k, pl.ds(j, 1), :], sem.at[s]).start(priority=k % 2)

    def wait(s):
        for k in range(TOP_K):
            pltpu.make_async_copy(y_ref.at[pl.ds(0, gt), :], bufs[s].at[k], sem.at[s]).wait()

    @pl.when(i == 0)
    def _():
        issue(0, 0)

    for s in range(2):
        @pl.when(i % 2 == s)
        def _(s=s):
            wait(s)
            issue(jnp.minimum(i + 1, n - 1), 1 - s)
            gates = gate_ref[...]
            moe = gates[:, 0:1] * bufs[s][0]
            for k in range(1, TOP_K):
                moe = moe + gates[:, k:k + 1] * bufs[s][k]
            x2 = x1_ref[...] + g2_ref[0] * moe
            ms = jnp.mean(x2 * x2, axis=-1, keepdims=True)
            o_ref[...] = x2 * lax.rsqrt(ms + EPS) * fw_ref[...]

        @pl.when(jnp.logical_and(i == n - 1, i % 2 == s))
        def _(s=s):
            wait(1 - s)


def _combine(dest_flat, x1, gate_pad, g2, fw, yout, seqlen):
    t, d = x1.shape
    gt = min(GATHER_TILE, seqlen)
    per_seq = seqlen // gt
    return pl.pallas_call(
        _combine_kernel,
        grid_spec=pltpu.PrefetchScalarGridSpec(
            num_scalar_prefetch=1,
            grid=(t // gt,),
            in_specs=[pl.BlockSpec((gt, d), lambda i, dst: (i, 0)),
                      pl.BlockSpec((gt, LANES), lambda i, dst: (i, 0)),
                      pl.BlockSpec((1, 1, d), lambda i, dst: (i // per_seq, 0, 0)),
                      pl.BlockSpec((1, d), lambda i, dst: (0, 0)),
                      pl.BlockSpec(memory_space=pl.ANY)],
            out_specs=pl.BlockSpec((gt, d), lambda i, dst: (i, 0)),
            scratch_shapes=[pltpu.VMEM((TOP_K, gt, d), F32), pltpu.VMEM((TOP_K, gt, d), F32),
                            pltpu.SemaphoreType.DMA((2,))],
        ),
        out_shape=jax.ShapeDtypeStruct((t, d), F32),
        compiler_params=_params(("arbitrary",)),
        name="combine_norm",
    )(dest_flat, x1, gate_pad, g2, fw, yout)


def _head_expand():
    head_of_lane = jnp.arange(SSM_WIDTH) // SSM_HEAD_DIM
    one = (jnp.arange(LANES)[:, None] == head_of_lane[None, :]).astype(BF16)
    return jnp.tile(one, (3, 1))


def _pad_lanes(v, fill=0.0):
    v = v.reshape(1, -1).astype(F32)
    return jnp.pad(v, ((0, 0), (0, LANES - v.shape[1])), constant_values=fill)


def kernel(x, c, ada_w, ada_b, norm1_w, w_in, hg_lb_logits, hg_norm_w, ssm_a_log, ssm_dt_bias, ssm_d,
           ssm_conv_w, ssm_conv_b, ssm_norm_w, w_out, norm2_w, router_w, router_b, exp_w1, exp_b1,
           exp_w2, exp_b2, final_norm_w):
    bsz, seqlen, d = x.shape
    t = bsz * seqlen
    l = 0

    c_pad = jnp.pad(c, ((0, SUBLANES - bsz), (0, 0)))
    mod = _ada_mod(c_pad, ada_w[l], ada_b[l])[:bsz]
    sh1, sc1, g1, sh2, sc2, g2 = [m.reshape(bsz, 1, d) for m in jnp.split(mod, N_MOD, axis=-1)]

    w_in_t = jnp.swapaxes(w_in[l], 0, 1)
    w_main = w_in_t[:_OFF[6]].astype(BF16)
    w_dt = jnp.pad(w_in_t[_OFF[6]:].astype(BF16), ((0, LANES - SSM_HEADS), (0, 0)))
    qs, kk, lf, vi, gg, zz, xbc, dt = _inproj(
        x, norm1_w[l].reshape(1, d), sh1, sc1, w_main, w_dt, hg_lb_logits, _pad_lanes(ssm_dt_bias[l]))

    o_hg = _hgrn(qs, kk, lf, vi, gg, hg_norm_w[l].reshape(1, HG_DV))
    y_ssd = _ssd(xbc, zz, dt, ssm_conv_w[l], ssm_conv_b[l].reshape(1, SSM_CONV_CH),
                 _pad_lanes(ssm_a_log[l]), jnp.repeat(ssm_d[l], SSM_HEAD_DIM).reshape(1, SSM_WIDTH),
                 ssm_norm_w[l].reshape(1, SSM_WIDTH), _head_expand())

    rw_pad = jnp.pad(router_w[l], ((0, 0), (0, LANES - N_EXPERTS)))
    rw_hi = rw_pad.astype(BF16)
    rw_pad = jnp.concatenate([rw_hi, (rw_pad - rw_hi.astype(F32)).astype(BF16)], axis=1)
    rb_pad = _pad_lanes(router_b[l], NEG_BIG)
    x1, h2, idx_pad, gate_pad, counts = _outproj(
        x, o_hg, y_ssd, w_out[l].astype(BF16), g1, norm2_w[l].reshape(1, d), sh2, sc2, rw_pad, rb_pad)

    cnt = counts[0, :N_EXPERTS].astype(jnp.int32)
    blocks_e = (cnt + MOE_ROWS - 1) // MOE_ROWS
    blk_end = jnp.cumsum(blocks_e)
    pstart = (blk_end - blocks_e) * MOE_ROWS
    n_blocks = (t * TOP_K) // MOE_ROWS + N_EXPERTS
    n_rows = n_blocks * MOE_ROWS
    n_steps = n_blocks + GATHER_DEPTH - 1
    block_e = jnp.minimum(jnp.sum(blk_end[None, :] <= jnp.arange(n_steps)[:, None], axis=1),
                          N_EXPERTS - 1).astype(jnp.int32)
    n_act = blk_end[-1:].astype(jnp.int32)
    owner = jnp.where(blocks_e > 0, jnp.arange(N_EXPERTS), N_EXPERTS)
    next_e = jnp.concatenate([lax.cummin(owner, reverse=True)[1:],
                              jnp.full((1,), N_EXPERTS)]).astype(jnp.int32)

    dest_flat = _route(idx_pad.reshape(t, LANES), _pad_lanes(pstart)).reshape(t * TOP_K)

    row_tok = _invert(dest_flat, n_rows)
    yout = _experts(block_e, n_act, next_e, row_tok, h2.reshape(t, d), exp_w1[l], exp_b1[l], exp_w2[l], exp_b2[l])
    out = _combine(dest_flat, x1.reshape(t, d), gate_pad.reshape(t, LANES), g2,
                   final_norm_w.reshape(1, d), yout, seqlen)
    return out.reshape(bsz, seqlen, d)
```

```python
import functools

import jax
import jax.numpy as jnp
from jax import lax
from jax.experimental import pallas as pl
from jax.experimental.pallas import tpu as pltpu

F32 = jnp.float32
BF16 = jnp.bfloat16
HIGHEST = lax.Precision.HIGHEST

EPS = 1e-6
D_MODEL = 1024
HG_HEADS = 4
HG_DK = 128
HG_DV = 128
HG_QF = HG_HEADS * HG_DK
HG_WIDTH = HG_HEADS * HG_DV
HG_CHUNK = 64
SSM_HEADS = 8
SSM_HEAD_DIM = 64
SSM_WIDTH = SSM_HEADS * SSM_HEAD_DIM
SSM_GROUPS = 2
SSM_GROUP_HEADS = SSM_HEADS // SSM_GROUPS
SSM_GROUP_WIDTH = SSM_WIDTH // SSM_GROUPS
SSM_STATE = 128
SSM_CONV = 4
SSM_CONV_CH = SSM_WIDTH + 2 * SSM_GROUPS * SSM_STATE
SSM_CHUNK = 128
IN_SPLITS = (HG_QF, HG_QF, HG_WIDTH, HG_WIDTH, SSM_WIDTH, SSM_CONV_CH, SSM_HEADS)
IN_COLS = sum(IN_SPLITS)
N_EXPERTS = 32
TOP_K = 4
D_FF = 1024
SWIGLU_LIMIT = 7.0
SWIGLU_ALPHA = 1.702
N_MOD = 6

LANES = 128
SUBLANES = 8
VMEM_LIMIT = 56 * 1024 * 1024

ROW_TILE = 1024
ROUTE_TILE = 512
SEQ_TILE = 1024
MOE_ROWS = 256
GATHER_TILE = 256
INVERT_TILE = 1024
GATHER_DEPTH = 3
NEG_BIG = -1e30


def _silu(v):
    return v * jax.nn.sigmoid(v)


def _softplus(v):
    return jnp.maximum(v, 0.0) + jnp.log1p(jnp.exp(-jnp.abs(v)))


def _split3(v):
    hi = v.astype(BF16)
    r1 = v - hi.astype(F32)
    mid = r1.astype(BF16)
    lo = (r1 - mid.astype(F32)).astype(BF16)
    return jnp.concatenate([hi, mid, lo], axis=-1)


def _params(sem):
    return pltpu.CompilerParams(dimension_semantics=sem, vmem_limit_bytes=VMEM_LIMIT)


def _ada_kernel(c_ref, w_ref, b_ref, o_ref):
    o_ref[...] = jnp.dot(_silu(c_ref[...]), w_ref[...], precision=HIGHEST,
                         preferred_element_type=F32) + b_ref[...]


def _ada_mod(c_pad, ada_w, ada_b):
    n = ada_w.shape[1]
    tn = D_MODEL
    return pl.pallas_call(
        _ada_kernel,
        grid=(n // tn,),
        in_specs=[pl.BlockSpec((SUBLANES, D_MODEL), lambda j: (0, 0)),
                  pl.BlockSpec((D_MODEL, tn), lambda j: (0, j)),
                  pl.BlockSpec((1, tn), lambda j: (0, j))],
        out_specs=pl.BlockSpec((SUBLANES, tn), lambda j: (0, j)),
        out_shape=jax.ShapeDtypeStruct((SUBLANES, n), F32),
        compiler_params=_params(("arbitrary",)),
        name="ada_mod",
    )(c_pad, ada_w, ada_b.reshape(1, n))


_OFF = [0]
for _w in IN_SPLITS:
    _OFF.append(_OFF[-1] + _w)


def _inproj_kernel(x_ref, nw_ref, sh_ref, sc_ref, w_ref, wdt_ref, lbl_ref, dtb_ref,
                   qs_ref, kk_ref, lf_ref, vi_ref, gg_ref, zz_ref, xbc_ref, dt_ref):
    x = x_ref[0]
    ms = jnp.mean(x * x, axis=-1, keepdims=True)
    h = x * lax.rsqrt(ms + EPS) * nw_ref[...]
    h = h * (1.0 + sc_ref[0]) + sh_ref[0]
    hb = h.astype(BF16)

    def seg(k):
        cols = w_ref[_OFF[k]:_OFF[k + 1], :] if k < 6 else wdt_ref[...]
        return lax.dot_general(hb, cols, (((1,), (1,)), ((), ())), preferred_element_type=F32)

    lbl = lbl_ref[...]
    le = jnp.exp(lbl - jnp.max(lbl, axis=0, keepdims=True))
    lb = le[0:1, :] / jnp.sum(le, axis=0, keepdims=True)

    qs_ref[0] = _silu(seg(0)).astype(BF16)
    fg = lb + (1.0 - lb) * jax.nn.sigmoid(seg(1))
    kk_ref[0] = (1.0 - fg).astype(BF16)
    lf_ref[0] = jnp.log(fg)
    vi_ref[0] = seg(2).astype(BF16)
    gg_ref[0] = _silu(seg(3)).astype(BF16)
    zz_ref[0] = _silu(seg(4)).astype(BF16)
    xbc_ref[0] = seg(5).astype(BF16)
    dt_ref[0] = _softplus(seg(6) + dtb_ref[...])


def _inproj(x, nw, sh, sc, w_main, w_dt, lb_logits, dtb_pad):
    bsz, seqlen, d = x.shape
    tm = min(ROW_TILE, seqlen)
    nt = seqlen // tm
    tok = lambda w: pl.BlockSpec((1, tm, w), lambda b, i: (b, i, 0))
    full = lambda a: pl.BlockSpec(a.shape, lambda b, i: (0,) * a.ndim)
    mod = pl.BlockSpec((1, 1, d), lambda b, i: (b, 0, 0))
    shp = lambda w, dt: jax.ShapeDtypeStruct((bsz, seqlen, w), dt)
    return pl.pallas_call(
        _inproj_kernel,
        grid=(bsz, nt),
        in_specs=[tok(d), full(nw), mod, mod, full(w_main), full(w_dt), full(lb_logits), full(dtb_pad)],
        out_specs=[tok(HG_QF), tok(HG_QF), tok(HG_QF), tok(HG_WIDTH), tok(HG_WIDTH),
                   tok(SSM_WIDTH), tok(SSM_CONV_CH), tok(LANES)],
        out_shape=[shp(HG_QF, BF16), shp(HG_QF, BF16), shp(HG_QF, F32), shp(HG_WIDTH, BF16),
                   shp(HG_WIDTH, BF16), shp(SSM_WIDTH, BF16), shp(SSM_CONV_CH, BF16),
                   shp(LANES, F32)],
        compiler_params=_params(("arbitrary", "arbitrary")),
        name="inproj",
    )(x, nw, sh, sc, w_main, w_dt, lb_logits, dtb_pad)


def _hgrn_kernel(qs_ref, kk_ref, lf_ref, vi_ref, gg_ref, nw_ref, o_ref, st_ref):
    @pl.when(pl.program_id(1) == 0)
    def _():
        st_ref[...] = jnp.zeros_like(st_ref)

    c = HG_CHUNK
    tq = qs_ref.shape[1]
    nc = tq // c
    rows = lax.broadcasted_iota(jnp.int32, (nc, c, c), 1)
    cols = lax.broadcasted_iota(jnp.int32, (nc, c, c), 2)
    causal = rows >= cols
    tril = causal.astype(BF16)
    bmm = lambda a, bb, dims: lax.dot_general(a, bb, (dims, ((0,), (0,))), preferred_element_type=F32)
    for h in range(HG_HEADS):
        hs = slice(h * HG_DK, (h + 1) * HG_DK)
        chunked = lambda ref: ref[0, :, hs].reshape(nc, c, HG_DK)
        q = chunked(qs_ref).astype(F32)
        k = chunked(kk_ref).astype(F32)
        v = chunked(vi_ref)
        b3 = bmm(tril, _split3(chunked(lf_ref)), ((2,), (1,)))
        b = b3[:, :, :HG_DK] + b3[:, :, HG_DK:2 * HG_DK] + b3[:, :, 2 * HG_DK:]
        b_mid = b[:, c // 2 - 1:c // 2, :]
        b_last = b[:, c - 1:c, :]
        qa = (q * jnp.exp(b - b_mid)).astype(BF16)
        ka = (k * jnp.exp(b_mid - b)).astype(BF16)
        att = bmm(qa, ka, ((2,), (2,)))
        att = jnp.where(causal, att, 0.0).astype(BF16)
        o = bmm(att, v, ((2,), (1,)))
        kd = (k * jnp.exp(b_last - b)).astype(BF16)
        d_st = bmm(v, kd, ((1,), (1,)))
        decay = jnp.exp(b_last)
        st = st_ref[h]
        entering = []
        for ci in range(nc):
            entering.append(st.astype(BF16))
            st = st * decay[ci] + d_st[ci]
        st_ref[h] = st
        qb = (q * jnp.exp(b)).astype(BF16)
        o = o + bmm(qb, jnp.stack(entering), ((2,), (2,)))
        ms = jnp.mean(o * o, axis=-1, keepdims=True)
        y = o * lax.rsqrt(ms + EPS) * nw_ref[...] * chunked(gg_ref).astype(F32)
        o_ref[0, :, hs] = y.reshape(tq, HG_DV).astype(BF16)


def _hgrn(qs, kk, lf, vi, gg, nw):
    bsz, seqlen, width = qs.shape
    tq = min(SEQ_TILE, seqlen)
    blk = pl.BlockSpec((1, tq, width), lambda b, i: (b, i, 0))
    return pl.pallas_call(
        _hgrn_kernel,
        grid=(bsz, seqlen // tq),
        in_specs=[blk, blk, blk, blk, blk, pl.BlockSpec((1, HG_DV), lambda b, i: (0, 0))],
        out_specs=blk,
        out_shape=jax.ShapeDtypeStruct((bsz, seqlen, HG_WIDTH), BF16),
        scratch_shapes=[pltpu.VMEM((HG_HEADS, HG_DV, HG_DK), F32)],
        compiler_params=_params(("arbitrary", "arbitrary")),
        name="hgrn2",
    )(qs, kk, lf, vi, gg, nw)


def _ssd_kernel(xbc_ref, zz_ref, dt_ref, cw_ref, cb_ref, alog_ref, dsk_ref, nw_ref, ex_ref,
                y_ref, buf_ref, xc_ref, yc_ref, st_ref):
    tq = xbc_ref.shape[1]
    halo = SUBLANES

    @pl.when(pl.program_id(1) == 0)
    def _():
        buf_ref[0:halo, :] = jnp.zeros((halo, SSM_CONV_CH), F32)
        st_ref[...] = jnp.zeros_like(st_ref)

    buf_ref[halo:halo + tq, :] = xbc_ref[0].astype(F32)
    conv = cb_ref[...] + buf_ref[halo - 3:halo - 3 + tq, :] * cw_ref[0:1, :]
    for w in range(1, SSM_CONV):
        conv = conv + buf_ref[halo - 3 + w:halo - 3 + w + tq, :] * cw_ref[w:w + 1, :]
    buf_ref[0:halo, :] = buf_ref[tq:tq + halo, :]
    xc_ref[...] = _silu(conv)

    c = SSM_CHUNK
    p = SSM_HEAD_DIM
    nc = tq // c
    rows = lax.broadcasted_iota(jnp.int32, (nc, c, c), 1)
    cols = lax.broadcasted_iota(jnp.int32, (nc, c, c), 2)
    causal = rows >= cols
    tril = causal.astype(BF16)
    bmm = lambda a, bb, dims, **kw: lax.dot_general(a, bb, (dims, ((0,), (0,))),
                                                    preferred_element_type=F32, **kw)
    chunked = lambda val: val.reshape(nc, c, val.shape[-1])
    dt = chunked(dt_ref[0])
    da = dt * (-jnp.exp(alog_ref[...]))
    da3 = _split3(da)
    a3 = bmm(tril, da3, ((2,), (1,)))
    acum = a3[:, :, :LANES] + a3[:, :, LANES:2 * LANES] + a3[:, :, 2 * LANES:]
    at3 = bmm(da3, tril, ((1,), (2,)))
    acum_t = at3[:, :LANES, :] + at3[:, LANES:2 * LANES, :] + at3[:, 2 * LANES:, :]
    widen = lambda v: jnp.dot(_split3(v.reshape(tq, LANES)), ex_ref[...], preferred_element_type=F32)
    dt_w = widen(dt)
    decay_out_w = widen(jnp.exp(acum))
    decay_in_w = widen(jnp.exp(acum[:, c - 1:c, :] - acum))
    xdt_w = xc_ref[:, 0:SSM_WIDTH] * dt_w
    xdd_w = xdt_w * decay_in_w
    y_inter_groups = []
    for g in range(SSM_GROUPS):
        gs = slice(g * SSM_GROUP_WIDTH, (g + 1) * SSM_GROUP_WIDTH)
        bm = chunked(xc_ref[:, SSM_WIDTH + g * SSM_STATE:SSM_WIDTH + (g + 1) * SSM_STATE]).astype(BF16)
        cm = chunked(xc_ref[:, SSM_WIDTH + (SSM_GROUPS + g) * SSM_STATE:
                            SSM_WIDTH + (SSM_GROUPS + g + 1) * SSM_STATE]).astype(BF16)
        cb = bmm(cm, bm, ((2,), (2,)))
        d_st = bmm(chunked(xdd_w[:, gs]).astype(BF16), bm, ((1,), (1,)))
        a_last = []
        for hh in range(SSM_GROUP_HEADS):
            h = g * SSM_GROUP_HEADS + hh
            hs = slice(h * p, (h + 1) * p)
            seg = jnp.exp(jnp.where(causal, acum[:, :, h:h + 1] - acum_t[:, h:h + 1, :], -jnp.inf))
            y = bmm((cb * seg).astype(BF16), chunked(xdt_w[:, hs]).astype(BF16), ((2,), (1,)))
            yc_ref[:, hs] = y.reshape(tq, p)
            a_last.append(jnp.exp(acum_t[:, h:h + 1, c - 1:c]))
        st = [st_ref[g, hh * p:(hh + 1) * p, :] for hh in range(SSM_GROUP_HEADS)]
        entering = []
        for ci in range(nc):
            entering.append(jnp.concatenate(st, axis=0).astype(BF16))
            st = [st[hh] * a_last[hh][ci] + d_st[ci, hh * p:(hh + 1) * p, :] for hh in range(SSM_GROUP_HEADS)]
        for hh in range(SSM_GROUP_HEADS):
            st_ref[g, hh * p:(hh + 1) * p, :] = st[hh]
        y_inter_groups.append(bmm(cm, jnp.stack(entering), ((2,), (2,))).reshape(tq, SSM_GROUP_WIDTH))
    y_inter = jnp.concatenate(y_inter_groups, axis=-1)
    y = yc_ref[...] + decay_out_w * y_inter
    y = (y + dsk_ref[...] * xc_ref[:, 0:SSM_WIDTH]) * zz_ref[0].astype(F32)
    for g in range(SSM_GROUPS):
        gs = slice(g * SSM_GROUP_WIDTH, (g + 1) * SSM_GROUP_WIDTH)
        yg = y[:, gs]
        ms = jnp.mean(yg * yg, axis=-1, keepdims=True)
        y_ref[0, :, gs] = (yg * lax.rsqrt(ms + EPS) * nw_ref[:, gs]).astype(BF16)


def _ssd(xbc, zz, dt, conv_w, conv_b, alog_pad, dsk_wide, nw, expand3):
    bsz, seqlen, _ = xbc.shape
    tq = min(SEQ_TILE, seqlen)
    tok = lambda w: pl.BlockSpec((1, tq, w), lambda b, i: (b, i, 0))
    full = lambda a: pl.BlockSpec(a.shape, lambda b, i: (0,) * a.ndim)
    return pl.pallas_call(
        _ssd_kernel,
        grid=(bsz, seqlen // tq),
        in_specs=[tok(SSM_CONV_CH), tok(SSM_WIDTH), tok(LANES), full(conv_w), full(conv_b),
                  full(alog_pad), full(dsk_wide), full(nw), full(expand3)],
        out_specs=tok(SSM_WIDTH),
        out_shape=jax.ShapeDtypeStruct((bsz, seqlen, SSM_WIDTH), BF16),
        scratch_shapes=[pltpu.VMEM((tq + 2 * SUBLANES, SSM_CONV_CH), F32),
                        pltpu.VMEM((tq, SSM_CONV_CH), F32),
                        pltpu.VMEM((tq, SSM_WIDTH), F32),
                        pltpu.VMEM((SSM_GROUPS, SSM_GROUP_WIDTH, SSM_STATE), F32)],
        compiler_params=_params(("arbitrary", "arbitrary")),
        name="ssd",
    )(xbc, zz, dt, conv_w, conv_b, alog_pad, dsk_wide, nw, expand3)


def _outproj_kernel(x_ref, oh_ref, ys_ref, wo_ref, g1_ref, nw_ref, sh_ref, sc_ref, rw_ref, rb_ref,
                    x1_ref, h2_ref, idx_ref, gate_ref, cnt_ref):
    first = jnp.logical_and(pl.program_id(0) == 0, pl.program_id(1) == 0)

    @pl.when(first)
    def _():
        cnt_ref[...] = jnp.zeros_like(cnt_ref)

    mix = jnp.dot(oh_ref[0], wo_ref[0:HG_WIDTH, :], preferred_element_type=F32)
    mix = mix + jnp.dot(ys_ref[0], wo_ref[HG_WIDTH:, :], preferred_element_type=F32)
    x1 = x_ref[0] + g1_ref[0] * mix
    x1_ref[0] = x1
    ms = jnp.mean(x1 * x1, axis=-1, keepdims=True)
    h2 = x1 * lax.rsqrt(ms + EPS) * nw_ref[...]
    h2 = h2 * (1.0 + sc_ref[0]) + sh_ref[0]
    h2_ref[0] = h2
    h_hi = h2.astype(BF16)
    h_lo = (h2 - h_hi.astype(F32)).astype(BF16)
    part = jnp.dot(h_hi, rw_ref[...], preferred_element_type=F32)
    logits = (part[:, :LANES] + part[:, LANES:]
              + jnp.dot(h_lo, rw_ref[:, :LANES], preferred_element_type=F32)) + rb_ref[...]
    lane = lax.broadcasted_iota(jnp.int32, logits.shape, 1).astype(F32)
    idx_out = jnp.zeros(logits.shape, F32)
    val_out = jnp.zeros(logits.shape, F32)
    sel = jnp.zeros(logits.shape, F32)
    work = logits
    top = None
    denom = None
    for k in range(TOP_K):
        m = jnp.max(work, axis=-1, keepdims=True)
        am = jnp.min(jnp.where(work == m, lane, float(LANES)), axis=-1, keepdims=True)
        hit = lane == am
        work = jnp.where(hit, -jnp.inf, work)
        sel = jnp.where(hit, 1.0, sel)
        if k == 0:
            top = m
        e = jnp.exp(m - top)
        denom = e if k == 0 else denom + e
        idx_out = jnp.where(lane == float(k), am, idx_out)
        val_out = jnp.where(lane == float(k), e, val_out)
    idx_ref[0] = idx_out.astype(jnp.int32)
    gate_ref[0] = val_out / denom
    cnt_ref[0:1, :] += jnp.sum(sel, axis=0, keepdims=True)


def _outproj(x, o_hg, y_ssd, w_out_b, g1, nw, sh, sc, rw_pad, rb_pad):
    bsz, seqlen, d = x.shape
    tm = min(ROW_TILE, seqlen)
    tok = lambda w: pl.BlockSpec((1, tm, w), lambda b, i: (b, i, 0))
    full = lambda a: pl.BlockSpec(a.shape, lambda b, i: (0,) * a.ndim)
    mod = pl.BlockSpec((1, 1, d), lambda b, i: (b, 0, 0))
    shp = lambda w, dt: jax.ShapeDtypeStruct((bsz, seqlen, w), dt)
    return pl.pallas_call(
        _outproj_kernel,
        grid=(bsz, seqlen // tm),
        in_specs=[tok(d), tok(HG_WIDTH), tok(SSM_WIDTH), full(w_out_b), mod, full(nw), mod, mod,
                  full(rw_pad), full(rb_pad)],
        out_specs=[tok(d), tok(d), tok(LANES), tok(LANES),
                   pl.BlockSpec((SUBLANES, LANES), lambda b, i: (0, 0))],
        out_shape=[shp(d, F32), shp(d, F32), shp(LANES, jnp.int32), shp(LANES, F32),
                   jax.ShapeDtypeStruct((SUBLANES, LANES), F32)],
        compiler_params=_params(("arbitrary", "arbitrary")),
        name="outproj_router",
    )(x, o_hg, y_ssd, w_out_b, g1, nw, sh, sc, rw_pad, rb_pad)


def _route_kernel(idx_ref, pstart_ref, dest_ref, carry_ref):
    @pl.when(pl.program_id(0) == 0)
    def _():
        carry_ref[...] = jnp.zeros_like(carry_ref)

    idx = idx_ref[...]
    tt = idx.shape[0]
    lane = lax.broadcasted_iota(jnp.int32, idx.shape, 1)
    hits = [lane == idx[:, k:k + 1] for k in range(TOP_K)]
    sel = jnp.zeros(idx.shape, F32)
    for hit in hits:
        sel = jnp.where(hit, 1.0, sel)
    rows = lax.broadcasted_iota(jnp.int32, (tt, tt), 0)
    cols = lax.broadcasted_iota(jnp.int32, (tt, tt), 1)
    before = (rows > cols).astype(BF16)
    rank = jnp.dot(before, sel.astype(BF16), preferred_element_type=F32) + carry_ref[0:1, :]
    carry_ref[0:1, :] += jnp.sum(sel, axis=0, keepdims=True)
    dense = pstart_ref[...] + rank
    pack = LANES // TOP_K
    slot = (lax.broadcasted_iota(jnp.int32, idx.shape, 0) % pack) * TOP_K
    spread = jnp.zeros(idx.shape, F32)
    for k, hit in enumerate(hits):
        dk = jnp.sum(jnp.where(hit, dense, 0.0), axis=-1, keepdims=True)
        spread = jnp.where(lane == slot + k, dk, spread)
    out_rows = tt // pack
    merge = (lax.broadcasted_iota(jnp.int32, (out_rows, tt), 0)
             == lax.broadcasted_iota(jnp.int32, (out_rows, tt), 1) // pack).astype(BF16)
    m3 = jnp.dot(merge, _split3(spread), preferred_element_type=F32)
    dest_ref[...] = (m3[:, :LANES] + m3[:, LANES:2 * LANES] + m3[:, 2 * LANES:]).astype(jnp.int32)


def _route(idx_pad, pstart_row):
    t = idx_pad.shape[0]
    tt = min(ROUTE_TILE, t)
    return pl.pallas_call(
        _route_kernel,
        grid=(t // tt,),
        in_specs=[pl.BlockSpec((tt, LANES), lambda i: (i, 0)),
                  pl.BlockSpec((1, LANES), lambda i: (0, 0))],
        out_specs=pl.BlockSpec((tt * TOP_K // LANES, LANES), lambda i: (i, 0)),
        out_shape=jax.ShapeDtypeStruct((t * TOP_K // LANES, LANES), jnp.int32),
        scratch_shapes=[pltpu.VMEM((SUBLANES, LANES), F32)],
        compiler_params=_params(("arbitrary",)),
        name="route_rank",
    )(idx_pad, pstart_row)


def _invert_kernel(dest_ref, zeros_ref, rowtok_ref, sem):
    i = pl.program_id(0)
    per_step = dest_ref.shape[0] // TOP_K // pl.num_programs(0)

    @pl.when(i == 0)
    def _():
        fill = pltpu.make_async_copy(zeros_ref, rowtok_ref, sem)
        fill.start()
        fill.wait()

    def tok_body(n, carry):
        tok = i * per_step + n
        for k in range(TOP_K):
            rowtok_ref[dest_ref[tok * TOP_K + k]] = tok
        return carry
    lax.fori_loop(0, per_step, tok_body, 0, unroll=8)


def _invert(dest_flat, n_rows):
    t = dest_flat.shape[0] // TOP_K
    return pl.pallas_call(
        _invert_kernel,
        grid=(max(t // INVERT_TILE, 1),),
        in_specs=[pl.BlockSpec(memory_space=pltpu.SMEM), pl.BlockSpec(memory_space=pl.ANY)],
        out_specs=pl.BlockSpec(memory_space=pltpu.SMEM),
        out_shape=jax.ShapeDtypeStruct((n_rows,), jnp.int32),
        scratch_shapes=[pltpu.SemaphoreType.DMA(())],
        compiler_params=_params(("arbitrary",)),
        name="invert_route",
    )(dest_flat, jnp.zeros((n_rows,), jnp.int32))


def _expert_kernel(be_ref, nact_ref, nxt_ref, rowtok_ref, h_ref, w1_ref, b1_ref, w2_ref, b2_ref, y_ref,
                   xbuf0_ref, xbuf1_ref, xbuf2_ref, w1s_ref, w2s_ref, w1b_ref, w2b_ref, sem, wsem):
    j = pl.program_id(0)
    n_act = nact_ref[0]
    active = j < n_act
    expert = be_ref[j]
    fresh = jnp.logical_or(j == 0, expert != be_ref[jnp.maximum(j - 1, 0)])
    phase = j % GATHER_DEPTH
    bufs = (xbuf0_ref, xbuf1_ref, xbuf2_ref)

    def issue(block, s):
        base = jnp.minimum(block, n_act - 1) * MOE_ROWS
        for r in range(MOE_ROWS):
            pltpu.make_async_copy(h_ref.at[pl.ds(rowtok_ref[base + r], 1), :],
                                  bufs[s].at[pl.ds(r, 1), :], sem.at[s]).start(priority=1)

    def wait(s):
        pltpu.make_async_copy(h_ref.at[pl.ds(0, MOE_ROWS), :], bufs[s], sem.at[s]).wait()

    def weight_copies(e):
        return (pltpu.make_async_copy(w1_ref.at[e], w1s_ref, wsem.at[0]),
                pltpu.make_async_copy(w2_ref.at[e], w2s_ref, wsem.at[1]))

    @pl.when(j == 0)
    def _():
        for cp in weight_copies(expert):
            cp.start()
        for s in range(GATHER_DEPTH - 1):
            issue(s, s)

    @pl.when(jnp.logical_and(active, fresh))
    def _():
        for cp in weight_copies(expert):
            cp.wait()
        w1b_ref[...] = w1s_ref[...].astype(BF16)
        w2b_ref[...] = w2s_ref[...].astype(BF16)

        @pl.when(nxt_ref[expert] < N_EXPERTS)
        def _():
            for cp in weight_copies(nxt_ref[expert]):
                cp.start()

    for s in range(GATHER_DEPTH):
        @pl.when(jnp.logical_and(j < n_act + GATHER_DEPTH - 1, phase == s))
        def _(s=s):
            wait(s)

        @pl.when(jnp.logical_and(active, phase == s))
        def _(s=s):
            issue(j + GATHER_DEPTH - 1, (s + GATHER_DEPTH - 1) % GATHER_DEPTH)
            hb = jnp.dot(bufs[s][...].astype(BF16), w1b_ref[...], preferred_element_type=F32) + b1_ref[0]
            glu = jnp.minimum(hb[:, :D_FF], SWIGLU_LIMIT)
            lin = jnp.clip(hb[:, D_FF:], -SWIGLU_LIMIT, SWIGLU_LIMIT)
            act = glu * jax.nn.sigmoid(SWIGLU_ALPHA * glu) * (lin + 1.0)
            y_ref[...] = jnp.dot(act.astype(BF16), w2b_ref[...], preferred_element_type=F32) + b2_ref[0]


def _experts(block_e, n_act, next_e, row_tok, h2, w1, b1, w2, b2):
    n_rows = row_tok.shape[0]
    d = h2.shape[1]
    nb = block_e.shape[0]
    row_map = lambda j, be, na, nx, rt: (jnp.maximum(jnp.minimum(j, na[0] - 1), 0), 0)
    exp_map = lambda j, be, na, nx, rt: (be[j], 0, 0)
    return pl.pallas_call(
        _expert_kernel,
        grid_spec=pltpu.PrefetchScalarGridSpec(
            num_scalar_prefetch=4,
            grid=(nb,),
            in_specs=[pl.BlockSpec(memory_space=pl.ANY),
                      pl.BlockSpec(memory_space=pl.ANY),
                      pl.BlockSpec((1, 1, 2 * D_FF), exp_map),
                      pl.BlockSpec(memory_space=pl.ANY),
                      pl.BlockSpec((1, 1, d), exp_map)],
            out_specs=pl.BlockSpec((MOE_ROWS, d), row_map),
            scratch_shapes=[pltpu.VMEM((MOE_ROWS, d), F32)] * GATHER_DEPTH + [
                pltpu.VMEM((d, 2 * D_FF), F32), pltpu.VMEM((D_FF, d), F32),
                pltpu.VMEM((d, 2 * D_FF), BF16), pltpu.VMEM((D_FF, d), BF16),
                pltpu.SemaphoreType.DMA((GATHER_DEPTH,)), pltpu.SemaphoreType.DMA((2,))],
        ),
        out_shape=jax.ShapeDtypeStruct((n_rows, d), F32),
        compiler_params=_params(("arbitrary",)),
        name="expert_ffn",
    )(block_e, n_act, next_e, row_tok, h2, w1, b1.reshape(N_EXPERTS, 1, 2 * D_FF), w2,
      b2.reshape(N_EXPERTS, 1, d))


def _combine_kernel(dest_ref, x1_ref, gate_ref, g2_ref, fw_ref, y_ref, o_ref,
                    buf0_ref, buf1_ref, buf2_ref, sem):
    i = pl.program_id(0)
    n = pl.num_programs(0)
    gt = x1_ref.shape[0]
    bufs = (buf0_ref, buf1_ref, buf2_ref)
    depth = len(bufs)

    def issue(tile, s):
        base = jnp.minimum(tile, n - 1) * (gt * TOP_K)
        for j in range(gt):
            for k in range(TOP_K):
                pltpu.make_async_copy(y_ref.at[pl.ds(dest_ref[base + j * TOP_K + k], 1), :],
                                      bufs[s].at[k, pl.ds(j, 1), :], sem.at[s]).start(priority=k % 2)

    def wait(s):
        for k in range(TOP_K):
            pltpu.make_async_copy(y_ref.at[pl.ds(0, gt), :], bufs[s].at[k], sem.at[s]).wait()

    @pl.when(i == 0)
    def _():
        for s in range(depth - 1):
            issue(s, s)

    for s in range(depth):
        @pl.when(i % depth == s)
        def _(s=s):
            wait(s)
            issue(i + depth - 1, (s + depth - 1) % depth)
            gates = gate_ref[...]
            moe = gates[:, 0:1] * bufs[s][0]
            for k in range(1, TOP_K):
                moe = moe + gates[:, k:k + 1] * bufs[s][k]
            x2 = x1_ref[...] + g2_ref[0] * moe
            ms = jnp.mean(x2 * x2, axis=-1, keepdims=True)
            o_ref[...] = x2 * lax.rsqrt(ms + EPS) * fw_ref[...]

        @pl.when(jnp.logical_and(i == n - 1, i % depth == s))
        def _(s=s):
            for ahead in range(1, depth):
                wait((s + ahead) % depth)


def _combine(dest_flat, x1, gate_pad, g2, fw, yout, seqlen):
    t, d = x1.shape
    gt = min(GATHER_TILE, seqlen)
    per_seq = seqlen // gt
    return pl.pallas_call(
        _combine_kernel,
        grid_spec=pltpu.PrefetchScalarGridSpec(
            num_scalar_prefetch=1,
            grid=(t // gt,),
            in_specs=[pl.BlockSpec((gt, d), lambda i, dst: (i, 0)),
                      pl.BlockSpec((gt, LANES), lambda i, dst: (i, 0)),
                      pl.BlockSpec((1, 1, d), lambda i, dst: (i // per_seq, 0, 0)),
                      pl.BlockSpec((1, d), lambda i, dst: (0, 0)),
                      pl.BlockSpec(memory_space=pl.ANY)],
            out_specs=pl.BlockSpec((gt, d), lambda i, dst: (i, 0)),
            scratch_shapes=[pltpu.VMEM((TOP_K, gt, d), F32)] * 3 + [pltpu.SemaphoreType.DMA((3,))],
        ),
        out_shape=jax.ShapeDtypeStruct((t, d), F32),
        compiler_params=_params(("arbitrary",)),
        name="combine_norm",
    )(dest_flat, x1, gate_pad, g2, fw, yout)


def _head_expand():
    head_of_lane = jnp.arange(SSM_WIDTH) // SSM_HEAD_DIM
    one = (jnp.arange(LANES)[:, None] == head_of_lane[None, :]).astype(BF16)
    return jnp.tile(one, (3, 1))


def _pad_lanes(v, fill=0.0):
    v = v.reshape(1, -1).astype(F32)
    return jnp.pad(v, ((0, 0), (0, LANES - v.shape[1])), constant_values=fill)


def kernel(x, c, ada_w, ada_b, norm1_w, w_in, hg_lb_logits, hg_norm_w, ssm_a_log, ssm_dt_bias, ssm_d,
           ssm_conv_w, ssm_conv_b, ssm_norm_w, w_out, norm2_w, router_w, router_b, exp_w1, exp_b1,
           exp_w2, exp_b2, final_norm_w):
    bsz, seqlen, d = x.shape
    t = bsz * seqlen
    l = 0

    c_pad = jnp.pad(c, ((0, SUBLANES - bsz), (0, 0)))
    mod = _ada_mod(c_pad, ada_w[l], ada_b[l])[:bsz]
    sh1, sc1, g1, sh2, sc2, g2 = [m.reshape(bsz, 1, d) for m in jnp.split(mod, N_MOD, axis=-1)]

    w_in_t = jnp.swapaxes(w_in[l], 0, 1)
    w_main = w_in_t[:_OFF[6]].astype(BF16)
    w_dt = jnp.pad(w_in_t[_OFF[6]:].astype(BF16), ((0, LANES - SSM_HEADS), (0, 0)))
    qs, kk, lf, vi, gg, zz, xbc, dt = _inproj(
        x, norm1_w[l].reshape(1, d), sh1, sc1, w_main, w_dt, hg_lb_logits, _pad_lanes(ssm_dt_bias[l]))

    o_hg = _hgrn(qs, kk, lf, vi, gg, hg_norm_w[l].reshape(1, HG_DV))
    y_ssd = _ssd(xbc, zz, dt, ssm_conv_w[l], ssm_conv_b[l].reshape(1, SSM_CONV_CH),
                 _pad_lanes(ssm_a_log[l]), jnp.repeat(ssm_d[l], SSM_HEAD_DIM).reshape(1, SSM_WIDTH),
                 ssm_norm_w[l].reshape(1, SSM_WIDTH), _head_expand())

    rw_pad = jnp.pad(router_w[l], ((0, 0), (0, LANES - N_EXPERTS)))
    rw_hi = rw_pad.astype(BF16)
    rw_pad = jnp.concatenate([rw_hi, (rw_pad - rw_hi.astype(F32)).astype(BF16)], axis=1)
    rb_pad = _pad_lanes(router_b[l], NEG_BIG)
    x1, h2, idx_pad, gate_pad, counts = _outproj(
        x, o_hg, y_ssd, w_out[l].astype(BF16), g1, norm2_w[l].reshape(1, d), sh2, sc2, rw_pad, rb_pad)

    cnt = counts[0, :N_EXPERTS].astype(jnp.int32)
    blocks_e = (cnt + MOE_ROWS - 1) // MOE_ROWS
    blk_end = jnp.cumsum(blocks_e)
    pstart = (blk_end - blocks_e) * MOE_ROWS
    n_blocks = (t * TOP_K) // MOE_ROWS + N_EXPERTS
    n_rows = n_blocks * MOE_ROWS
    n_steps = n_blocks + GATHER_DEPTH - 1
    block_e = jnp.minimum(jnp.sum(blk_end[None, :] <= jnp.arange(n_steps)[:, None], axis=1),
                          N_EXPERTS - 1).astype(jnp.int32)
    n_act = blk_end[-1:].astype(jnp.int32)
    owner = jnp.where(blocks_e > 0, jnp.arange(N_EXPERTS), N_EXPERTS)
    next_e = jnp.concatenate([lax.cummin(owner, reverse=True)[1:],
                              jnp.full((1,), N_EXPERTS)]).astype(jnp.int32)

    dest_flat = _route(idx_pad.reshape(t, LANES), _pad_lanes(pstart)).reshape(t * TOP_K)

    row_tok = _invert(dest_flat, n_rows)
    yout = _experts(block_e, n_act, next_e, row_tok, h2.reshape(t, d), exp_w1[l], exp_b1[l], exp_w2[l], exp_b2[l])
    out = _combine(dest_flat, x1.reshape(t, d), gate_pad.reshape(t, LANES), g2,
                   final_norm_w.reshape(1, d), yout, seqlen)
    return out.reshape(bsz, seqlen, d)
```

```python
import functools

import jax
import jax.numpy as jnp
from jax import lax
from jax.experimental import pallas as pl
from jax.experimental.pallas import tpu as pltpu

F32 = jnp.float32
BF16 = jnp.bfloat16
HIGHEST = lax.Precision.HIGHEST

EPS = 1e-6
D_MODEL = 1024
HG_HEADS = 4
HG_DK = 128
HG_DV = 128
HG_QF = HG_HEADS * HG_DK
HG_WIDTH = HG_HEADS * HG_DV
HG_CHUNK = 64
SSM_HEADS = 8
SSM_HEAD_DIM = 64
SSM_WIDTH = SSM_HEADS * SSM_HEAD_DIM
SSM_GROUPS = 2
SSM_GROUP_HEADS = SSM_HEADS // SSM_GROUPS
SSM_GROUP_WIDTH = SSM_WIDTH // SSM_GROUPS
SSM_STATE = 128
SSM_CONV = 4
SSM_CONV_CH = SSM_WIDTH + 2 * SSM_GROUPS * SSM_STATE
SSM_CHUNK = 128
IN_SPLITS = (HG_QF, HG_QF, HG_WIDTH, HG_WIDTH, SSM_WIDTH, SSM_CONV_CH, SSM_HEADS)
IN_COLS = sum(IN_SPLITS)
N_EXPERTS = 32
TOP_K = 4
D_FF = 1024
SWIGLU_LIMIT = 7.0
SWIGLU_ALPHA = 1.702
N_MOD = 6

LANES = 128
SUBLANES = 8
VMEM_LIMIT = 56 * 1024 * 1024

ROW_TILE = 1024
ADA_TILE = 3072
ROUTE_TILE = 512
SEQ_TILE = 1024
MOE_ROWS = 256
GATHER_TILE = 256
INVERT_TILE = 1024
GATHER_DEPTH = 3
NEG_BIG = -1e30


def _silu(v):
    return v * jax.nn.sigmoid(v)


def _softplus(v):
    return jnp.maximum(v, 0.0) + jnp.log1p(jnp.exp(-jnp.abs(v)))


def _split3(v):
    hi = v.astype(BF16)
    r1 = v - hi.astype(F32)
    mid = r1.astype(BF16)
    lo = (r1 - mid.astype(F32)).astype(BF16)
    return jnp.concatenate([hi, mid, lo], axis=-1)


def _params(sem):
    return pltpu.CompilerParams(dimension_semantics=sem, vmem_limit_bytes=VMEM_LIMIT)


def _ada_kernel(c_ref, w_ref, b_ref, o_ref):
    o_ref[...] = jnp.dot(_silu(c_ref[...]), w_ref[...], precision=HIGHEST,
                         preferred_element_type=F32) + b_ref[...]


def _ada_mod(c_pad, ada_w, ada_b):
    n = ada_w.shape[1]
    tn = ADA_TILE
    return pl.pallas_call(
        _ada_kernel,
        grid=(n // tn,),
        in_specs=[pl.BlockSpec((SUBLANES, D_MODEL), lambda j: (0, 0)),
                  pl.BlockSpec((D_MODEL, tn), lambda j: (0, j)),
                  pl.BlockSpec((1, tn), lambda j: (0, j))],
        out_specs=pl.BlockSpec((SUBLANES, tn), lambda j: (0, j)),
        out_shape=jax.ShapeDtypeStruct((SUBLANES, n), F32),
        compiler_params=_params(("arbitrary",)),
        name="ada_mod",
    )(c_pad, ada_w, ada_b.reshape(1, n))


_OFF = [0]
for _w in IN_SPLITS:
    _OFF.append(_OFF[-1] + _w)


def _inproj_kernel(x_ref, nw_ref, sh_ref, sc_ref, w_ref, wdt_ref, lbl_ref, dtb_ref,
                   qs_ref, kk_ref, lf_ref, vi_ref, gg_ref, zz_ref, xbc_ref, dt_ref):
    x = x_ref[0]
    ms = jnp.mean(x * x, axis=-1, keepdims=True)
    h = x * lax.rsqrt(ms + EPS) * nw_ref[...]
    h = h * (1.0 + sc_ref[0]) + sh_ref[0]
    hb = h.astype(BF16)

    def seg(k):
        cols = w_ref[_OFF[k]:_OFF[k + 1], :] if k < 6 else wdt_ref[...]
        return lax.dot_general(hb, cols, (((1,), (1,)), ((), ())), preferred_element_type=F32)

    lbl = lbl_ref[...]
    le = jnp.exp(lbl - jnp.max(lbl, axis=0, keepdims=True))
    lb = le[0:1, :] / jnp.sum(le, axis=0, keepdims=True)

    qs_ref[0] = _silu(seg(0)).astype(BF16)
    fg = lb + (1.0 - lb) * jax.nn.sigmoid(seg(1))
    kk_ref[0] = (1.0 - fg).astype(BF16)
    lf_ref[0] = jnp.log(fg)
    vi_ref[0] = seg(2).astype(BF16)
    gg_ref[0] = _silu(seg(3)).astype(BF16)
    zz_ref[0] = _silu(seg(4)).astype(BF16)
    xbc_ref[0] = seg(5).astype(BF16)
    dt_ref[0] = _softplus(seg(6) + dtb_ref[...])


def _inproj(x, nw, sh, sc, w_main, w_dt, lb_logits, dtb_pad):
    bsz, seqlen, d = x.shape
    tm = min(ROW_TILE, seqlen)
    nt = seqlen // tm
    tok = lambda w: pl.BlockSpec((1, tm, w), lambda b, i: (b, i, 0))
    full = lambda a: pl.BlockSpec(a.shape, lambda b, i: (0,) * a.ndim)
    mod = pl.BlockSpec((1, 1, d), lambda b, i: (b, 0, 0))
    shp = lambda w, dt: jax.ShapeDtypeStruct((bsz, seqlen, w), dt)
    return pl.pallas_call(
        _inproj_kernel,
        grid=(bsz, nt),
        in_specs=[tok(d), full(nw), mod, mod, full(w_main), full(w_dt), full(lb_logits), full(dtb_pad)],
        out_specs=[tok(HG_QF), tok(HG_QF), tok(HG_QF), tok(HG_WIDTH), tok(HG_WIDTH),
                   tok(SSM_WIDTH), tok(SSM_CONV_CH), tok(LANES)],
        out_shape=[shp(HG_QF, BF16), shp(HG_QF, BF16), shp(HG_QF, F32), shp(HG_WIDTH, BF16),
                   shp(HG_WIDTH, BF16), shp(SSM_WIDTH, BF16), shp(SSM_CONV_CH, BF16),
                   shp(LANES, F32)],
        compiler_params=_params(("arbitrary", "arbitrary")),
        name="inproj",
    )(x, nw, sh, sc, w_main, w_dt, lb_logits, dtb_pad)


def _hgrn_kernel(qs_ref, kk_ref, lf_ref, vi_ref, gg_ref, nw_ref, o_ref, st_ref):
    @pl.when(pl.program_id(1) == 0)
    def _():
        st_ref[...] = jnp.zeros_like(st_ref)

    c = HG_CHUNK
    tq = qs_ref.shape[1]
    nc = tq // c
    rows = lax.broadcasted_iota(jnp.int32, (nc, c, c), 1)
    cols = lax.broadcasted_iota(jnp.int32, (nc, c, c), 2)
    causal = rows >= cols
    tril = causal.astype(BF16)
    bmm = lambda a, bb, dims: lax.dot_general(a, bb, (dims, ((0,), (0,))), preferred_element_type=F32)
    for h in range(HG_HEADS):
        hs = slice(h * HG_DK, (h + 1) * HG_DK)
        chunked = lambda ref: ref[0, :, hs].reshape(nc, c, HG_DK)
        q = chunked(qs_ref).astype(F32)
        k = chunked(kk_ref).astype(F32)
        v = chunked(vi_ref)
        b3 = bmm(tril, _split3(chunked(lf_ref)), ((2,), (1,)))
        b = b3[:, :, :HG_DK] + b3[:, :, HG_DK:2 * HG_DK] + b3[:, :, 2 * HG_DK:]
        b_mid = b[:, c // 2 - 1:c // 2, :]
        b_last = b[:, c - 1:c, :]
        qa = (q * jnp.exp(b - b_mid)).astype(BF16)
        ka = (k * jnp.exp(b_mid - b)).astype(BF16)
        att = bmm(qa, ka, ((2,), (2,)))
        att = jnp.where(causal, att, 0.0).astype(BF16)
        o = bmm(att, v, ((2,), (1,)))
        kd = (k * jnp.exp(b_last - b)).astype(BF16)
        d_st = bmm(v, kd, ((1,), (1,)))
        decay = jnp.exp(b_last)
        st = st_ref[h]
        entering = []
        for ci in range(nc):
            entering.append(st.astype(BF16))
            st = st * decay[ci] + d_st[ci]
        st_ref[h] = st
        qb = (q * jnp.exp(b)).astype(BF16)
        o = o + bmm(qb, jnp.stack(entering), ((2,), (2,)))
        ms = jnp.mean(o * o, axis=-1, keepdims=True)
        y = o * lax.rsqrt(ms + EPS) * nw_ref[...] * chunked(gg_ref).astype(F32)
        o_ref[0, :, hs] = y.reshape(tq, HG_DV).astype(BF16)


def _hgrn(qs, kk, lf, vi, gg, nw):
    bsz, seqlen, width = qs.shape
    tq = min(SEQ_TILE, seqlen)
    blk = pl.BlockSpec((1, tq, width), lambda b, i: (b, i, 0))
    return pl.pallas_call(
        _hgrn_kernel,
        grid=(bsz, seqlen // tq),
        in_specs=[blk, blk, blk, blk, blk, pl.BlockSpec((1, HG_DV), lambda b, i: (0, 0))],
        out_specs=blk,
        out_shape=jax.ShapeDtypeStruct((bsz, seqlen, HG_WIDTH), BF16),
        scratch_shapes=[pltpu.VMEM((HG_HEADS, HG_DV, HG_DK), F32)],
        compiler_params=_params(("arbitrary", "arbitrary")),
        name="hgrn2",
    )(qs, kk, lf, vi, gg, nw)


def _ssd_kernel(xbc_ref, zz_ref, dt_ref, cw_ref, cb_ref, alog_ref, dsk_ref, nw_ref, ex_ref,
                y_ref, buf_ref, xc_ref, yc_ref, st_ref):
    tq = xbc_ref.shape[1]
    halo = SUBLANES

    @pl.when(pl.program_id(1) == 0)
    def _():
        buf_ref[0:halo, :] = jnp.zeros((halo, SSM_CONV_CH), F32)
        st_ref[...] = jnp.zeros_like(st_ref)

    buf_ref[halo:halo + tq, :] = xbc_ref[0].astype(F32)
    conv = cb_ref[...] + buf_ref[halo - 3:halo - 3 + tq, :] * cw_ref[0:1, :]
    for w in range(1, SSM_CONV):
        conv = conv + buf_ref[halo - 3 + w:halo - 3 + w + tq, :] * cw_ref[w:w + 1, :]
    buf_ref[0:halo, :] = buf_ref[tq:tq + halo, :]
    xc_ref[...] = _silu(conv)

    c = SSM_CHUNK
    p = SSM_HEAD_DIM
    nc = tq // c
    rows = lax.broadcasted_iota(jnp.int32, (nc, c, c), 1)
    cols = lax.broadcasted_iota(jnp.int32, (nc, c, c), 2)
    causal = rows >= cols
    tril = causal.astype(BF16)
    bmm = lambda a, bb, dims, **kw: lax.dot_general(a, bb, (dims, ((0,), (0,))),
                                                    preferred_element_type=F32, **kw)
    chunked = lambda val: val.reshape(nc, c, val.shape[-1])
    dt = chunked(dt_ref[0])
    da = dt * (-jnp.exp(alog_ref[...]))
    da3 = _split3(da)
    a3 = bmm(tril, da3, ((2,), (1,)))
    acum = a3[:, :, :LANES] + a3[:, :, LANES:2 * LANES] + a3[:, :, 2 * LANES:]
    at3 = bmm(da3, tril, ((1,), (2,)))
    acum_t = at3[:, :LANES, :] + at3[:, LANES:2 * LANES, :] + at3[:, 2 * LANES:, :]
    widen = lambda v: jnp.dot(_split3(v.reshape(tq, LANES)), ex_ref[...], preferred_element_type=F32)
    dt_w = widen(dt)
    decay_out_w = widen(jnp.exp(acum))
    decay_in_w = widen(jnp.exp(acum[:, c - 1:c, :] - acum))
    xdt_w = xc_ref[:, 0:SSM_WIDTH] * dt_w
    xdd_w = xdt_w * decay_in_w
    y_inter_groups = []
    for g in range(SSM_GROUPS):
        gs = slice(g * SSM_GROUP_WIDTH, (g + 1) * SSM_GROUP_WIDTH)
        bm = chunked(xc_ref[:, SSM_WIDTH + g * SSM_STATE:SSM_WIDTH + (g + 1) * SSM_STATE]).astype(BF16)
        cm = chunked(xc_ref[:, SSM_WIDTH + (SSM_GROUPS + g) * SSM_STATE:
                            SSM_WIDTH + (SSM_GROUPS + g + 1) * SSM_STATE]).astype(BF16)
        cb = bmm(cm, bm, ((2,), (2,)))
        d_st = bmm(chunked(xdd_w[:, gs]).astype(BF16), bm, ((1,), (1,)))
        a_last = []
        for hh in range(SSM_GROUP_HEADS):
            h = g * SSM_GROUP_HEADS + hh
            hs = slice(h * p, (h + 1) * p)
            seg = jnp.exp(jnp.where(causal, acum[:, :, h:h + 1] - acum_t[:, h:h + 1, :], -jnp.inf))
            y = bmm((cb * seg).astype(BF16), chunked(xdt_w[:, hs]).astype(BF16), ((2,), (1,)))
            yc_ref[:, hs] = y.reshape(tq, p)
            a_last.append(jnp.exp(acum_t[:, h:h + 1, c - 1:c]))
        st = [st_ref[g, hh * p:(hh + 1) * p, :] for hh in range(SSM_GROUP_HEADS)]
        entering = []
        for ci in range(nc):
            entering.append(jnp.concatenate(st, axis=0).astype(BF16))
            st = [st[hh] * a_last[hh][ci] + d_st[ci, hh * p:(hh + 1) * p, :] for hh in range(SSM_GROUP_HEADS)]
        for hh in range(SSM_GROUP_HEADS):
            st_ref[g, hh * p:(hh + 1) * p, :] = st[hh]
        y_inter_groups.append(bmm(cm, jnp.stack(entering), ((2,), (2,))).reshape(tq, SSM_GROUP_WIDTH))
    y_inter = jnp.concatenate(y_inter_groups, axis=-1)
    y = yc_ref[...] + decay_out_w * y_inter
    y = (y + dsk_ref[...] * xc_ref[:, 0:SSM_WIDTH]) * zz_ref[0].astype(F32)
    for g in range(SSM_GROUPS):
        gs = slice(g * SSM_GROUP_WIDTH, (g + 1) * SSM_GROUP_WIDTH)
        yg = y[:, gs]
        ms = jnp.mean(yg * yg, axis=-1, keepdims=True)
        y_ref[0, :, gs] = (yg * lax.rsqrt(ms + EPS) * nw_ref[:, gs]).astype(BF16)


def _ssd(xbc, zz, dt, conv_w, conv_b, alog_pad, dsk_wide, nw, expand3):
    bsz, seqlen, _ = xbc.shape
    tq = min(SEQ_TILE, seqlen)
    tok = lambda w: pl.BlockSpec((1, tq, w), lambda b, i: (b, i, 0))
    full = lambda a: pl.BlockSpec(a.shape, lambda b, i: (0,) * a.ndim)
    return pl.pallas_call(
        _ssd_kernel,
        grid=(bsz, seqlen // tq),
        in_specs=[tok(SSM_CONV_CH), tok(SSM_WIDTH), tok(LANES), full(conv_w), full(conv_b),
                  full(alog_pad), full(dsk_wide), full(nw), full(expand3)],
        out_specs=tok(SSM_WIDTH),
        out_shape=jax.ShapeDtypeStruct((bsz, seqlen, SSM_WIDTH), BF16),
        scratch_shapes=[pltpu.VMEM((tq + 2 * SUBLANES, SSM_CONV_CH), F32),
                        pltpu.VMEM((tq, SSM_CONV_CH), F32),
                        pltpu.VMEM((tq, SSM_WIDTH), F32),
                        pltpu.VMEM((SSM_GROUPS, SSM_GROUP_WIDTH, SSM_STATE), F32)],
        compiler_params=_params(("arbitrary", "arbitrary")),
        name="ssd",
    )(xbc, zz, dt, conv_w, conv_b, alog_pad, dsk_wide, nw, expand3)


def _outproj_kernel(x_ref, oh_ref, ys_ref, wo_ref, g1_ref, nw_ref, sh_ref, sc_ref, rw_ref, rb_ref,
                    x1_ref, h2_ref, idx_ref, gate_ref, cnt_ref):
    first = jnp.logical_and(pl.program_id(0) == 0, pl.program_id(1) == 0)

    @pl.when(first)
    def _():
        cnt_ref[...] = jnp.zeros_like(cnt_ref)

    mix = jnp.dot(oh_ref[0], wo_ref[0:HG_WIDTH, :], preferred_element_type=F32)
    mix = mix + jnp.dot(ys_ref[0], wo_ref[HG_WIDTH:, :], preferred_element_type=F32)
    x1 = x_ref[0] + g1_ref[0] * mix
    x1_ref[0] = x1
    ms = jnp.mean(x1 * x1, axis=-1, keepdims=True)
    h2 = x1 * lax.rsqrt(ms + EPS) * nw_ref[...]
    h2 = h2 * (1.0 + sc_ref[0]) + sh_ref[0]
    h2_ref[0] = h2
    h_hi = h2.astype(BF16)
    h_lo = (h2 - h_hi.astype(F32)).astype(BF16)
    part = jnp.dot(h_hi, rw_ref[...], preferred_element_type=F32)
    logits = (part[:, :LANES] + part[:, LANES:]
              + jnp.dot(h_lo, rw_ref[:, :LANES], preferred_element_type=F32)) + rb_ref[...]
    lane = lax.broadcasted_iota(jnp.int32, logits.shape, 1).astype(F32)
    idx_out = jnp.zeros(logits.shape, F32)
    val_out = jnp.zeros(logits.shape, F32)
    sel = jnp.zeros(logits.shape, F32)
    work = logits
    top = None
    denom = None
    for k in range(TOP_K):
        m = jnp.max(work, axis=-1, keepdims=True)
        am = jnp.min(jnp.where(work == m, lane, float(LANES)), axis=-1, keepdims=True)
        hit = lane == am
        work = jnp.where(hit, -jnp.inf, work)
        sel = jnp.where(hit, 1.0, sel)
        if k == 0:
            top = m
        e = jnp.exp(m - top)
        denom = e if k == 0 else denom + e
        idx_out = jnp.where(lane == float(k), am, idx_out)
        val_out = jnp.where(lane == float(k), e, val_out)
    idx_ref[0] = idx_out.astype(jnp.int32)
    gate_ref[0] = val_out / denom
    cnt_ref[0:1, :] += jnp.sum(sel, axis=0, keepdims=True)


def _outproj(x, o_hg, y_ssd, w_out_b, g1, nw, sh, sc, rw_pad, rb_pad):
    bsz, seqlen, d = x.shape
    tm = min(ROW_TILE, seqlen)
    tok = lambda w: pl.BlockSpec((1, tm, w), lambda b, i: (b, i, 0))
    full = lambda a: pl.BlockSpec(a.shape, lambda b, i: (0,) * a.ndim)
    mod = pl.BlockSpec((1, 1, d), lambda b, i: (b, 0, 0))
    shp = lambda w, dt: jax.ShapeDtypeStruct((bsz, seqlen, w), dt)
    return pl.pallas_call(
        _outproj_kernel,
        grid=(bsz, seqlen // tm),
        in_specs=[tok(d), tok(HG_WIDTH), tok(SSM_WIDTH), full(w_out_b), mod, full(nw), mod, mod,
                  full(rw_pad), full(rb_pad)],
        out_specs=[tok(d), tok(d), tok(LANES), tok(LANES),
                   pl.BlockSpec((SUBLANES, LANES), lambda b, i: (0, 0))],
        out_shape=[shp(d, F32), shp(d, F32), shp(LANES, jnp.int32), shp(LANES, F32),
                   jax.ShapeDtypeStruct((SUBLANES, LANES), F32)],
        compiler_params=_params(("arbitrary", "arbitrary")),
        name="outproj_router",
    )(x, o_hg, y_ssd, w_out_b, g1, nw, sh, sc, rw_pad, rb_pad)


def _route_kernel(idx_ref, pstart_ref, dest_ref, carry_ref):
    @pl.when(pl.program_id(0) == 0)
    def _():
        carry_ref[...] = jnp.zeros_like(carry_ref)

    idx = idx_ref[...]
    tt = idx.shape[0]
    lane = lax.broadcasted_iota(jnp.int32, idx.shape, 1)
    hits = [lane == idx[:, k:k + 1] for k in range(TOP_K)]
    sel = jnp.zeros(idx.shape, F32)
    for hit in hits:
        sel = jnp.where(hit, 1.0, sel)
    rows = lax.broadcasted_iota(jnp.int32, (tt, tt), 0)
    cols = lax.broadcasted_iota(jnp.int32, (tt, tt), 1)
    before = (rows > cols).astype(BF16)
    rank = jnp.dot(before, sel.astype(BF16), preferred_element_type=F32) + carry_ref[0:1, :]
    carry_ref[0:1, :] += jnp.sum(sel, axis=0, keepdims=True)
    dense = pstart_ref[...] + rank
    pack = LANES // TOP_K
    slot = (lax.broadcasted_iota(jnp.int32, idx.shape, 0) % pack) * TOP_K
    spread = jnp.zeros(idx.shape, F32)
    for k, hit in enumerate(hits):
        dk = jnp.sum(jnp.where(hit, dense, 0.0), axis=-1, keepdims=True)
        spread = jnp.where(lane == slot + k, dk, spread)
    out_rows = tt // pack
    merge = (lax.broadcasted_iota(jnp.int32, (out_rows, tt), 0)
             == lax.broadcasted_iota(jnp.int32, (out_rows, tt), 1) // pack).astype(BF16)
    m3 = jnp.dot(merge, _split3(spread), preferred_element_type=F32)
    dest_ref[...] = (m3[:, :LANES] + m3[:, LANES:2 * LANES] + m3[:, 2 * LANES:]).astype(jnp.int32)


def _route(idx_pad, pstart_row):
    t = idx_pad.shape[0]
    tt = min(ROUTE_TILE, t)
    return pl.pallas_call(
        _route_kernel,
        grid=(t // tt,),
        in_specs=[pl.BlockSpec((tt, LANES), lambda i: (i, 0)),
                  pl.BlockSpec((1, LANES), lambda i: (0, 0))],
        out_specs=pl.BlockSpec((tt * TOP_K // LANES, LANES), lambda i: (i, 0)),
        out_shape=jax.ShapeDtypeStruct((t * TOP_K // LANES, LANES), jnp.int32),
        scratch_shapes=[pltpu.VMEM((SUBLANES, LANES), F32)],
        compiler_params=_params(("arbitrary",)),
        name="route_rank",
    )(idx_pad, pstart_row)


def _invert_kernel(dest_ref, zeros_ref, rowtok_ref, sem):
    i = pl.program_id(0)
    per_step = dest_ref.shape[0] // TOP_K // pl.num_programs(0)

    @pl.when(i == 0)
    def _():
        fill = pltpu.make_async_copy(zeros_ref, rowtok_ref, sem)
        fill.start()
        fill.wait()

    def tok_body(n, carry):
        tok = i * per_step + n
        for k in range(TOP_K):
            rowtok_ref[dest_ref[tok * TOP_K + k]] = tok
        return carry
    lax.fori_loop(0, per_step, tok_body, 0, unroll=8)


def _invert(dest_flat, n_rows):
    t = dest_flat.shape[0] // TOP_K
    return pl.pallas_call(
        _invert_kernel,
        grid=(max(t // INVERT_TILE, 1),),
        in_specs=[pl.BlockSpec(memory_space=pltpu.SMEM), pl.BlockSpec(memory_space=pl.ANY)],
        out_specs=pl.BlockSpec(memory_space=pltpu.SMEM),
        out_shape=jax.ShapeDtypeStruct((n_rows,), jnp.int32),
        scratch_shapes=[pltpu.SemaphoreType.DMA(())],
        compiler_params=_params(("arbitrary",)),
        name="invert_route",
    )(dest_flat, jnp.zeros((n_rows,), jnp.int32))


def _expert_kernel(be_ref, nact_ref, nxt_ref, rowtok_ref, h_ref, w1_ref, b1_ref, w2_ref, b2_ref, y_ref,
                   xbuf0_ref, xbuf1_ref, xbuf2_ref, w1s_ref, w2s_ref, w1b_ref, w2b_ref, sem, wsem):
    j = pl.program_id(0)
    n_act = nact_ref[0]
    active = j < n_act
    expert = be_ref[j]
    fresh = jnp.logical_or(j == 0, expert != be_ref[jnp.maximum(j - 1, 0)])
    phase = j % GATHER_DEPTH
    bufs = (xbuf0_ref, xbuf1_ref, xbuf2_ref)

    def issue(block, s):
        base = jnp.minimum(block, n_act - 1) * MOE_ROWS
        for r in range(MOE_ROWS):
            pltpu.make_async_copy(h_ref.at[pl.ds(rowtok_ref[base + r], 1), :],
                                  bufs[s].at[pl.ds(r, 1), :], sem.at[s]).start(priority=1)

    def wait(s):
        pltpu.make_async_copy(h_ref.at[pl.ds(0, MOE_ROWS), :], bufs[s], sem.at[s]).wait()

    def weight_copies(e):
        return (pltpu.make_async_copy(w1_ref.at[e], w1s_ref, wsem.at[0]),
                pltpu.make_async_copy(w2_ref.at[e], w2s_ref, wsem.at[1]))

    @pl.when(j == 0)
    def _():
        for cp in weight_copies(expert):
            cp.start()
        for s in range(GATHER_DEPTH - 1):
            issue(s, s)

    @pl.when(jnp.logical_and(active, fresh))
    def _():
        for cp in weight_copies(expert):
            cp.wait()
        w1b_ref[...] = w1s_ref[...].astype(BF16)
        w2b_ref[...] = w2s_ref[...].astype(BF16)

        @pl.when(nxt_ref[expert] < N_EXPERTS)
        def _():
            for cp in weight_copies(nxt_ref[expert]):
                cp.start()

    for s in range(GATHER_DEPTH):
        @pl.when(jnp.logical_and(j < n_act + GATHER_DEPTH - 1, phase == s))
        def _(s=s):
            wait(s)

        @pl.when(jnp.logical_and(active, phase == s))
        def _(s=s):
            issue(j + GATHER_DEPTH - 1, (s + GATHER_DEPTH - 1) % GATHER_DEPTH)
            hb = jnp.dot(bufs[s][...].astype(BF16), w1b_ref[...], preferred_element_type=F32) + b1_ref[0]
            glu = jnp.minimum(hb[:, :D_FF], SWIGLU_LIMIT)
            lin = jnp.clip(hb[:, D_FF:], -SWIGLU_LIMIT, SWIGLU_LIMIT)
            act = glu * jax.nn.sigmoid(SWIGLU_ALPHA * glu) * (lin + 1.0)
            y_ref[...] = jnp.dot(act.astype(BF16), w2b_ref[...], preferred_element_type=F32) + b2_ref[0]


def _experts(block_e, n_act, next_e, row_tok, h2, w1, b1, w2, b2):
    n_rows = row_tok.shape[0]
    d = h2.shape[1]
    nb = block_e.shape[0]
    row_map = lambda j, be, na, nx, rt: (jnp.maximum(jnp.minimum(j, na[0] - 1), 0), 0)
    exp_map = lambda j, be, na, nx, rt: (be[j], 0, 0)
    return pl.pallas_call(
        _expert_kernel,
        grid_spec=pltpu.PrefetchScalarGridSpec(
            num_scalar_prefetch=4,
            grid=(nb,),
            in_specs=[pl.BlockSpec(memory_space=pl.ANY),
                      pl.BlockSpec(memory_space=pl.ANY),
                      pl.BlockSpec((1, 1, 2 * D_FF), exp_map),
                      pl.BlockSpec(memory_space=pl.ANY),
                      pl.BlockSpec((1, 1, d), exp_map)],
            out_specs=pl.BlockSpec((MOE_ROWS, d), row_map),
            scratch_shapes=[pltpu.VMEM((MOE_ROWS, d), F32)] * GATHER_DEPTH + [
                pltpu.VMEM((d, 2 * D_FF), F32), pltpu.VMEM((D_FF, d), F32),
                pltpu.VMEM((d, 2 * D_FF), BF16), pltpu.VMEM((D_FF, d), BF16),
                pltpu.SemaphoreType.DMA((GATHER_DEPTH,)), pltpu.SemaphoreType.DMA((2,))],
        ),
        out_shape=jax.ShapeDtypeStruct((n_rows, d), F32),
        compiler_params=_params(("arbitrary",)),
        name="expert_ffn",
    )(block_e, n_act, next_e, row_tok, h2, w1, b1.reshape(N_EXPERTS, 1, 2 * D_FF), w2,
      b2.reshape(N_EXPERTS, 1, d))


def _combine_kernel(dest_ref, x1_ref, gate_ref, g2_ref, fw_ref, y_ref, o_ref,
                    buf0_ref, buf1_ref, buf2_ref, sem):
    i = pl.program_id(0)
    n = pl.num_programs(0)
    gt = x1_ref.shape[0]
    bufs = (buf0_ref, buf1_ref, buf2_ref)
    depth = len(bufs)

    def issue(tile, s):
        base = jnp.minimum(tile, n - 1) * (gt * TOP_K)
        for j in range(gt):
            for k in range(TOP_K):
                pltpu.make_async_copy(y_ref.at[pl.ds(dest_ref[base + j * TOP_K + k], 1), :],
                                      bufs[s].at[k, pl.ds(j, 1), :], sem.at[s]).start(priority=k % 2)

    def wait(s):
        for k in range(TOP_K):
            pltpu.make_async_copy(y_ref.at[pl.ds(0, gt), :], bufs[s].at[k], sem.at[s]).wait()

    @pl.when(i == 0)
    def _():
        for s in range(depth - 1):
            issue(s, s)

    for s in range(depth):
        @pl.when(i % depth == s)
        def _(s=s):
            wait(s)
            issue(i + depth - 1, (s + depth - 1) % depth)
            gates = gate_ref[...]
            moe = gates[:, 0:1] * bufs[s][0]
            for k in range(1, TOP_K):
                moe = moe + gates[:, k:k + 1] * bufs[s][k]
            x2 = x1_ref[...] + g2_ref[0] * moe
            ms = jnp.mean(x2 * x2, axis=-1, keepdims=True)
            o_ref[...] = x2 * lax.rsqrt(ms + EPS) * fw_ref[...]

        @pl.when(jnp.logical_and(i == n - 1, i % depth == s))
        def _(s=s):
            for ahead in range(1, depth):
                wait((s + ahead) % depth)


def _combine(dest_flat, x1, gate_pad, g2, fw, yout, seqlen):
    t, d = x1.shape
    gt = min(GATHER_TILE, seqlen)
    per_seq = seqlen // gt
    return pl.pallas_call(
        _combine_kernel,
        grid_spec=pltpu.PrefetchScalarGridSpec(
            num_scalar_prefetch=1,
            grid=(t // gt,),
            in_specs=[pl.BlockSpec((gt, d), lambda i, dst: (i, 0)),
                      pl.BlockSpec((gt, LANES), lambda i, dst: (i, 0)),
                      pl.BlockSpec((1, 1, d), lambda i, dst: (i // per_seq, 0, 0)),
                      pl.BlockSpec((1, d), lambda i, dst: (0, 0)),
                      pl.BlockSpec(memory_space=pl.ANY)],
            out_specs=pl.BlockSpec((gt, d), lambda i, dst: (i, 0)),
            scratch_shapes=[pltpu.VMEM((TOP_K, gt, d), F32)] * 3 + [pltpu.SemaphoreType.DMA((3,))],
        ),
        out_shape=jax.ShapeDtypeStruct((t, d), F32),
        compiler_params=_params(("arbitrary",)),
        name="combine_norm",
    )(dest_flat, x1, gate_pad, g2, fw, yout)


def _head_expand():
    head_of_lane = jnp.arange(SSM_WIDTH) // SSM_HEAD_DIM
    one = (jnp.arange(LANES)[:, None] == head_of_lane[None, :]).astype(BF16)
    return jnp.tile(one, (3, 1))


def _pad_lanes(v, fill=0.0):
    v = v.reshape(1, -1).astype(F32)
    return jnp.pad(v, ((0, 0), (0, LANES - v.shape[1])), constant_values=fill)


def kernel(x, c, ada_w, ada_b, norm1_w, w_in, hg_lb_logits, hg_norm_w, ssm_a_log, ssm_dt_bias, ssm_d,
           ssm_conv_w, ssm_conv_b, ssm_norm_w, w_out, norm2_w, router_w, router_b, exp_w1, exp_b1,
           exp_w2, exp_b2, final_norm_w):
    bsz, seqlen, d = x.shape
    t = bsz * seqlen
    l = 0

    c_pad = jnp.pad(c, ((0, SUBLANES - bsz), (0, 0)))
    mod = _ada_mod(c_pad, ada_w[l], ada_b[l])[:bsz]
    sh1, sc1, g1, sh2, sc2, g2 = [m.reshape(bsz, 1, d) for m in jnp.split(mod, N_MOD, axis=-1)]

    w_in_t = jnp.swapaxes(w_in[l], 0, 1)
    w_main = w_in_t.astype(BF16)
    w_dt = jnp.pad(w_in_t[_OFF[6]:].astype(BF16), ((0, LANES - SSM_HEADS), (0, 0)))
    qs, kk, lf, vi, gg, zz, xbc, dt = _inproj(
        x, norm1_w[l].reshape(1, d), sh1, sc1, w_main, w_dt, hg_lb_logits, _pad_lanes(ssm_dt_bias[l]))

    o_hg = _hgrn(qs, kk, lf, vi, gg, hg_norm_w[l].reshape(1, HG_DV))
    y_ssd = _ssd(xbc, zz, dt, ssm_conv_w[l], ssm_conv_b[l].reshape(1, SSM_CONV_CH),
                 _pad_lanes(ssm_a_log[l]), jnp.repeat(ssm_d[l], SSM_HEAD_DIM).reshape(1, SSM_WIDTH),
                 ssm_norm_w[l].reshape(1, SSM_WIDTH), _head_expand())

    rw_pad = jnp.pad(router_w[l], ((0, 0), (0, LANES - N_EXPERTS)))
    rw_hi = rw_pad.astype(BF16)
    rw_pad = jnp.concatenate([rw_hi, (rw_pad - rw_hi.astype(F32)).astype(BF16)], axis=1)
    rb_pad = _pad_lanes(router_b[l], NEG_BIG)
    x1, h2, idx_pad, gate_pad, counts = _outproj(
        x, o_hg, y_ssd, w_out[l].astype(BF16), g1, norm2_w[l].reshape(1, d), sh2, sc2, rw_pad, rb_pad)

    cnt = counts[0, :N_EXPERTS].astype(jnp.int32)
    blocks_e = (cnt + MOE_ROWS - 1) // MOE_ROWS
    blk_end = jnp.cumsum(blocks_e)
    pstart = (blk_end - blocks_e) * MOE_ROWS
    n_blocks = (t * TOP_K) // MOE_ROWS + N_EXPERTS
    n_rows = n_blocks * MOE_ROWS
    n_steps = n_blocks + GATHER_DEPTH - 1
    block_e = jnp.minimum(jnp.sum(blk_end[None, :] <= jnp.arange(n_steps)[:, None], axis=1),
                          N_EXPERTS - 1).astype(jnp.int32)
    n_act = blk_end[-1:].astype(jnp.int32)
    owner = jnp.where(blocks_e > 0, jnp.arange(N_EXPERTS), N_EXPERTS)
    next_e = jnp.concatenate([lax.cummin(owner, reverse=True)[1:],
                              jnp.full((1,), N_EXPERTS)]).astype(jnp.int32)

    dest_flat = _route(idx_pad.reshape(t, LANES), _pad_lanes(pstart)).reshape(t * TOP_K)

    row_tok = _invert(dest_flat, n_rows)
    yout = _experts(block_e, n_act, next_e, row_tok, h2.reshape(t, d), exp_w1[l], exp_b1[l], exp_w2[l], exp_b2[l])
    out = _combine(dest_flat, x1.reshape(t, d), gate_pad.reshape(t, LANES), g2,
                   final_norm_w.reshape(1, d), yout, seqlen)
    return out.reshape(bsz, seqlen, d)
```

```python
import functools

import jax
import jax.numpy as jnp
from jax import lax
from jax.experimental import pallas as pl
from jax.experimental.pallas import tpu as pltpu

F32 = jnp.float32
BF16 = jnp.bfloat16

EPS = 1e-6
D_MODEL = 1024
HG_HEADS = 4
HG_DK = 128
HG_DV = 128
HG_QF = HG_HEADS * HG_DK
HG_WIDTH = HG_HEADS * HG_DV
HG_CHUNK = 64
SSM_HEADS = 8
SSM_HEAD_DIM = 64
SSM_WIDTH = SSM_HEADS * SSM_HEAD_DIM
SSM_GROUPS = 2
SSM_GROUP_HEADS = SSM_HEADS // SSM_GROUPS
SSM_GROUP_WIDTH = SSM_WIDTH // SSM_GROUPS
SSM_STATE = 128
SSM_CONV = 4
SSM_CONV_CH = SSM_WIDTH + 2 * SSM_GROUPS * SSM_STATE
SSM_CHUNK = 128
IN_SPLITS = (HG_QF, HG_QF, HG_WIDTH, HG_WIDTH, SSM_WIDTH, SSM_CONV_CH, SSM_HEADS)
IN_COLS = sum(IN_SPLITS)
N_EXPERTS = 32
TOP_K = 4
D_FF = 1024
SWIGLU_LIMIT = 7.0
SWIGLU_ALPHA = 1.702
N_MOD = 6

LANES = 128
SUBLANES = 8
VMEM_LIMIT = 56 * 1024 * 1024

ROW_TILE = 1024
ADA_TILE = 2048
ROUTE_TILE = 512
SEQ_TILE = 1024
MOE_ROWS = 256
GATHER_TILE = 256
INVERT_TILE = 1024
GATHER_DEPTH = 3
NEG_BIG = -1e30


def _silu(v):
    return v * jax.nn.sigmoid(v)


def _softplus(v):
    return jnp.maximum(v, 0.0) + jnp.log1p(jnp.exp(-jnp.abs(v)))


def _split3(v):
    hi = v.astype(BF16)
    r1 = v - hi.astype(F32)
    mid = r1.astype(BF16)
    lo = (r1 - mid.astype(F32)).astype(BF16)
    return jnp.concatenate([hi, mid, lo], axis=-1)


def _params(sem):
    return pltpu.CompilerParams(dimension_semantics=sem, vmem_limit_bytes=VMEM_LIMIT)


def _ada_kernel(c_ref, w_ref, b_ref, o_ref):
    o_ref[...] = jnp.dot(_silu(c_ref[...]).astype(BF16), w_ref[...].astype(BF16),
                         preferred_element_type=F32) + b_ref[...]


def _ada_mod(c_pad, ada_w, ada_b):
    n = ada_w.shape[1]
    tn = ADA_TILE
    return pl.pallas_call(
        _ada_kernel,
        grid=(n // tn,),
        in_specs=[pl.BlockSpec((SUBLANES, D_MODEL), lambda j: (0, 0)),
                  pl.BlockSpec((D_MODEL, tn), lambda j: (0, j)),
                  pl.BlockSpec((1, tn), lambda j: (0, j))],
        out_specs=pl.BlockSpec((SUBLANES, tn), lambda j: (0, j)),
        out_shape=jax.ShapeDtypeStruct((SUBLANES, n), F32),
        compiler_params=_params(("arbitrary",)),
        name="ada_mod",
    )(c_pad, ada_w, ada_b.reshape(1, n))


_OFF = [0]
for _w in IN_SPLITS:
    _OFF.append(_OFF[-1] + _w)


def _inproj_kernel(x_ref, nw_ref, sh_ref, sc_ref, w_ref, wdt_ref, lbl_ref, dtb_ref,
                   qs_ref, kk_ref, lf_ref, vi_ref, gg_ref, zz_ref, xbc_ref, dt_ref):
    x = x_ref[0]
    ms = jnp.mean(x * x, axis=-1, keepdims=True)
    h = x * lax.rsqrt(ms + EPS) * nw_ref[...]
    h = h * (1.0 + sc_ref[0]) + sh_ref[0]
    hb = h.astype(BF16)

    def seg(k):
        cols = w_ref[_OFF[k]:_OFF[k + 1], :] if k < 6 else wdt_ref[...]
        return lax.dot_general(hb, cols, (((1,), (1,)), ((), ())), preferred_element_type=F32)

    lbl = lbl_ref[...]
    le = jnp.exp(lbl - jnp.max(lbl, axis=0, keepdims=True))
    lb = le[0:1, :] / jnp.sum(le, axis=0, keepdims=True)

    qs_ref[0] = _silu(seg(0)).astype(BF16)
    fg = lb + (1.0 - lb) * jax.nn.sigmoid(seg(1))
    kk_ref[0] = (1.0 - fg).astype(BF16)
    lf_ref[0] = jnp.log(fg)
    vi_ref[0] = seg(2).astype(BF16)
    gg_ref[0] = _silu(seg(3)).astype(BF16)
    zz_ref[0] = _silu(seg(4)).astype(BF16)
    xbc_ref[0] = seg(5).astype(BF16)
    dt_ref[0] = _softplus(seg(6) + dtb_ref[...])


def _inproj(x, nw, sh, sc, w_main, w_dt, lb_logits, dtb_pad):
    bsz, seqlen, d = x.shape
    tm = min(ROW_TILE, seqlen)
    nt = seqlen // tm
    tok = lambda w: pl.BlockSpec((1, tm, w), lambda b, i: (b, i, 0))
    full = lambda a: pl.BlockSpec(a.shape, lambda b, i: (0,) * a.ndim)
    mod = pl.BlockSpec((1, 1, d), lambda b, i: (b, 0, 0))
    shp = lambda w, dt: jax.ShapeDtypeStruct((bsz, seqlen, w), dt)
    return pl.pallas_call(
        _inproj_kernel,
        grid=(bsz, nt),
        in_specs=[tok(d), full(nw), mod, mod, full(w_main), full(w_dt), full(lb_logits), full(dtb_pad)],
        out_specs=[tok(HG_QF), tok(HG_QF), tok(HG_QF), tok(HG_WIDTH), tok(HG_WIDTH),
                   tok(SSM_WIDTH), tok(SSM_CONV_CH), tok(LANES)],
        out_shape=[shp(HG_QF, BF16), shp(HG_QF, BF16), shp(HG_QF, F32), shp(HG_WIDTH, BF16),
                   shp(HG_WIDTH, BF16), shp(SSM_WIDTH, BF16), shp(SSM_CONV_CH, BF16),
                   shp(LANES, F32)],
        compiler_params=_params(("arbitrary", "arbitrary")),
        name="inproj",
    )(x, nw, sh, sc, w_main, w_dt, lb_logits, dtb_pad)


def _hgrn_kernel(qs_ref, kk_ref, lf_ref, vi_ref, gg_ref, nw_ref, o_ref, st_ref):
    @pl.when(pl.program_id(1) == 0)
    def _():
        st_ref[...] = jnp.zeros_like(st_ref)

    c = HG_CHUNK
    tq = qs_ref.shape[1]
    nc = tq // c
    rows = lax.broadcasted_iota(jnp.int32, (nc, c, c), 1)
    cols = lax.broadcasted_iota(jnp.int32, (nc, c, c), 2)
    causal = rows >= cols
    tril = causal.astype(BF16)
    bmm = lambda a, bb, dims: lax.dot_general(a, bb, (dims, ((0,), (0,))), preferred_element_type=F32)
    for h in range(HG_HEADS):
        hs = slice(h * HG_DK, (h + 1) * HG_DK)
        chunked = lambda ref: ref[0, :, hs].reshape(nc, c, HG_DK)
        q = chunked(qs_ref).astype(F32)
        k = chunked(kk_ref).astype(F32)
        v = chunked(vi_ref)
        b3 = bmm(tril, _split3(chunked(lf_ref)), ((2,), (1,)))
        b = b3[:, :, :HG_DK] + b3[:, :, HG_DK:2 * HG_DK] + b3[:, :, 2 * HG_DK:]
        b_mid = b[:, c // 2 - 1:c // 2, :]
        b_last = b[:, c - 1:c, :]
        qa = (q * jnp.exp(b - b_mid)).astype(BF16)
        ka = (k * jnp.exp(b_mid - b)).astype(BF16)
        att = bmm(qa, ka, ((2,), (2,)))
        att = jnp.where(causal, att, 0.0).astype(BF16)
        o = bmm(att, v, ((2,), (1,)))
        kd = (k * jnp.exp(b_last - b)).astype(BF16)
        d_st = bmm(v, kd, ((1,), (1,)))
        decay = jnp.exp(b_last)
        st = st_ref[h]
        entering = []
        for ci in range(nc):
            entering.append(st.astype(BF16))
            st = st * decay[ci] + d_st[ci]
        st_ref[h] = st
        qb = (q * jnp.exp(b)).astype(BF16)
        o = o + bmm(qb, jnp.stack(entering), ((2,), (2,)))
        ms = jnp.mean(o * o, axis=-1, keepdims=True)
        y = o * lax.rsqrt(ms + EPS) * nw_ref[...] * chunked(gg_ref).astype(F32)
        o_ref[0, :, hs] = y.reshape(tq, HG_DV).astype(BF16)


def _hgrn(qs, kk, lf, vi, gg, nw):
    bsz, seqlen, width = qs.shape
    tq = min(SEQ_TILE, seqlen)
    blk = pl.BlockSpec((1, tq, width), lambda b, i: (b, i, 0))
    return pl.pallas_call(
        _hgrn_kernel,
        grid=(bsz, seqlen // tq),
        in_specs=[blk, blk, blk, blk, blk, pl.BlockSpec((1, HG_DV), lambda b, i: (0, 0))],
        out_specs=blk,
        out_shape=jax.ShapeDtypeStruct((bsz, seqlen, HG_WIDTH), BF16),
        scratch_shapes=[pltpu.VMEM((HG_HEADS, HG_DV, HG_DK), F32)],
        compiler_params=_params(("arbitrary", "arbitrary")),
        name="hgrn2",
    )(qs, kk, lf, vi, gg, nw)


def _ssd_kernel(xbc_ref, zz_ref, dt_ref, cw_ref, cb_ref, alog_ref, dsk_ref, nw_ref, ex_ref,
                y_ref, buf_ref, xc_ref, yc_ref, st_ref):
    tq = xbc_ref.shape[1]
    halo = SUBLANES

    @pl.when(pl.program_id(1) == 0)
    def _():
        buf_ref[0:halo, :] = jnp.zeros((halo, SSM_CONV_CH), F32)
        st_ref[...] = jnp.zeros_like(st_ref)

    buf_ref[halo:halo + tq, :] = xbc_ref[0].astype(F32)
    conv = cb_ref[...] + buf_ref[halo - 3:halo - 3 + tq, :] * cw_ref[0:1, :]
    for w in range(1, SSM_CONV):
        conv = conv + buf_ref[halo - 3 + w:halo - 3 + w + tq, :] * cw_ref[w:w + 1, :]
    buf_ref[0:halo, :] = buf_ref[tq:tq + halo, :]
    xc_ref[...] = _silu(conv)

    c = SSM_CHUNK
    p = SSM_HEAD_DIM
    nc = tq // c
    rows = lax.broadcasted_iota(jnp.int32, (nc, c, c), 1)
    cols = lax.broadcasted_iota(jnp.int32, (nc, c, c), 2)
    causal = rows >= cols
    tril = causal.astype(BF16)
    bmm = lambda a, bb, dims, **kw: lax.dot_general(a, bb, (dims, ((0,), (0,))),
                                                    preferred_element_type=F32, **kw)
    chunked = lambda val: val.reshape(nc, c, val.shape[-1])
    dt = chunked(dt_ref[0])
    da = dt * (-jnp.exp(alog_ref[...]))
    da3 = _split3(da)
    a3 = bmm(tril, da3, ((2,), (1,)))
    acum = a3[:, :, :LANES] + a3[:, :, LANES:2 * LANES] + a3[:, :, 2 * LANES:]
    at3 = bmm(da3, tril, ((1,), (2,)))
    acum_t = at3[:, :LANES, :] + at3[:, LANES:2 * LANES, :] + at3[:, 2 * LANES:, :]
    widen = lambda v: jnp.dot(_split3(v.reshape(tq, LANES)), ex_ref[...], preferred_element_type=F32)
    dt_w = widen(dt)
    decay_out_w = widen(jnp.exp(acum))
    decay_in_w = widen(jnp.exp(acum[:, c - 1:c, :] - acum))
    xdt_w = xc_ref[:, 0:SSM_WIDTH] * dt_w
    xdd_w = xdt_w * decay_in_w
    y_inter_groups = []
    for g in range(SSM_GROUPS):
        gs = slice(g * SSM_GROUP_WIDTH, (g + 1) * SSM_GROUP_WIDTH)
        bm = chunked(xc_ref[:, SSM_WIDTH + g * SSM_STATE:SSM_WIDTH + (g + 1) * SSM_STATE]).astype(BF16)
        cm = chunked(xc_ref[:, SSM_WIDTH + (SSM_GROUPS + g) * SSM_STATE:
                            SSM_WIDTH + (SSM_GROUPS + g + 1) * SSM_STATE]).astype(BF16)
        cb = bmm(cm, bm, ((2,), (2,)))
        d_st = bmm(chunked(xdd_w[:, gs]).astype(BF16), bm, ((1,), (1,)))
        a_last = []
        for hh in range(SSM_GROUP_HEADS):
            h = g * SSM_GROUP_HEADS + hh
            hs = slice(h * p, (h + 1) * p)
            seg = jnp.exp(jnp.where(causal, acum[:, :, h:h + 1] - acum_t[:, h:h + 1, :], -jnp.inf))
            y = bmm((cb * seg).astype(BF16), chunked(xdt_w[:, hs]).astype(BF16), ((2,), (1,)))
            yc_ref[:, hs] = y.reshape(tq, p)
            a_last.append(jnp.exp(acum_t[:, h:h + 1, c - 1:c]))
        st = [st_ref[g, hh * p:(hh + 1) * p, :] for hh in range(SSM_GROUP_HEADS)]
        entering = []
        for ci in range(nc):
            entering.append(jnp.concatenate(st, axis=0).astype(BF16))
            st = [st[hh] * a_last[hh][ci] + d_st[ci, hh * p:(hh + 1) * p, :] for hh in range(SSM_GROUP_HEADS)]
        for hh in range(SSM_GROUP_HEADS):
            st_ref[g, hh * p:(hh + 1) * p, :] = st[hh]
        y_inter_groups.append(bmm(cm, jnp.stack(entering), ((2,), (2,))).reshape(tq, SSM_GROUP_WIDTH))
    y_inter = jnp.concatenate(y_inter_groups, axis=-1)
    y = yc_ref[...] + decay_out_w * y_inter
    y = (y + dsk_ref[...] * xc_ref[:, 0:SSM_WIDTH]) * zz_ref[0].astype(F32)
    for g in range(SSM_GROUPS):
        gs = slice(g * SSM_GROUP_WIDTH, (g + 1) * SSM_GROUP_WIDTH)
        yg = y[:, gs]
        ms = jnp.mean(yg * yg, axis=-1, keepdims=True)
        y_ref[0, :, gs] = (yg * lax.rsqrt(ms + EPS) * nw_ref[:, gs]).astype(BF16)


def _ssd(xbc, zz, dt, conv_w, conv_b, alog_pad, dsk_wide, nw, expand3):
    bsz, seqlen, _ = xbc.shape
    tq = min(SEQ_TILE, seqlen)
    tok = lambda w: pl.BlockSpec((1, tq, w), lambda b, i: (b, i, 0))
    full = lambda a: pl.BlockSpec(a.shape, lambda b, i: (0,) * a.ndim)
    return pl.pallas_call(
        _ssd_kernel,
        grid=(bsz, seqlen // tq),
        in_specs=[tok(SSM_CONV_CH), tok(SSM_WIDTH), tok(LANES), full(conv_w), full(conv_b),
                  full(alog_pad), full(dsk_wide), full(nw), full(expand3)],
        out_specs=tok(SSM_WIDTH),
        out_shape=jax.ShapeDtypeStruct((bsz, seqlen, SSM_WIDTH), BF16),
        scratch_shapes=[pltpu.VMEM((tq + 2 * SUBLANES, SSM_CONV_CH), F32),
                        pltpu.VMEM((tq, SSM_CONV_CH), F32),
                        pltpu.VMEM((tq, SSM_WIDTH), F32),
                        pltpu.VMEM((SSM_GROUPS, SSM_GROUP_WIDTH, SSM_STATE), F32)],
        compiler_params=_params(("arbitrary", "arbitrary")),
        name="ssd",
    )(xbc, zz, dt, conv_w, conv_b, alog_pad, dsk_wide, nw, expand3)


def _outproj_kernel(x_ref, oh_ref, ys_ref, wo_ref, g1_ref, nw_ref, sh_ref, sc_ref, rw_ref, rb_ref,
                    x1_ref, h2_ref, idx_ref, gate_ref, cnt_ref):
    first = jnp.logical_and(pl.program_id(0) == 0, pl.program_id(1) == 0)

    @pl.when(first)
    def _():
        cnt_ref[...] = jnp.zeros_like(cnt_ref)

    mix = jnp.dot(oh_ref[0], wo_ref[0:HG_WIDTH, :], preferred_element_type=F32)
    mix = mix + jnp.dot(ys_ref[0], wo_ref[HG_WIDTH:, :], preferred_element_type=F32)
    x1 = x_ref[0] + g1_ref[0] * mix
    x1_ref[0] = x1
    ms = jnp.mean(x1 * x1, axis=-1, keepdims=True)
    h2 = x1 * lax.rsqrt(ms + EPS) * nw_ref[...]
    h2 = h2 * (1.0 + sc_ref[0]) + sh_ref[0]
    h2_ref[0] = h2
    h_hi = h2.astype(BF16)
    h_lo = (h2 - h_hi.astype(F32)).astype(BF16)
    part = jnp.dot(h_hi, rw_ref[...], preferred_element_type=F32)
    logits = (part[:, :LANES] + part[:, LANES:]
              + jnp.dot(h_lo, rw_ref[:, :LANES], preferred_element_type=F32)) + rb_ref[...]
    lane = lax.broadcasted_iota(jnp.int32, logits.shape, 1).astype(F32)
    idx_out = jnp.zeros(logits.shape, F32)
    val_out = jnp.zeros(logits.shape, F32)
    sel = jnp.zeros(logits.shape, F32)
    work = logits
    top = None
    denom = None
    for k in range(TOP_K):
        m = jnp.max(work, axis=-1, keepdims=True)
        am = jnp.min(jnp.where(work == m, lane, float(LANES)), axis=-1, keepdims=True)
        hit = lane == am
        work = jnp.where(hit, -jnp.inf, work)
        sel = jnp.where(hit, 1.0, sel)
        if k == 0:
            top = m
        e = jnp.exp(m - top)
        denom = e if k == 0 else denom + e
        idx_out = jnp.where(lane == float(k), am, idx_out)
        val_out = jnp.where(lane == float(k), e, val_out)
    idx_ref[0] = idx_out.astype(jnp.int32)
    gate_ref[0] = val_out / denom
    cnt_ref[0:1, :] += jnp.sum(sel, axis=0, keepdims=True)


def _outproj(x, o_hg, y_ssd, w_out_b, g1, nw, sh, sc, rw_pad, rb_pad):
    bsz, seqlen, d = x.shape
    tm = min(ROW_TILE, seqlen)
    tok = lambda w: pl.BlockSpec((1, tm, w), lambda b, i: (b, i, 0))
    full = lambda a: pl.BlockSpec(a.shape, lambda b, i: (0,) * a.ndim)
    mod = pl.BlockSpec((1, 1, d), lambda b, i: (b, 0, 0))
    shp = lambda w, dt: jax.ShapeDtypeStruct((bsz, seqlen, w), dt)
    return pl.pallas_call(
        _outproj_kernel,
        grid=(bsz, seqlen // tm),
        in_specs=[tok(d), tok(HG_WIDTH), tok(SSM_WIDTH), full(w_out_b), mod, full(nw), mod, mod,
                  full(rw_pad), full(rb_pad)],
        out_specs=[tok(d), tok(d), tok(LANES), tok(LANES),
                   pl.BlockSpec((SUBLANES, LANES), lambda b, i: (0, 0))],
        out_shape=[shp(d, F32), shp(d, F32), shp(LANES, jnp.int32), shp(LANES, F32),
                   jax.ShapeDtypeStruct((SUBLANES, LANES), F32)],
        compiler_params=_params(("arbitrary", "arbitrary")),
        name="outproj_router",
    )(x, o_hg, y_ssd, w_out_b, g1, nw, sh, sc, rw_pad, rb_pad)


def _route_kernel(idx_ref, pstart_ref, dest_ref, carry_ref):
    @pl.when(pl.program_id(0) == 0)
    def _():
        carry_ref[...] = jnp.zeros_like(carry_ref)

    idx = idx_ref[...]
    tt = idx.shape[0]
    lane = lax.broadcasted_iota(jnp.int32, idx.shape, 1)
    hits = [lane == idx[:, k:k + 1] for k in range(TOP_K)]
    sel = jnp.zeros(idx.shape, F32)
    for hit in hits:
        sel = jnp.where(hit, 1.0, sel)
    rows = lax.broadcasted_iota(jnp.int32, (tt, tt), 0)
    cols = lax.broadcasted_iota(jnp.int32, (tt, tt), 1)
    before = (rows > cols).astype(BF16)
    rank = jnp.dot(before, sel.astype(BF16), preferred_element_type=F32) + carry_ref[0:1, :]
    carry_ref[0:1, :] += jnp.sum(sel, axis=0, keepdims=True)
    dense = pstart_ref[...] + rank
    pack = LANES // TOP_K
    slot = (lax.broadcasted_iota(jnp.int32, idx.shape, 0) % pack) * TOP_K
    spread = jnp.zeros(idx.shape, F32)
    for k, hit in enumerate(hits):
        dk = jnp.sum(jnp.where(hit, dense, 0.0), axis=-1, keepdims=True)
        spread = jnp.where(lane == slot + k, dk, spread)
    out_rows = tt // pack
    merge = (lax.broadcasted_iota(jnp.int32, (out_rows, tt), 0)
             == lax.broadcasted_iota(jnp.int32, (out_rows, tt), 1) // pack).astype(BF16)
    m3 = jnp.dot(merge, _split3(spread), preferred_element_type=F32)
    dest_ref[...] = (m3[:, :LANES] + m3[:, LANES:2 * LANES] + m3[:, 2 * LANES:]).astype(jnp.int32)


def _route(idx_pad, pstart_row):
    t = idx_pad.shape[0]
    tt = min(ROUTE_TILE, t)
    return pl.pallas_call(
        _route_kernel,
        grid=(t // tt,),
        in_specs=[pl.BlockSpec((tt, LANES), lambda i: (i, 0)),
                  pl.BlockSpec((1, LANES), lambda i: (0, 0))],
        out_specs=pl.BlockSpec((tt * TOP_K // LANES, LANES), lambda i: (i, 0)),
        out_shape=jax.ShapeDtypeStruct((t * TOP_K // LANES, LANES), jnp.int32),
        scratch_shapes=[pltpu.VMEM((SUBLANES, LANES), F32)],
        compiler_params=_params(("arbitrary",)),
        name="route_rank",
    )(idx_pad, pstart_row)


def _invert_kernel(dest_ref, zeros_ref, rowtok_ref, sem):
    i = pl.program_id(0)
    per_step = dest_ref.shape[0] // TOP_K // pl.num_programs(0)

    @pl.when(i == 0)
    def _():
        fill = pltpu.make_async_copy(zeros_ref, rowtok_ref, sem)
        fill.start()
        fill.wait()

    def tok_body(n, carry):
        tok = i * per_step + n
        for k in range(TOP_K):
            rowtok_ref[dest_ref[tok * TOP_K + k]] = tok
        return carry
    lax.fori_loop(0, per_step, tok_body, 0, unroll=8)


def _invert(dest_flat, n_rows):
    t = dest_flat.shape[0] // TOP_K
    return pl.pallas_call(
        _invert_kernel,
        grid=(max(t // INVERT_TILE, 1),),
        in_specs=[pl.BlockSpec(memory_space=pltpu.SMEM), pl.BlockSpec(memory_space=pl.ANY)],
        out_specs=pl.BlockSpec(memory_space=pltpu.SMEM),
        out_shape=jax.ShapeDtypeStruct((n_rows,), jnp.int32),
        scratch_shapes=[pltpu.SemaphoreType.DMA(())],
        compiler_params=_params(("arbitrary",)),
        name="invert_route",
    )(dest_flat, jnp.zeros((n_rows,), jnp.int32))


def _expert_kernel(be_ref, nact_ref, nxt_ref, rowtok_ref, h_ref, w1_ref, b1_ref, w2_ref, b2_ref, y_ref,
                   xbuf0_ref, xbuf1_ref, xbuf2_ref, w1s_ref, w2s_ref, w1b_ref, w2b_ref, sem, wsem):
    j = pl.program_id(0)
    n_act = nact_ref[0]
    active = j < n_act
    expert = be_ref[j]
    fresh = jnp.logical_or(j == 0, expert != be_ref[jnp.maximum(j - 1, 0)])
    phase = j % GATHER_DEPTH
    bufs = (xbuf0_ref, xbuf1_ref, xbuf2_ref)

    def issue(block, s):
        base = jnp.minimum(block, n_act - 1) * MOE_ROWS
        for r in range(MOE_ROWS):
            pltpu.make_async_copy(h_ref.at[pl.ds(rowtok_ref[base + r], 1), :],
                                  bufs[s].at[pl.ds(r, 1), :], sem.at[s]).start(priority=1)

    def wait(s):
        pltpu.make_async_copy(h_ref.at[pl.ds(0, MOE_ROWS), :], bufs[s], sem.at[s]).wait()

    def weight_copies(e):
        return (pltpu.make_async_copy(w1_ref.at[e], w1s_ref, wsem.at[0]),
                pltpu.make_async_copy(w2_ref.at[e], w2s_ref, wsem.at[1]))

    @pl.when(j == 0)
    def _():
        for cp in weight_copies(expert):
            cp.start()
        for s in range(GATHER_DEPTH - 1):
            issue(s, s)

    @pl.when(jnp.logical_and(active, fresh))
    def _():
        for cp in weight_copies(expert):
            cp.wait()
        w1b_ref[...] = w1s_ref[...].astype(BF16)
        w2b_ref[...] = w2s_ref[...].astype(BF16)

        @pl.when(nxt_ref[expert] < N_EXPERTS)
        def _():
            for cp in weight_copies(nxt_ref[expert]):
                cp.start()

    for s in range(GATHER_DEPTH):
        @pl.when(jnp.logical_and(j < n_act + GATHER_DEPTH - 1, phase == s))
        def _(s=s):
            wait(s)

        @pl.when(jnp.logical_and(active, phase == s))
        def _(s=s):
            issue(j + GATHER_DEPTH - 1, (s + GATHER_DEPTH - 1) % GATHER_DEPTH)
            hb = jnp.dot(bufs[s][...].astype(BF16), w1b_ref[...], preferred_element_type=F32) + b1_ref[0]
            glu = jnp.minimum(hb[:, :D_FF], SWIGLU_LIMIT)
            lin = jnp.clip(hb[:, D_FF:], -SWIGLU_LIMIT, SWIGLU_LIMIT)
            act = glu * jax.nn.sigmoid(SWIGLU_ALPHA * glu) * (lin + 1.0)
            y_ref[...] = jnp.dot(act.astype(BF16), w2b_ref[...], preferred_element_type=F32) + b2_ref[0]


def _experts(block_e, n_act, next_e, row_tok, h2, w1, b1, w2, b2):
    n_rows = row_tok.shape[0]
    d = h2.shape[1]
    nb = block_e.shape[0]
    row_map = lambda j, be, na, nx, rt: (jnp.maximum(jnp.minimum(j, na[0] - 1), 0), 0)
    exp_map = lambda j, be, na, nx, rt: (be[j], 0, 0)
    return pl.pallas_call(
        _expert_kernel,
        grid_spec=pltpu.PrefetchScalarGridSpec(
            num_scalar_prefetch=4,
            grid=(nb,),
            in_specs=[pl.BlockSpec(memory_space=pl.ANY),
                      pl.BlockSpec(memory_space=pl.ANY),
                      pl.BlockSpec((1, 1, 2 * D_FF), exp_map),
                      pl.BlockSpec(memory_space=pl.ANY),
                      pl.BlockSpec((1, 1, d), exp_map)],
            out_specs=pl.BlockSpec((MOE_ROWS, d), row_map),
            scratch_shapes=[pltpu.VMEM((MOE_ROWS, d), F32)] * GATHER_DEPTH + [
                pltpu.VMEM((d, 2 * D_FF), F32), pltpu.VMEM((D_FF, d), F32),
                pltpu.VMEM((d, 2 * D_FF), BF16), pltpu.VMEM((D_FF, d), BF16),
                pltpu.SemaphoreType.DMA((GATHER_DEPTH,)), pltpu.SemaphoreType.DMA((2,))],
        ),
        out_shape=jax.ShapeDtypeStruct((n_rows, d), F32),
        compiler_params=_params(("arbitrary",)),
        name="expert_ffn",
    )(block_e, n_act, next_e, row_tok, h2, w1, b1.reshape(N_EXPERTS, 1, 2 * D_FF), w2,
      b2.reshape(N_EXPERTS, 1, d))


def _combine_kernel(dest_ref, x1_ref, gate_ref, g2_ref, fw_ref, y_ref, o_ref,
                    buf0_ref, buf1_ref, buf2_ref, sem):
    i = pl.program_id(0)
    n = pl.num_programs(0)
    gt = x1_ref.shape[0]
    bufs = (buf0_ref, buf1_ref, buf2_ref)
    depth = len(bufs)

    def issue(tile, s):
        base = jnp.minimum(tile, n - 1) * (gt * TOP_K)
        for j in range(gt):
            for k in range(TOP_K):
                pltpu.make_async_copy(y_ref.at[pl.ds(dest_ref[base + j * TOP_K + k], 1), :],
                                      bufs[s].at[k, pl.ds(j, 1), :], sem.at[s]).start(priority=k % 2)

    def wait(s):
        for k in range(TOP_K):
            pltpu.make_async_copy(y_ref.at[pl.ds(0, gt), :], bufs[s].at[k], sem.at[s]).wait()

    @pl.when(i == 0)
    def _():
        for s in range(depth - 1):
            issue(s, s)

    for s in range(depth):
        @pl.when(i % depth == s)
        def _(s=s):
            wait(s)
            issue(i + depth - 1, (s + depth - 1) % depth)
            gates = gate_ref[...]
            moe = gates[:, 0:1] * bufs[s][0]
            for k in range(1, TOP_K):
                moe = moe + gates[:, k:k + 1] * bufs[s][k]
            x2 = x1_ref[...] + g2_ref[0] * moe
            ms = jnp.mean(x2 * x2, axis=-1, keepdims=True)
            o_ref[...] = x2 * lax.rsqrt(ms + EPS) * fw_ref[...]

        @pl.when(jnp.logical_and(i == n - 1, i % depth == s))
        def _(s=s):
            for ahead in range(1, depth):
                wait((s + ahead) % depth)


def _combine(dest_flat, x1, gate_pad, g2, fw, yout, seqlen):
    t, d = x1.shape
    gt = min(GATHER_TILE, seqlen)
    per_seq = seqlen // gt
    return pl.pallas_call(
        _combine_kernel,
        grid_spec=pltpu.PrefetchScalarGridSpec(
            num_scalar_prefetch=1,
            grid=(t // gt,),
            in_specs=[pl.BlockSpec((gt, d), lambda i, dst: (i, 0)),
                      pl.BlockSpec((gt, LANES), lambda i, dst: (i, 0)),
                      pl.BlockSpec((1, 1, d), lambda i, dst: (i // per_seq, 0, 0)),
                      pl.BlockSpec((1, d), lambda i, dst: (0, 0)),
                      pl.BlockSpec(memory_space=pl.ANY)],
            out_specs=pl.BlockSpec((gt, d), lambda i, dst: (i, 0)),
            scratch_shapes=[pltpu.VMEM((TOP_K, gt, d), F32)] * 3 + [pltpu.SemaphoreType.DMA((3,))],
        ),
        out_shape=jax.ShapeDtypeStruct((t, d), F32),
        compiler_params=_params(("arbitrary",)),
        name="combine_norm",
    )(dest_flat, x1, gate_pad, g2, fw, yout)


def _head_expand():
    head_of_lane = jnp.arange(SSM_WIDTH) // SSM_HEAD_DIM
    one = (jnp.arange(LANES)[:, None] == head_of_lane[None, :]).astype(BF16)
    return jnp.tile(one, (3, 1))


def _pad_lanes(v, fill=0.0):
    v = v.reshape(1, -1).astype(F32)
    return jnp.pad(v, ((0, 0), (0, LANES - v.shape[1])), constant_values=fill)


def kernel(x, c, ada_w, ada_b, norm1_w, w_in, hg_lb_logits, hg_norm_w, ssm_a_log, ssm_dt_bias, ssm_d,
           ssm_conv_w, ssm_conv_b, ssm_norm_w, w_out, norm2_w, router_w, router_b, exp_w1, exp_b1,
           exp_w2, exp_b2, final_norm_w):
    bsz, seqlen, d = x.shape
    t = bsz * seqlen
    l = 0

    c_pad = jnp.pad(c, ((0, SUBLANES - bsz), (0, 0)))
    mod = _ada_mod(c_pad, ada_w[l], ada_b[l])[:bsz]
    sh1, sc1, g1, sh2, sc2, g2 = [m.reshape(bsz, 1, d) for m in jnp.split(mod, N_MOD, axis=-1)]

    w_in_t = jnp.swapaxes(w_in[l], 0, 1)
    w_main = w_in_t.astype(BF16)
    w_dt = jnp.pad(w_in_t[_OFF[6]:].astype(BF16), ((0, LANES - SSM_HEADS), (0, 0)))
    qs, kk, lf, vi, gg, zz, xbc, dt = _inproj(
        x, norm1_w[l].reshape(1, d), sh1, sc1, w_main, w_dt, hg_lb_logits, _pad_lanes(ssm_dt_bias[l]))

    o_hg = _hgrn(qs, kk, lf, vi, gg, hg_norm_w[l].reshape(1, HG_DV))
    y_ssd = _ssd(xbc, zz, dt, ssm_conv_w[l], ssm_conv_b[l].reshape(1, SSM_CONV_CH),
                 _pad_lanes(ssm_a_log[l]), jnp.repeat(ssm_d[l], SSM_HEAD_DIM).reshape(1, SSM_WIDTH),
                 ssm_norm_w[l].reshape(1, SSM_WIDTH), _head_expand())

    rw_pad = jnp.pad(router_w[l], ((0, 0), (0, LANES - N_EXPERTS)))
    rw_hi = rw_pad.astype(BF16)
    rw_pad = jnp.concatenate([rw_hi, (rw_pad - rw_hi.astype(F32)).astype(BF16)], axis=1)
    rb_pad = _pad_lanes(router_b[l], NEG_BIG)
    x1, h2, idx_pad, gate_pad, counts = _outproj(
        x, o_hg, y_ssd, w_out[l].astype(BF16), g1, norm2_w[l].reshape(1, d), sh2, sc2, rw_pad, rb_pad)

    cnt = counts[0, :N_EXPERTS].astype(jnp.int32)
    blocks_e = (cnt + MOE_ROWS - 1) // MOE_ROWS
    blk_end = jnp.cumsum(blocks_e)
    pstart = (blk_end - blocks_e) * MOE_ROWS
    n_blocks = (t * TOP_K) // MOE_ROWS + N_EXPERTS
    n_rows = n_blocks * MOE_ROWS
    n_steps = n_blocks + GATHER_DEPTH - 1
    block_e = jnp.minimum(jnp.sum(blk_end[None, :] <= jnp.arange(n_steps)[:, None], axis=1),
                          N_EXPERTS - 1).astype(jnp.int32)
    n_act = blk_end[-1:].astype(jnp.int32)
    owner = jnp.where(blocks_e > 0, jnp.arange(N_EXPERTS), N_EXPERTS)
    next_e = jnp.concatenate([lax.cummin(owner, reverse=True)[1:],
                              jnp.full((1,), N_EXPERTS)]).astype(jnp.int32)

    dest_flat = _route(idx_pad.reshape(t, LANES), _pad_lanes(pstart)).reshape(t * TOP_K)

    row_tok = _invert(dest_flat, n_rows)
    yout = _experts(block_e, n_act, next_e, row_tok, h2.reshape(t, d), exp_w1[l], exp_b1[l], exp_w2[l], exp_b2[l])
    out = _combine(dest_flat, x1.reshape(t, d), gate_pad.reshape(t, LANES), g2,
                   final_norm_w.reshape(1, d), yout, seqlen)
    return out.reshape(bsz, seqlen, d)
```
